```python
import math
import jax, jax.numpy as jnp
from jax import lax
import numpy as np

D_MODEL = 2048
BATCH = 4
SEQ = 2048
DEPTH = 1

HEAD_DIM = 64
N_Q_HEADS = 16
N_KV_HEADS = 4
Q_PER_KV = N_Q_HEADS // N_KV_HEADS
ATTN_WIDTH = N_Q_HEADS * HEAD_DIM
KV_WIDTH = N_KV_HEADS * HEAD_DIM
WINDOW = 128
BLOCK = 128
NUM_BUCKETS = 32
MAX_EXACT = NUM_BUCKETS // 2
MAX_DISTANCE = 128
CONV_CHANNELS = D_MODEL // 2
CONV_WIDTH = 31
D_FF = ((8 * D_MODEL + 3 * 256 - 1) // (3 * 256)) * 256
EPS = 1e-6
IN_SIZES = [ATTN_WIDTH, KV_WIDTH, KV_WIDTH, CONV_CHANNELS, CONV_CHANNELS, D_MODEL, D_MODEL]
IN_WIDTH = sum(IN_SIZES)
IN_SPLITS = [int(s) for s in np.cumsum(IN_SIZES)[:-1]]
N_MOD = 6

kernel_name = "hybrid_swa_sink_conformer_conv_gated_block"


def rms_norm(x, g):
    xf = x.astype(jnp.float32)
    y = xf * lax.rsqrt(jnp.mean(xf * xf, axis=-1, keepdims=True) + EPS)
    return (y * g.astype(jnp.float32)).astype(x.dtype)


def layer_norm(x, g, b):
    xf = x.astype(jnp.float32)
    mu = jnp.mean(xf, axis=-1, keepdims=True)
    var = jnp.mean(jnp.square(xf - mu), axis=-1, keepdims=True)
    y = (xf - mu) * lax.rsqrt(var + EPS)
    return (y * g.astype(jnp.float32) + b.astype(jnp.float32)).astype(x.dtype)


def t5_causal_bucket(dist):
    n = jnp.maximum(dist, 0)
    nf = jnp.maximum(n, 1).astype(jnp.float32)
    large = MAX_EXACT + (jnp.log(nf / MAX_EXACT) / math.log(MAX_DISTANCE / MAX_EXACT)
                         * (NUM_BUCKETS - MAX_EXACT)).astype(jnp.int32)
    large = jnp.minimum(large, NUM_BUCKETS - 1)
    return jnp.where(n < MAX_EXACT, n, large)


def sliding_window_attention(q, k, v, rel_bias, sinks, q_g, k_g):
    b_sz, s_len = q.shape[0], q.shape[1]
    nb = s_len // BLOCK
    q = rms_norm(q, q_g)
    k = rms_norm(k, k_g)
    qb = q.reshape(b_sz, nb, BLOCK, N_KV_HEADS, Q_PER_KV, HEAD_DIM)

    def band(t):
        tb = t.reshape(b_sz, nb, BLOCK, N_KV_HEADS, HEAD_DIM)
        prev = jnp.concatenate([jnp.zeros_like(tb[:, :1]), tb[:, :-1]], axis=1)
        return jnp.concatenate([prev, tb], axis=2)

    kw, vw = band(k), band(v)
    logits = jnp.einsum('bnqhgd,bnkhd->bnhgqk', qb, kw).astype(jnp.float32) * (HEAD_DIM ** -0.5)

    q_off = jnp.arange(BLOCK)
    k_off = jnp.arange(2 * BLOCK)
    dist = q_off[:, None] + BLOCK - k_off[None, :]
    bias = rel_bias.astype(jnp.float32)[t5_causal_bucket(dist)]
    bias = jnp.transpose(bias, (2, 0, 1)).reshape(N_KV_HEADS, Q_PER_KV, BLOCK, 2 * BLOCK)
    allowed = (dist >= 0) & (dist < WINDOW)
    real_key = (jnp.arange(nb)[:, None, None] > 0) | (k_off[None, None, :] >= BLOCK)
    mask = allowed[None] & real_key

    logits = jnp.where(mask[None, :, None, None], logits + bias, -jnp.inf)
    sink = sinks.astype(jnp.float32).reshape(N_KV_HEADS, Q_PER_KV)[None, None, :, :, None, None]
    m = jnp.maximum(jnp.max(logits, axis=-1, keepdims=True), sink)
    p = jnp.exp(logits - m)
    probs = p / (jnp.sum(p, axis=-1, keepdims=True) + jnp.exp(sink - m))
    out = jnp.einsum('bnhgqk,bnkhd->bnqhgd', probs.astype(v.dtype), vw)
    return out.reshape(b_sz, s_len, ATTN_WIDTH)


def conformer_conv(a, gate, conv_w, conv_b, ln_g, ln_b, w_out):
    u = a * jax.nn.sigmoid(gate)
    u = lax.conv_general_dilated(
        u, conv_w[:, None, :].astype(u.dtype), window_strides=(1,),
        padding=[(CONV_WIDTH - 1, 0)], dimension_numbers=('NWC', 'WIO', 'NWC'),
        feature_group_count=CONV_CHANNELS) + conv_b
    u = jax.nn.silu(layer_norm(u, ln_g, ln_b))
    return u @ w_out


def setup_inputs(seed: int = 0) -> dict:
    key = jax.random.key(seed)
    ks = jax.random.split(key, 24)
    nrm = lambda k, shape, s: jax.random.normal(k, shape, jnp.float32) * s
    d = D_MODEL
    return {
        "x": nrm(ks[0], (BATCH, SEQ, d), 1.0),
        "c": nrm(ks[1], (BATCH, d), 1.0),
        "w_ada": nrm(ks[2], (DEPTH, d, N_MOD * d), 0.5 * d ** -0.5),
        "b_ada": nrm(ks[3], (DEPTH, N_MOD * d), 0.02),
        "norm_mix_g": 1.0 + nrm(ks[4], (DEPTH, d), 0.02),
        "w_in": nrm(ks[5], (DEPTH, d, IN_WIDTH), d ** -0.5),
        "q_norm_g": 1.0 + nrm(ks[6], (DEPTH, HEAD_DIM), 0.02),
        "k_norm_g": 1.0 + nrm(ks[7], (DEPTH, HEAD_DIM), 0.02),
        "attn_sinks": nrm(ks[8], (DEPTH, N_Q_HEADS), 0.5),
        "rel_bias": nrm(ks[9], (NUM_BUCKETS, N_Q_HEADS), 0.5),
        "w_attn_out": nrm(ks[10], (DEPTH, ATTN_WIDTH, d), ATTN_WIDTH ** -0.5),
        "conv_w": nrm(ks[11], (DEPTH, CONV_WIDTH, CONV_CHANNELS), CONV_WIDTH ** -0.5),
        "conv_b": nrm(ks[12], (DEPTH, CONV_CHANNELS), 0.02),
        "conv_ln_g": 1.0 + nrm(ks[13], (DEPTH, CONV_CHANNELS), 0.02),
        "conv_ln_b": nrm(ks[14], (DEPTH, CONV_CHANNELS), 0.02),
        "w_conv_out": nrm(ks[15], (DEPTH, CONV_CHANNELS, d), CONV_CHANNELS ** -0.5),
        "w_mix_out": nrm(ks[16], (DEPTH, d, d), d ** -0.5),
        "norm_ffn_g": 1.0 + nrm(ks[17], (DEPTH, d), 0.02),
        "w_ffn_in": nrm(ks[18], (DEPTH, d, 2 * D_FF), d ** -0.5),
        "w_ffn_out": nrm(ks[19], (DEPTH, D_FF, d), D_FF ** -0.5),
    }


def reference(x, c, w_ada, b_ada, norm_mix_g, w_in, q_norm_g, k_norm_g, attn_sinks, rel_bias,
              w_attn_out, conv_w, conv_b, conv_ln_g, conv_ln_b, w_conv_out, w_mix_out,
              norm_ffn_g, w_ffn_in, w_ffn_out):
    b_sz, s_len = x.shape[0], x.shape[1]
    c_act = jax.nn.silu(c)
    for l in range(DEPTH):
        mod = (c_act @ w_ada[l] + b_ada[l])[:, None, :]
        sh_m, sc_m, gt_m, sh_f, sc_f, gt_f = jnp.split(mod, N_MOD, axis=-1)

        h = rms_norm(x, norm_mix_g[l]) * (1.0 + sc_m) + sh_m
        q, k, v, ca, cb, g_attn, g_conv = jnp.split(h @ w_in[l], IN_SPLITS, axis=-1)
        q = q.reshape(b_sz, s_len, N_Q_HEADS, HEAD_DIM)
        k = k.reshape(b_sz, s_len, N_KV_HEADS, HEAD_DIM)
        v = v.reshape(b_sz, s_len, N_KV_HEADS, HEAD_DIM)
        y_attn = sliding_window_attention(q, k, v, rel_bias, attn_sinks[l],
                                          q_norm_g[l], k_norm_g[l]) @ w_attn_out[l]
        y_conv = conformer_conv(ca, cb, conv_w[l], conv_b[l], conv_ln_g[l], conv_ln_b[l],
                                w_conv_out[l])
        merged = jax.nn.sigmoid(g_attn) * y_attn + jax.nn.sigmoid(g_conv) * y_conv
        x = x + gt_m * (merged @ w_mix_out[l])

        h = rms_norm(x, norm_ffn_g[l]) * (1.0 + sc_f) + sh_f
        f_gate, f_up = jnp.split(h @ w_ffn_in[l], 2, axis=-1)
        x = x + gt_f * ((jax.nn.silu(f_gate) * f_up) @ w_ffn_out[l])
    return x
```

```python
import functools
import math

import jax
import jax.numpy as jnp
from jax import lax
from jax.experimental import pallas as pl
from jax.experimental.pallas import tpu as pltpu

F32 = jnp.float32
BF16 = jnp.bfloat16

HEAD_DIM = 64
WINDOW = 128
BLOCK = 128
NUM_BUCKETS = 32
MAX_EXACT = NUM_BUCKETS // 2
MAX_DISTANCE = 128
N_MOD = 6
EPS = 1e-6

V7X_VMEM_LIMIT_BYTES = 56 * 1024 * 1024


def _params(n_axes):
    return pltpu.CompilerParams(
        dimension_semantics=("arbitrary",) * n_axes,
        vmem_limit_bytes=V7X_VMEM_LIMIT_BYTES,
    )


def _sigmoid(x):
    return jax.nn.sigmoid(x)


def _ada_body(c_ref, w_ref, b_ref, o_ref):
    c = c_ref[...]
    act = (c * _sigmoid(c)).astype(BF16)
    o_ref[...] = jnp.dot(act, w_ref[...].astype(BF16), preferred_element_type=F32) + b_ref[...]


def _ada(c, w_ada, b_ada, layer, tn=1024):
    b_sz, d = c.shape
    n = w_ada.shape[2]
    rows = 8
    c_pad = jnp.pad(c, ((0, rows - b_sz), (0, 0)))
    out = pl.pallas_call(
        _ada_body,
        grid=(n // tn,),
        in_specs=[
            pl.BlockSpec((rows, d), lambda j: (0, 0)),
            pl.BlockSpec((None, d, tn), lambda j: (layer, 0, j)),
            pl.BlockSpec((1, tn), lambda j: (0, j)),
        ],
        out_specs=pl.BlockSpec((rows, tn), lambda j: (0, j)),
        out_shape=jax.ShapeDtypeStruct((rows, n), F32),
        compiler_params=_params(1),
        name="ada_mod",
    )(c_pad, w_ada, b_ada.reshape(1, n))
    return out[:b_sz]


def _norm_mod_body(x_ref, g_ref, sc_ref, sh_ref, o_ref):
    x = x_ref[0]
    ms = jnp.mean(x * x, axis=-1, keepdims=True)
    y = x * lax.rsqrt(ms + EPS) * g_ref[...]
    o_ref[0] = (y * (1.0 + sc_ref[0]) + sh_ref[0]).astype(o_ref.dtype)


def _norm_mod(x, g, mod3, scale_idx, shift_idx, ts=512):
    b_sz, s_len, d = x.shape
    return pl.pallas_call(
        _norm_mod_body,
        grid=(b_sz, s_len // ts),
        in_specs=[
            pl.BlockSpec((1, ts, d), lambda b, s: (b, s, 0)),
            pl.BlockSpec((1, d), lambda b, s: (0, 0)),
            pl.BlockSpec((1, 1, d), lambda b, s: (b * N_MOD + scale_idx, 0, 0)),
            pl.BlockSpec((1, 1, d), lambda b, s: (b * N_MOD + shift_idx, 0, 0)),
        ],
        out_specs=pl.BlockSpec((1, ts, d), lambda b, s: (b, s, 0)),
        out_shape=jax.ShapeDtypeStruct((b_sz, s_len, d), BF16),
        compiler_params=_params(2),
        name="norm_mod",
    )(x, g.reshape(1, d), mod3, mod3)


def _cast_weights_once(w_refs, wbf_refs):
    @pl.when(pl.program_id(1) == 0)
    def _():
        for w_ref, wbf_ref in zip(w_refs, wbf_refs):
            wbf_ref[...] = w_ref[...].astype(BF16)


def _in_proj_body(a_ref, w_ref, o_ref, wbf_ref):
    _cast_weights_once([w_ref], [wbf_ref])
    acc = jnp.dot(a_ref[...], wbf_ref[...], preferred_element_type=F32)
    o_ref[...] = acc.astype(o_ref.dtype)


def _in_proj(h, w, layer, tail_lo, tail_hi, tm=1024, tn=512):
    m, k = h.shape
    n = w.shape[2]
    assert tail_lo % tn == 0 and tail_hi % tn == 0 and n % tn == 0
    lo_blk, hi_blk, n_blk = tail_lo // tn, tail_hi // tn, n // tn

    def out_block(j):
        moved = j + (n_blk - hi_blk)
        shifted = j - (hi_blk - lo_blk)
        return jnp.where(j < lo_blk, j, jnp.where(j < hi_blk, moved, shifted))

    return pl.pallas_call(
        _in_proj_body,
        grid=(n // tn, m // tm),
        in_specs=[
            pl.BlockSpec((tm, k), lambda j, i: (i, 0)),
            pl.BlockSpec((None, k, tn), lambda j, i: (layer, 0, j)),
        ],
        out_specs=pl.BlockSpec((tm, tn), lambda j, i: (i, out_block(j))),
        out_shape=jax.ShapeDtypeStruct((m, n), BF16),
        scratch_shapes=[pltpu.VMEM((k, tn), BF16)],
        compiler_params=_params(2),
        name="in_proj",
    )(h, w)


def _merge_body(ya_ref, yc_ref, wa_ref, wc_ref, ga_ref, gc_ref, o_ref, wabf_ref, wcbf_ref):
    _cast_weights_once([wa_ref, wc_ref], [wabf_ref, wcbf_ref])
    acc_a = jnp.dot(ya_ref[...], wabf_ref[...], preferred_element_type=F32)
    acc_c = jnp.dot(yc_ref[...], wcbf_ref[...], preferred_element_type=F32)
    ga = _sigmoid(ga_ref[...].astype(F32))
    gc = _sigmoid(gc_ref[...].astype(F32))
    o_ref[...] = (ga * acc_a + gc * acc_c).astype(o_ref.dtype)


def _merge(attn, conv, w_attn_out, w_conv_out, layer, proj, ga_col, gc_col, tm=1024, tn=512):
    m, ka = attn.shape
    kc = conv.shape[1]
    n = w_attn_out.shape[2]
    assert ga_col % tn == 0 and gc_col % tn == 0
    ga_blk, gc_blk = ga_col // tn, gc_col // tn
    return pl.pallas_call(
        _merge_body,
        grid=(n // tn, m // tm),
        in_specs=[
            pl.BlockSpec((tm, ka), lambda j, i: (i, 0)),
            pl.BlockSpec((tm, kc), lambda j, i: (i, 0)),
            pl.BlockSpec((None, ka, tn), lambda j, i: (layer, 0, j)),
            pl.BlockSpec((None, kc, tn), lambda j, i: (layer, 0, j)),
            pl.BlockSpec((tm, tn), lambda j, i: (i, ga_blk + j)),
            pl.BlockSpec((tm, tn), lambda j, i: (i, gc_blk + j)),
        ],
        out_specs=pl.BlockSpec((tm, tn), lambda j, i: (i, j)),
        out_shape=jax.ShapeDtypeStruct((m, n), BF16),
        scratch_shapes=[pltpu.VMEM((ka, tn), BF16), pltpu.VMEM((kc, tn), BF16)],
        compiler_params=_params(2),
        name="merge",
    )(attn, conv, w_attn_out, w_conv_out, proj, proj)


def _residual_proj_body(a_ref, w_ref, x_ref, gt_ref, o_ref, wbf_ref):
    _cast_weights_once([w_ref], [wbf_ref])
    acc = jnp.dot(a_ref[...], wbf_ref[...], preferred_element_type=F32)
    o_ref[...] = x_ref[...] + gt_ref[0] * acc


def _residual_proj(a, w, layer, x2d, mod3, gate_idx, rows_per_batch, tm, tn):
    m, k = a.shape
    n = w.shape[2]
    tiles_per_batch = rows_per_batch // tm
    return pl.pallas_call(
        _residual_proj_body,
        grid=(n // tn, m // tm),
        in_specs=[
            pl.BlockSpec((tm, k), lambda j, i: (i, 0)),
            pl.BlockSpec((None, k, tn), lambda j, i: (layer, 0, j)),
            pl.BlockSpec((tm, tn), lambda j, i: (i, j)),
            pl.BlockSpec((1, 1, tn), lambda j, i: ((i // tiles_per_batch) * N_MOD + gate_idx, 0, j)),
        ],
        out_specs=pl.BlockSpec((tm, tn), lambda j, i: (i, j)),
        out_shape=jax.ShapeDtypeStruct((m, n), F32),
        scratch_shapes=[pltpu.VMEM((k, tn), BF16)],
        compiler_params=_params(2),
        name="residual_proj",
    )(a, w, x2d, mod3)


def _ffn_in_body(a_ref, wg_ref, wu_ref, o_ref, wgbf_ref, wubf_ref):
    _cast_weights_once([wg_ref, wu_ref], [wgbf_ref, wubf_ref])
    a = a_ref[...]
    gate = jnp.dot(a, wgbf_ref[...], preferred_element_type=F32)
    up = jnp.dot(a, wubf_ref[...], preferred_element_type=F32)
    o_ref[...] = (gate * _sigmoid(gate) * up).astype(o_ref.dtype)


def _ffn_in(h, w_ffn_in, layer, tm=1024, tn=512):
    m, k = h.shape
    d_ff = w_ffn_in.shape[2] // 2
    up_blk = d_ff // tn
    return pl.pallas_call(
        _ffn_in_body,
        grid=(d_ff // tn, m // tm),
        in_specs=[
            pl.BlockSpec((tm, k), lambda j, i: (i, 0)),
            pl.BlockSpec((None, k, tn), lambda j, i: (layer, 0, j)),
            pl.BlockSpec((None, k, tn), lambda j, i: (layer, 0, up_blk + j)),
        ],
        out_specs=pl.BlockSpec((tm, tn), lambda j, i: (i, j)),
        out_shape=jax.ShapeDtypeStruct((m, d_ff), BF16),
        scratch_shapes=[pltpu.VMEM((k, tn), BF16), pltpu.VMEM((k, tn), BF16)],
        compiler_params=_params(2),
        name="ffn_in",
    )(h, w_ffn_in, w_ffn_in)


def _head_rms(t, g):
    ms = jnp.mean(t * t, axis=-1, keepdims=True)
    return t * lax.rsqrt(ms + EPS) * g


def _attn_body(sink_ref, q_ref, kc_ref, kp_ref, vc_ref, vp_ref, bias_ref, qg_ref, kg_ref, o_ref,
               *, n_kv, q_per_kv):
    first_block = pl.program_id(1) == 0
    qg = qg_ref[...]
    kg = kg_ref[...]
    col = lax.broadcasted_iota(jnp.int32, (BLOCK, 2 * BLOCK), 1)
    pad_key = jnp.logical_and(first_block, col < BLOCK)
    scale = HEAD_DIM ** -0.5
    for g in range(n_kv):
        ks = slice(g * HEAD_DIM, (g + 1) * HEAD_DIM)
        k_g = jnp.concatenate([kp_ref[:, ks], kc_ref[:, ks]], axis=0).astype(F32)
        k_n = _head_rms(k_g, kg).astype(BF16)
        v_g = jnp.concatenate([vp_ref[:, ks], vc_ref[:, ks]], axis=0)
        for j in range(q_per_kv):
            h = g * q_per_kv + j
            hs = slice(h * HEAD_DIM, (h + 1) * HEAD_DIM)
            q_n = _head_rms(q_ref[:, hs].astype(F32), qg).astype(BF16)
            logits = lax.dot_general(q_n, k_n, (((1,), (1,)), ((), ())),
                                     preferred_element_type=F32) * scale
            logits = jnp.where(pad_key, -jnp.inf, logits + bias_ref[h])
            sink = sink_ref[h]
            m = jnp.maximum(jnp.max(logits, axis=-1, keepdims=True), sink)
            p = jnp.exp(logits - m)
            denom = jnp.sum(p, axis=-1, keepdims=True) + jnp.exp(sink - m)
            probs = (p / denom).astype(BF16)
            o_ref[:, hs] = jnp.dot(probs, v_g, preferred_element_type=F32).astype(o_ref.dtype)


def _attention(proj, k_col, v_col, bias_mask, sinks, q_g, k_g, b_sz, s_len, n_q, n_kv):
    nb = s_len // BLOCK
    attn_w = n_q * HEAD_DIM
    kv_w = n_kv * HEAD_DIM
    assert k_col % kv_w == 0 and v_col % kv_w == 0
    k_blk, v_blk = k_col // kv_w, v_col // kv_w
    cur = lambda b, n: b * nb + n
    prev = lambda b, n: b * nb + jnp.maximum(n - 1, 0)
    body = functools.partial(_attn_body, n_kv=n_kv, q_per_kv=n_q // n_kv)
    return pl.pallas_call(
        body,
        grid=(b_sz, nb),
        in_specs=[
            pl.BlockSpec(memory_space=pltpu.SMEM),
            pl.BlockSpec((BLOCK, attn_w), lambda b, n: (cur(b, n), 0)),
            pl.BlockSpec((BLOCK, kv_w), lambda b, n: (cur(b, n), k_blk)),
            pl.BlockSpec((BLOCK, kv_w), lambda b, n: (prev(b, n), k_blk)),
            pl.BlockSpec((BLOCK, kv_w), lambda b, n: (cur(b, n), v_blk)),
            pl.BlockSpec((BLOCK, kv_w), lambda b, n: (prev(b, n), v_blk)),
            pl.BlockSpec((n_q, BLOCK, 2 * BLOCK), lambda b, n: (0, 0, 0)),
            pl.BlockSpec((1, HEAD_DIM), lambda b, n: (0, 0)),
            pl.BlockSpec((1, HEAD_DIM), lambda b, n: (0, 0)),
        ],
        out_specs=pl.BlockSpec((BLOCK, attn_w), lambda b, n: (cur(b, n), 0)),
        out_shape=jax.ShapeDtypeStruct((b_sz * s_len, attn_w), BF16),
        compiler_params=_params(2),
        name="swa_attention",
    )(sinks, proj, proj, proj, proj, proj, bias_mask, q_g.reshape(1, HEAD_DIM), k_g.reshape(1, HEAD_DIM))


def _t5_causal_bucket(dist):
    n = jnp.maximum(dist, 0)
    nf = jnp.maximum(n, 1).astype(jnp.float32)
    large = MAX_EXACT + (jnp.log(nf / MAX_EXACT) / math.log(MAX_DISTANCE / MAX_EXACT)
                         * (NUM_BUCKETS - MAX_EXACT)).astype(jnp.int32)
    large = jnp.minimum(large, NUM_BUCKETS - 1)
    return jnp.where(n < MAX_EXACT, n, large)


def _band_bias(rel_bias):
    q_off = jnp.arange(BLOCK)
    k_off = jnp.arange(2 * BLOCK)
    dist = q_off[:, None] + BLOCK - k_off[None, :]
    bias = rel_bias.astype(F32)[_t5_causal_bucket(dist)]
    allowed = (dist >= 0) & (dist < WINDOW)
    bias = jnp.where(allowed[:, :, None], bias, -jnp.inf)
    return jnp.transpose(bias, (2, 0, 1))


_CONV_HALO = 32
_CONV_LANES = 256


def _conv_body(ac_ref, gc_ref, ap_ref, gp_ref, w_ref, cb_ref, lg_ref, lb_ref, o_ref, u_ref, y_ref,
               *, width, ts):
    channels = u_ref.shape[1]
    u_ref[_CONV_HALO:, :] = ac_ref[...].astype(F32) * _sigmoid(gc_ref[...].astype(F32))
    u_prev = ap_ref[...].astype(F32) * _sigmoid(gp_ref[...].astype(F32))
    u_ref[:_CONV_HALO, :] = jnp.where(pl.program_id(1) == 0, 0.0, u_prev)
    first = _CONV_HALO - (width - 1)
    for c0 in range(0, channels, _CONV_LANES):
        cs = slice(c0, c0 + _CONV_LANES)
        acc = jnp.broadcast_to(cb_ref[:, cs], (ts, _CONV_LANES))
        for j in range(width):
            acc = acc + u_ref[first + j:first + j + ts, cs] * w_ref[j:j + 1, cs]
        y_ref[:, cs] = acc
    y = y_ref[...]
    mu = jnp.mean(y, axis=-1, keepdims=True)
    yc = y - mu
    var = jnp.mean(yc * yc, axis=-1, keepdims=True)
    z = yc * lax.rsqrt(var + EPS) * lg_ref[...] + lb_ref[...]
    o_ref[...] = (z * _sigmoid(z)).astype(o_ref.dtype)


def _conformer_conv(proj, a_col, g_col, conv_w, conv_b, ln_g, ln_b, b_sz, s_len, ts=128):
    width, channels = conv_w.shape
    assert a_col % channels == 0 and g_col % channels == 0
    a_blk, g_blk = a_col // channels, g_col // channels
    tiles = s_len // ts
    halo_per_tile = ts // _CONV_HALO
    cur = lambda b, s: b * tiles + s
    prev = lambda b, s: jnp.maximum((b * tiles + s) * halo_per_tile - 1, 0)
    row = lambda v: v.reshape(1, channels)
    body = functools.partial(_conv_body, width=width, ts=ts)
    return pl.pallas_call(
        body,
        grid=(b_sz, tiles),
        in_specs=[
            pl.BlockSpec((ts, channels), lambda b, s: (cur(b, s), a_blk)),
            pl.BlockSpec((ts, channels), lambda b, s: (cur(b, s), g_blk)),
            pl.BlockSpec((_CONV_HALO, channels), lambda b, s: (prev(b, s), a_blk)),
            pl.BlockSpec((_CONV_HALO, channels), lambda b, s: (prev(b, s), g_blk)),
            pl.BlockSpec((width, channels), lambda b, s: (0, 0)),
            pl.BlockSpec((1, channels), lambda b, s: (0, 0)),
            pl.BlockSpec((1, channels), lambda b, s: (0, 0)),
            pl.BlockSpec((1, channels), lambda b, s: (0, 0)),
        ],
        out_specs=pl.BlockSpec((ts, channels), lambda b, s: (cur(b, s), 0)),
        out_shape=jax.ShapeDtypeStruct((b_sz * s_len, channels), BF16),
        scratch_shapes=[pltpu.VMEM((_CONV_HALO + ts, channels), F32), pltpu.VMEM((ts, channels), F32)],
        compiler_params=_params(2),
        name="conformer_conv",
    )(proj, proj, proj, proj, conv_w, row(conv_b), row(ln_g), row(ln_b))


def kernel(x, c, w_ada, b_ada, norm_mix_g, w_in, q_norm_g, k_norm_g, attn_sinks, rel_bias,
           w_attn_out, conv_w, conv_b, conv_ln_g, conv_ln_b, w_conv_out, w_mix_out,
           norm_ffn_g, w_ffn_in, w_ffn_out):
    b_sz, s_len, d = x.shape
    depth = w_ada.shape[0]
    n_q = attn_sinks.shape[1]
    attn_w = w_attn_out.shape[1]
    channels = conv_w.shape[2]
    kv_w = (w_in.shape[2] - attn_w - 2 * channels - 2 * d) // 2
    n_kv = kv_w // HEAD_DIM
    a_col = attn_w
    g_col = a_col + channels
    ga_col = g_col + channels
    gc_col = ga_col + d
    k_col = gc_col + d
    v_col = k_col + kv_w
    m = b_sz * s_len

    bias_mask = _band_bias(rel_bias)
    for l in range(depth):
        mod = _ada(c, w_ada, b_ada[l], l)
        mod3 = mod.reshape(b_sz * N_MOD, 1, d)

        h = _norm_mod(x, norm_mix_g[l], mod3, scale_idx=1, shift_idx=0).reshape(m, d)
        proj = _in_proj(h, w_in, l, attn_w, attn_w + 2 * kv_w)
        y_attn = _attention(proj, k_col, v_col, bias_mask, attn_sinks[l], q_norm_g[l], k_norm_g[l],
                            b_sz, s_len, n_q, n_kv)
        y_conv = _conformer_conv(proj, a_col, g_col, conv_w[l], conv_b[l], conv_ln_g[l],
                                 conv_ln_b[l], b_sz, s_len)
        merged = _merge(y_attn, y_conv, w_attn_out, w_conv_out, l, proj, ga_col, gc_col)
        x1 = _residual_proj(merged, w_mix_out, l, x.reshape(m, d), mod3, 2, s_len, tm=1024, tn=512)

        h = _norm_mod(x1.reshape(b_sz, s_len, d), norm_ffn_g[l], mod3, scale_idx=4, shift_idx=3)
        act = _ffn_in(h.reshape(m, d), w_ffn_in, l)
        x = _residual_proj(act, w_ffn_out, l, x1, mod3, 5, s_len, tm=512, tn=512).reshape(b_sz, s_len, d)
    return x
```

```python
import functools
import math

import jax
import jax.numpy as jnp
from jax import lax
from jax.experimental import pallas as pl
from jax.experimental.pallas import tpu as pltpu

F32 = jnp.float32
BF16 = jnp.bfloat16

HEAD_DIM = 64
WINDOW = 128
BLOCK = 128
NUM_BUCKETS = 32
MAX_EXACT = NUM_BUCKETS // 2
MAX_DISTANCE = 128
N_MOD = 6
EPS = 1e-6
LOG2E = math.log2(math.e)

V7X_VMEM_LIMIT_BYTES = 56 * 1024 * 1024


def _params(n_axes):
    return pltpu.CompilerParams(
        dimension_semantics=("arbitrary",) * n_axes,
        vmem_limit_bytes=V7X_VMEM_LIMIT_BYTES,
    )


def _sigmoid(x):
    return jax.nn.sigmoid(x)


def _orig_head(slot, n_kv, q_per_kv):
    return (slot % n_kv) * q_per_kv + slot // n_kv


def _ada_body(c_ref, w_ref, b_ref, o_ref):
    c = c_ref[...]
    act = (c * _sigmoid(c)).astype(BF16)
    o_ref[...] = jnp.dot(act, w_ref[...].astype(BF16), preferred_element_type=F32) + b_ref[...]


def _ada(c, w_ada, b_ada, layer, tn=1024):
    b_sz, d = c.shape
    n = w_ada.shape[2]
    rows = 8
    c_pad = jnp.pad(c, ((0, rows - b_sz), (0, 0)))
    out = pl.pallas_call(
        _ada_body,
        grid=(n // tn,),
        in_specs=[
            pl.BlockSpec((rows, d), lambda j: (0, 0)),
            pl.BlockSpec((None, d, tn), lambda j: (layer, 0, j)),
            pl.BlockSpec((1, tn), lambda j: (0, j)),
        ],
        out_specs=pl.BlockSpec((rows, tn), lambda j: (0, j)),
        out_shape=jax.ShapeDtypeStruct((rows, n), F32),
        compiler_params=_params(1),
        name="ada_mod",
    )(c_pad, w_ada, b_ada.reshape(1, n))
    return out[:b_sz]


def _norm_mod_body(x_ref, g_ref, sc_ref, sh_ref, o_ref):
    x = x_ref[0]
    ms = jnp.mean(x * x, axis=-1, keepdims=True)
    y = x * lax.rsqrt(ms + EPS) * g_ref[...]
    o_ref[0] = (y * (1.0 + sc_ref[0]) + sh_ref[0]).astype(o_ref.dtype)


def _norm_mod(x, g, mod3, scale_idx, shift_idx, ts=512):
    b_sz, s_len, d = x.shape
    return pl.pallas_call(
        _norm_mod_body,
        grid=(b_sz, s_len // ts),
        in_specs=[
            pl.BlockSpec((1, ts, d), lambda b, s: (b, s, 0)),
            pl.BlockSpec((1, d), lambda b, s: (0, 0)),
            pl.BlockSpec((1, 1, d), lambda b, s: (b * N_MOD + scale_idx, 0, 0)),
            pl.BlockSpec((1, 1, d), lambda b, s: (b * N_MOD + shift_idx, 0, 0)),
        ],
        out_specs=pl.BlockSpec((1, ts, d), lambda b, s: (b, s, 0)),
        out_shape=jax.ShapeDtypeStruct((b_sz, s_len, d), BF16),
        compiler_params=_params(2),
        name="norm_mod",
    )(x, g.reshape(1, d), mod3, mod3)


def _qkv_body(a_ref, w_ref, gq_ref, gk_ref, o_ref, wbf_ref, seg_ref, *, n_q, n_kv, half):
    attn_w = n_q * HEAD_DIM
    kv_w = n_kv * HEAD_DIM

    @pl.when(pl.program_id(0) == 0)
    def _():
        for slot in range(n_q):
            src = _orig_head(slot, n_kv, n_q // n_kv) * HEAD_DIM
            wbf_ref[:, slot * HEAD_DIM:(slot + 1) * HEAD_DIM] = w_ref[:, src:src + HEAD_DIM].astype(BF16)
        wbf_ref[:, attn_w:] = w_ref[:, attn_w:].astype(BF16)
        r = lax.broadcasted_iota(jnp.int32, (half, half), 0) // HEAD_DIM
        c = lax.broadcasted_iota(jnp.int32, (half, half), 1) // HEAD_DIM
        seg_ref[...] = (r == c).astype(BF16)

    a = a_ref[...]

    def head_norm(acc, seg, gain):
        ss = jnp.dot((acc * acc).astype(BF16), seg, preferred_element_type=F32)
        return acc * lax.rsqrt(ss / HEAD_DIM + EPS) * gain

    for c0 in range(0, attn_w, half):
        acc = jnp.dot(a, wbf_ref[:, c0:c0 + half], preferred_element_type=F32)
        o_ref[:, c0:c0 + half] = head_norm(acc, seg_ref[...], gq_ref[...]).astype(o_ref.dtype)
    acc = jnp.dot(a, wbf_ref[:, attn_w:attn_w + kv_w], preferred_element_type=F32)
    o_ref[:, attn_w:attn_w + kv_w] = head_norm(acc, seg_ref[:kv_w, :kv_w], gk_ref[...]).astype(o_ref.dtype)
    acc = jnp.dot(a, wbf_ref[:, attn_w + kv_w:], preferred_element_type=F32)
    o_ref[:, attn_w + kv_w:] = acc.astype(o_ref.dtype)


def _qkv_proj(h, w_in, layer, q_g, k_g, n_q, n_kv, tm=1024, half=512):
    m, k = h.shape
    attn_w, kv_w = n_q * HEAD_DIM, n_kv * HEAD_DIM
    width = attn_w + 2 * kv_w
    assert attn_w % half == 0 and kv_w <= half and half % HEAD_DIM == 0
    gq = jnp.tile(q_g * (HEAD_DIM ** -0.5 * LOG2E), half // HEAD_DIM).reshape(1, half)
    gk = jnp.tile(k_g, n_kv).reshape(1, kv_w)
    body = functools.partial(_qkv_body, n_q=n_q, n_kv=n_kv, half=half)
    return pl.pallas_call(
        body,
        grid=(m // tm,),
        in_specs=[
            pl.BlockSpec((tm, k), lambda i: (i, 0)),
            pl.BlockSpec((None, k, width), lambda i: (layer, 0, 0), pipeline_mode=pl.Buffered(1)),
            pl.BlockSpec((1, half), lambda i: (0, 0)),
            pl.BlockSpec((1, kv_w), lambda i: (0, 0)),
        ],
        out_specs=pl.BlockSpec((tm, width), lambda i: (i, 0)),
        out_shape=jax.ShapeDtypeStruct((m, width), BF16),
        scratch_shapes=[pltpu.VMEM((k, width), BF16), pltpu.VMEM((half, half), BF16)],
        compiler_params=_params(1),
        name="qkv_proj",
    )(h, w_in, gq, gk)


def _cast_weights_once(w_refs, wbf_refs):
    @pl.when(pl.program_id(1) == 0)
    def _():
        for w_ref, wbf_ref in zip(w_refs, wbf_refs):
            wbf_ref[...] = w_ref[...].astype(BF16)


def _proj_body(a_ref, w_ref, o_ref, wbf_ref):
    _cast_weights_once([w_ref], [wbf_ref])
    acc = jnp.dot(a_ref[...], wbf_ref[...], preferred_element_type=F32)
    o_ref[...] = acc.astype(o_ref.dtype)


def _proj(h, w, layer, col0, tm, tn):
    m, k = h.shape
    n = w.shape[2] - col0
    assert col0 % tn == 0 and n % tn == 0
    blk0 = col0 // tn
    return pl.pallas_call(
        _proj_body,
        grid=(n // tn, m // tm),
        in_specs=[
            pl.BlockSpec((tm, k), lambda j, i: (i, 0)),
            pl.BlockSpec((None, k, tn), lambda j, i: (layer, 0, blk0 + j)),
        ],
        out_specs=pl.BlockSpec((tm, tn), lambda j, i: (i, j)),
        out_shape=jax.ShapeDtypeStruct((m, n), BF16),
        scratch_shapes=[pltpu.VMEM((k, tn), BF16)],
        compiler_params=_params(2),
        name="proj",
    )(h, w)


def _merge_body(ya_ref, yc_ref, wa_ref, wc_ref, ga_ref, gc_ref, o_ref, wabf_ref, wcbf_ref,
                *, n_q, n_kv):
    @pl.when(pl.program_id(1) == 0)
    def _():
        for slot in range(n_q):
            src = _orig_head(slot, n_kv, n_q // n_kv) * HEAD_DIM
            wabf_ref[slot * HEAD_DIM:(slot + 1) * HEAD_DIM, :] = wa_ref[src:src + HEAD_DIM, :].astype(BF16)
        wcbf_ref[...] = wc_ref[...].astype(BF16)

    acc_a = jnp.dot(ya_ref[...], wabf_ref[...], preferred_element_type=F32)
    acc_c = jnp.dot(yc_ref[...], wcbf_ref[...], preferred_element_type=F32)
    ga = _sigmoid(ga_ref[...].astype(F32))
    gc = _sigmoid(gc_ref[...].astype(F32))
    o_ref[...] = (ga * acc_a + gc * acc_c).astype(o_ref.dtype)


def _merge(attn, conv, w_attn_out, w_conv_out, layer, gates, ga_col, gc_col, n_q, n_kv,
           tm=1024, tn=512):
    m, ka = attn.shape
    kc = conv.shape[1]
    n = w_attn_out.shape[2]
    assert ga_col % tn == 0 and gc_col % tn == 0
    ga_blk, gc_blk = ga_col // tn, gc_col // tn
    body = functools.partial(_merge_body, n_q=n_q, n_kv=n_kv)
    return pl.pallas_call(
        body,
        grid=(n // tn, m // tm),
        in_specs=[
            pl.BlockSpec((tm, ka), lambda j, i: (i, 0)),
            pl.BlockSpec((tm, kc), lambda j, i: (i, 0)),
            pl.BlockSpec((None, ka, tn), lambda j, i: (layer, 0, j)),
            pl.BlockSpec((None, kc, tn), lambda j, i: (layer, 0, j)),
            pl.BlockSpec((tm, tn), lambda j, i: (i, ga_blk + j)),
            pl.BlockSpec((tm, tn), lambda j, i: (i, gc_blk + j)),
        ],
        out_specs=pl.BlockSpec((tm, tn), lambda j, i: (i, j)),
        out_shape=jax.ShapeDtypeStruct((m, n), BF16),
        scratch_shapes=[pltpu.VMEM((ka, tn), BF16), pltpu.VMEM((kc, tn), BF16)],
        compiler_params=_params(2),
        name="merge",
    )(attn, conv, w_attn_out, w_conv_out, gates, gates)


def _residual_proj_body(a_ref, w_ref, x_ref, gt_ref, o_ref, wbf_ref):
    _cast_weights_once([w_ref], [wbf_ref])
    acc = jnp.dot(a_ref[...], wbf_ref[...], preferred_element_type=F32)
    o_ref[...] = x_ref[...] + gt_ref[0] * acc


def _residual_proj(a, w, layer, x2d, mod3, gate_idx, rows_per_batch, tm, tn):
    m, k = a.shape
    n = w.shape[2]
    tiles_per_batch = rows_per_batch // tm
    return pl.pallas_call(
        _residual_proj_body,
        grid=(n // tn, m // tm),
        in_specs=[
            pl.BlockSpec((tm, k), lambda j, i: (i, 0)),
            pl.BlockSpec((None, k, tn), lambda j, i: (layer, 0, j)),
            pl.BlockSpec((tm, tn), lambda j, i: (i, j)),
            pl.BlockSpec((1, 1, tn), lambda j, i: ((i // tiles_per_batch) * N_MOD + gate_idx, 0, j)),
        ],
        out_specs=pl.BlockSpec((tm, tn), lambda j, i: (i, j)),
        out_shape=jax.ShapeDtypeStruct((m, n), F32),
        scratch_shapes=[pltpu.VMEM((k, tn), BF16)],
        compiler_params=_params(2),
        name="residual_proj",
    )(a, w, x2d, mod3)


def _ffn_in_body(a_ref, wg_ref, wu_ref, o_ref, wgbf_ref, wubf_ref):
    _cast_weights_once([wg_ref, wu_ref], [wgbf_ref, wubf_ref])
    a = a_ref[...]
    gate = jnp.dot(a, wgbf_ref[...], preferred_element_type=F32)
    up = jnp.dot(a, wubf_ref[...], preferred_element_type=F32)
    o_ref[...] = (gate * _sigmoid(gate) * up).astype(o_ref.dtype)


def _ffn_in(h, w_ffn_in, layer, tm=1024, tn=512):
    m, k = h.shape
    d_ff = w_ffn_in.shape[2] // 2
    up_blk = d_ff // tn
    return pl.pallas_call(
        _ffn_in_body,
        grid=(d_ff // tn, m // tm),
        in_specs=[
            pl.BlockSpec((tm, k), lambda j, i: (i, 0)),
            pl.BlockSpec((None, k, tn), lambda j, i: (layer, 0, j)),
            pl.BlockSpec((None, k, tn), lambda j, i: (layer, 0, up_blk + j)),
        ],
        out_specs=pl.BlockSpec((tm, tn), lambda j, i: (i, j)),
        out_shape=jax.ShapeDtypeStruct((m, d_ff), BF16),
        scratch_shapes=[pltpu.VMEM((k, tn), BF16), pltpu.VMEM((k, tn), BF16)],
        compiler_params=_params(2),
        name="ffn_in",
    )(h, w_ffn_in, w_ffn_in)


def _t5_causal_bucket(dist):
    n = jnp.maximum(dist, 0)
    nf = jnp.maximum(n, 1).astype(jnp.float32)
    large = MAX_EXACT + (jnp.log(nf / MAX_EXACT) / math.log(MAX_DISTANCE / MAX_EXACT)
                         * (NUM_BUCKETS - MAX_EXACT)).astype(jnp.int32)
    large = jnp.minimum(large, NUM_BUCKETS - 1)
    return jnp.where(n < MAX_EXACT, n, large)


def _band_buckets():
    q_off = jnp.arange(BLOCK)
    k_off = jnp.arange(2 * BLOCK)
    dist = q_off[:, None] + BLOCK - k_off[None, :]
    allowed = (dist >= 0) & (dist < WINDOW)
    return jnp.where(allowed, _t5_causal_bucket(dist), -1).astype(jnp.int32)


def _attn_body(rb_ref, sink_ref, bucket_ref, qmask_ref, krow_ref, q_ref, kc_ref, kp_ref, vc_ref, vp_ref,
               o_ref, bias_ref, lg_ref, p_ref, pv_ref, rs_ref, *, n_q, n_kv, nsub):
    q_per_kv = n_q // n_kv
    kv_w = n_kv * HEAD_DIM
    first_step = jnp.logical_and(pl.program_id(0) == 0, pl.program_id(1) == 0)

    @pl.when(first_step)
    def _():
        bucket = bucket_ref[...]
        col = lax.broadcasted_iota(jnp.int32, bucket.shape, 1)
        for slot in range(n_q):
            head = _orig_head(slot, n_kv, q_per_kv)
            tile = jnp.zeros(bucket.shape, F32)
            for b in range(NUM_BUCKETS):
                tile = jnp.where(bucket == b, rb_ref[b * n_q + head] * LOG2E, tile)
            tile = jnp.where(bucket < 0, -jnp.inf, tile)
            sink = sink_ref[head] * LOG2E
            bias_ref[0, slot] = jnp.where(col == 0, sink, tile)
            bias_ref[1, slot] = jnp.where(col == 0, sink, jnp.where(col < BLOCK, -jnp.inf, tile))

    lane_slot = lax.broadcasted_iota(jnp.int32, (BLOCK, BLOCK), 1) // HEAD_DIM
    ones = jnp.ones((2 * BLOCK, BLOCK), BF16)
    first_block = jnp.where(pl.program_id(1) == 0, 1, 0)

    for sub in range(nsub):
        rows = slice(sub * BLOCK, (sub + 1) * BLOCK)
        if sub == 0:
            keys = jnp.concatenate([kp_ref[...], kc_ref[:BLOCK, :]], axis=0)
            vals = jnp.concatenate([vp_ref[...], vc_ref[:BLOCK, :]], axis=0)
        else:
            keys = kc_ref[(sub - 1) * BLOCK:(sub + 1) * BLOCK, :]
            vals = vc_ref[(sub - 1) * BLOCK:(sub + 1) * BLOCK, :]
        keys = keys * krow_ref[...]
        vals = vals * krow_ref[...]
        lhs = jnp.concatenate(
            [q_ref[rows, j * kv_w:(j + 1) * kv_w] * qmask_ref[g]
             for j in range(q_per_kv) for g in range(n_kv)], axis=0)
        lg_ref[sub] = lax.dot_general(lhs, keys, (((1,), (1,)), ((), ())),
                                      preferred_element_type=F32).reshape(n_q, BLOCK, 2 * BLOCK)
        for slot in range(n_q):
            bias = bias_ref[first_block, slot] if sub == 0 else bias_ref[0, slot]
            logit = lg_ref[sub, slot] + bias
            m = jnp.max(logit, axis=-1, keepdims=True)
            p_ref[sub, slot * BLOCK:(slot + 1) * BLOCK, :] = jnp.exp2(logit - m).astype(BF16)
        p = p_ref[sub]
        pv_ref[sub] = jnp.dot(p, vals, preferred_element_type=F32).reshape(n_q, BLOCK, kv_w)
        rs_ref[sub] = jnp.dot(p, ones, preferred_element_type=F32).reshape(n_q, BLOCK, BLOCK)
        per_half = BLOCK // HEAD_DIM
        for j in range(q_per_kv):
            for half in range(kv_w // BLOCK):
                lanes = slice(half * BLOCK, (half + 1) * BLOCK)
                slots = [j * n_kv + half * per_half + i for i in range(per_half)]
                num = pv_ref[sub, slots[-1], :, lanes]
                den = rs_ref[sub, slots[-1]]
                for i in range(per_half - 2, -1, -1):
                    num = jnp.where(lane_slot == i, pv_ref[sub, slots[i], :, lanes], num)
                    den = jnp.where(lane_slot == i, rs_ref[sub, slots[i]], den)
                o_ref[rows, j * kv_w + half * BLOCK:j * kv_w + (half + 1) * BLOCK] = (
                    num * (1.0 / den)).astype(o_ref.dtype)


def _attention(qkv, rel_bias, sinks, b_sz, s_len, n_q, n_kv, nsub=2):
    tq = nsub * BLOCK
    nt = s_len // tq
    attn_w, kv_w = n_q * HEAD_DIM, n_kv * HEAD_DIM
    assert attn_w % kv_w == 0 and kv_w == 2 * BLOCK and BLOCK % HEAD_DIM == 0
    k_blk = attn_w // kv_w
    v_blk = k_blk + 1
    cur = lambda b, n: b * nt + n
    prev = lambda b, n: jnp.maximum((b * nt + n) * nsub - 1, 0)
    lane_group = jnp.arange(kv_w) // HEAD_DIM
    qmask = jnp.broadcast_to((lane_group[None, :] == jnp.arange(n_kv)[:, None])[:, None, :],
                             (n_kv, BLOCK, kv_w)).astype(BF16)
    krow = jnp.broadcast_to((jnp.arange(2 * BLOCK) > 0)[:, None], (2 * BLOCK, kv_w)).astype(BF16)
    body = functools.partial(_attn_body, n_q=n_q, n_kv=n_kv, nsub=nsub)
    const2 = lambda b, n: (0, 0)
    return pl.pallas_call(
        body,
        grid=(b_sz, nt),
        in_specs=[
            pl.BlockSpec(memory_space=pltpu.SMEM),
            pl.BlockSpec(memory_space=pltpu.SMEM),
            pl.BlockSpec((BLOCK, 2 * BLOCK), const2),
            pl.BlockSpec((n_kv, BLOCK, kv_w), lambda b, n: (0, 0, 0)),
            pl.BlockSpec((2 * BLOCK, kv_w), const2),
            pl.BlockSpec((tq, attn_w), lambda b, n: (cur(b, n), 0)),
            pl.BlockSpec((tq, kv_w), lambda b, n: (cur(b, n), k_blk)),
            pl.BlockSpec((BLOCK, kv_w), lambda b, n: (prev(b, n), k_blk)),
            pl.BlockSpec((tq, kv_w), lambda b, n: (cur(b, n), v_blk)),
            pl.BlockSpec((BLOCK, kv_w), lambda b, n: (prev(b, n), v_blk)),
        ],
        out_specs=pl.BlockSpec((tq, attn_w), lambda b, n: (cur(b, n), 0)),
        out_shape=jax.ShapeDtypeStruct((b_sz * s_len, attn_w), BF16),
        scratch_shapes=[
            pltpu.VMEM((2, n_q, BLOCK, 2 * BLOCK), F32),
            pltpu.VMEM((nsub, n_q, BLOCK, 2 * BLOCK), F32),
            pltpu.VMEM((nsub, n_q * BLOCK, 2 * BLOCK), BF16),
            pltpu.VMEM((nsub, n_q, BLOCK, kv_w), F32),
            pltpu.VMEM((nsub, n_q, BLOCK, BLOCK), F32),
        ],
        compiler_params=_params(2),
        name="swa_attention",
    )(rel_bias.astype(F32).reshape(-1), sinks.astype(F32), _band_buckets(), qmask, krow,
      qkv, qkv, qkv, qkv, qkv)


_CONV_HALO = 32
_CONV_LANES = 256


def _conv_body(ac_ref, gc_ref, ap_ref, gp_ref, w_ref, cb_ref, lg_ref, lb_ref, o_ref, u_ref, y_ref,
               *, width, ts):
    channels = u_ref.shape[1]
    u_ref[_CONV_HALO:, :] = ac_ref[...].astype(F32) * _sigmoid(gc_ref[...].astype(F32))
    u_prev = ap_ref[...].astype(F32) * _sigmoid(gp_ref[...].astype(F32))
    u_ref[:_CONV_HALO, :] = jnp.where(pl.program_id(1) == 0, 0.0, u_prev)
    first = _CONV_HALO - (width - 1)
    for c0 in range(0, channels, _CONV_LANES):
        cs = slice(c0, c0 + _CONV_LANES)
        acc = jnp.broadcast_to(cb_ref[:, cs], (ts, _CONV_LANES))
        for j in range(width):
            acc = acc + u_ref[first + j:first + j + ts, cs] * w_ref[j:j + 1, cs]
        y_ref[:, cs] = acc
    y = y_ref[...]
    mu = jnp.mean(y, axis=-1, keepdims=True)
    yc = y - mu
    var = jnp.mean(yc * yc, axis=-1, keepdims=True)
    z = yc * lax.rsqrt(var + EPS) * lg_ref[...] + lb_ref[...]
    o_ref[...] = (z * _sigmoid(z)).astype(o_ref.dtype)


def _conformer_conv(proj, a_col, g_col, conv_w, conv_b, ln_g, ln_b, b_sz, s_len, ts=128):
    width, channels = conv_w.shape
    assert a_col % channels == 0 and g_col % channels == 0
    a_blk, g_blk = a_col // channels, g_col // channels
    tiles = s_len // ts
    halo_per_tile = ts // _CONV_HALO
    cur = lambda b, s: b * tiles + s
    prev = lambda b, s: jnp.maximum((b * tiles + s) * halo_per_tile - 1, 0)
    row = lambda v: v.reshape(1, channels)
    body = functools.partial(_conv_body, width=width, ts=ts)
    return pl.pallas_call(
        body,
        grid=(b_sz, tiles),
        in_specs=[
            pl.BlockSpec((ts, channels), lambda b, s: (cur(b, s), a_blk)),
            pl.BlockSpec((ts, channels), lambda b, s: (cur(b, s), g_blk)),
            pl.BlockSpec((_CONV_HALO, channels), lambda b, s: (prev(b, s), a_blk)),
            pl.BlockSpec((_CONV_HALO, channels), lambda b, s: (prev(b, s), g_blk)),
            pl.BlockSpec((width, channels), lambda b, s: (0, 0)),
            pl.BlockSpec((1, channels), lambda b, s: (0, 0)),
            pl.BlockSpec((1, channels), lambda b, s: (0, 0)),
            pl.BlockSpec((1, channels), lambda b, s: (0, 0)),
        ],
        out_specs=pl.BlockSpec((ts, channels), lambda b, s: (cur(b, s), 0)),
        out_shape=jax.ShapeDtypeStruct((b_sz * s_len, channels), BF16),
        scratch_shapes=[pltpu.VMEM((_CONV_HALO + ts, channels), F32), pltpu.VMEM((ts, channels), F32)],
        compiler_params=_params(2),
        name="conformer_conv",
    )(proj, proj, proj, proj, conv_w, row(conv_b), row(ln_g), row(ln_b))


def kernel(x, c, w_ada, b_ada, norm_mix_g, w_in, q_norm_g, k_norm_g, attn_sinks, rel_bias,
           w_attn_out, conv_w, conv_b, conv_ln_g, conv_ln_b, w_conv_out, w_mix_out,
           norm_ffn_g, w_ffn_in, w_ffn_out):
    b_sz, s_len, d = x.shape
    depth = w_ada.shape[0]
    n_q = attn_sinks.shape[1]
    attn_w = w_attn_out.shape[1]
    channels = conv_w.shape[2]
    kv_w = (w_in.shape[2] - attn_w - 2 * channels - 2 * d) // 2
    n_kv = kv_w // HEAD_DIM
    qkv_w = attn_w + 2 * kv_w
    a_col = 0
    g_col = a_col + channels
    ga_col = g_col + channels
    gc_col = ga_col + d
    m = b_sz * s_len

    for l in range(depth):
        mod = _ada(c, w_ada, b_ada[l], l)
        mod3 = mod.reshape(b_sz * N_MOD, 1, d)

        h = _norm_mod(x, norm_mix_g[l], mod3, scale_idx=1, shift_idx=0).reshape(m, d)
        qkv = _qkv_proj(h, w_in, l, q_norm_g[l], k_norm_g[l], n_q, n_kv)
        rest = _proj(h, w_in, l, qkv_w, tm=1024, tn=qkv_w)
        y_attn = _attention(qkv, rel_bias, attn_sinks[l], b_sz, s_len, n_q, n_kv)
        y_conv = _conformer_conv(rest, a_col, g_col, conv_w[l], conv_b[l], conv_ln_g[l],
                                 conv_ln_b[l], b_sz, s_len)
        merged = _merge(y_attn, y_conv, w_attn_out, w_conv_out, l, rest, ga_col, gc_col, n_q, n_kv)
        x1 = _residual_proj(merged, w_mix_out, l, x.reshape(m, d), mod3, 2, s_len, tm=1024, tn=512)

        h = _norm_mod(x1.reshape(b_sz, s_len, d), norm_ffn_g[l], mod3, scale_idx=4, shift_idx=3)
        act = _ffn_in(h.reshape(m, d), w_ffn_in, l)
        x = _residual_proj(act, w_ffn_out, l, x1, mod3, 5, s_len, tm=512, tn=512).reshape(b_sz, s_len, d)
    return x
```

```python
import functools
import math

import jax
import jax.numpy as jnp
from jax import lax
from jax.experimental import pallas as pl
from jax.experimental.pallas import tpu as pltpu

F32 = jnp.float32
BF16 = jnp.bfloat16

HEAD_DIM = 64
WINDOW = 128
BLOCK = 128
NUM_BUCKETS = 32
MAX_EXACT = NUM_BUCKETS // 2
MAX_DISTANCE = 128
N_MOD = 6
EPS = 1e-6
LOG2E = math.log2(math.e)

V7X_VMEM_LIMIT_BYTES = 56 * 1024 * 1024


def _params(n_axes):
    return pltpu.CompilerParams(
        dimension_semantics=("arbitrary",) * n_axes,
        vmem_limit_bytes=V7X_VMEM_LIMIT_BYTES,
    )


def _sigmoid(x):
    return jax.nn.sigmoid(x)


def _orig_head(slot, n_kv, q_per_kv):
    return (slot % n_kv) * q_per_kv + slot // n_kv


def _ada_body(c_ref, w_ref, b_ref, o_ref):
    c = c_ref[...]
    act = (c * _sigmoid(c)).astype(BF16)
    o_ref[...] = jnp.dot(act, w_ref[...].astype(BF16), preferred_element_type=F32) + b_ref[...]


def _ada(c, w_ada, b_ada, layer, tn=1024):
    b_sz, d = c.shape
    n = w_ada.shape[2]
    rows = 8
    c_pad = jnp.pad(c, ((0, rows - b_sz), (0, 0)))
    out = pl.pallas_call(
        _ada_body,
        grid=(n // tn,),
        in_specs=[
            pl.BlockSpec((rows, d), lambda j: (0, 0)),
            pl.BlockSpec((None, d, tn), lambda j: (layer, 0, j)),
            pl.BlockSpec((1, tn), lambda j: (0, j)),
        ],
        out_specs=pl.BlockSpec((rows, tn), lambda j: (0, j)),
        out_shape=jax.ShapeDtypeStruct((rows, n), F32),
        compiler_params=_params(1),
        name="ada_mod",
    )(c_pad, w_ada, b_ada.reshape(1, n))
    return out[:b_sz]


def _norm_mod_body(x_ref, g_ref, sc_ref, sh_ref, o_ref):
    x = x_ref[0]
    ms = jnp.mean(x * x, axis=-1, keepdims=True)
    y = x * lax.rsqrt(ms + EPS) * g_ref[...]
    o_ref[0] = (y * (1.0 + sc_ref[0]) + sh_ref[0]).astype(o_ref.dtype)


def _norm_mod(x, g, mod3, scale_idx, shift_idx, ts=512):
    b_sz, s_len, d = x.shape
    return pl.pallas_call(
        _norm_mod_body,
        grid=(b_sz, s_len // ts),
        in_specs=[
            pl.BlockSpec((1, ts, d), lambda b, s: (b, s, 0)),
            pl.BlockSpec((1, d), lambda b, s: (0, 0)),
            pl.BlockSpec((1, 1, d), lambda b, s: (b * N_MOD + scale_idx, 0, 0)),
            pl.BlockSpec((1, 1, d), lambda b, s: (b * N_MOD + shift_idx, 0, 0)),
        ],
        out_specs=pl.BlockSpec((1, ts, d), lambda b, s: (b, s, 0)),
        out_shape=jax.ShapeDtypeStruct((b_sz, s_len, d), BF16),
        compiler_params=_params(2),
        name="norm_mod",
    )(x, g.reshape(1, d), mod3, mod3)


def _qkv_body(a_ref, w_ref, gq_ref, gk_ref, o_ref, wbf_ref, seg_ref, *, n_q, n_kv, half):
    attn_w = n_q * HEAD_DIM
    kv_w = n_kv * HEAD_DIM

    @pl.when(pl.program_id(0) == 0)
    def _():
        for slot in range(n_q):
            src = _orig_head(slot, n_kv, n_q // n_kv) * HEAD_DIM
            wbf_ref[:, slot * HEAD_DIM:(slot + 1) * HEAD_DIM] = w_ref[:, src:src + HEAD_DIM].astype(BF16)
        wbf_ref[:, attn_w:] = w_ref[:, attn_w:].astype(BF16)
        r = lax.broadcasted_iota(jnp.int32, (half, half), 0) // HEAD_DIM
        c = lax.broadcasted_iota(jnp.int32, (half, half), 1) // HEAD_DIM
        seg_ref[...] = (r == c).astype(BF16)

    a = a_ref[...]

    def head_norm(acc, seg, gain):
        ss = jnp.dot((acc * acc).astype(BF16), seg, preferred_element_type=F32)
        return acc * lax.rsqrt(ss / HEAD_DIM + EPS) * gain

    for c0 in range(0, attn_w, half):
        acc = jnp.dot(a, wbf_ref[:, c0:c0 + half], preferred_element_type=F32)
        o_ref[:, c0:c0 + half] = head_norm(acc, seg_ref[...], gq_ref[...]).astype(o_ref.dtype)
    acc = jnp.dot(a, wbf_ref[:, attn_w:attn_w + kv_w], preferred_element_type=F32)
    o_ref[:, attn_w:attn_w + kv_w] = head_norm(acc, seg_ref[:kv_w, :kv_w], gk_ref[...]).astype(o_ref.dtype)
    acc = jnp.dot(a, wbf_ref[:, attn_w + kv_w:], preferred_element_type=F32)
    o_ref[:, attn_w + kv_w:] = acc.astype(o_ref.dtype)


def _qkv_proj(h, w_in, layer, q_g, k_g, n_q, n_kv, tm=1024, half=512):
    m, k = h.shape
    attn_w, kv_w = n_q * HEAD_DIM, n_kv * HEAD_DIM
    width = attn_w + 2 * kv_w
    assert attn_w % half == 0 and kv_w <= half and half % HEAD_DIM == 0
    gq = jnp.tile(q_g * (HEAD_DIM ** -0.5 * LOG2E), half // HEAD_DIM).reshape(1, half)
    gk = jnp.tile(k_g, n_kv).reshape(1, kv_w)
    body = functools.partial(_qkv_body, n_q=n_q, n_kv=n_kv, half=half)
    return pl.pallas_call(
        body,
        grid=(m // tm,),
        in_specs=[
            pl.BlockSpec((tm, k), lambda i: (i, 0)),
            pl.BlockSpec((None, k, width), lambda i: (layer, 0, 0), pipeline_mode=pl.Buffered(1)),
            pl.BlockSpec((1, half), lambda i: (0, 0)),
            pl.BlockSpec((1, kv_w), lambda i: (0, 0)),
        ],
        out_specs=pl.BlockSpec((tm, width), lambda i: (i, 0)),
        out_shape=jax.ShapeDtypeStruct((m, width), BF16),
        scratch_shapes=[pltpu.VMEM((k, width), BF16), pltpu.VMEM((half, half), BF16)],
        compiler_params=_params(1),
        name="qkv_proj",
    )(h, w_in, gq, gk)


def _cast_weights_once(w_refs, wbf_refs):
    @pl.when(pl.program_id(1) == 0)
    def _():
        for w_ref, wbf_ref in zip(w_refs, wbf_refs):
            wbf_ref[...] = w_ref[...].astype(BF16)


def _proj_body(a_ref, w_ref, o_ref, wbf_ref):
    _cast_weights_once([w_ref], [wbf_ref])
    acc = jnp.dot(a_ref[...], wbf_ref[...], preferred_element_type=F32)
    o_ref[...] = acc.astype(o_ref.dtype)


def _proj(h, w, layer, col0, tm, tn):
    m, k = h.shape
    n = w.shape[2] - col0
    assert col0 % tn == 0 and n % tn == 0
    blk0 = col0 // tn
    return pl.pallas_call(
        _proj_body,
        grid=(n // tn, m // tm),
        in_specs=[
            pl.BlockSpec((tm, k), lambda j, i: (i, 0)),
            pl.BlockSpec((None, k, tn), lambda j, i: (layer, 0, blk0 + j)),
        ],
        out_specs=pl.BlockSpec((tm, tn), lambda j, i: (i, j)),
        out_shape=jax.ShapeDtypeStruct((m, n), BF16),
        scratch_shapes=[pltpu.VMEM((k, tn), BF16)],
        compiler_params=_params(2),
        name="proj",
    )(h, w)


def _merge_body(ya_ref, yc_ref, wa_ref, wc_ref, ga_ref, gc_ref, o_ref, wabf_ref, wcbf_ref,
                *, n_q, n_kv):
    @pl.when(pl.program_id(1) == 0)
    def _():
        for slot in range(n_q):
            src = _orig_head(slot, n_kv, n_q // n_kv) * HEAD_DIM
            wabf_ref[slot * HEAD_DIM:(slot + 1) * HEAD_DIM, :] = wa_ref[src:src + HEAD_DIM, :].astype(BF16)
        wcbf_ref[...] = wc_ref[...].astype(BF16)

    acc_a = jnp.dot(ya_ref[...], wabf_ref[...], preferred_element_type=F32)
    acc_c = jnp.dot(yc_ref[...], wcbf_ref[...], preferred_element_type=F32)
    ga = _sigmoid(ga_ref[...].astype(F32))
    gc = _sigmoid(gc_ref[...].astype(F32))
    o_ref[...] = (ga * acc_a + gc * acc_c).astype(o_ref.dtype)


def _merge(attn, conv, w_attn_out, w_conv_out, layer, gates, ga_col, gc_col, n_q, n_kv,
           tm=1024, tn=512):
    m, ka = attn.shape
    kc = conv.shape[1]
    n = w_attn_out.shape[2]
    assert ga_col % tn == 0 and gc_col % tn == 0
    ga_blk, gc_blk = ga_col // tn, gc_col // tn
    body = functools.partial(_merge_body, n_q=n_q, n_kv=n_kv)
    return pl.pallas_call(
        body,
        grid=(n // tn, m // tm),
        in_specs=[
            pl.BlockSpec((tm, ka), lambda j, i: (i, 0)),
            pl.BlockSpec((tm, kc), lambda j, i: (i, 0)),
            pl.BlockSpec((None, ka, tn), lambda j, i: (layer, 0, j)),
            pl.BlockSpec((None, kc, tn), lambda j, i: (layer, 0, j)),
            pl.BlockSpec((tm, tn), lambda j, i: (i, ga_blk + j)),
            pl.BlockSpec((tm, tn), lambda j, i: (i, gc_blk + j)),
        ],
        out_specs=pl.BlockSpec((tm, tn), lambda j, i: (i, j)),
        out_shape=jax.ShapeDtypeStruct((m, n), BF16),
        scratch_shapes=[pltpu.VMEM((ka, tn), BF16), pltpu.VMEM((kc, tn), BF16)],
        compiler_params=_params(2),
        name="merge",
    )(attn, conv, w_attn_out, w_conv_out, gates, gates)


def _residual_proj_body(a_ref, w_ref, x_ref, gt_ref, o_ref, wbf_ref):
    _cast_weights_once([w_ref], [wbf_ref])
    acc = jnp.dot(a_ref[...], wbf_ref[...], preferred_element_type=F32)
    o_ref[...] = x_ref[...] + gt_ref[0] * acc


def _residual_proj(a, w, layer, x2d, mod3, gate_idx, rows_per_batch, tm, tn):
    m, k = a.shape
    n = w.shape[2]
    tiles_per_batch = rows_per_batch // tm
    return pl.pallas_call(
        _residual_proj_body,
        grid=(n // tn, m // tm),
        in_specs=[
            pl.BlockSpec((tm, k), lambda j, i: (i, 0)),
            pl.BlockSpec((None, k, tn), lambda j, i: (layer, 0, j)),
            pl.BlockSpec((tm, tn), lambda j, i: (i, j)),
            pl.BlockSpec((1, 1, tn), lambda j, i: ((i // tiles_per_batch) * N_MOD + gate_idx, 0, j)),
        ],
        out_specs=pl.BlockSpec((tm, tn), lambda j, i: (i, j)),
        out_shape=jax.ShapeDtypeStruct((m, n), F32),
        scratch_shapes=[pltpu.VMEM((k, tn), BF16)],
        compiler_params=_params(2),
        name="residual_proj",
    )(a, w, x2d, mod3)


def _ffn_in_body(a_ref, wg_ref, wu_ref, o_ref, wgbf_ref, wubf_ref):
    _cast_weights_once([wg_ref, wu_ref], [wgbf_ref, wubf_ref])
    a = a_ref[...]
    gate = jnp.dot(a, wgbf_ref[...], preferred_element_type=F32)
    up = jnp.dot(a, wubf_ref[...], preferred_element_type=F32)
    o_ref[...] = (gate * _sigmoid(gate) * up).astype(o_ref.dtype)


def _ffn_in(h, w_ffn_in, layer, tm=1024, tn=512):
    m, k = h.shape
    d_ff = w_ffn_in.shape[2] // 2
    up_blk = d_ff // tn
    return pl.pallas_call(
        _ffn_in_body,
        grid=(d_ff // tn, m // tm),
        in_specs=[
            pl.BlockSpec((tm, k), lambda j, i: (i, 0)),
            pl.BlockSpec((None, k, tn), lambda j, i: (layer, 0, j)),
            pl.BlockSpec((None, k, tn), lambda j, i: (layer, 0, up_blk + j)),
        ],
        out_specs=pl.BlockSpec((tm, tn), lambda j, i: (i, j)),
        out_shape=jax.ShapeDtypeStruct((m, d_ff), BF16),
        scratch_shapes=[pltpu.VMEM((k, tn), BF16), pltpu.VMEM((k, tn), BF16)],
        compiler_params=_params(2),
        name="ffn_in",
    )(h, w_ffn_in, w_ffn_in)


def _t5_causal_bucket(dist):
    n = jnp.maximum(dist, 0)
    nf = jnp.maximum(n, 1).astype(jnp.float32)
    large = MAX_EXACT + (jnp.log(nf / MAX_EXACT) / math.log(MAX_DISTANCE / MAX_EXACT)
                         * (NUM_BUCKETS - MAX_EXACT)).astype(jnp.int32)
    large = jnp.minimum(large, NUM_BUCKETS - 1)
    return jnp.where(n < MAX_EXACT, n, large)


def _band_buckets():
    q_off = jnp.arange(BLOCK)
    k_off = jnp.arange(2 * BLOCK)
    dist = q_off[:, None] + BLOCK - k_off[None, :]
    allowed = (dist >= 0) & (dist < WINDOW)
    return jnp.where(allowed, _t5_causal_bucket(dist), -1).astype(jnp.int32)


def _attn_body(rb_ref, sink_ref, bucket_ref, qmask_ref, krow_ref, q_ref, kc_ref, kp_ref, vc_ref, vp_ref,
               o_ref, bias_ref, lg_ref, p_ref, pv_ref, rs_ref, *, n_q, n_kv, nsub):
    q_per_kv = n_q // n_kv
    kv_w = n_kv * HEAD_DIM
    first_step = jnp.logical_and(pl.program_id(0) == 0, pl.program_id(1) == 0)

    @pl.when(first_step)
    def _():
        bucket = bucket_ref[...]
        col = lax.broadcasted_iota(jnp.int32, bucket.shape, 1)
        for slot in range(n_q):
            head = _orig_head(slot, n_kv, q_per_kv)
            tile = jnp.zeros(bucket.shape, F32)
            for b in range(NUM_BUCKETS):
                tile = jnp.where(bucket == b, rb_ref[b * n_q + head] * LOG2E, tile)
            tile = jnp.where(bucket < 0, -jnp.inf, tile)
            sink = sink_ref[head] * LOG2E
            bias_ref[0, slot] = jnp.where(col == 0, sink, tile)
            bias_ref[1, slot] = jnp.where(col == 0, sink, jnp.where(col < BLOCK, -jnp.inf, tile))

    lane_slot = lax.broadcasted_iota(jnp.int32, (BLOCK, BLOCK), 1) // HEAD_DIM
    ones = jnp.ones((2 * BLOCK, BLOCK), BF16)
    first_block = jnp.where(pl.program_id(1) == 0, 1, 0)

    for sub in range(nsub):
        rows = slice(sub * BLOCK, (sub + 1) * BLOCK)
        if sub == 0:
            keys = jnp.concatenate([kp_ref[...], kc_ref[:BLOCK, :]], axis=0)
            vals = jnp.concatenate([vp_ref[...], vc_ref[:BLOCK, :]], axis=0)
        else:
            keys = kc_ref[(sub - 1) * BLOCK:(sub + 1) * BLOCK, :]
            vals = vc_ref[(sub - 1) * BLOCK:(sub + 1) * BLOCK, :]
        keys = keys * krow_ref[...]
        vals = vals * krow_ref[...]
        lhs = jnp.concatenate(
            [q_ref[rows, j * kv_w:(j + 1) * kv_w] * qmask_ref[g]
             for j in range(q_per_kv) for g in range(n_kv)], axis=0)
        lg_ref[sub] = lax.dot_general(lhs, keys, (((1,), (1,)), ((), ())),
                                      preferred_element_type=F32).reshape(n_q, BLOCK, 2 * BLOCK)
        for slot in range(n_q):
            bias = bias_ref[first_block, slot] if sub == 0 else bias_ref[0, slot]
            logit = lg_ref[sub, slot] + bias
            m = jnp.max(logit, axis=-1, keepdims=True)
            p_ref[sub, slot * BLOCK:(slot + 1) * BLOCK, :] = jnp.exp2(logit - m).astype(BF16)
        p = p_ref[sub]
        pv_ref[sub] = jnp.dot(p, vals, preferred_element_type=F32).reshape(n_q, BLOCK, kv_w)
        rs_ref[sub] = jnp.dot(p, ones, preferred_element_type=F32).reshape(n_q, BLOCK, BLOCK)
        per_half = BLOCK // HEAD_DIM
        for j in range(q_per_kv):
            for half in range(kv_w // BLOCK):
                lanes = slice(half * BLOCK, (half + 1) * BLOCK)
                slots = [j * n_kv + half * per_half + i for i in range(per_half)]
                num = pv_ref[sub, slots[-1], :, lanes]
                den = rs_ref[sub, slots[-1]]
                for i in range(per_half - 2, -1, -1):
                    num = jnp.where(lane_slot == i, pv_ref[sub, slots[i], :, lanes], num)
                    den = jnp.where(lane_slot == i, rs_ref[sub, slots[i]], den)
                o_ref[rows, j * kv_w + half * BLOCK:j * kv_w + (half + 1) * BLOCK] = (
                    num * (1.0 / den)).astype(o_ref.dtype)


def _attention(qkv, rel_bias, sinks, b_sz, s_len, n_q, n_kv, nsub=2):
    tq = nsub * BLOCK
    nt = s_len // tq
    attn_w, kv_w = n_q * HEAD_DIM, n_kv * HEAD_DIM
    assert attn_w % kv_w == 0 and kv_w == 2 * BLOCK and BLOCK % HEAD_DIM == 0
    k_blk = attn_w // kv_w
    v_blk = k_blk + 1
    cur = lambda b, n: b * nt + n
    prev = lambda b, n: jnp.maximum((b * nt + n) * nsub - 1, 0)
    lane_group = jnp.arange(kv_w) // HEAD_DIM
    qmask = jnp.broadcast_to((lane_group[None, :] == jnp.arange(n_kv)[:, None])[:, None, :],
                             (n_kv, BLOCK, kv_w)).astype(BF16)
    krow = jnp.broadcast_to((jnp.arange(2 * BLOCK) > 0)[:, None], (2 * BLOCK, kv_w)).astype(BF16)
    body = functools.partial(_attn_body, n_q=n_q, n_kv=n_kv, nsub=nsub)
    const2 = lambda b, n: (0, 0)
    return pl.pallas_call(
        body,
        grid=(b_sz, nt),
        in_specs=[
            pl.BlockSpec(memory_space=pltpu.SMEM),
            pl.BlockSpec(memory_space=pltpu.SMEM),
            pl.BlockSpec((BLOCK, 2 * BLOCK), const2),
            pl.BlockSpec((n_kv, BLOCK, kv_w), lambda b, n: (0, 0, 0)),
            pl.BlockSpec((2 * BLOCK, kv_w), const2),
            pl.BlockSpec((tq, attn_w), lambda b, n: (cur(b, n), 0)),
            pl.BlockSpec((tq, kv_w), lambda b, n: (cur(b, n), k_blk)),
            pl.BlockSpec((BLOCK, kv_w), lambda b, n: (prev(b, n), k_blk)),
            pl.BlockSpec((tq, kv_w), lambda b, n: (cur(b, n), v_blk)),
            pl.BlockSpec((BLOCK, kv_w), lambda b, n: (prev(b, n), v_blk)),
        ],
        out_specs=pl.BlockSpec((tq, attn_w), lambda b, n: (cur(b, n), 0)),
        out_shape=jax.ShapeDtypeStruct((b_sz * s_len, attn_w), BF16),
        scratch_shapes=[
            pltpu.VMEM((2, n_q, BLOCK, 2 * BLOCK), F32),
            pltpu.VMEM((nsub, n_q, BLOCK, 2 * BLOCK), F32),
            pltpu.VMEM((nsub, n_q * BLOCK, 2 * BLOCK), BF16),
            pltpu.VMEM((nsub, n_q, BLOCK, kv_w), F32),
            pltpu.VMEM((nsub, n_q, BLOCK, BLOCK), F32),
        ],
        compiler_params=_params(2),
        name="swa_attention",
    )(rel_bias.astype(F32).reshape(-1), sinks.astype(F32), _band_buckets(), qmask, krow,
      qkv, qkv, qkv, qkv, qkv)


_CONV_HALO = 32
_CONV_LANES = 256
_SUBLANES = 8


def _conv_body(a_ref, g_ref, w_ref, cb_ref, lg_ref, lb_ref, o_ref, u_ref, sh_ref, y_ref, *, width, ts):
    channels = u_ref.shape[1]

    @pl.when(pl.program_id(1) == 0)
    def _():
        u_ref[:_CONV_HALO, :] = jnp.zeros((_CONV_HALO, channels), F32)

    u_ref[_CONV_HALO:, :] = a_ref[...].astype(F32) * _sigmoid(g_ref[...].astype(F32))
    span = sh_ref.shape[1]
    for r in range(1, _SUBLANES):
        sh_ref[r - 1] = u_ref[r:r + span, :]
    first = _CONV_HALO - (width - 1)
    for c0 in range(0, channels, _CONV_LANES):
        cs = slice(c0, c0 + _CONV_LANES)
        acc = jnp.broadcast_to(cb_ref[:, cs], (ts, _CONV_LANES))
        for j in range(width):
            tile, r = divmod(first + j, _SUBLANES)
            rows = slice(tile * _SUBLANES, tile * _SUBLANES + ts)
            taps = u_ref[rows, cs] if r == 0 else sh_ref[r - 1, rows, cs]
            acc = acc + taps * w_ref[j:j + 1, cs]
        y_ref[:, cs] = acc
    y = y_ref[...]
    mu = jnp.mean(y, axis=-1, keepdims=True)
    yc = y - mu
    var = jnp.mean(yc * yc, axis=-1, keepdims=True)
    z = yc * lax.rsqrt(var + EPS) * lg_ref[...] + lb_ref[...]
    o_ref[...] = (z * _sigmoid(z)).astype(o_ref.dtype)
    u_ref[:_CONV_HALO, :] = u_ref[ts:, :]


def _conformer_conv(proj, a_col, g_col, conv_w, conv_b, ln_g, ln_b, b_sz, s_len, ts=128):
    width, channels = conv_w.shape
    assert a_col % channels == 0 and g_col % channels == 0
    assert width - 1 <= _CONV_HALO and _CONV_HALO % _SUBLANES == 0 and ts >= _CONV_HALO
    a_blk, g_blk = a_col // channels, g_col // channels
    tiles = s_len // ts
    cur = lambda b, s: b * tiles + s
    row = lambda v: v.reshape(1, channels)
    body = functools.partial(_conv_body, width=width, ts=ts)
    return pl.pallas_call(
        body,
        grid=(b_sz, tiles),
        in_specs=[
            pl.BlockSpec((ts, channels), lambda b, s: (cur(b, s), a_blk)),
            pl.BlockSpec((ts, channels), lambda b, s: (cur(b, s), g_blk)),
            pl.BlockSpec((width, channels), lambda b, s: (0, 0)),
            pl.BlockSpec((1, channels), lambda b, s: (0, 0)),
            pl.BlockSpec((1, channels), lambda b, s: (0, 0)),
            pl.BlockSpec((1, channels), lambda b, s: (0, 0)),
        ],
        out_specs=pl.BlockSpec((ts, channels), lambda b, s: (cur(b, s), 0)),
        out_shape=jax.ShapeDtypeStruct((b_sz * s_len, channels), BF16),
        scratch_shapes=[
            pltpu.VMEM((_CONV_HALO + ts, channels), F32),
            pltpu.VMEM((_SUBLANES - 1, _CONV_HALO - _SUBLANES + ts, channels), F32),
            pltpu.VMEM((ts, channels), F32),
        ],
        compiler_params=_params(2),
        name="conformer_conv",
    )(proj, proj, conv_w, row(conv_b), row(ln_g), row(ln_b))


def kernel(x, c, w_ada, b_ada, norm_mix_g, w_in, q_norm_g, k_norm_g, attn_sinks, rel_bias,
           w_attn_out, conv_w, conv_b, conv_ln_g, conv_ln_b, w_conv_out, w_mix_out,
           norm_ffn_g, w_ffn_in, w_ffn_out):
    b_sz, s_len, d = x.shape
    depth = w_ada.shape[0]
    n_q = attn_sinks.shape[1]
    attn_w = w_attn_out.shape[1]
    channels = conv_w.shape[2]
    kv_w = (w_in.shape[2] - attn_w - 2 * channels - 2 * d) // 2
    n_kv = kv_w // HEAD_DIM
    qkv_w = attn_w + 2 * kv_w
    a_col = 0
    g_col = a_col + channels
    ga_col = g_col + channels
    gc_col = ga_col + d
    m = b_sz * s_len

    for l in range(depth):
        mod = _ada(c, w_ada, b_ada[l], l)
        mod3 = mod.reshape(b_sz * N_MOD, 1, d)

        h = _norm_mod(x, norm_mix_g[l], mod3, scale_idx=1, shift_idx=0).reshape(m, d)
        qkv = _qkv_proj(h, w_in, l, q_norm_g[l], k_norm_g[l], n_q, n_kv)
        rest = _proj(h, w_in, l, qkv_w, tm=1024, tn=qkv_w)
        y_attn = _attention(qkv, rel_bias, attn_sinks[l], b_sz, s_len, n_q, n_kv)
        y_conv = _conformer_conv(rest, a_col, g_col, conv_w[l], conv_b[l], conv_ln_g[l],
                                 conv_ln_b[l], b_sz, s_len)
        merged = _merge(y_attn, y_conv, w_attn_out, w_conv_out, l, rest, ga_col, gc_col, n_q, n_kv)
        x1 = _residual_proj(merged, w_mix_out, l, x.reshape(m, d), mod3, 2, s_len, tm=1024, tn=512)

        h = _norm_mod(x1.reshape(b_sz, s_len, d), norm_ffn_g[l], mod3, scale_idx=4, shift_idx=3)
        act = _ffn_in(h.reshape(m, d), w_ffn_in, l)
        x = _residual_proj(act, w_ffn_out, l, x1, mod3, 5, s_len, tm=512, tn=512).reshape(b_sz, s_len, d)
    return x
```

```python
import functools
import math

import jax
import jax.numpy as jnp
from jax import lax
from jax.experimental import pallas as pl
from jax.experimental.pallas import tpu as pltpu

F32 = jnp.float32
BF16 = jnp.bfloat16

HEAD_DIM = 64
WINDOW = 128
BLOCK = 128
NUM_BUCKETS = 32
MAX_EXACT = NUM_BUCKETS // 2
MAX_DISTANCE = 128
N_MOD = 6
EPS = 1e-6
LOG2E = math.log2(math.e)

V7X_VMEM_LIMIT_BYTES = 56 * 1024 * 1024


def _params(n_axes):
    return pltpu.CompilerParams(
        dimension_semantics=("arbitrary",) * n_axes,
        vmem_limit_bytes=V7X_VMEM_LIMIT_BYTES,
    )


def _sigmoid(x):
    return jax.nn.sigmoid(x)


def _orig_head(slot, n_kv, q_per_kv):
    return (slot % n_kv) * q_per_kv + slot // n_kv


def _ada_body(c_ref, w_ref, b_ref, o_ref):
    c = c_ref[...]
    act = (c * _sigmoid(c)).astype(BF16)
    o_ref[...] = jnp.dot(act, w_ref[...].astype(BF16), preferred_element_type=F32) + b_ref[...]


def _ada(c, w_ada, b_ada, layer, tn=1024):
    b_sz, d = c.shape
    n = w_ada.shape[2]
    rows = 8
    c_pad = jnp.pad(c, ((0, rows - b_sz), (0, 0)))
    out = pl.pallas_call(
        _ada_body,
        grid=(n // tn,),
        in_specs=[
            pl.BlockSpec((rows, d), lambda j: (0, 0)),
            pl.BlockSpec((None, d, tn), lambda j: (layer, 0, j)),
            pl.BlockSpec((1, tn), lambda j: (0, j)),
        ],
        out_specs=pl.BlockSpec((rows, tn), lambda j: (0, j)),
        out_shape=jax.ShapeDtypeStruct((rows, n), F32),
        compiler_params=_params(1),
        name="ada_mod",
    )(c_pad, w_ada, b_ada.reshape(1, n))
    return out[:b_sz]


def _norm_mod_body(x_ref, g_ref, sc_ref, sh_ref, o_ref):
    x = x_ref[0]
    ms = jnp.mean(x * x, axis=-1, keepdims=True)
    y = x * lax.rsqrt(ms + EPS) * g_ref[...]
    o_ref[0] = (y * (1.0 + sc_ref[0]) + sh_ref[0]).astype(o_ref.dtype)


def _qkv_body(x_ref, g_ref, sc_ref, sh_ref, w_ref, gq_ref, gk_ref, o_ref, h_ref, wbf_ref, seg_ref,
              *, n_q, n_kv, half):
    attn_w = n_q * HEAD_DIM
    kv_w = n_kv * HEAD_DIM
    _norm_mod_body(x_ref, g_ref, sc_ref, sh_ref, h_ref)

    @pl.when(jnp.logical_and(pl.program_id(0) == 0, pl.program_id(1) == 0))
    def _():
        for slot in range(n_q):
            src = _orig_head(slot, n_kv, n_q // n_kv) * HEAD_DIM
            wbf_ref[:, slot * HEAD_DIM:(slot + 1) * HEAD_DIM] = w_ref[:, src:src + HEAD_DIM].astype(BF16)
        wbf_ref[:, attn_w:] = w_ref[:, attn_w:].astype(BF16)
        r = lax.broadcasted_iota(jnp.int32, (half, half), 0) // HEAD_DIM
        c = lax.broadcasted_iota(jnp.int32, (half, half), 1) // HEAD_DIM
        seg_ref[...] = (r == c).astype(BF16)

    a = h_ref[0]

    def head_norm(acc, seg, gain):
        ss = jnp.dot((acc * acc).astype(BF16), seg, preferred_element_type=F32)
        return acc * lax.rsqrt(ss / HEAD_DIM + EPS) * gain

    for c0 in range(0, attn_w, half):
        acc = jnp.dot(a, wbf_ref[:, c0:c0 + half], preferred_element_type=F32)
        o_ref[:, c0:c0 + half] = head_norm(acc, seg_ref[...], gq_ref[...]).astype(o_ref.dtype)
    acc = jnp.dot(a, wbf_ref[:, attn_w:attn_w + kv_w], preferred_element_type=F32)
    o_ref[:, attn_w:attn_w + kv_w] = head_norm(acc, seg_ref[:kv_w, :kv_w], gk_ref[...]).astype(o_ref.dtype)
    acc = jnp.dot(a, wbf_ref[:, attn_w + kv_w:], preferred_element_type=F32)
    o_ref[:, attn_w + kv_w:] = acc.astype(o_ref.dtype)


def _qkv_proj(x, norm_g, mod3, scale_idx, shift_idx, w_in, layer, q_g, k_g, n_q, n_kv, ts=512, half=512):
    b_sz, s_len, d = x.shape
    attn_w, kv_w = n_q * HEAD_DIM, n_kv * HEAD_DIM
    width = attn_w + 2 * kv_w
    assert attn_w % half == 0 and kv_w <= half and half % HEAD_DIM == 0
    gq = jnp.tile(q_g * (HEAD_DIM ** -0.5 * LOG2E), half // HEAD_DIM).reshape(1, half)
    gk = jnp.tile(k_g, n_kv).reshape(1, kv_w)
    tiles = s_len // ts
    const2 = lambda b, s: (0, 0)
    body = functools.partial(_qkv_body, n_q=n_q, n_kv=n_kv, half=half)
    return pl.pallas_call(
        body,
        grid=(b_sz, tiles),
        in_specs=[
            pl.BlockSpec((1, ts, d), lambda b, s: (b, s, 0)),
            pl.BlockSpec((1, d), const2),
            pl.BlockSpec((1, 1, d), lambda b, s: (b * N_MOD + scale_idx, 0, 0)),
            pl.BlockSpec((1, 1, d), lambda b, s: (b * N_MOD + shift_idx, 0, 0)),
            pl.BlockSpec((None, d, width), lambda b, s: (layer, 0, 0), pipeline_mode=pl.Buffered(1)),
            pl.BlockSpec((1, half), const2),
            pl.BlockSpec((1, kv_w), const2),
        ],
        out_specs=[
            pl.BlockSpec((ts, width), lambda b, s: (b * tiles + s, 0)),
            pl.BlockSpec((1, ts, d), lambda b, s: (b, s, 0)),
        ],
        out_shape=[
            jax.ShapeDtypeStruct((b_sz * s_len, width), BF16),
            jax.ShapeDtypeStruct((b_sz, s_len, d), BF16),
        ],
        scratch_shapes=[pltpu.VMEM((d, width), BF16), pltpu.VMEM((half, half), BF16)],
        compiler_params=_params(2),
        name="qkv_proj",
    )(x, norm_g.reshape(1, d), mod3, mod3, w_in, gq, gk)


def _cast_weights_once(w_refs, wbf_refs):
    @pl.when(pl.program_id(1) == 0)
    def _():
        for w_ref, wbf_ref in zip(w_refs, wbf_refs):
            wbf_ref[...] = w_ref[...].astype(BF16)


def _proj_body(a_ref, w_ref, o_ref, wbf_ref):
    _cast_weights_once([w_ref], [wbf_ref])
    acc = jnp.dot(a_ref[...], wbf_ref[...], preferred_element_type=F32)
    o_ref[...] = acc.astype(o_ref.dtype)


def _proj(h, w, layer, col0, tm, tn):
    m, k = h.shape
    n = w.shape[2] - col0
    assert col0 % tn == 0 and n % tn == 0
    blk0 = col0 // tn
    return pl.pallas_call(
        _proj_body,
        grid=(n // tn, m // tm),
        in_specs=[
            pl.BlockSpec((tm, k), lambda j, i: (i, 0)),
            pl.BlockSpec((None, k, tn), lambda j, i: (layer, 0, blk0 + j)),
        ],
        out_specs=pl.BlockSpec((tm, tn), lambda j, i: (i, j)),
        out_shape=jax.ShapeDtypeStruct((m, n), BF16),
        scratch_shapes=[pltpu.VMEM((k, tn), BF16)],
        compiler_params=_params(2),
        name="proj",
    )(h, w)


_EPILOGUE_LANES = 512


def _merge_body(ya_ref, yc_ref, wa_ref, wc_ref, ga_ref, gc_ref, o_ref, wabf_ref, wcbf_ref,
                *, n_q, n_kv):
    @pl.when(pl.program_id(0) == 0)
    def _():
        for slot in range(n_q):
            src = _orig_head(slot, n_kv, n_q // n_kv) * HEAD_DIM
            wabf_ref[slot * HEAD_DIM:(slot + 1) * HEAD_DIM, :] = wa_ref[src:src + HEAD_DIM, :].astype(BF16)
        wcbf_ref[...] = wc_ref[...].astype(BF16)

    ya = ya_ref[...]
    yc = yc_ref[...]
    for n0 in range(0, o_ref.shape[1], _EPILOGUE_LANES):
        ns = slice(n0, n0 + _EPILOGUE_LANES)
        acc_a = jnp.dot(ya, wabf_ref[:, ns], preferred_element_type=F32)
        acc_c = jnp.dot(yc, wcbf_ref[:, ns], preferred_element_type=F32)
        ga = _sigmoid(ga_ref[:, ns].astype(F32))
        gc = _sigmoid(gc_ref[:, ns].astype(F32))
        o_ref[:, ns] = (ga * acc_a + gc * acc_c).astype(o_ref.dtype)


def _merge(attn, conv, w_attn_out, w_conv_out, layer, gates, ga_col, gc_col, n_q, n_kv, tm=256):
    m, ka = attn.shape
    kc = conv.shape[1]
    n = w_attn_out.shape[2]
    assert ga_col % n == 0 and gc_col % n == 0 and n % _EPILOGUE_LANES == 0
    ga_blk, gc_blk = ga_col // n, gc_col // n
    body = functools.partial(_merge_body, n_q=n_q, n_kv=n_kv)
    resident = pl.Buffered(1)
    return pl.pallas_call(
        body,
        grid=(m // tm,),
        in_specs=[
            pl.BlockSpec((tm, ka), lambda i: (i, 0)),
            pl.BlockSpec((tm, kc), lambda i: (i, 0)),
            pl.BlockSpec((None, ka, n), lambda i: (layer, 0, 0), pipeline_mode=resident),
            pl.BlockSpec((None, kc, n), lambda i: (layer, 0, 0), pipeline_mode=resident),
            pl.BlockSpec((tm, n), lambda i: (i, ga_blk)),
            pl.BlockSpec((tm, n), lambda i: (i, gc_blk)),
        ],
        out_specs=pl.BlockSpec((tm, n), lambda i: (i, 0)),
        out_shape=jax.ShapeDtypeStruct((m, n), BF16),
        scratch_shapes=[pltpu.VMEM((ka, n), BF16), pltpu.VMEM((kc, n), BF16)],
        compiler_params=_params(1),
        name="merge",
    )(attn, conv, w_attn_out, w_conv_out, gates, gates)


def _mix_out_body(a_ref, w_ref, x_ref, gt_ref, g_ref, sc_ref, sh_ref, x1_ref, h_ref, wbf_ref):
    @pl.when(jnp.logical_and(pl.program_id(0) == 0, pl.program_id(1) == 0))
    def _():
        wbf_ref[...] = w_ref[...].astype(BF16)

    a = a_ref[...]
    ts, d = a.shape
    sumsq = jnp.zeros((ts, 1), F32)
    for n0 in range(0, d, _EPILOGUE_LANES):
        ns = slice(n0, n0 + _EPILOGUE_LANES)
        acc = jnp.dot(a, wbf_ref[:, ns], preferred_element_type=F32)
        x1 = x_ref[0, :, ns] + gt_ref[0, :, ns] * acc
        x1_ref[0, :, ns] = x1
        sumsq = sumsq + jnp.sum(x1 * x1, axis=-1, keepdims=True)
    inv = lax.rsqrt(sumsq / d + EPS)
    y = x1_ref[0] * inv * g_ref[...]
    h_ref[0] = (y * (1.0 + sc_ref[0]) + sh_ref[0]).astype(h_ref.dtype)


def _mix_out(a, w, layer, x, mod3, gate_idx, norm_g, scale_idx, shift_idx, ts=256):
    b_sz, s_len, d = x.shape
    k = a.shape[1]
    assert d % _EPILOGUE_LANES == 0
    tiles = s_len // ts
    const2 = lambda b, s: (0, 0)
    mod_row = lambda idx: pl.BlockSpec((1, 1, d), lambda b, s: (b * N_MOD + idx, 0, 0))
    act = pl.BlockSpec((1, ts, d), lambda b, s: (b, s, 0))
    return pl.pallas_call(
        _mix_out_body,
        grid=(b_sz, tiles),
        in_specs=[
            pl.BlockSpec((ts, k), lambda b, s: (b * tiles + s, 0)),
            pl.BlockSpec((None, k, d), lambda b, s: (layer, 0, 0), pipeline_mode=pl.Buffered(1)),
            act,
            mod_row(gate_idx),
            pl.BlockSpec((1, d), const2),
            mod_row(scale_idx),
            mod_row(shift_idx),
        ],
        out_specs=[act, act],
        out_shape=[jax.ShapeDtypeStruct((b_sz, s_len, d), F32), jax.ShapeDtypeStruct((b_sz, s_len, d), BF16)],
        scratch_shapes=[pltpu.VMEM((k, d), BF16)],
        compiler_params=_params(2),
        name="mix_out",
    )(a, w, x, mod3, norm_g.reshape(1, d), mod3, mod3)


def _residual_proj_body(a_ref, w_ref, x_ref, gt_ref, o_ref, wbf_ref):
    _cast_weights_once([w_ref], [wbf_ref])
    a = a_ref[...]
    for n0 in range(0, o_ref.shape[1], _EPILOGUE_LANES):
        ns = slice(n0, n0 + _EPILOGUE_LANES)
        acc = jnp.dot(a, wbf_ref[:, ns], preferred_element_type=F32)
        o_ref[:, ns] = x_ref[:, ns] + gt_ref[0, :, ns] * acc


def _residual_proj(a, w, layer, x2d, mod3, gate_idx, rows_per_batch, tm, tn):
    m, k = a.shape
    n = w.shape[2]
    assert n % tn == 0 and tn % _EPILOGUE_LANES == 0 and rows_per_batch % tm == 0
    tiles_per_batch = rows_per_batch // tm
    return pl.pallas_call(
        _residual_proj_body,
        grid=(n // tn, m // tm),
        in_specs=[
            pl.BlockSpec((tm, k), lambda j, i: (i, 0)),
            pl.BlockSpec((None, k, tn), lambda j, i: (layer, 0, j), pipeline_mode=pl.Buffered(1)),
            pl.BlockSpec((tm, tn), lambda j, i: (i, j)),
            pl.BlockSpec((1, 1, tn), lambda j, i: ((i // tiles_per_batch) * N_MOD + gate_idx, 0, j)),
        ],
        out_specs=pl.BlockSpec((tm, tn), lambda j, i: (i, j)),
        out_shape=jax.ShapeDtypeStruct((m, n), F32),
        scratch_shapes=[pltpu.VMEM((k, tn), BF16)],
        compiler_params=_params(2),
        name="residual_proj",
    )(a, w, x2d, mod3)


def _ffn_in_body(a_ref, wg_ref, wu_ref, o_ref, wgbf_ref, wubf_ref):
    _cast_weights_once([wg_ref, wu_ref], [wgbf_ref, wubf_ref])
    a = a_ref[...]
    gate = jnp.dot(a, wgbf_ref[...], preferred_element_type=F32)
    up = jnp.dot(a, wubf_ref[...], preferred_element_type=F32)
    o_ref[...] = (gate * _sigmoid(gate) * up).astype(o_ref.dtype)


def _ffn_in(h, w_ffn_in, layer, tm=1024, tn=512):
    m, k = h.shape
    d_ff = w_ffn_in.shape[2] // 2
    up_blk = d_ff // tn
    return pl.pallas_call(
        _ffn_in_body,
        grid=(d_ff // tn, m // tm),
        in_specs=[
            pl.BlockSpec((tm, k), lambda j, i: (i, 0)),
            pl.BlockSpec((None, k, tn), lambda j, i: (layer, 0, j)),
            pl.BlockSpec((None, k, tn), lambda j, i: (layer, 0, up_blk + j)),
        ],
        out_specs=pl.BlockSpec((tm, tn), lambda j, i: (i, j)),
        out_shape=jax.ShapeDtypeStruct((m, d_ff), BF16),
        scratch_shapes=[pltpu.VMEM((k, tn), BF16), pltpu.VMEM((k, tn), BF16)],
        compiler_params=_params(2),
        name="ffn_in",
    )(h, w_ffn_in, w_ffn_in)


def _t5_causal_bucket(dist):
    n = jnp.maximum(dist, 0)
    nf = jnp.maximum(n, 1).astype(jnp.float32)
    large = MAX_EXACT + (jnp.log(nf / MAX_EXACT) / math.log(MAX_DISTANCE / MAX_EXACT)
                         * (NUM_BUCKETS - MAX_EXACT)).astype(jnp.int32)
    large = jnp.minimum(large, NUM_BUCKETS - 1)
    return jnp.where(n < MAX_EXACT, n, large)


def _band_buckets():
    q_off = jnp.arange(BLOCK)
    k_off = jnp.arange(2 * BLOCK)
    dist = q_off[:, None] + BLOCK - k_off[None, :]
    allowed = (dist >= 0) & (dist < WINDOW)
    return jnp.where(allowed, _t5_causal_bucket(dist), -1).astype(jnp.int32)


def _attn_body(rb_ref, sink_ref, bucket_ref, qmask_ref, krow_ref, q_ref, kc_ref, kp_ref, vc_ref, vp_ref,
               o_ref, bias_ref, lg_ref, p_ref, pv_ref, rs_ref, *, n_q, n_kv, nsub):
    q_per_kv = n_q // n_kv
    kv_w = n_kv * HEAD_DIM
    first_step = jnp.logical_and(pl.program_id(0) == 0, pl.program_id(1) == 0)

    @pl.when(first_step)
    def _():
        bucket = bucket_ref[...]
        col = lax.broadcasted_iota(jnp.int32, bucket.shape, 1)
        for slot in range(n_q):
            head = _orig_head(slot, n_kv, q_per_kv)
            tile = jnp.zeros(bucket.shape, F32)
            for b in range(NUM_BUCKETS):
                tile = jnp.where(bucket == b, rb_ref[b * n_q + head] * LOG2E, tile)
            tile = jnp.where(bucket < 0, -jnp.inf, tile)
            sink = sink_ref[head] * LOG2E
            bias_ref[0, slot] = jnp.where(col == 0, sink, tile)
            bias_ref[1, slot] = jnp.where(col == 0, sink, jnp.where(col < BLOCK, -jnp.inf, tile))

    lane_slot = lax.broadcasted_iota(jnp.int32, (BLOCK, BLOCK), 1) // HEAD_DIM
    ones = jnp.ones((2 * BLOCK, BLOCK), BF16)
    first_block = jnp.where(pl.program_id(1) == 0, 1, 0)

    for sub in range(nsub):
        rows = slice(sub * BLOCK, (sub + 1) * BLOCK)
        if sub == 0:
            keys = jnp.concatenate([kp_ref[...], kc_ref[:BLOCK, :]], axis=0)
            vals = jnp.concatenate([vp_ref[...], vc_ref[:BLOCK, :]], axis=0)
        else:
            keys = kc_ref[(sub - 1) * BLOCK:(sub + 1) * BLOCK, :]
            vals = vc_ref[(sub - 1) * BLOCK:(sub + 1) * BLOCK, :]
        keys = keys * krow_ref[...]
        vals = vals * krow_ref[...]
        lhs = jnp.concatenate(
            [q_ref[rows, j * kv_w:(j + 1) * kv_w] * qmask_ref[g]
             for j in range(q_per_kv) for g in range(n_kv)], axis=0)
        lg_ref[sub] = lax.dot_general(lhs, keys, (((1,), (1,)), ((), ())),
                                      preferred_element_type=F32).reshape(n_q, BLOCK, 2 * BLOCK)
        for slot in range(n_q):
            bias = bias_ref[first_block, slot] if sub == 0 else bias_ref[0, slot]
            logit = lg_ref[sub, slot] + bias
            m = jnp.max(logit, axis=-1, keepdims=True)
            p_ref[sub, slot * BLOCK:(slot + 1) * BLOCK, :] = jnp.exp2(logit - m).astype(BF16)
        p = p_ref[sub]
        pv_ref[sub] = jnp.dot(p, vals, preferred_element_type=F32).reshape(n_q, BLOCK, kv_w)
        rs_ref[sub] = jnp.dot(p, ones, preferred_element_type=F32).reshape(n_q, BLOCK, BLOCK)
        per_half = BLOCK // HEAD_DIM
        for j in range(q_per_kv):
            for half in range(kv_w // BLOCK):
                lanes = slice(half * BLOCK, (half + 1) * BLOCK)
                slots = [j * n_kv + half * per_half + i for i in range(per_half)]
                num = pv_ref[sub, slots[-1], :, lanes]
                den = rs_ref[sub, slots[-1]]
                for i in range(per_half - 2, -1, -1):
                    num = jnp.where(lane_slot == i, pv_ref[sub, slots[i], :, lanes], num)
                    den = jnp.where(lane_slot == i, rs_ref[sub, slots[i]], den)
                o_ref[rows, j * kv_w + half * BLOCK:j * kv_w + (half + 1) * BLOCK] = (
                    num * (1.0 / den)).astype(o_ref.dtype)


def _attention(qkv, rel_bias, sinks, b_sz, s_len, n_q, n_kv, nsub=2):
    tq = nsub * BLOCK
    nt = s_len // tq
    attn_w, kv_w = n_q * HEAD_DIM, n_kv * HEAD_DIM
    assert attn_w % kv_w == 0 and kv_w == 2 * BLOCK and BLOCK % HEAD_DIM == 0
    k_blk = attn_w // kv_w
    v_blk = k_blk + 1
    cur = lambda b, n: b * nt + n
    prev = lambda b, n: jnp.maximum((b * nt + n) * nsub - 1, 0)
    lane_group = jnp.arange(kv_w) // HEAD_DIM
    qmask = jnp.broadcast_to((lane_group[None, :] == jnp.arange(n_kv)[:, None])[:, None, :],
                             (n_kv, BLOCK, kv_w)).astype(BF16)
    krow = jnp.broadcast_to((jnp.arange(2 * BLOCK) > 0)[:, None], (2 * BLOCK, kv_w)).astype(BF16)
    body = functools.partial(_attn_body, n_q=n_q, n_kv=n_kv, nsub=nsub)
    const2 = lambda b, n: (0, 0)
    return pl.pallas_call(
        body,
        grid=(b_sz, nt),
        in_specs=[
            pl.BlockSpec(memory_space=pltpu.SMEM),
            pl.BlockSpec(memory_space=pltpu.SMEM),
            pl.BlockSpec((BLOCK, 2 * BLOCK), const2),
            pl.BlockSpec((n_kv, BLOCK, kv_w), lambda b, n: (0, 0, 0)),
            pl.BlockSpec((2 * BLOCK, kv_w), const2),
            pl.BlockSpec((tq, attn_w), lambda b, n: (cur(b, n), 0)),
            pl.BlockSpec((tq, kv_w), lambda b, n: (cur(b, n), k_blk)),
            pl.BlockSpec((BLOCK, kv_w), lambda b, n: (prev(b, n), k_blk)),
            pl.BlockSpec((tq, kv_w), lambda b, n: (cur(b, n), v_blk)),
            pl.BlockSpec((BLOCK, kv_w), lambda b, n: (prev(b, n), v_blk)),
        ],
        out_specs=pl.BlockSpec((tq, attn_w), lambda b, n: (cur(b, n), 0)),
        out_shape=jax.ShapeDtypeStruct((b_sz * s_len, attn_w), BF16),
        scratch_shapes=[
            pltpu.VMEM((2, n_q, BLOCK, 2 * BLOCK), F32),
            pltpu.VMEM((nsub, n_q, BLOCK, 2 * BLOCK), F32),
            pltpu.VMEM((nsub, n_q * BLOCK, 2 * BLOCK), BF16),
            pltpu.VMEM((nsub, n_q, BLOCK, kv_w), F32),
            pltpu.VMEM((nsub, n_q, BLOCK, BLOCK), F32),
        ],
        compiler_params=_params(2),
        name="swa_attention",
    )(rel_bias.astype(F32).reshape(-1), sinks.astype(F32), _band_buckets(), qmask, krow,
      qkv, qkv, qkv, qkv, qkv)


_CONV_HALO = 32
_CONV_LANES = 256
_SUBLANES = 8


def _conv_body(a_ref, g_ref, w_ref, cb_ref, lg_ref, lb_ref, o_ref, u_ref, sh_ref, y_ref, *, width, ts):
    channels = u_ref.shape[1]

    @pl.when(pl.program_id(1) == 0)
    def _():
        u_ref[:_CONV_HALO, :] = jnp.zeros((_CONV_HALO, channels), F32)

    u_ref[_CONV_HALO:, :] = a_ref[...].astype(F32) * _sigmoid(g_ref[...].astype(F32))
    span = sh_ref.shape[1]
    for r in range(1, _SUBLANES):
        sh_ref[r - 1] = u_ref[r:r + span, :]
    first = _CONV_HALO - (width - 1)
    for c0 in range(0, channels, _CONV_LANES):
        cs = slice(c0, c0 + _CONV_LANES)
        acc = jnp.broadcast_to(cb_ref[:, cs], (ts, _CONV_LANES))
        for j in range(width):
            tile, r = divmod(first + j, _SUBLANES)
            rows = slice(tile * _SUBLANES, tile * _SUBLANES + ts)
            taps = u_ref[rows, cs] if r == 0 else sh_ref[r - 1, rows, cs]
            acc = acc + taps * w_ref[j:j + 1, cs]
        y_ref[:, cs] = acc
    y = y_ref[...]
    mu = jnp.mean(y, axis=-1, keepdims=True)
    yc = y - mu
    var = jnp.mean(yc * yc, axis=-1, keepdims=True)
    z = yc * lax.rsqrt(var + EPS) * lg_ref[...] + lb_ref[...]
    o_ref[...] = (z * _sigmoid(z)).astype(o_ref.dtype)
    u_ref[:_CONV_HALO, :] = u_ref[ts:, :]


def _conformer_conv(proj, a_col, g_col, conv_w, conv_b, ln_g, ln_b, b_sz, s_len, ts=128):
    width, channels = conv_w.shape
    assert a_col % channels == 0 and g_col % channels == 0
    assert width - 1 <= _CONV_HALO and _CONV_HALO % _SUBLANES == 0 and ts >= _CONV_HALO
    a_blk, g_blk = a_col // channels, g_col // channels
    tiles = s_len // ts
    cur = lambda b, s: b * tiles + s
    row = lambda v: v.reshape(1, channels)
    body = functools.partial(_conv_body, width=width, ts=ts)
    return pl.pallas_call(
        body,
        grid=(b_sz, tiles),
        in_specs=[
            pl.BlockSpec((ts, channels), lambda b, s: (cur(b, s), a_blk)),
            pl.BlockSpec((ts, channels), lambda b, s: (cur(b, s), g_blk)),
            pl.BlockSpec((width, channels), lambda b, s: (0, 0)),
            pl.BlockSpec((1, channels), lambda b, s: (0, 0)),
            pl.BlockSpec((1, channels), lambda b, s: (0, 0)),
            pl.BlockSpec((1, channels), lambda b, s: (0, 0)),
        ],
        out_specs=pl.BlockSpec((ts, channels), lambda b, s: (cur(b, s), 0)),
        out_shape=jax.ShapeDtypeStruct((b_sz * s_len, channels), BF16),
        scratch_shapes=[
            pltpu.VMEM((_CONV_HALO + ts, channels), F32),
            pltpu.VMEM((_SUBLANES - 1, _CONV_HALO - _SUBLANES + ts, channels), F32),
            pltpu.VMEM((ts, channels), F32),
        ],
        compiler_params=_params(2),
        name="conformer_conv",
    )(proj, proj, conv_w, row(conv_b), row(ln_g), row(ln_b))


def kernel(x, c, w_ada, b_ada, norm_mix_g, w_in, q_norm_g, k_norm_g, attn_sinks, rel_bias,
           w_attn_out, conv_w, conv_b, conv_ln_g, conv_ln_b, w_conv_out, w_mix_out,
           norm_ffn_g, w_ffn_in, w_ffn_out):
    b_sz, s_len, d = x.shape
    depth = w_ada.shape[0]
    n_q = attn_sinks.shape[1]
    attn_w = w_attn_out.shape[1]
    channels = conv_w.shape[2]
    kv_w = (w_in.shape[2] - attn_w - 2 * channels - 2 * d) // 2
    n_kv = kv_w // HEAD_DIM
    qkv_w = attn_w + 2 * kv_w
    a_col = 0
    g_col = a_col + channels
    ga_col = g_col + channels
    gc_col = ga_col + d
    m = b_sz * s_len

    for l in range(depth):
        mod = _ada(c, w_ada, b_ada[l], l)
        mod3 = mod.reshape(b_sz * N_MOD, 1, d)

        qkv, h = _qkv_proj(x, norm_mix_g[l], mod3, 1, 0, w_in, l, q_norm_g[l], k_norm_g[l], n_q, n_kv)
        rest = _proj(h.reshape(m, d), w_in, l, qkv_w, tm=1024, tn=qkv_w)
        y_attn = _attention(qkv, rel_bias, attn_sinks[l], b_sz, s_len, n_q, n_kv)
        y_conv = _conformer_conv(rest, a_col, g_col, conv_w[l], conv_b[l], conv_ln_g[l],
                                 conv_ln_b[l], b_sz, s_len)
        merged = _merge(y_attn, y_conv, w_attn_out, w_conv_out, l, rest, ga_col, gc_col, n_q, n_kv)
        x1, h = _mix_out(merged, w_mix_out, l, x, mod3, 2, norm_ffn_g[l], 4, 3)

        act = _ffn_in(h.reshape(m, d), w_ffn_in, l)
        x = _residual_proj(act, w_ffn_out, l, x1.reshape(m, d), mod3, 5, s_len,
                           tm=256, tn=1024).reshape(b_sz, s_len, d)
    return x
```

```python
import functools
import math

import jax
import jax.numpy as jnp
from jax import lax
from jax.experimental import pallas as pl
from jax.experimental.pallas import tpu as pltpu

F32 = jnp.float32
BF16 = jnp.bfloat16

HEAD_DIM = 64
WINDOW = 128
BLOCK = 128
NUM_BUCKETS = 32
MAX_EXACT = NUM_BUCKETS // 2
MAX_DISTANCE = 128
N_MOD = 6
EPS = 1e-6
LOG2E = math.log2(math.e)

V7X_VMEM_LIMIT_BYTES = 56 * 1024 * 1024


def _params(n_axes):
    return pltpu.CompilerParams(
        dimension_semantics=("arbitrary",) * n_axes,
        vmem_limit_bytes=V7X_VMEM_LIMIT_BYTES,
    )


def _sigmoid(x):
    return jax.nn.sigmoid(x)


def _orig_head(slot, n_kv, q_per_kv):
    return (slot % n_kv) * q_per_kv + slot // n_kv


def _ada_body(c_ref, w_ref, b_ref, o_ref):
    c = c_ref[...]
    act = (c * _sigmoid(c)).astype(BF16)
    o_ref[...] = jnp.dot(act, w_ref[...].astype(BF16), preferred_element_type=F32) + b_ref[...]


def _ada(c, w_ada, b_ada, layer, tn=1024):
    b_sz, d = c.shape
    n = w_ada.shape[2]
    rows = 8
    c_pad = jnp.pad(c, ((0, rows - b_sz), (0, 0)))
    out = pl.pallas_call(
        _ada_body,
        grid=(n // tn,),
        in_specs=[
            pl.BlockSpec((rows, d), lambda j: (0, 0)),
            pl.BlockSpec((None, d, tn), lambda j: (layer, 0, j)),
            pl.BlockSpec((1, tn), lambda j: (0, j)),
        ],
        out_specs=pl.BlockSpec((rows, tn), lambda j: (0, j)),
        out_shape=jax.ShapeDtypeStruct((rows, n), F32),
        compiler_params=_params(1),
        name="ada_mod",
    )(c_pad, w_ada, b_ada.reshape(1, n))
    return out[:b_sz]


def _norm_mod_body(x_ref, g_ref, sc_ref, sh_ref, o_ref):
    x = x_ref[0]
    ms = jnp.mean(x * x, axis=-1, keepdims=True)
    y = x * lax.rsqrt(ms + EPS) * g_ref[...]
    o_ref[0] = (y * (1.0 + sc_ref[0]) + sh_ref[0]).astype(o_ref.dtype)


def _qkv_body(x_ref, g_ref, sc_ref, sh_ref, w_ref, gq_ref, gk_ref, o_ref, h_ref, wbf_ref, seg_ref,
              *, n_q, n_kv, half):
    attn_w = n_q * HEAD_DIM
    kv_w = n_kv * HEAD_DIM
    _norm_mod_body(x_ref, g_ref, sc_ref, sh_ref, h_ref)

    @pl.when(jnp.logical_and(pl.program_id(0) == 0, pl.program_id(1) == 0))
    def _():
        for slot in range(n_q):
            src = _orig_head(slot, n_kv, n_q // n_kv) * HEAD_DIM
            wbf_ref[:, slot * HEAD_DIM:(slot + 1) * HEAD_DIM] = w_ref[:, src:src + HEAD_DIM].astype(BF16)
        wbf_ref[:, attn_w:] = w_ref[:, attn_w:].astype(BF16)
        r = lax.broadcasted_iota(jnp.int32, (half, half), 0) // HEAD_DIM
        c = lax.broadcasted_iota(jnp.int32, (half, half), 1) // HEAD_DIM
        seg_ref[...] = (r == c).astype(BF16)

    a = h_ref[0]

    def head_norm(acc, seg, gain):
        ss = jnp.dot((acc * acc).astype(BF16), seg, preferred_element_type=F32)
        return acc * lax.rsqrt(ss / HEAD_DIM + EPS) * gain

    for c0 in range(0, attn_w, half):
        acc = jnp.dot(a, wbf_ref[:, c0:c0 + half], preferred_element_type=F32)
        o_ref[:, c0:c0 + half] = head_norm(acc, seg_ref[...], gq_ref[...]).astype(o_ref.dtype)
    acc = jnp.dot(a, wbf_ref[:, attn_w:attn_w + kv_w], preferred_element_type=F32)
    o_ref[:, attn_w:attn_w + kv_w] = head_norm(acc, seg_ref[:kv_w, :kv_w], gk_ref[...]).astype(o_ref.dtype)
    acc = jnp.dot(a, wbf_ref[:, attn_w + kv_w:], preferred_element_type=F32)
    o_ref[:, attn_w + kv_w:] = acc.astype(o_ref.dtype)


def _qkv_proj(x, norm_g, mod3, scale_idx, shift_idx, w_in, layer, q_g, k_g, n_q, n_kv, ts=512, half=512):
    b_sz, s_len, d = x.shape
    attn_w, kv_w = n_q * HEAD_DIM, n_kv * HEAD_DIM
    width = attn_w + 2 * kv_w
    assert attn_w % half == 0 and kv_w <= half and half % HEAD_DIM == 0
    gq = jnp.tile(q_g * (HEAD_DIM ** -0.5 * LOG2E), half // HEAD_DIM).reshape(1, half)
    gk = jnp.tile(k_g, n_kv).reshape(1, kv_w)
    tiles = s_len // ts
    const2 = lambda b, s: (0, 0)
    body = functools.partial(_qkv_body, n_q=n_q, n_kv=n_kv, half=half)
    return pl.pallas_call(
        body,
        grid=(b_sz, tiles),
        in_specs=[
            pl.BlockSpec((1, ts, d), lambda b, s: (b, s, 0)),
            pl.BlockSpec((1, d), const2),
            pl.BlockSpec((1, 1, d), lambda b, s: (b * N_MOD + scale_idx, 0, 0)),
            pl.BlockSpec((1, 1, d), lambda b, s: (b * N_MOD + shift_idx, 0, 0)),
            pl.BlockSpec((None, d, width), lambda b, s: (layer, 0, 0), pipeline_mode=pl.Buffered(1)),
            pl.BlockSpec((1, half), const2),
            pl.BlockSpec((1, kv_w), const2),
        ],
        out_specs=[
            pl.BlockSpec((ts, width), lambda b, s: (b * tiles + s, 0)),
            pl.BlockSpec((1, ts, d), lambda b, s: (b, s, 0)),
        ],
        out_shape=[
            jax.ShapeDtypeStruct((b_sz * s_len, width), BF16),
            jax.ShapeDtypeStruct((b_sz, s_len, d), BF16),
        ],
        scratch_shapes=[pltpu.VMEM((d, width), BF16), pltpu.VMEM((half, half), BF16)],
        compiler_params=_params(2),
        name="qkv_proj",
    )(x, norm_g.reshape(1, d), mod3, mod3, w_in, gq, gk)


def _cast_weights_once(w_refs, wbf_refs):
    @pl.when(pl.program_id(1) == 0)
    def _():
        for w_ref, wbf_ref in zip(w_refs, wbf_refs):
            wbf_ref[...] = w_ref[...].astype(BF16)


def _proj_body(a_ref, w_ref, o_ref, wbf_ref):
    _cast_weights_once([w_ref], [wbf_ref])
    acc = jnp.dot(a_ref[...], wbf_ref[...], preferred_element_type=F32)
    o_ref[...] = acc.astype(o_ref.dtype)


def _proj(h, w, layer, col0, n, tm, tn):
    m, k = h.shape
    assert col0 % tn == 0 and n % tn == 0
    blk0 = col0 // tn
    return pl.pallas_call(
        _proj_body,
        grid=(n // tn, m // tm),
        in_specs=[
            pl.BlockSpec((tm, k), lambda j, i: (i, 0)),
            pl.BlockSpec((None, k, tn), lambda j, i: (layer, 0, blk0 + j)),
        ],
        out_specs=pl.BlockSpec((tm, tn), lambda j, i: (i, j)),
        out_shape=jax.ShapeDtypeStruct((m, n), BF16),
        scratch_shapes=[pltpu.VMEM((k, tn), BF16)],
        compiler_params=_params(2),
        name="proj",
    )(h, w)


_EPILOGUE_LANES = 512


def _merge_body(ya_ref, yc_ref, wa_ref, wc_ref, ga_ref, gc_ref, o_ref, wabf_ref, wcbf_ref,
                *, n_q, n_kv):
    @pl.when(pl.program_id(0) == 0)
    def _():
        for slot in range(n_q):
            src = _orig_head(slot, n_kv, n_q // n_kv) * HEAD_DIM
            wabf_ref[slot * HEAD_DIM:(slot + 1) * HEAD_DIM, :] = wa_ref[src:src + HEAD_DIM, :].astype(BF16)
        wcbf_ref[...] = wc_ref[...].astype(BF16)

    ya = ya_ref[...]
    yc = yc_ref[...]
    for n0 in range(0, o_ref.shape[1], _EPILOGUE_LANES):
        ns = slice(n0, n0 + _EPILOGUE_LANES)
        acc_a = jnp.dot(ya, wabf_ref[:, ns], preferred_element_type=F32)
        acc_c = jnp.dot(yc, wcbf_ref[:, ns], preferred_element_type=F32)
        ga = _sigmoid(ga_ref[:, ns].astype(F32))
        gc = _sigmoid(gc_ref[:, ns].astype(F32))
        o_ref[:, ns] = (ga * acc_a + gc * acc_c).astype(o_ref.dtype)


def _merge(attn, conv, w_attn_out, w_conv_out, layer, gates, ga_col, gc_col, n_q, n_kv, tm=256):
    m, ka = attn.shape
    kc = conv.shape[1]
    n = w_attn_out.shape[2]
    assert ga_col % n == 0 and gc_col % n == 0 and n % _EPILOGUE_LANES == 0
    ga_blk, gc_blk = ga_col // n, gc_col // n
    body = functools.partial(_merge_body, n_q=n_q, n_kv=n_kv)
    resident = pl.Buffered(1)
    return pl.pallas_call(
        body,
        grid=(m // tm,),
        in_specs=[
            pl.BlockSpec((tm, ka), lambda i: (i, 0)),
            pl.BlockSpec((tm, kc), lambda i: (i, 0)),
            pl.BlockSpec((None, ka, n), lambda i: (layer, 0, 0), pipeline_mode=resident),
            pl.BlockSpec((None, kc, n), lambda i: (layer, 0, 0), pipeline_mode=resident),
            pl.BlockSpec((tm, n), lambda i: (i, ga_blk)),
            pl.BlockSpec((tm, n), lambda i: (i, gc_blk)),
        ],
        out_specs=pl.BlockSpec((tm, n), lambda i: (i, 0)),
        out_shape=jax.ShapeDtypeStruct((m, n), BF16),
        scratch_shapes=[pltpu.VMEM((ka, n), BF16), pltpu.VMEM((kc, n), BF16)],
        compiler_params=_params(1),
        name="merge",
    )(attn, conv, w_attn_out, w_conv_out, gates, gates)


def _mix_out_body(a_ref, w_ref, x_ref, gt_ref, g_ref, sc_ref, sh_ref, x1_ref, h_ref, wbf_ref):
    @pl.when(jnp.logical_and(pl.program_id(0) == 0, pl.program_id(1) == 0))
    def _():
        wbf_ref[...] = w_ref[...].astype(BF16)

    a = a_ref[...]
    ts, d = a.shape
    sumsq = jnp.zeros((ts, 1), F32)
    for n0 in range(0, d, _EPILOGUE_LANES):
        ns = slice(n0, n0 + _EPILOGUE_LANES)
        acc = jnp.dot(a, wbf_ref[:, ns], preferred_element_type=F32)
        x1 = x_ref[0, :, ns] + gt_ref[0, :, ns] * acc
        x1_ref[0, :, ns] = x1
        sumsq = sumsq + jnp.sum(x1 * x1, axis=-1, keepdims=True)
    inv = lax.rsqrt(sumsq / d + EPS)
    y = x1_ref[0] * inv * g_ref[...]
    h_ref[0] = (y * (1.0 + sc_ref[0]) + sh_ref[0]).astype(h_ref.dtype)


def _mix_out(a, w, layer, x, mod3, gate_idx, norm_g, scale_idx, shift_idx, ts=256):
    b_sz, s_len, d = x.shape
    k = a.shape[1]
    assert d % _EPILOGUE_LANES == 0
    tiles = s_len // ts
    const2 = lambda b, s: (0, 0)
    mod_row = lambda idx: pl.BlockSpec((1, 1, d), lambda b, s: (b * N_MOD + idx, 0, 0))
    act = pl.BlockSpec((1, ts, d), lambda b, s: (b, s, 0))
    return pl.pallas_call(
        _mix_out_body,
        grid=(b_sz, tiles),
        in_specs=[
            pl.BlockSpec((ts, k), lambda b, s: (b * tiles + s, 0)),
            pl.BlockSpec((None, k, d), lambda b, s: (layer, 0, 0), pipeline_mode=pl.Buffered(1)),
            act,
            mod_row(gate_idx),
            pl.BlockSpec((1, d), const2),
            mod_row(scale_idx),
            mod_row(shift_idx),
        ],
        out_specs=[act, act],
        out_shape=[jax.ShapeDtypeStruct((b_sz, s_len, d), F32), jax.ShapeDtypeStruct((b_sz, s_len, d), BF16)],
        scratch_shapes=[pltpu.VMEM((k, d), BF16)],
        compiler_params=_params(2),
        name="mix_out",
    )(a, w, x, mod3, norm_g.reshape(1, d), mod3, mod3)


def _residual_proj_body(a_ref, w_ref, x_ref, gt_ref, o_ref, wbf_ref):
    _cast_weights_once([w_ref], [wbf_ref])
    a = a_ref[...]
    for n0 in range(0, o_ref.shape[1], _EPILOGUE_LANES):
        ns = slice(n0, n0 + _EPILOGUE_LANES)
        acc = jnp.dot(a, wbf_ref[:, ns], preferred_element_type=F32)
        o_ref[:, ns] = x_ref[:, ns] + gt_ref[0, :, ns] * acc


def _residual_proj(a, w, layer, x2d, mod3, gate_idx, rows_per_batch, tm, tn):
    m, k = a.shape
    n = w.shape[2]
    assert n % tn == 0 and tn % _EPILOGUE_LANES == 0 and rows_per_batch % tm == 0
    tiles_per_batch = rows_per_batch // tm
    return pl.pallas_call(
        _residual_proj_body,
        grid=(n // tn, m // tm),
        in_specs=[
            pl.BlockSpec((tm, k), lambda j, i: (i, 0)),
            pl.BlockSpec((None, k, tn), lambda j, i: (layer, 0, j), pipeline_mode=pl.Buffered(1)),
            pl.BlockSpec((tm, tn), lambda j, i: (i, j)),
            pl.BlockSpec((1, 1, tn), lambda j, i: ((i // tiles_per_batch) * N_MOD + gate_idx, 0, j)),
        ],
        out_specs=pl.BlockSpec((tm, tn), lambda j, i: (i, j)),
        out_shape=jax.ShapeDtypeStruct((m, n), F32),
        scratch_shapes=[pltpu.VMEM((k, tn), BF16)],
        compiler_params=_params(2),
        name="residual_proj",
    )(a, w, x2d, mod3)


def _ffn_in_body(a_ref, wg_ref, wu_ref, o_ref, wgbf_ref, wubf_ref):
    _cast_weights_once([wg_ref, wu_ref], [wgbf_ref, wubf_ref])
    a = a_ref[...]
    gate = jnp.dot(a, wgbf_ref[...], preferred_element_type=F32)
    up = jnp.dot(a, wubf_ref[...], preferred_element_type=F32)
    o_ref[...] = (gate * _sigmoid(gate) * up).astype(o_ref.dtype)


def _ffn_in(h, w_ffn_in, layer, tm=1024, tn=512):
    m, k = h.shape
    d_ff = w_ffn_in.shape[2] // 2
    up_blk = d_ff // tn
    return pl.pallas_call(
        _ffn_in_body,
        grid=(d_ff // tn, m // tm),
        in_specs=[
            pl.BlockSpec((tm, k), lambda j, i: (i, 0)),
            pl.BlockSpec((None, k, tn), lambda j, i: (layer, 0, j)),
            pl.BlockSpec((None, k, tn), lambda j, i: (layer, 0, up_blk + j)),
        ],
        out_specs=pl.BlockSpec((tm, tn), lambda j, i: (i, j)),
        out_shape=jax.ShapeDtypeStruct((m, d_ff), BF16),
        scratch_shapes=[pltpu.VMEM((k, tn), BF16), pltpu.VMEM((k, tn), BF16)],
        compiler_params=_params(2),
        name="ffn_in",
    )(h, w_ffn_in, w_ffn_in)


def _t5_causal_bucket(dist):
    n = jnp.maximum(dist, 0)
    nf = jnp.maximum(n, 1).astype(jnp.float32)
    large = MAX_EXACT + (jnp.log(nf / MAX_EXACT) / math.log(MAX_DISTANCE / MAX_EXACT)
                         * (NUM_BUCKETS - MAX_EXACT)).astype(jnp.int32)
    large = jnp.minimum(large, NUM_BUCKETS - 1)
    return jnp.where(n < MAX_EXACT, n, large)


def _band_buckets():
    q_off = jnp.arange(BLOCK)
    k_off = jnp.arange(2 * BLOCK)
    dist = q_off[:, None] + BLOCK - k_off[None, :]
    allowed = (dist >= 0) & (dist < WINDOW)
    return jnp.where(allowed, _t5_causal_bucket(dist), -1).astype(jnp.int32)


def _attn_build_bias(first_step, rb_ref, sink_ref, bucket_ref, bias_ref, *, n_q, n_kv):
    q_per_kv = n_q // n_kv

    @pl.when(first_step)
    def _():
        bucket = bucket_ref[...]
        col = lax.broadcasted_iota(jnp.int32, bucket.shape, 1)
        for slot in range(n_q):
            head = _orig_head(slot, n_kv, q_per_kv)
            tile = jnp.zeros(bucket.shape, F32)
            for b in range(NUM_BUCKETS):
                tile = jnp.where(bucket == b, rb_ref[b * n_q + head] * LOG2E, tile)
            tile = jnp.where(bucket < 0, -jnp.inf, tile)
            sink = sink_ref[head] * LOG2E
            bias_ref[0, slot] = jnp.where(col == 0, sink, tile)
            bias_ref[1, slot] = jnp.where(col == 0, sink, jnp.where(col < BLOCK, -jnp.inf, tile))


def _attn_block(seq_start, qmask_ref, krow_ref, q_ref, kc_ref, kp_ref, vc_ref, vp_ref, o_ref,
                bias_ref, lg_ref, p_ref, pv_ref, rs_ref, *, n_q, n_kv):
    q_per_kv = n_q // n_kv
    kv_w = n_kv * HEAD_DIM
    lane_slot = lax.broadcasted_iota(jnp.int32, (BLOCK, BLOCK), 1) // HEAD_DIM
    ones = jnp.ones((2 * BLOCK, BLOCK), BF16)
    bias_idx = jnp.where(seq_start, 1, 0)

    keys = jnp.concatenate([kp_ref[...], kc_ref[...]], axis=0) * krow_ref[...]
    vals = jnp.concatenate([vp_ref[...], vc_ref[...]], axis=0) * krow_ref[...]
    lhs = jnp.concatenate(
        [q_ref[:, j * kv_w:(j + 1) * kv_w] * qmask_ref[g]
         for j in range(q_per_kv) for g in range(n_kv)], axis=0)
    lg_ref[...] = lax.dot_general(lhs, keys, (((1,), (1,)), ((), ())),
                                  preferred_element_type=F32).reshape(n_q, BLOCK, 2 * BLOCK)
    for slot in range(n_q):
        logit = lg_ref[slot] + bias_ref[bias_idx, slot]
        m = jnp.max(logit, axis=-1, keepdims=True)
        p_ref[slot * BLOCK:(slot + 1) * BLOCK, :] = jnp.exp2(logit - m).astype(BF16)
    p = p_ref[...]
    pv_ref[...] = jnp.dot(p, vals, preferred_element_type=F32).reshape(n_q, BLOCK, kv_w)
    rs_ref[...] = jnp.dot(p, ones, preferred_element_type=F32).reshape(n_q, BLOCK, BLOCK)
    per_half = BLOCK // HEAD_DIM
    for j in range(q_per_kv):
        for half in range(kv_w // BLOCK):
            lanes = slice(half * BLOCK, (half + 1) * BLOCK)
            slots = [j * n_kv + half * per_half + i for i in range(per_half)]
            num = pv_ref[slots[-1], :, lanes]
            den = rs_ref[slots[-1]]
            for i in range(per_half - 2, -1, -1):
                num = jnp.where(lane_slot == i, pv_ref[slots[i], :, lanes], num)
                den = jnp.where(lane_slot == i, rs_ref[slots[i]], den)
            o_ref[:, j * kv_w + half * BLOCK:j * kv_w + (half + 1) * BLOCK] = (
                num * (1.0 / den)).astype(o_ref.dtype)


_CONV_HALO = 32
_CONV_LANES = 256
_SUBLANES = 8


def _conv_zero_history(seq_start, u_ref):
    @pl.when(seq_start)
    def _():
        u_ref[:_CONV_HALO, :] = jnp.zeros((_CONV_HALO, u_ref.shape[1]), F32)


def _conv_block(a_ref, g_ref, w_ref, cb_ref, lg_ref, lb_ref, o_ref, u_ref, sh_ref, y_ref):
    width = w_ref.shape[0]
    ts, channels = o_ref.shape
    u_ref[_CONV_HALO:, :] = a_ref[...].astype(F32) * _sigmoid(g_ref[...].astype(F32))
    span = sh_ref.shape[1]
    for r in range(1, _SUBLANES):
        sh_ref[r - 1] = u_ref[r:r + span, :]
    first = _CONV_HALO - (width - 1)
    for c0 in range(0, channels, _CONV_LANES):
        cs = slice(c0, c0 + _CONV_LANES)
        acc = jnp.broadcast_to(cb_ref[:, cs], (ts, _CONV_LANES))
        for j in range(width):
            tile, r = divmod(first + j, _SUBLANES)
            rows = slice(tile * _SUBLANES, tile * _SUBLANES + ts)
            taps = u_ref[rows, cs] if r == 0 else sh_ref[r - 1, rows, cs]
            acc = acc + taps * w_ref[j:j + 1, cs]
        y_ref[:, cs] = acc
    y = y_ref[...]
    mu = jnp.mean(y, axis=-1, keepdims=True)
    yc = y - mu
    var = jnp.mean(yc * yc, axis=-1, keepdims=True)
    z = yc * lax.rsqrt(var + EPS) * lg_ref[...] + lb_ref[...]
    o_ref[...] = (z * _sigmoid(z)).astype(o_ref.dtype)
    u_ref[:_CONV_HALO, :] = u_ref[ts:, :]


def _core_body(h_ref, w_ref,
               a_ref, g_ref, cw_ref, cb_ref, lg_ref, lb_ref,
               rb_ref, sink_ref, bucket_ref, qmask_ref, krow_ref, q_ref, kc_ref, kp_ref, vc_ref, vp_ref,
               gates_ref, yconv_ref, yattn_ref,
               wbf_ref, u_ref, sh_ref, y_ref, bias_ref, lgt_ref, p_ref, pv_ref, rs_ref,
               *, n_q, n_kv, blocks_per_seq):
    t = pl.program_id(0) * pl.num_programs(1) + pl.program_id(1)
    seq_start = t % blocks_per_seq == 0
    _cast_weights_once([w_ref], [wbf_ref])
    _conv_zero_history(seq_start, u_ref)
    _attn_build_bias(t == 0, rb_ref, sink_ref, bucket_ref, bias_ref, n_q=n_q, n_kv=n_kv)

    acc = jnp.dot(h_ref[...], wbf_ref[...], preferred_element_type=F32)
    gates_ref[...] = acc.astype(gates_ref.dtype)
    _conv_block(a_ref, g_ref, cw_ref, cb_ref, lg_ref, lb_ref, yconv_ref, u_ref, sh_ref, y_ref)
    _attn_block(seq_start, qmask_ref, krow_ref, q_ref, kc_ref, kp_ref, vc_ref, vp_ref, yattn_ref,
                bias_ref, lgt_ref, p_ref, pv_ref, rs_ref, n_q=n_q, n_kv=n_kv)


def _mixer_core(h, w_in, layer, gate_col, conv_in, qkv, conv_w, conv_b, ln_g, ln_b, rel_bias, sinks,
                s_len, n_q, n_kv, tn=512):
    m, k = h.shape
    n = w_in.shape[2] - gate_col
    width, channels = conv_w.shape
    attn_w, kv_w = n_q * HEAD_DIM, n_kv * HEAD_DIM
    n_blocks = m // BLOCK
    n_tiles = n // tn
    assert gate_col % tn == 0 and n % tn == 0 and n_blocks % n_tiles == 0
    assert s_len % BLOCK == 0 and width - 1 <= _CONV_HALO and _CONV_HALO % _SUBLANES == 0
    assert attn_w % kv_w == 0 and kv_w == 2 * BLOCK and BLOCK % HEAD_DIM == 0
    m_tiles = n_blocks // n_tiles
    tm = m // m_tiles
    col_blk0 = gate_col // tn
    k_blk = attn_w // kv_w
    v_blk = k_blk + 1

    step = lambda j, i: j * m_tiles + i
    prev_step = lambda j, i: jnp.maximum(step(j, i) - 1, 0)
    const2 = lambda j, i: (0, 0)
    row = lambda v: v.reshape(1, channels)
    lane_group = jnp.arange(kv_w) // HEAD_DIM
    qmask = jnp.broadcast_to((lane_group[None, :] == jnp.arange(n_kv)[:, None])[:, None, :],
                             (n_kv, BLOCK, kv_w)).astype(BF16)
    krow = jnp.broadcast_to((jnp.arange(2 * BLOCK) > 0)[:, None], (2 * BLOCK, kv_w)).astype(BF16)
    body = functools.partial(_core_body, n_q=n_q, n_kv=n_kv, blocks_per_seq=s_len // BLOCK)
    return pl.pallas_call(
        body,
        grid=(n_tiles, m_tiles),
        in_specs=[
            pl.BlockSpec((tm, k), lambda j, i: (i, 0)),
            pl.BlockSpec((None, k, tn), lambda j, i: (layer, 0, col_blk0 + j)),
            pl.BlockSpec((BLOCK, channels), lambda j, i: (step(j, i), 0)),
            pl.BlockSpec((BLOCK, channels), lambda j, i: (step(j, i), 1)),
            pl.BlockSpec((width, channels), const2),
            pl.BlockSpec((1, channels), const2),
            pl.BlockSpec((1, channels), const2),
            pl.BlockSpec((1, channels), const2),
            pl.BlockSpec(memory_space=pltpu.SMEM),
            pl.BlockSpec(memory_space=pltpu.SMEM),
            pl.BlockSpec((BLOCK, 2 * BLOCK), const2),
            pl.BlockSpec((n_kv, BLOCK, kv_w), lambda j, i: (0, 0, 0)),
            pl.BlockSpec((2 * BLOCK, kv_w), const2),
            pl.BlockSpec((BLOCK, attn_w), lambda j, i: (step(j, i), 0)),
            pl.BlockSpec((BLOCK, kv_w), lambda j, i: (step(j, i), k_blk)),
            pl.BlockSpec((BLOCK, kv_w), lambda j, i: (prev_step(j, i), k_blk)),
            pl.BlockSpec((BLOCK, kv_w), lambda j, i: (step(j, i), v_blk)),
            pl.BlockSpec((BLOCK, kv_w), lambda j, i: (prev_step(j, i), v_blk)),
        ],
        out_specs=[
            pl.BlockSpec((tm, tn), lambda j, i: (i, j)),
            pl.BlockSpec((BLOCK, channels), lambda j, i: (step(j, i), 0)),
            pl.BlockSpec((BLOCK, attn_w), lambda j, i: (step(j, i), 0)),
        ],
        out_shape=[
            jax.ShapeDtypeStruct((m, n), BF16),
            jax.ShapeDtypeStruct((m, channels), BF16),
            jax.ShapeDtypeStruct((m, attn_w), BF16),
        ],
        scratch_shapes=[
            pltpu.VMEM((k, tn), BF16),
            pltpu.VMEM((_CONV_HALO + BLOCK, channels), F32),
            pltpu.VMEM((_SUBLANES - 1, _CONV_HALO - _SUBLANES + BLOCK, channels), F32),
            pltpu.VMEM((BLOCK, channels), F32),
            pltpu.VMEM((2, n_q, BLOCK, 2 * BLOCK), F32),
            pltpu.VMEM((n_q, BLOCK, 2 * BLOCK), F32),
            pltpu.VMEM((n_q * BLOCK, 2 * BLOCK), BF16),
            pltpu.VMEM((n_q, BLOCK, kv_w), F32),
            pltpu.VMEM((n_q, BLOCK, BLOCK), F32),
        ],
        compiler_params=_params(2),
        name="mixer_core",
    )(h, w_in,
      conv_in, conv_in, conv_w, row(conv_b), row(ln_g), row(ln_b),
      rel_bias.astype(F32).reshape(-1), sinks.astype(F32), _band_buckets(), qmask, krow,
      qkv, qkv, qkv, qkv, qkv)


def kernel(x, c, w_ada, b_ada, norm_mix_g, w_in, q_norm_g, k_norm_g, attn_sinks, rel_bias,
           w_attn_out, conv_w, conv_b, conv_ln_g, conv_ln_b, w_conv_out, w_mix_out,
           norm_ffn_g, w_ffn_in, w_ffn_out):
    b_sz, s_len, d = x.shape
    depth = w_ada.shape[0]
    n_q = attn_sinks.shape[1]
    attn_w = w_attn_out.shape[1]
    channels = conv_w.shape[2]
    kv_w = (w_in.shape[2] - attn_w - 2 * channels - 2 * d) // 2
    n_kv = kv_w // HEAD_DIM
    conv_col = attn_w + 2 * kv_w
    gate_col = conv_col + 2 * channels
    m = b_sz * s_len

    for l in range(depth):
        mod = _ada(c, w_ada, b_ada[l], l)
        mod3 = mod.reshape(b_sz * N_MOD, 1, d)

        qkv, h = _qkv_proj(x, norm_mix_g[l], mod3, 1, 0, w_in, l, q_norm_g[l], k_norm_g[l], n_q, n_kv)
        h = h.reshape(m, d)
        conv_in = _proj(h, w_in, l, conv_col, 2 * channels, tm=1024, tn=512)
        gates, y_conv, y_attn = _mixer_core(h, w_in, l, gate_col, conv_in, qkv, conv_w[l], conv_b[l],
                                            conv_ln_g[l], conv_ln_b[l], rel_bias, attn_sinks[l],
                                            s_len, n_q, n_kv)
        merged = _merge(y_attn, y_conv, w_attn_out, w_conv_out, l, gates, 0, d, n_q, n_kv)
        x1, h = _mix_out(merged, w_mix_out, l, x, mod3, 2, norm_ffn_g[l], 4, 3)

        act = _ffn_in(h.reshape(m, d), w_ffn_in, l)
        x = _residual_proj(act, w_ffn_out, l, x1.reshape(m, d), mod3, 5, s_len,
                           tm=256, tn=1024).reshape(b_sz, s_len, d)
    return x
```

```python
import functools
import math

import jax
import jax.numpy as jnp
from jax import lax
from jax.experimental import pallas as pl
from jax.experimental.pallas import tpu as pltpu

F32 = jnp.float32
BF16 = jnp.bfloat16

HEAD_DIM = 64
WINDOW = 128
BLOCK = 128
NUM_BUCKETS = 32
MAX_EXACT = NUM_BUCKETS // 2
MAX_DISTANCE = 128
N_MOD = 6
EPS = 1e-6
LOG2E = math.log2(math.e)

V7X_VMEM_LIMIT_BYTES = 56 * 1024 * 1024


def _params(n_axes):
    return pltpu.CompilerParams(
        dimension_semantics=("arbitrary",) * n_axes,
        vmem_limit_bytes=V7X_VMEM_LIMIT_BYTES,
    )


def _sigmoid(x):
    return jax.nn.sigmoid(x)


def _orig_head(slot, n_kv, q_per_kv):
    return (slot % n_kv) * q_per_kv + slot // n_kv


def _ada_body(c_ref, w_ref, b_ref, o_ref):
    c = c_ref[...]
    act = (c * _sigmoid(c)).astype(BF16)
    o_ref[...] = jnp.dot(act, w_ref[...].astype(BF16), preferred_element_type=F32) + b_ref[...]


def _ada(c, w_ada, b_ada, layer, tn=1024):
    b_sz, d = c.shape
    n = w_ada.shape[2]
    rows = 8
    c_pad = jnp.pad(c, ((0, rows - b_sz), (0, 0)))
    out = pl.pallas_call(
        _ada_body,
        grid=(n // tn,),
        in_specs=[
            pl.BlockSpec((rows, d), lambda j: (0, 0)),
            pl.BlockSpec((None, d, tn), lambda j: (layer, 0, j)),
            pl.BlockSpec((1, tn), lambda j: (0, j)),
        ],
        out_specs=pl.BlockSpec((rows, tn), lambda j: (0, j)),
        out_shape=jax.ShapeDtypeStruct((rows, n), F32),
        compiler_params=_params(1),
        name="ada_mod",
    )(c_pad, w_ada, b_ada.reshape(1, n))
    return out[:b_sz]


def _norm_mod_body(x_ref, g_ref, sc_ref, sh_ref, o_ref):
    x = x_ref[0]
    ms = jnp.mean(x * x, axis=-1, keepdims=True)
    y = x * lax.rsqrt(ms + EPS) * g_ref[...]
    o_ref[0] = (y * (1.0 + sc_ref[0]) + sh_ref[0]).astype(o_ref.dtype)


def _qkv_body(x_ref, g_ref, sc_ref, sh_ref, w_ref, gq_ref, gk_ref, o_ref, h_ref, wbf_ref, seg_ref,
              *, n_q, n_kv, half):
    attn_w = n_q * HEAD_DIM
    kv_w = n_kv * HEAD_DIM
    _norm_mod_body(x_ref, g_ref, sc_ref, sh_ref, h_ref)

    @pl.when(jnp.logical_and(pl.program_id(0) == 0, pl.program_id(1) == 0))
    def _():
        for slot in range(n_q):
            src = _orig_head(slot, n_kv, n_q // n_kv) * HEAD_DIM
            wbf_ref[:, slot * HEAD_DIM:(slot + 1) * HEAD_DIM] = w_ref[:, src:src + HEAD_DIM].astype(BF16)
        wbf_ref[:, attn_w:] = w_ref[:, attn_w:].astype(BF16)
        r = lax.broadcasted_iota(jnp.int32, (half, half), 0) // HEAD_DIM
        c = lax.broadcasted_iota(jnp.int32, (half, half), 1) // HEAD_DIM
        seg_ref[...] = (r == c).astype(BF16)

    a = h_ref[0]

    def head_norm(acc, seg, gain):
        ss = jnp.dot((acc * acc).astype(BF16), seg, preferred_element_type=F32)
        return acc * lax.rsqrt(ss / HEAD_DIM + EPS) * gain

    for c0 in range(0, attn_w, half):
        acc = jnp.dot(a, wbf_ref[:, c0:c0 + half], preferred_element_type=F32)
        o_ref[:, c0:c0 + half] = head_norm(acc, seg_ref[...], gq_ref[...]).astype(o_ref.dtype)
    acc = jnp.dot(a, wbf_ref[:, attn_w:attn_w + kv_w], preferred_element_type=F32)
    o_ref[:, attn_w:attn_w + kv_w] = head_norm(acc, seg_ref[:kv_w, :kv_w], gk_ref[...]).astype(o_ref.dtype)
    acc = jnp.dot(a, wbf_ref[:, attn_w + kv_w:], preferred_element_type=F32)
    o_ref[:, attn_w + kv_w:] = acc.astype(o_ref.dtype)


def _qkv_proj(x, norm_g, mod3, scale_idx, shift_idx, w_in, layer, q_g, k_g, n_q, n_kv, ts=512, half=512):
    b_sz, s_len, d = x.shape
    attn_w, kv_w = n_q * HEAD_DIM, n_kv * HEAD_DIM
    width = attn_w + 2 * kv_w
    assert attn_w % half == 0 and kv_w <= half and half % HEAD_DIM == 0
    gq = jnp.tile(q_g * (HEAD_DIM ** -0.5 * LOG2E), half // HEAD_DIM).reshape(1, half)
    gk = jnp.tile(k_g, n_kv).reshape(1, kv_w)
    tiles = s_len // ts
    const2 = lambda b, s: (0, 0)
    body = functools.partial(_qkv_body, n_q=n_q, n_kv=n_kv, half=half)
    return pl.pallas_call(
        body,
        grid=(b_sz, tiles),
        in_specs=[
            pl.BlockSpec((1, ts, d), lambda b, s: (b, s, 0)),
            pl.BlockSpec((1, d), const2),
            pl.BlockSpec((1, 1, d), lambda b, s: (b * N_MOD + scale_idx, 0, 0)),
            pl.BlockSpec((1, 1, d), lambda b, s: (b * N_MOD + shift_idx, 0, 0)),
            pl.BlockSpec((None, d, width), lambda b, s: (layer, 0, 0), pipeline_mode=pl.Buffered(1)),
            pl.BlockSpec((1, half), const2),
            pl.BlockSpec((1, kv_w), const2),
        ],
        out_specs=[
            pl.BlockSpec((ts, width), lambda b, s: (b * tiles + s, 0)),
            pl.BlockSpec((1, ts, d), lambda b, s: (b, s, 0)),
        ],
        out_shape=[
            jax.ShapeDtypeStruct((b_sz * s_len, width), BF16),
            jax.ShapeDtypeStruct((b_sz, s_len, d), BF16),
        ],
        scratch_shapes=[pltpu.VMEM((d, width), BF16), pltpu.VMEM((half, half), BF16)],
        compiler_params=_params(2),
        name="qkv_proj",
    )(x, norm_g.reshape(1, d), mod3, mod3, w_in, gq, gk)


def _cast_weights_once(w_refs, wbf_refs):
    @pl.when(pl.program_id(1) == 0)
    def _():
        for w_ref, wbf_ref in zip(w_refs, wbf_refs):
            wbf_ref[...] = w_ref[...].astype(BF16)


def _proj_body(a_ref, w_ref, o_ref, wbf_ref):
    _cast_weights_once([w_ref], [wbf_ref])
    acc = jnp.dot(a_ref[...], wbf_ref[...], preferred_element_type=F32)
    o_ref[...] = acc.astype(o_ref.dtype)


def _proj(h, w, layer, col0, n, tm, tn):
    m, k = h.shape
    assert col0 % tn == 0 and n % tn == 0
    blk0 = col0 // tn
    return pl.pallas_call(
        _proj_body,
        grid=(n // tn, m // tm),
        in_specs=[
            pl.BlockSpec((tm, k), lambda j, i: (i, 0)),
            pl.BlockSpec((None, k, tn), lambda j, i: (layer, 0, blk0 + j)),
        ],
        out_specs=pl.BlockSpec((tm, tn), lambda j, i: (i, j)),
        out_shape=jax.ShapeDtypeStruct((m, n), BF16),
        scratch_shapes=[pltpu.VMEM((k, tn), BF16)],
        compiler_params=_params(2),
        name="proj",
    )(h, w)


_EPILOGUE_LANES = 512


def _merge_body(ya_ref, yc_ref, wa_ref, wc_ref, ga_ref, gc_ref, o_ref, wabf_ref, wcbf_ref,
                *, n_q, n_kv):
    @pl.when(pl.program_id(0) == 0)
    def _():
        for slot in range(n_q):
            src = _orig_head(slot, n_kv, n_q // n_kv) * HEAD_DIM
            wabf_ref[slot * HEAD_DIM:(slot + 1) * HEAD_DIM, :] = wa_ref[src:src + HEAD_DIM, :].astype(BF16)
        wcbf_ref[...] = wc_ref[...].astype(BF16)

    ya = ya_ref[...]
    yc = yc_ref[...]
    for n0 in range(0, o_ref.shape[1], _EPILOGUE_LANES):
        ns = slice(n0, n0 + _EPILOGUE_LANES)
        acc_a = jnp.dot(ya, wabf_ref[:, ns], preferred_element_type=F32)
        acc_c = jnp.dot(yc, wcbf_ref[:, ns], preferred_element_type=F32)
        ga = _sigmoid(ga_ref[:, ns].astype(F32))
        gc = _sigmoid(gc_ref[:, ns].astype(F32))
        o_ref[:, ns] = (ga * acc_a + gc * acc_c).astype(o_ref.dtype)


def _merge(attn, conv, w_attn_out, w_conv_out, layer, gates, ga_col, gc_col, n_q, n_kv, tm=512):
    m, ka = attn.shape
    kc = conv.shape[1]
    n = w_attn_out.shape[2]
    assert ga_col % n == 0 and gc_col % n == 0 and n % _EPILOGUE_LANES == 0
    ga_blk, gc_blk = ga_col // n, gc_col // n
    body = functools.partial(_merge_body, n_q=n_q, n_kv=n_kv)
    resident = pl.Buffered(1)
    return pl.pallas_call(
        body,
        grid=(m // tm,),
        in_specs=[
            pl.BlockSpec((tm, ka), lambda i: (i, 0)),
            pl.BlockSpec((tm, kc), lambda i: (i, 0)),
            pl.BlockSpec((None, ka, n), lambda i: (layer, 0, 0), pipeline_mode=resident),
            pl.BlockSpec((None, kc, n), lambda i: (layer, 0, 0), pipeline_mode=resident),
            pl.BlockSpec((tm, n), lambda i: (i, ga_blk)),
            pl.BlockSpec((tm, n), lambda i: (i, gc_blk)),
        ],
        out_specs=pl.BlockSpec((tm, n), lambda i: (i, 0)),
        out_shape=jax.ShapeDtypeStruct((m, n), BF16),
        scratch_shapes=[pltpu.VMEM((ka, n), BF16), pltpu.VMEM((kc, n), BF16)],
        compiler_params=_params(1),
        name="merge",
    )(attn, conv, w_attn_out, w_conv_out, gates, gates)


def _mix_out_body(a_ref, w_ref, x_ref, gt_ref, g_ref, sc_ref, sh_ref, x1_ref, h_ref, wbf_ref):
    @pl.when(jnp.logical_and(pl.program_id(0) == 0, pl.program_id(1) == 0))
    def _():
        wbf_ref[...] = w_ref[...].astype(BF16)

    a = a_ref[...]
    ts, d = a.shape
    sumsq = jnp.zeros((ts, 1), F32)
    for n0 in range(0, d, _EPILOGUE_LANES):
        ns = slice(n0, n0 + _EPILOGUE_LANES)
        acc = jnp.dot(a, wbf_ref[:, ns], preferred_element_type=F32)
        x1 = x_ref[0, :, ns] + gt_ref[0, :, ns] * acc
        x1_ref[0, :, ns] = x1
        sumsq = sumsq + jnp.sum(x1 * x1, axis=-1, keepdims=True)
    inv = lax.rsqrt(sumsq / d + EPS)
    y = x1_ref[0] * inv * g_ref[...]
    h_ref[0] = (y * (1.0 + sc_ref[0]) + sh_ref[0]).astype(h_ref.dtype)


def _mix_out(a, w, layer, x, mod3, gate_idx, norm_g, scale_idx, shift_idx, ts=512):
    b_sz, s_len, d = x.shape
    k = a.shape[1]
    assert d % _EPILOGUE_LANES == 0
    tiles = s_len // ts
    const2 = lambda b, s: (0, 0)
    mod_row = lambda idx: pl.BlockSpec((1, 1, d), lambda b, s: (b * N_MOD + idx, 0, 0))
    act = pl.BlockSpec((1, ts, d), lambda b, s: (b, s, 0))
    return pl.pallas_call(
        _mix_out_body,
        grid=(b_sz, tiles),
        in_specs=[
            pl.BlockSpec((ts, k), lambda b, s: (b * tiles + s, 0)),
            pl.BlockSpec((None, k, d), lambda b, s: (layer, 0, 0), pipeline_mode=pl.Buffered(1)),
            act,
            mod_row(gate_idx),
            pl.BlockSpec((1, d), const2),
            mod_row(scale_idx),
            mod_row(shift_idx),
        ],
        out_specs=[act, act],
        out_shape=[jax.ShapeDtypeStruct((b_sz, s_len, d), F32), jax.ShapeDtypeStruct((b_sz, s_len, d), BF16)],
        scratch_shapes=[pltpu.VMEM((k, d), BF16)],
        compiler_params=_params(2),
        name="mix_out",
    )(a, w, x, mod3, norm_g.reshape(1, d), mod3, mod3)


def _residual_proj_body(a_ref, w_ref, x_ref, gt_ref, o_ref, wbf_ref):
    _cast_weights_once([w_ref], [wbf_ref])
    a = a_ref[...]
    for n0 in range(0, o_ref.shape[1], _EPILOGUE_LANES):
        ns = slice(n0, n0 + _EPILOGUE_LANES)
        acc = jnp.dot(a, wbf_ref[:, ns], preferred_element_type=F32)
        o_ref[:, ns] = x_ref[:, ns] + gt_ref[0, :, ns] * acc


def _residual_proj(a, w, layer, x2d, mod3, gate_idx, rows_per_batch, tm, tn):
    m, k = a.shape
    n = w.shape[2]
    assert n % tn == 0 and tn % _EPILOGUE_LANES == 0 and rows_per_batch % tm == 0
    tiles_per_batch = rows_per_batch // tm
    return pl.pallas_call(
        _residual_proj_body,
        grid=(n // tn, m // tm),
        in_specs=[
            pl.BlockSpec((tm, k), lambda j, i: (i, 0)),
            pl.BlockSpec((None, k, tn), lambda j, i: (layer, 0, j), pipeline_mode=pl.Buffered(1)),
            pl.BlockSpec((tm, tn), lambda j, i: (i, j)),
            pl.BlockSpec((1, 1, tn), lambda j, i: ((i // tiles_per_batch) * N_MOD + gate_idx, 0, j)),
        ],
        out_specs=pl.BlockSpec((tm, tn), lambda j, i: (i, j)),
        out_shape=jax.ShapeDtypeStruct((m, n), F32),
        scratch_shapes=[pltpu.VMEM((k, tn), BF16)],
        compiler_params=_params(2),
        name="residual_proj",
    )(a, w, x2d, mod3)


def _ffn_in_body(a_ref, wg_ref, wu_ref, o_ref, wgbf_ref, wubf_ref):
    _cast_weights_once([wg_ref, wu_ref], [wgbf_ref, wubf_ref])
    a = a_ref[...]
    gate = jnp.dot(a, wgbf_ref[...], preferred_element_type=F32)
    up = jnp.dot(a, wubf_ref[...], preferred_element_type=F32)
    o_ref[...] = (gate * _sigmoid(gate) * up).astype(o_ref.dtype)


def _ffn_in(h, w_ffn_in, layer, tm=1024, tn=512):
    m, k = h.shape
    d_ff = w_ffn_in.shape[2] // 2
    up_blk = d_ff // tn
    return pl.pallas_call(
        _ffn_in_body,
        grid=(d_ff // tn, m // tm),
        in_specs=[
            pl.BlockSpec((tm, k), lambda j, i: (i, 0)),
            pl.BlockSpec((None, k, tn), lambda j, i: (layer, 0, j)),
            pl.BlockSpec((None, k, tn), lambda j, i: (layer, 0, up_blk + j)),
        ],
        out_specs=pl.BlockSpec((tm, tn), lambda j, i: (i, j)),
        out_shape=jax.ShapeDtypeStruct((m, d_ff), BF16),
        scratch_shapes=[pltpu.VMEM((k, tn), BF16), pltpu.VMEM((k, tn), BF16)],
        compiler_params=_params(2),
        name="ffn_in",
    )(h, w_ffn_in, w_ffn_in)


def _t5_causal_bucket(dist):
    n = jnp.maximum(dist, 0)
    nf = jnp.maximum(n, 1).astype(jnp.float32)
    large = MAX_EXACT + (jnp.log(nf / MAX_EXACT) / math.log(MAX_DISTANCE / MAX_EXACT)
                         * (NUM_BUCKETS - MAX_EXACT)).astype(jnp.int32)
    large = jnp.minimum(large, NUM_BUCKETS - 1)
    return jnp.where(n < MAX_EXACT, n, large)


def _band_buckets():
    q_off = jnp.arange(BLOCK)
    k_off = jnp.arange(2 * BLOCK)
    dist = q_off[:, None] + BLOCK - k_off[None, :]
    allowed = (dist >= 0) & (dist < WINDOW)
    return jnp.where(allowed, _t5_causal_bucket(dist), -1).astype(jnp.int32)


def _attn_build_bias(first_step, rb_ref, sink_ref, bucket_ref, bias_ref, *, n_q, n_kv):
    q_per_kv = n_q // n_kv

    @pl.when(first_step)
    def _():
        bucket = bucket_ref[...]
        col = lax.broadcasted_iota(jnp.int32, bucket.shape, 1)
        for slot in range(n_q):
            head = _orig_head(slot, n_kv, q_per_kv)
            tile = jnp.zeros(bucket.shape, F32)
            for b in range(NUM_BUCKETS):
                tile = jnp.where(bucket == b, rb_ref[b * n_q + head] * LOG2E, tile)
            tile = jnp.where(bucket < 0, -jnp.inf, tile)
            sink = sink_ref[head] * LOG2E
            bias_ref[0, slot] = jnp.where(col == 0, sink, tile)
            bias_ref[1, slot] = jnp.where(col == 0, sink, jnp.where(col < BLOCK, -jnp.inf, tile))


def _attn_block(seq_start, qmask_ref, krow_ref, q_ref, kc_ref, kp_ref, vc_ref, vp_ref, o_ref,
                bias_ref, lg_ref, p_ref, pv_ref, rs_ref, *, n_q, n_kv):
    q_per_kv = n_q // n_kv
    kv_w = n_kv * HEAD_DIM
    lane_slot = lax.broadcasted_iota(jnp.int32, (BLOCK, BLOCK), 1) // HEAD_DIM
    ones = jnp.ones((2 * BLOCK, BLOCK), BF16)
    bias_idx = jnp.where(seq_start, 1, 0)

    keys = jnp.concatenate([kp_ref[...], kc_ref[...]], axis=0) * krow_ref[...]
    vals = jnp.concatenate([vp_ref[...], vc_ref[...]], axis=0) * krow_ref[...]
    lhs = jnp.concatenate(
        [q_ref[:, j * kv_w:(j + 1) * kv_w] * qmask_ref[g]
         for j in range(q_per_kv) for g in range(n_kv)], axis=0)
    lg_ref[...] = lax.dot_general(lhs, keys, (((1,), (1,)), ((), ())),
                                  preferred_element_type=F32).reshape(n_q, BLOCK, 2 * BLOCK)
    for slot in range(n_q):
        logit = lg_ref[slot] + bias_ref[bias_idx, slot]
        m = jnp.max(logit, axis=-1, keepdims=True)
        p_ref[slot * BLOCK:(slot + 1) * BLOCK, :] = jnp.exp2(logit - m).astype(BF16)
    p = p_ref[...]
    pv_ref[...] = jnp.dot(p, vals, preferred_element_type=F32).reshape(n_q, BLOCK, kv_w)
    rs_ref[...] = jnp.dot(p, ones, preferred_element_type=F32).reshape(n_q, BLOCK, BLOCK)
    per_half = BLOCK // HEAD_DIM
    for j in range(q_per_kv):
        for half in range(kv_w // BLOCK):
            lanes = slice(half * BLOCK, (half + 1) * BLOCK)
            slots = [j * n_kv + half * per_half + i for i in range(per_half)]
            num = pv_ref[slots[-1], :, lanes]
            den = rs_ref[slots[-1]]
            for i in range(per_half - 2, -1, -1):
                num = jnp.where(lane_slot == i, pv_ref[slots[i], :, lanes], num)
                den = jnp.where(lane_slot == i, rs_ref[slots[i]], den)
            o_ref[:, j * kv_w + half * BLOCK:j * kv_w + (half + 1) * BLOCK] = (
                num * (1.0 / den)).astype(o_ref.dtype)


_CONV_HALO = 32
_CONV_LANES = 256
_SUBLANES = 8


def _conv_zero_history(seq_start, u_ref):
    @pl.when(seq_start)
    def _():
        u_ref[:_CONV_HALO, :] = jnp.zeros((_CONV_HALO, u_ref.shape[1]), F32)


def _conv_block(a_ref, g_ref, w_ref, cb_ref, lg_ref, lb_ref, o_ref, u_ref, sh_ref, y_ref):
    width = w_ref.shape[0]
    ts, channels = o_ref.shape
    u_ref[_CONV_HALO:, :] = a_ref[...].astype(F32) * _sigmoid(g_ref[...].astype(F32))
    span = sh_ref.shape[1]
    for r in range(1, _SUBLANES):
        sh_ref[r - 1] = u_ref[r:r + span, :]
    first = _CONV_HALO - (width - 1)
    for c0 in range(0, channels, _CONV_LANES):
        cs = slice(c0, c0 + _CONV_LANES)
        acc = jnp.broadcast_to(cb_ref[:, cs], (ts, _CONV_LANES))
        for j in range(width):
            tile, r = divmod(first + j, _SUBLANES)
            rows = slice(tile * _SUBLANES, tile * _SUBLANES + ts)
            taps = u_ref[rows, cs] if r == 0 else sh_ref[r - 1, rows, cs]
            acc = acc + taps * w_ref[j:j + 1, cs]
        y_ref[:, cs] = acc
    y = y_ref[...]
    mu = jnp.mean(y, axis=-1, keepdims=True)
    yc = y - mu
    var = jnp.mean(yc * yc, axis=-1, keepdims=True)
    z = yc * lax.rsqrt(var + EPS) * lg_ref[...] + lb_ref[...]
    o_ref[...] = (z * _sigmoid(z)).astype(o_ref.dtype)
    u_ref[:_CONV_HALO, :] = u_ref[ts:, :]


def _core_body(h_ref, w_ref,
               a_ref, g_ref, cw_ref, cb_ref, lg_ref, lb_ref,
               rb_ref, sink_ref, bucket_ref, qmask_ref, krow_ref, q_ref, kc_ref, kp_ref, vc_ref, vp_ref,
               gates_ref, yconv_ref, yattn_ref,
               wbf_ref, u_ref, sh_ref, y_ref, bias_ref, lgt_ref, p_ref, pv_ref, rs_ref,
               *, n_q, n_kv, blocks_per_seq):
    t = pl.program_id(0) * pl.num_programs(1) + pl.program_id(1)
    seq_start = t % blocks_per_seq == 0
    _cast_weights_once([w_ref], [wbf_ref])
    _conv_zero_history(seq_start, u_ref)
    _attn_build_bias(t == 0, rb_ref, sink_ref, bucket_ref, bias_ref, n_q=n_q, n_kv=n_kv)

    acc = jnp.dot(h_ref[...], wbf_ref[...], preferred_element_type=F32)
    gates_ref[...] = acc.astype(gates_ref.dtype)
    _conv_block(a_ref, g_ref, cw_ref, cb_ref, lg_ref, lb_ref, yconv_ref, u_ref, sh_ref, y_ref)
    _attn_block(seq_start, qmask_ref, krow_ref, q_ref, kc_ref, kp_ref, vc_ref, vp_ref, yattn_ref,
                bias_ref, lgt_ref, p_ref, pv_ref, rs_ref, n_q=n_q, n_kv=n_kv)


def _mixer_core(h, w_in, layer, gate_col, conv_in, qkv, conv_w, conv_b, ln_g, ln_b, rel_bias, sinks,
                s_len, n_q, n_kv, tn=512):
    m, k = h.shape
    n = w_in.shape[2] - gate_col
    width, channels = conv_w.shape
    attn_w, kv_w = n_q * HEAD_DIM, n_kv * HEAD_DIM
    n_blocks = m // BLOCK
    n_tiles = n // tn
    assert gate_col % tn == 0 and n % tn == 0 and n_blocks % n_tiles == 0
    assert s_len % BLOCK == 0 and width - 1 <= _CONV_HALO and _CONV_HALO % _SUBLANES == 0
    assert attn_w % kv_w == 0 and kv_w == 2 * BLOCK and BLOCK % HEAD_DIM == 0
    m_tiles = n_blocks // n_tiles
    tm = m // m_tiles
    col_blk0 = gate_col // tn
    k_blk = attn_w // kv_w
    v_blk = k_blk + 1

    step = lambda j, i: j * m_tiles + i
    prev_step = lambda j, i: jnp.maximum(step(j, i) - 1, 0)
    const2 = lambda j, i: (0, 0)
    row = lambda v: v.reshape(1, channels)
    lane_group = jnp.arange(kv_w) // HEAD_DIM
    qmask = jnp.broadcast_to((lane_group[None, :] == jnp.arange(n_kv)[:, None])[:, None, :],
                             (n_kv, BLOCK, kv_w)).astype(BF16)
    krow = jnp.broadcast_to((jnp.arange(2 * BLOCK) > 0)[:, None], (2 * BLOCK, kv_w)).astype(BF16)
    body = functools.partial(_core_body, n_q=n_q, n_kv=n_kv, blocks_per_seq=s_len // BLOCK)
    return pl.pallas_call(
        body,
        grid=(n_tiles, m_tiles),
        in_specs=[
            pl.BlockSpec((tm, k), lambda j, i: (i, 0)),
            pl.BlockSpec((None, k, tn), lambda j, i: (layer, 0, col_blk0 + j)),
            pl.BlockSpec((BLOCK, channels), lambda j, i: (step(j, i), 0)),
            pl.BlockSpec((BLOCK, channels), lambda j, i: (step(j, i), 1)),
            pl.BlockSpec((width, channels), const2),
            pl.BlockSpec((1, channels), const2),
            pl.BlockSpec((1, channels), const2),
            pl.BlockSpec((1, channels), const2),
            pl.BlockSpec(memory_space=pltpu.SMEM),
            pl.BlockSpec(memory_space=pltpu.SMEM),
            pl.BlockSpec((BLOCK, 2 * BLOCK), const2),
            pl.BlockSpec((n_kv, BLOCK, kv_w), lambda j, i: (0, 0, 0)),
            pl.BlockSpec((2 * BLOCK, kv_w), const2),
            pl.BlockSpec((BLOCK, attn_w), lambda j, i: (step(j, i), 0)),
            pl.BlockSpec((BLOCK, kv_w), lambda j, i: (step(j, i), k_blk)),
            pl.BlockSpec((BLOCK, kv_w), lambda j, i: (prev_step(j, i), k_blk)),
            pl.BlockSpec((BLOCK, kv_w), lambda j, i: (step(j, i), v_blk)),
            pl.BlockSpec((BLOCK, kv_w), lambda j, i: (prev_step(j, i), v_blk)),
        ],
        out_specs=[
            pl.BlockSpec((tm, tn), lambda j, i: (i, j)),
            pl.BlockSpec((BLOCK, channels), lambda j, i: (step(j, i), 0)),
            pl.BlockSpec((BLOCK, attn_w), lambda j, i: (step(j, i), 0)),
        ],
        out_shape=[
            jax.ShapeDtypeStruct((m, n), BF16),
            jax.ShapeDtypeStruct((m, channels), BF16),
            jax.ShapeDtypeStruct((m, attn_w), BF16),
        ],
        scratch_shapes=[
            pltpu.VMEM((k, tn), BF16),
            pltpu.VMEM((_CONV_HALO + BLOCK, channels), F32),
            pltpu.VMEM((_SUBLANES - 1, _CONV_HALO - _SUBLANES + BLOCK, channels), F32),
            pltpu.VMEM((BLOCK, channels), F32),
            pltpu.VMEM((2, n_q, BLOCK, 2 * BLOCK), F32),
            pltpu.VMEM((n_q, BLOCK, 2 * BLOCK), F32),
            pltpu.VMEM((n_q * BLOCK, 2 * BLOCK), BF16),
            pltpu.VMEM((n_q, BLOCK, kv_w), F32),
            pltpu.VMEM((n_q, BLOCK, BLOCK), F32),
        ],
        compiler_params=_params(2),
        name="mixer_core",
    )(h, w_in,
      conv_in, conv_in, conv_w, row(conv_b), row(ln_g), row(ln_b),
      rel_bias.astype(F32).reshape(-1), sinks.astype(F32), _band_buckets(), qmask, krow,
      qkv, qkv, qkv, qkv, qkv)


def kernel(x, c, w_ada, b_ada, norm_mix_g, w_in, q_norm_g, k_norm_g, attn_sinks, rel_bias,
           w_attn_out, conv_w, conv_b, conv_ln_g, conv_ln_b, w_conv_out, w_mix_out,
           norm_ffn_g, w_ffn_in, w_ffn_out):
    b_sz, s_len, d = x.shape
    depth = w_ada.shape[0]
    n_q = attn_sinks.shape[1]
    attn_w = w_attn_out.shape[1]
    channels = conv_w.shape[2]
    kv_w = (w_in.shape[2] - attn_w - 2 * channels - 2 * d) // 2
    n_kv = kv_w // HEAD_DIM
    conv_col = attn_w + 2 * kv_w
    gate_col = conv_col + 2 * channels
    m = b_sz * s_len

    for l in range(depth):
        mod = _ada(c, w_ada, b_ada[l], l)
        mod3 = mod.reshape(b_sz * N_MOD, 1, d)

        qkv, h = _qkv_proj(x, norm_mix_g[l], mod3, 1, 0, w_in, l, q_norm_g[l], k_norm_g[l], n_q, n_kv)
        h = h.reshape(m, d)
        conv_in = _proj(h, w_in, l, conv_col, 2 * channels, tm=1024, tn=512)
        gates, y_conv, y_attn = _mixer_core(h, w_in, l, gate_col, conv_in, qkv, conv_w[l], conv_b[l],
                                            conv_ln_g[l], conv_ln_b[l], rel_bias, attn_sinks[l],
                                            s_len, n_q, n_kv)
        merged = _merge(y_attn, y_conv, w_attn_out, w_conv_out, l, gates, 0, d, n_q, n_kv)
        x1, h = _mix_out(merged, w_mix_out, l, x, mod3, 2, norm_ffn_g[l], 4, 3)

        act = _ffn_in(h.reshape(m, d), w_ffn_in, l)
        x = _residual_proj(act, w_ffn_out, l, x1.reshape(m, d), mod3, 5, s_len,
                           tm=256, tn=1024).reshape(b_sz, s_len, d)
    return x
```

```python
import functools
import math

import jax
import jax.numpy as jnp
from jax import lax
from jax.experimental import pallas as pl
from jax.experimental.pallas import tpu as pltpu

F32 = jnp.float32
BF16 = jnp.bfloat16

HEAD_DIM = 64
WINDOW = 128
BLOCK = 128
NUM_BUCKETS = 32
MAX_EXACT = NUM_BUCKETS // 2
MAX_DISTANCE = 128
N_MOD = 6
EPS = 1e-6
LOG2E = math.log2(math.e)

V7X_VMEM_LIMIT_BYTES = 56 * 1024 * 1024


def _params(n_axes):
    return pltpu.CompilerParams(
        dimension_semantics=("arbitrary",) * n_axes,
        vmem_limit_bytes=V7X_VMEM_LIMIT_BYTES,
    )


def _sigmoid(x):
    return jax.nn.sigmoid(x)


def _orig_head(slot, n_kv, q_per_kv):
    return (slot % n_kv) * q_per_kv + slot // n_kv


def _ada_body(c_ref, w_ref, b_ref, o_ref):
    c = c_ref[...]
    act = (c * _sigmoid(c)).astype(BF16)
    o_ref[...] = jnp.dot(act, w_ref[...].astype(BF16), preferred_element_type=F32) + b_ref[...]


def _ada(c, w_ada, b_ada, layer, tn=1024):
    b_sz, d = c.shape
    n = w_ada.shape[2]
    rows = 8
    c_pad = jnp.pad(c, ((0, rows - b_sz), (0, 0)))
    out = pl.pallas_call(
        _ada_body,
        grid=(n // tn,),
        in_specs=[
            pl.BlockSpec((rows, d), lambda j: (0, 0)),
            pl.BlockSpec((None, d, tn), lambda j: (layer, 0, j)),
            pl.BlockSpec((1, tn), lambda j: (0, j)),
        ],
        out_specs=pl.BlockSpec((rows, tn), lambda j: (0, j)),
        out_shape=jax.ShapeDtypeStruct((rows, n), F32),
        compiler_params=_params(1),
        name="ada_mod",
    )(c_pad, w_ada, b_ada.reshape(1, n))
    return out[:b_sz]


_CHAIN_ROWS = 128


def _qkv_body(x_ref, g_ref, sc_ref, sh_ref, w_ref, gq_ref, gk_ref, o_ref, h_ref, wbf_ref, seg_ref,
              *, n_q, n_kv, half):
    attn_w = n_q * HEAD_DIM
    kv_w = n_kv * HEAD_DIM

    @pl.when(jnp.logical_and(pl.program_id(0) == 0, pl.program_id(1) == 0))
    def _():
        for slot in range(n_q):
            src = _orig_head(slot, n_kv, n_q // n_kv) * HEAD_DIM
            wbf_ref[:, slot * HEAD_DIM:(slot + 1) * HEAD_DIM] = w_ref[:, src:src + HEAD_DIM].astype(BF16)
        wbf_ref[:, attn_w:] = w_ref[:, attn_w:].astype(BF16)
        r = lax.broadcasted_iota(jnp.int32, (half, half), 0) // HEAD_DIM
        c = lax.broadcasted_iota(jnp.int32, (half, half), 1) // HEAD_DIM
        seg_ref[...] = (r == c).astype(BF16)

    def head_norm(acc, seg, gain):
        ss = jnp.dot((acc * acc).astype(BF16), seg, preferred_element_type=F32)
        return acc * lax.rsqrt(ss / HEAD_DIM + EPS) * gain

    for r0 in range(0, x_ref.shape[1], _CHAIN_ROWS):
        rs = slice(r0, r0 + _CHAIN_ROWS)
        x = x_ref[0, rs, :]
        ms = jnp.mean(x * x, axis=-1, keepdims=True)
        y = x * lax.rsqrt(ms + EPS) * g_ref[...]
        a = (y * (1.0 + sc_ref[0]) + sh_ref[0]).astype(BF16)
        h_ref[0, rs, :] = a
        for c0 in range(0, attn_w, half):
            acc = jnp.dot(a, wbf_ref[:, c0:c0 + half], preferred_element_type=F32)
            o_ref[rs, c0:c0 + half] = head_norm(acc, seg_ref[...], gq_ref[...]).astype(o_ref.dtype)
        acc = jnp.dot(a, wbf_ref[:, attn_w:attn_w + kv_w], preferred_element_type=F32)
        o_ref[rs, attn_w:attn_w + kv_w] = head_norm(acc, seg_ref[:kv_w, :kv_w], gk_ref[...]).astype(o_ref.dtype)
        acc = jnp.dot(a, wbf_ref[:, attn_w + kv_w:], preferred_element_type=F32)
        o_ref[rs, attn_w + kv_w:] = acc.astype(o_ref.dtype)


def _qkv_proj(x, norm_g, mod3, scale_idx, shift_idx, w_in, layer, q_g, k_g, n_q, n_kv, ts=512, half=512):
    b_sz, s_len, d = x.shape
    attn_w, kv_w = n_q * HEAD_DIM, n_kv * HEAD_DIM
    width = attn_w + 2 * kv_w
    assert attn_w % half == 0 and kv_w <= half and half % HEAD_DIM == 0 and ts % _CHAIN_ROWS == 0
    gq = jnp.tile(q_g * (HEAD_DIM ** -0.5 * LOG2E), half // HEAD_DIM).reshape(1, half)
    gk = jnp.tile(k_g, n_kv).reshape(1, kv_w)
    tiles = s_len // ts
    const2 = lambda b, s: (0, 0)
    body = functools.partial(_qkv_body, n_q=n_q, n_kv=n_kv, half=half)
    return pl.pallas_call(
        body,
        grid=(b_sz, tiles),
        in_specs=[
            pl.BlockSpec((1, ts, d), lambda b, s: (b, s, 0)),
            pl.BlockSpec((1, d), const2),
            pl.BlockSpec((1, 1, d), lambda b, s: (b * N_MOD + scale_idx, 0, 0)),
            pl.BlockSpec((1, 1, d), lambda b, s: (b * N_MOD + shift_idx, 0, 0)),
            pl.BlockSpec((None, d, width), lambda b, s: (layer, 0, 0), pipeline_mode=pl.Buffered(1)),
            pl.BlockSpec((1, half), const2),
            pl.BlockSpec((1, kv_w), const2),
        ],
        out_specs=[
            pl.BlockSpec((ts, width), lambda b, s: (b * tiles + s, 0)),
            pl.BlockSpec((1, ts, d), lambda b, s: (b, s, 0)),
        ],
        out_shape=[
            jax.ShapeDtypeStruct((b_sz * s_len, width), BF16),
            jax.ShapeDtypeStruct((b_sz, s_len, d), BF16),
        ],
        scratch_shapes=[pltpu.VMEM((d, width), BF16), pltpu.VMEM((half, half), BF16)],
        compiler_params=_params(2),
        name="qkv_proj",
    )(x, norm_g.reshape(1, d), mod3, mod3, w_in, gq, gk)


def _cast_weights_once(w_refs, wbf_refs):
    @pl.when(pl.program_id(1) == 0)
    def _():
        for w_ref, wbf_ref in zip(w_refs, wbf_refs):
            wbf_ref[...] = w_ref[...].astype(BF16)


def _proj_body(a_ref, w_ref, o_ref, wbf_ref):
    _cast_weights_once([w_ref], [wbf_ref])
    for r0 in range(0, a_ref.shape[0], _MATMUL_ROWS):
        rs = slice(r0, r0 + _MATMUL_ROWS)
        acc = jnp.dot(a_ref[rs, :], wbf_ref[...], preferred_element_type=F32)
        o_ref[rs, :] = acc.astype(o_ref.dtype)


def _proj(h, w, layer, col0, n, tm, tn):
    m, k = h.shape
    assert col0 % tn == 0 and n % tn == 0
    blk0 = col0 // tn
    return pl.pallas_call(
        _proj_body,
        grid=(n // tn, m // tm),
        in_specs=[
            pl.BlockSpec((tm, k), lambda j, i: (i, 0)),
            pl.BlockSpec((None, k, tn), lambda j, i: (layer, 0, blk0 + j)),
        ],
        out_specs=pl.BlockSpec((tm, tn), lambda j, i: (i, j)),
        out_shape=jax.ShapeDtypeStruct((m, n), BF16),
        scratch_shapes=[pltpu.VMEM((k, tn), BF16)],
        compiler_params=_params(2),
        name="proj",
    )(h, w)


_EPILOGUE_LANES = 512
_MATMUL_ROWS = 1024


def _out_weights_body(wa_ref, wc_ref, wabf_ref, wcbf_ref):
    wabf_ref[...] = wa_ref[...].astype(BF16)
    wcbf_ref[...] = wc_ref[...].astype(BF16)


def _prep_out_weights(w_attn_out, w_conv_out, layer, n_q, n_kv):
    attn_w, d = w_attn_out.shape[1:]
    channels = w_conv_out.shape[1]
    assert attn_w == n_q * HEAD_DIM and channels % n_q == 0
    conv_rows = channels // n_q
    q_per_kv = n_q // n_kv
    return pl.pallas_call(
        _out_weights_body,
        grid=(n_q,),
        in_specs=[
            pl.BlockSpec((None, HEAD_DIM, d), lambda slot: (layer, _orig_head(slot, n_kv, q_per_kv), 0)),
            pl.BlockSpec((None, conv_rows, d), lambda slot: (layer, slot, 0)),
        ],
        out_specs=[
            pl.BlockSpec((HEAD_DIM, d), lambda slot: (slot, 0)),
            pl.BlockSpec((conv_rows, d), lambda slot: (slot, 0)),
        ],
        out_shape=[jax.ShapeDtypeStruct((attn_w, d), BF16), jax.ShapeDtypeStruct((channels, d), BF16)],
        compiler_params=_params(1),
        name="prep_out_weights",
    )(w_attn_out, w_conv_out)


def _mix_out_body(za_ref, zc_ref, ga_ref, gc_ref, w_ref, x_ref, gt_ref, g_ref, sc_ref, sh_ref,
                  x1_ref, h_ref, wbf_ref):
    @pl.when(jnp.logical_and(pl.program_id(0) == 0, pl.program_id(1) == 0))
    def _():
        wbf_ref[...] = w_ref[...].astype(BF16)

    ts, d = za_ref.shape
    for r0 in range(0, ts, _CHAIN_ROWS):
        rs = slice(r0, r0 + _CHAIN_ROWS)
        merged = (_sigmoid(ga_ref[rs, :].astype(F32)) * za_ref[rs, :].astype(F32)
                  + _sigmoid(gc_ref[rs, :].astype(F32)) * zc_ref[rs, :].astype(F32)).astype(BF16)
        sumsq = jnp.zeros((_CHAIN_ROWS, 1), F32)
        for n0 in range(0, d, _EPILOGUE_LANES):
            ns = slice(n0, n0 + _EPILOGUE_LANES)
            acc = jnp.dot(merged, wbf_ref[:, ns], preferred_element_type=F32)
            x1 = x_ref[0, rs, ns] + gt_ref[0, :, ns] * acc
            x1_ref[0, rs, ns] = x1
            sumsq = sumsq + jnp.sum(x1 * x1, axis=-1, keepdims=True)
        inv = lax.rsqrt(sumsq / d + EPS)
        y = x1_ref[0, rs, :] * inv * g_ref[...]
        h_ref[0, rs, :] = (y * (1.0 + sc_ref[0]) + sh_ref[0]).astype(h_ref.dtype)


def _mix_out(z_attn, z_conv, gates, ga_col, gc_col, w, layer, x, mod3, gate_idx, norm_g, scale_idx,
             shift_idx, ts=256):
    b_sz, s_len, d = x.shape
    assert d % _EPILOGUE_LANES == 0 and ts % _CHAIN_ROWS == 0 and ga_col % d == 0 and gc_col % d == 0
    tiles = s_len // ts
    const2 = lambda b, s: (0, 0)
    mod_row = lambda idx: pl.BlockSpec((1, 1, d), lambda b, s: (b * N_MOD + idx, 0, 0))
    act = pl.BlockSpec((1, ts, d), lambda b, s: (b, s, 0))
    rows = lambda col_blk: pl.BlockSpec((ts, d), lambda b, s: (b * tiles + s, col_blk))
    return pl.pallas_call(
        _mix_out_body,
        grid=(b_sz, tiles),
        in_specs=[
            rows(0),
            rows(0),
            rows(ga_col // d),
            rows(gc_col // d),
            pl.BlockSpec((None, d, d), lambda b, s: (layer, 0, 0), pipeline_mode=pl.Buffered(1)),
            act,
            mod_row(gate_idx),
            pl.BlockSpec((1, d), const2),
            mod_row(scale_idx),
            mod_row(shift_idx),
        ],
        out_specs=[act, act],
        out_shape=[jax.ShapeDtypeStruct((b_sz, s_len, d), F32), jax.ShapeDtypeStruct((b_sz, s_len, d), BF16)],
        scratch_shapes=[pltpu.VMEM((d, d), BF16)],
        compiler_params=_params(2),
        name="mix_out",
    )(z_attn, z_conv, gates, gates, w, x, mod3, norm_g.reshape(1, d), mod3, mod3)


def _residual_proj_body(a_ref, w_ref, x_ref, gt_ref, o_ref, wbf_ref):
    _cast_weights_once([w_ref], [wbf_ref])
    a = a_ref[...]
    for n0 in range(0, o_ref.shape[1], _EPILOGUE_LANES):
        ns = slice(n0, n0 + _EPILOGUE_LANES)
        acc = jnp.dot(a, wbf_ref[:, ns], preferred_element_type=F32)
        o_ref[:, ns] = x_ref[:, ns] + gt_ref[0, :, ns] * acc


def _residual_proj(a, w, layer, x2d, mod3, gate_idx, rows_per_batch, tm, tn):
    m, k = a.shape
    n = w.shape[2]
    assert n % tn == 0 and tn % _EPILOGUE_LANES == 0 and rows_per_batch % tm == 0
    tiles_per_batch = rows_per_batch // tm
    return pl.pallas_call(
        _residual_proj_body,
        grid=(n // tn, m // tm),
        in_specs=[
            pl.BlockSpec((tm, k), lambda j, i: (i, 0)),
            pl.BlockSpec((None, k, tn), lambda j, i: (layer, 0, j), pipeline_mode=pl.Buffered(1)),
            pl.BlockSpec((tm, tn), lambda j, i: (i, j)),
            pl.BlockSpec((1, 1, tn), lambda j, i: ((i // tiles_per_batch) * N_MOD + gate_idx, 0, j)),
        ],
        out_specs=pl.BlockSpec((tm, tn), lambda j, i: (i, j)),
        out_shape=jax.ShapeDtypeStruct((m, n), F32),
        scratch_shapes=[pltpu.VMEM((k, tn), BF16)],
        compiler_params=_params(2),
        name="residual_proj",
    )(a, w, x2d, mod3)


def _ffn_in_body(a_ref, wg_ref, wu_ref, o_ref, wgbf_ref, wubf_ref):
    _cast_weights_once([wg_ref, wu_ref], [wgbf_ref, wubf_ref])
    for r0 in range(0, a_ref.shape[0], _MATMUL_ROWS):
        rs = slice(r0, r0 + _MATMUL_ROWS)
        a = a_ref[rs, :]
        gate = jnp.dot(a, wgbf_ref[...], preferred_element_type=F32)
        up = jnp.dot(a, wubf_ref[...], preferred_element_type=F32)
        o_ref[rs, :] = (gate * _sigmoid(gate) * up).astype(o_ref.dtype)


def _ffn_in(h, w_ffn_in, layer, tm=2048, tn=512):
    m, k = h.shape
    d_ff = w_ffn_in.shape[2] // 2
    up_blk = d_ff // tn
    return pl.pallas_call(
        _ffn_in_body,
        grid=(d_ff // tn, m // tm),
        in_specs=[
            pl.BlockSpec((tm, k), lambda j, i: (i, 0)),
            pl.BlockSpec((None, k, tn), lambda j, i: (layer, 0, j)),
            pl.BlockSpec((None, k, tn), lambda j, i: (layer, 0, up_blk + j)),
        ],
        out_specs=pl.BlockSpec((tm, tn), lambda j, i: (i, j)),
        out_shape=jax.ShapeDtypeStruct((m, d_ff), BF16),
        scratch_shapes=[pltpu.VMEM((k, tn), BF16), pltpu.VMEM((k, tn), BF16)],
        compiler_params=_params(2),
        name="ffn_in",
    )(h, w_ffn_in, w_ffn_in)


def _t5_causal_bucket(dist):
    n = jnp.maximum(dist, 0)
    nf = jnp.maximum(n, 1).astype(jnp.float32)
    large = MAX_EXACT + (jnp.log(nf / MAX_EXACT) / math.log(MAX_DISTANCE / MAX_EXACT)
                         * (NUM_BUCKETS - MAX_EXACT)).astype(jnp.int32)
    large = jnp.minimum(large, NUM_BUCKETS - 1)
    return jnp.where(n < MAX_EXACT, n, large)


def _band_buckets():
    q_off = jnp.arange(BLOCK)
    k_off = jnp.arange(2 * BLOCK)
    dist = q_off[:, None] + BLOCK - k_off[None, :]
    allowed = (dist >= 0) & (dist < WINDOW)
    return jnp.where(allowed, _t5_causal_bucket(dist), -1).astype(jnp.int32)


def _attn_build_bias(first_step, rb_ref, sink_ref, bucket_ref, bias_ref, *, n_q, n_kv):
    q_per_kv = n_q // n_kv

    @pl.when(first_step)
    def _():
        bucket = bucket_ref[...]
        col = lax.broadcasted_iota(jnp.int32, bucket.shape, 1)
        for slot in range(n_q):
            head = _orig_head(slot, n_kv, q_per_kv)
            tile = jnp.zeros(bucket.shape, F32)
            for b in range(NUM_BUCKETS):
                tile = jnp.where(bucket == b, rb_ref[b * n_q + head] * LOG2E, tile)
            tile = jnp.where(bucket < 0, -jnp.inf, tile)
            sink = sink_ref[head] * LOG2E
            bias_ref[0, slot] = jnp.where(col == 0, sink, tile)
            bias_ref[1, slot] = jnp.where(col == 0, sink, jnp.where(col < BLOCK, -jnp.inf, tile))


def _attn_block(seq_start, qmask_ref, krow_ref, q_ref, kc_ref, kp_ref, vc_ref, vp_ref, o_ref,
                bias_ref, lg_ref, p_ref, pv_ref, rs_ref, *, n_q, n_kv):
    q_per_kv = n_q // n_kv
    kv_w = n_kv * HEAD_DIM
    lane_slot = lax.broadcasted_iota(jnp.int32, (BLOCK, BLOCK), 1) // HEAD_DIM
    ones = jnp.ones((2 * BLOCK, BLOCK), BF16)
    bias_idx = jnp.where(seq_start, 1, 0)

    keys = jnp.concatenate([kp_ref[...], kc_ref[...]], axis=0) * krow_ref[...]
    vals = jnp.concatenate([vp_ref[...], vc_ref[...]], axis=0) * krow_ref[...]
    lhs = jnp.concatenate(
        [q_ref[:, j * kv_w:(j + 1) * kv_w] * qmask_ref[g]
         for j in range(q_per_kv) for g in range(n_kv)], axis=0)
    lg_ref[...] = lax.dot_general(lhs, keys, (((1,), (1,)), ((), ())),
                                  preferred_element_type=F32).reshape(n_q, BLOCK, 2 * BLOCK)
    for slot in range(n_q):
        logit = lg_ref[slot] + bias_ref[bias_idx, slot]
        m = jnp.max(logit, axis=-1, keepdims=True)
        p_ref[slot * BLOCK:(slot + 1) * BLOCK, :] = jnp.exp2(logit - m).astype(BF16)
    p = p_ref[...]
    pv_ref[...] = jnp.dot(p, vals, preferred_element_type=F32).reshape(n_q, BLOCK, kv_w)
    rs_ref[...] = jnp.dot(p, ones, preferred_element_type=F32).reshape(n_q, BLOCK, BLOCK)
    per_half = BLOCK // HEAD_DIM
    for j in range(q_per_kv):
        for half in range(kv_w // BLOCK):
            lanes = slice(half * BLOCK, (half + 1) * BLOCK)
            slots = [j * n_kv + half * per_half + i for i in range(per_half)]
            num = pv_ref[slots[-1], :, lanes]
            den = rs_ref[slots[-1]]
            for i in range(per_half - 2, -1, -1):
                num = jnp.where(lane_slot == i, pv_ref[slots[i], :, lanes], num)
                den = jnp.where(lane_slot == i, rs_ref[slots[i]], den)
            o_ref[:, j * kv_w + half * BLOCK:j * kv_w + (half + 1) * BLOCK] = (
                num * (1.0 / den)).astype(o_ref.dtype)


_CONV_HALO = 32
_CONV_LANES = 256
_SUBLANES = 8


def _conv_zero_history(seq_start, u_ref):
    @pl.when(seq_start)
    def _():
        u_ref[:_CONV_HALO, :] = jnp.zeros((_CONV_HALO, u_ref.shape[1]), F32)


def _conv_block(a_ref, g_ref, w_ref, cb_ref, lg_ref, lb_ref, o_ref, u_ref, sh_ref, y_ref):
    width = w_ref.shape[0]
    ts, channels = o_ref.shape
    u_ref[_CONV_HALO:, :] = a_ref[...].astype(F32) * _sigmoid(g_ref[...].astype(F32))
    span = sh_ref.shape[1]
    for r in range(1, _SUBLANES):
        sh_ref[r - 1] = u_ref[r:r + span, :]
    first = _CONV_HALO - (width - 1)
    for c0 in range(0, channels, _CONV_LANES):
        cs = slice(c0, c0 + _CONV_LANES)
        acc = jnp.broadcast_to(cb_ref[:, cs], (ts, _CONV_LANES))
        for j in range(width):
            tile, r = divmod(first + j, _SUBLANES)
            rows = slice(tile * _SUBLANES, tile * _SUBLANES + ts)
            taps = u_ref[rows, cs] if r == 0 else sh_ref[r - 1, rows, cs]
            acc = acc + taps * w_ref[j:j + 1, cs]
        y_ref[:, cs] = acc
    y = y_ref[...]
    mu = jnp.mean(y, axis=-1, keepdims=True)
    yc = y - mu
    var = jnp.mean(yc * yc, axis=-1, keepdims=True)
    z = yc * lax.rsqrt(var + EPS) * lg_ref[...] + lb_ref[...]
    o_ref[...] = (z * _sigmoid(z)).astype(o_ref.dtype)
    u_ref[:_CONV_HALO, :] = u_ref[ts:, :]


def _core_body(h_ref, w_ref,
               a_ref, g_ref, cw_ref, cb_ref, lg_ref, lb_ref, wc_ref,
               rb_ref, sink_ref, bucket_ref, qmask_ref, krow_ref, q_ref, kc_ref, kp_ref, vc_ref, vp_ref, wa_ref,
               gates_ref, zconv_ref, zattn_ref,
               wbf_ref, u_ref, sh_ref, y_ref, yconv_ref, yattn_ref, bias_ref, lgt_ref, p_ref, pv_ref, rs_ref,
               *, n_q, n_kv, blocks_per_seq):
    t = pl.program_id(0) * pl.num_programs(1) + pl.program_id(1)
    seq_start = t % blocks_per_seq == 0
    _cast_weights_once([w_ref], [wbf_ref])
    _conv_zero_history(seq_start, u_ref)
    _attn_build_bias(t == 0, rb_ref, sink_ref, bucket_ref, bias_ref, n_q=n_q, n_kv=n_kv)

    acc = jnp.dot(h_ref[...], wbf_ref[...], preferred_element_type=F32)
    gates_ref[...] = acc.astype(gates_ref.dtype)
    _conv_block(a_ref, g_ref, cw_ref, cb_ref, lg_ref, lb_ref, yconv_ref, u_ref, sh_ref, y_ref)
    zconv_ref[...] = jnp.dot(yconv_ref[...], wc_ref[...], preferred_element_type=F32).astype(zconv_ref.dtype)
    _attn_block(seq_start, qmask_ref, krow_ref, q_ref, kc_ref, kp_ref, vc_ref, vp_ref, yattn_ref,
                bias_ref, lgt_ref, p_ref, pv_ref, rs_ref, n_q=n_q, n_kv=n_kv)
    zattn_ref[...] = jnp.dot(yattn_ref[...], wa_ref[...], preferred_element_type=F32).astype(zattn_ref.dtype)


def _mixer_core(h, w_in, layer, gate_col, conv_in, qkv, conv_w, conv_b, ln_g, ln_b, wc_bf, rel_bias, sinks,
                wa_bf, s_len, n_q, n_kv, tn=512):
    m, k = h.shape
    n = w_in.shape[2] - gate_col
    d_out = wc_bf.shape[1]
    width, channels = conv_w.shape
    attn_w, kv_w = n_q * HEAD_DIM, n_kv * HEAD_DIM
    n_blocks = m // BLOCK
    n_tiles = n // tn
    assert gate_col % tn == 0 and n % tn == 0 and n_blocks % n_tiles == 0
    assert s_len % BLOCK == 0 and width - 1 <= _CONV_HALO and _CONV_HALO % _SUBLANES == 0
    assert attn_w % kv_w == 0 and kv_w == 2 * BLOCK and BLOCK % HEAD_DIM == 0
    m_tiles = n_blocks // n_tiles
    tm = m // m_tiles
    col_blk0 = gate_col // tn
    k_blk = attn_w // kv_w
    v_blk = k_blk + 1

    step = lambda j, i: j * m_tiles + i
    prev_step = lambda j, i: jnp.maximum(step(j, i) - 1, 0)
    const2 = lambda j, i: (0, 0)
    row = lambda v: v.reshape(1, channels)
    lane_group = jnp.arange(kv_w) // HEAD_DIM
    qmask = jnp.broadcast_to((lane_group[None, :] == jnp.arange(n_kv)[:, None])[:, None, :],
                             (n_kv, BLOCK, kv_w)).astype(BF16)
    krow = jnp.broadcast_to((jnp.arange(2 * BLOCK) > 0)[:, None], (2 * BLOCK, kv_w)).astype(BF16)
    body = functools.partial(_core_body, n_q=n_q, n_kv=n_kv, blocks_per_seq=s_len // BLOCK)
    return pl.pallas_call(
        body,
        grid=(n_tiles, m_tiles),
        in_specs=[
            pl.BlockSpec((tm, k), lambda j, i: (i, 0)),
            pl.BlockSpec((None, k, tn), lambda j, i: (layer, 0, col_blk0 + j)),
            pl.BlockSpec((BLOCK, channels), lambda j, i: (step(j, i), 0)),
            pl.BlockSpec((BLOCK, channels), lambda j, i: (step(j, i), 1)),
            pl.BlockSpec((width, channels), const2),
            pl.BlockSpec((1, channels), const2),
            pl.BlockSpec((1, channels), const2),
            pl.BlockSpec((1, channels), const2),
            pl.BlockSpec((channels, d_out), const2, pipeline_mode=pl.Buffered(1)),
            pl.BlockSpec(memory_space=pltpu.SMEM),
            pl.BlockSpec(memory_space=pltpu.SMEM),
            pl.BlockSpec((BLOCK, 2 * BLOCK), const2),
            pl.BlockSpec((n_kv, BLOCK, kv_w), lambda j, i: (0, 0, 0)),
            pl.BlockSpec((2 * BLOCK, kv_w), const2),
            pl.BlockSpec((BLOCK, attn_w), lambda j, i: (step(j, i), 0)),
            pl.BlockSpec((BLOCK, kv_w), lambda j, i: (step(j, i), k_blk)),
            pl.BlockSpec((BLOCK, kv_w), lambda j, i: (prev_step(j, i), k_blk)),
            pl.BlockSpec((BLOCK, kv_w), lambda j, i: (step(j, i), v_blk)),
            pl.BlockSpec((BLOCK, kv_w), lambda j, i: (prev_step(j, i), v_blk)),
            pl.BlockSpec((attn_w, d_out), const2, pipeline_mode=pl.Buffered(1)),
        ],
        out_specs=[
            pl.BlockSpec((tm, tn), lambda j, i: (i, j)),
            pl.BlockSpec((BLOCK, d_out), lambda j, i: (step(j, i), 0)),
            pl.BlockSpec((BLOCK, d_out), lambda j, i: (step(j, i), 0)),
        ],
        out_shape=[
            jax.ShapeDtypeStruct((m, n), BF16),
            jax.ShapeDtypeStruct((m, d_out), BF16),
            jax.ShapeDtypeStruct((m, d_out), BF16),
        ],
        scratch_shapes=[
            pltpu.VMEM((k, tn), BF16),
            pltpu.VMEM((_CONV_HALO + BLOCK, channels), F32),
            pltpu.VMEM((_SUBLANES - 1, _CONV_HALO - _SUBLANES + BLOCK, channels), F32),
            pltpu.VMEM((BLOCK, channels), F32),
            pltpu.VMEM((BLOCK, channels), BF16),
            pltpu.VMEM((BLOCK, attn_w), BF16),
            pltpu.VMEM((2, n_q, BLOCK, 2 * BLOCK), F32),
            pltpu.VMEM((n_q, BLOCK, 2 * BLOCK), F32),
            pltpu.VMEM((n_q * BLOCK, 2 * BLOCK), BF16),
            pltpu.VMEM((n_q, BLOCK, kv_w), F32),
            pltpu.VMEM((n_q, BLOCK, BLOCK), F32),
        ],
        compiler_params=_params(2),
        name="mixer_core",
    )(h, w_in,
      conv_in, conv_in, conv_w, row(conv_b), row(ln_g), row(ln_b), wc_bf,
      rel_bias.astype(F32).reshape(-1), sinks.astype(F32), _band_buckets(), qmask, krow,
      qkv, qkv, qkv, qkv, qkv, wa_bf)


def kernel(x, c, w_ada, b_ada, norm_mix_g, w_in, q_norm_g, k_norm_g, attn_sinks, rel_bias,
           w_attn_out, conv_w, conv_b, conv_ln_g, conv_ln_b, w_conv_out, w_mix_out,
           norm_ffn_g, w_ffn_in, w_ffn_out):
    b_sz, s_len, d = x.shape
    depth = w_ada.shape[0]
    n_q = attn_sinks.shape[1]
    attn_w = w_attn_out.shape[1]
    channels = conv_w.shape[2]
    kv_w = (w_in.shape[2] - attn_w - 2 * channels - 2 * d) // 2
    n_kv = kv_w // HEAD_DIM
    conv_col = attn_w + 2 * kv_w
    gate_col = conv_col + 2 * channels
    m = b_sz * s_len

    for l in range(depth):
        mod = _ada(c, w_ada, b_ada[l], l)
        mod3 = mod.reshape(b_sz * N_MOD, 1, d)

        qkv, h = _qkv_proj(x, norm_mix_g[l], mod3, 1, 0, w_in, l, q_norm_g[l], k_norm_g[l], n_q, n_kv)
        h = h.reshape(m, d)
        conv_in = _proj(h, w_in, l, conv_col, 2 * channels, tm=2048, tn=512)
        wa_bf, wc_bf = _prep_out_weights(w_attn_out, w_conv_out, l, n_q, n_kv)
        gates, z_conv, z_attn = _mixer_core(h, w_in, l, gate_col, conv_in, qkv, conv_w[l], conv_b[l],
                                            conv_ln_g[l], conv_ln_b[l], wc_bf, rel_bias, attn_sinks[l],
                                            wa_bf, s_len, n_q, n_kv)
        x1, h = _mix_out(z_attn, z_conv, gates, 0, d, w_mix_out, l, x, mod3, 2, norm_ffn_g[l], 4, 3)

        act = _ffn_in(h.reshape(m, d), w_ffn_in, l)
        x = _residual_proj(act, w_ffn_out, l, x1.reshape(m, d), mod3, 5, s_len,
                           tm=256, tn=1024).reshape(b_sz, s_len, d)
    return x
```

```python
import functools
import math

import jax
import jax.numpy as jnp
from jax import lax
from jax.experimental import pallas as pl
from jax.experimental.pallas import tpu as pltpu

F32 = jnp.float32
BF16 = jnp.bfloat16

HEAD_DIM = 64
WINDOW = 128
BLOCK = 128
NUM_BUCKETS = 32
MAX_EXACT = NUM_BUCKETS // 2
MAX_DISTANCE = 128
N_MOD = 6
EPS = 1e-6
LOG2E = math.log2(math.e)

V7X_VMEM_LIMIT_BYTES = 56 * 1024 * 1024


def _params(n_axes):
    return pltpu.CompilerParams(
        dimension_semantics=("arbitrary",) * n_axes,
        vmem_limit_bytes=V7X_VMEM_LIMIT_BYTES,
    )


def _sigmoid(x):
    return jax.nn.sigmoid(x)


def _orig_head(slot, n_kv, q_per_kv):
    return (slot % n_kv) * q_per_kv + slot // n_kv


def _ada_body(c_ref, w_ref, b_ref, o_ref):
    c = c_ref[...]
    act = (c * _sigmoid(c)).astype(BF16)
    o_ref[...] = jnp.dot(act, w_ref[...].astype(BF16), preferred_element_type=F32) + b_ref[...]


def _ada(c, w_ada, b_ada, layer, tn=1024):
    b_sz, d = c.shape
    n = w_ada.shape[2]
    rows = 8
    c_pad = jnp.pad(c, ((0, rows - b_sz), (0, 0)))
    out = pl.pallas_call(
        _ada_body,
        grid=(n // tn,),
        in_specs=[
            pl.BlockSpec((rows, d), lambda j: (0, 0)),
            pl.BlockSpec((None, d, tn), lambda j: (layer, 0, j)),
            pl.BlockSpec((1, tn), lambda j: (0, j)),
        ],
        out_specs=pl.BlockSpec((rows, tn), lambda j: (0, j)),
        out_shape=jax.ShapeDtypeStruct((rows, n), F32),
        compiler_params=_params(1),
        name="ada_mod",
    )(c_pad, w_ada, b_ada.reshape(1, n))
    return out[:b_sz]


def _norm_mod_body(x_ref, g_ref, sc_ref, sh_ref, o_ref):
    x = x_ref[0]
    ms = jnp.mean(x * x, axis=-1, keepdims=True)
    y = x * lax.rsqrt(ms + EPS) * g_ref[...]
    o_ref[0] = (y * (1.0 + sc_ref[0]) + sh_ref[0]).astype(o_ref.dtype)


def _qkv_body(x_ref, g_ref, sc_ref, sh_ref, w_ref, gq_ref, gk_ref, o_ref, h_ref, wbf_ref, seg_ref,
              *, n_q, n_kv, half):
    attn_w = n_q * HEAD_DIM
    kv_w = n_kv * HEAD_DIM
    _norm_mod_body(x_ref, g_ref, sc_ref, sh_ref, h_ref)

    @pl.when(jnp.logical_and(pl.program_id(0) == 0, pl.program_id(1) == 0))
    def _():
        for slot in range(n_q):
            src = _orig_head(slot, n_kv, n_q // n_kv) * HEAD_DIM
            wbf_ref[:, slot * HEAD_DIM:(slot + 1) * HEAD_DIM] = w_ref[:, src:src + HEAD_DIM].astype(BF16)
        wbf_ref[:, attn_w:] = w_ref[:, attn_w:].astype(BF16)
        r = lax.broadcasted_iota(jnp.int32, (half, half), 0) // HEAD_DIM
        c = lax.broadcasted_iota(jnp.int32, (half, half), 1) // HEAD_DIM
        seg_ref[...] = (r == c).astype(BF16)

    a = h_ref[0]

    def head_norm(acc, seg, gain):
        ss = jnp.dot((acc * acc).astype(BF16), seg, preferred_element_type=F32)
        return acc * lax.rsqrt(ss / HEAD_DIM + EPS) * gain

    for c0 in range(0, attn_w, half):
        acc = jnp.dot(a, wbf_ref[:, c0:c0 + half], preferred_element_type=F32)
        o_ref[:, c0:c0 + half] = head_norm(acc, seg_ref[...], gq_ref[...]).astype(o_ref.dtype)
    acc = jnp.dot(a, wbf_ref[:, attn_w:attn_w + kv_w], preferred_element_type=F32)
    o_ref[:, attn_w:attn_w + kv_w] = head_norm(acc, seg_ref[:kv_w, :kv_w], gk_ref[...]).astype(o_ref.dtype)
    acc = jnp.dot(a, wbf_ref[:, attn_w + kv_w:], preferred_element_type=F32)
    o_ref[:, attn_w + kv_w:] = acc.astype(o_ref.dtype)


def _qkv_proj(x, norm_g, mod3, scale_idx, shift_idx, w_in, layer, q_g, k_g, n_q, n_kv, ts=512, half=512):
    b_sz, s_len, d = x.shape
    attn_w, kv_w = n_q * HEAD_DIM, n_kv * HEAD_DIM
    width = attn_w + 2 * kv_w
    assert attn_w % half == 0 and kv_w <= half and half % HEAD_DIM == 0
    gq = jnp.tile(q_g * (HEAD_DIM ** -0.5 * LOG2E), half // HEAD_DIM).reshape(1, half)
    gk = jnp.tile(k_g, n_kv).reshape(1, kv_w)
    tiles = s_len // ts
    const2 = lambda b, s: (0, 0)
    body = functools.partial(_qkv_body, n_q=n_q, n_kv=n_kv, half=half)
    return pl.pallas_call(
        body,
        grid=(b_sz, tiles),
        in_specs=[
            pl.BlockSpec((1, ts, d), lambda b, s: (b, s, 0)),
            pl.BlockSpec((1, d), const2),
            pl.BlockSpec((1, 1, d), lambda b, s: (b * N_MOD + scale_idx, 0, 0)),
            pl.BlockSpec((1, 1, d), lambda b, s: (b * N_MOD + shift_idx, 0, 0)),
            pl.BlockSpec((None, d, width), lambda b, s: (layer, 0, 0), pipeline_mode=pl.Buffered(1)),
            pl.BlockSpec((1, half), const2),
            pl.BlockSpec((1, kv_w), const2),
        ],
        out_specs=[
            pl.BlockSpec((ts, width), lambda b, s: (b * tiles + s, 0)),
            pl.BlockSpec((1, ts, d), lambda b, s: (b, s, 0)),
        ],
        out_shape=[
            jax.ShapeDtypeStruct((b_sz * s_len, width), BF16),
            jax.ShapeDtypeStruct((b_sz, s_len, d), BF16),
        ],
        scratch_shapes=[pltpu.VMEM((d, width), BF16), pltpu.VMEM((half, half), BF16)],
        compiler_params=_params(2),
        name="qkv_proj",
    )(x, norm_g.reshape(1, d), mod3, mod3, w_in, gq, gk)


def _cast_weights_once(w_refs, wbf_refs):
    @pl.when(pl.program_id(1) == 0)
    def _():
        for w_ref, wbf_ref in zip(w_refs, wbf_refs):
            wbf_ref[...] = w_ref[...].astype(BF16)


def _proj_body(a_ref, w_ref, o_ref, wbf_ref):
    _cast_weights_once([w_ref], [wbf_ref])
    for r0 in range(0, a_ref.shape[0], _MATMUL_ROWS):
        rs = slice(r0, r0 + _MATMUL_ROWS)
        acc = jnp.dot(a_ref[rs, :], wbf_ref[...], preferred_element_type=F32)
        o_ref[rs, :] = acc.astype(o_ref.dtype)


def _proj(h, w, layer, col0, n, tm, tn):
    m, k = h.shape
    assert col0 % tn == 0 and n % tn == 0
    blk0 = col0 // tn
    return pl.pallas_call(
        _proj_body,
        grid=(n // tn, m // tm),
        in_specs=[
            pl.BlockSpec((tm, k), lambda j, i: (i, 0)),
            pl.BlockSpec((None, k, tn), lambda j, i: (layer, 0, blk0 + j)),
        ],
        out_specs=pl.BlockSpec((tm, tn), lambda j, i: (i, j)),
        out_shape=jax.ShapeDtypeStruct((m, n), BF16),
        scratch_shapes=[pltpu.VMEM((k, tn), BF16)],
        compiler_params=_params(2),
        name="proj",
    )(h, w)


_EPILOGUE_LANES = 512
_MATMUL_ROWS = 1024


def _merge_body(ya_ref, yc_ref, wa_ref, wc_ref, ga_ref, gc_ref, o_ref, wabf_ref, wcbf_ref,
                *, n_q, n_kv):
    @pl.when(pl.program_id(0) == 0)
    def _():
        for slot in range(n_q):
            src = _orig_head(slot, n_kv, n_q // n_kv) * HEAD_DIM
            wabf_ref[slot * HEAD_DIM:(slot + 1) * HEAD_DIM, :] = wa_ref[src:src + HEAD_DIM, :].astype(BF16)
        wcbf_ref[...] = wc_ref[...].astype(BF16)

    ya = ya_ref[...]
    yc = yc_ref[...]
    for n0 in range(0, o_ref.shape[1], _EPILOGUE_LANES):
        ns = slice(n0, n0 + _EPILOGUE_LANES)
        acc_a = jnp.dot(ya, wabf_ref[:, ns], preferred_element_type=F32)
        acc_c = jnp.dot(yc, wcbf_ref[:, ns], preferred_element_type=F32)
        ga = _sigmoid(ga_ref[:, ns].astype(F32))
        gc = _sigmoid(gc_ref[:, ns].astype(F32))
        o_ref[:, ns] = (ga * acc_a + gc * acc_c).astype(o_ref.dtype)


def _merge(attn, conv, w_attn_out, w_conv_out, layer, gates, ga_col, gc_col, n_q, n_kv, tm=512):
    m, ka = attn.shape
    kc = conv.shape[1]
    n = w_attn_out.shape[2]
    assert ga_col % n == 0 and gc_col % n == 0 and n % _EPILOGUE_LANES == 0
    ga_blk, gc_blk = ga_col // n, gc_col // n
    body = functools.partial(_merge_body, n_q=n_q, n_kv=n_kv)
    resident = pl.Buffered(1)
    return pl.pallas_call(
        body,
        grid=(m // tm,),
        in_specs=[
            pl.BlockSpec((tm, ka), lambda i: (i, 0)),
            pl.BlockSpec((tm, kc), lambda i: (i, 0)),
            pl.BlockSpec((None, ka, n), lambda i: (layer, 0, 0), pipeline_mode=resident),
            pl.BlockSpec((None, kc, n), lambda i: (layer, 0, 0), pipeline_mode=resident),
            pl.BlockSpec((tm, n), lambda i: (i, ga_blk)),
            pl.BlockSpec((tm, n), lambda i: (i, gc_blk)),
        ],
        out_specs=pl.BlockSpec((tm, n), lambda i: (i, 0)),
        out_shape=jax.ShapeDtypeStruct((m, n), BF16),
        scratch_shapes=[pltpu.VMEM((ka, n), BF16), pltpu.VMEM((kc, n), BF16)],
        compiler_params=_params(1),
        name="merge",
    )(attn, conv, w_attn_out, w_conv_out, gates, gates)


def _mix_out_body(a_ref, w_ref, x_ref, gt_ref, g_ref, sc_ref, sh_ref, x1_ref, h_ref, wbf_ref):
    @pl.when(jnp.logical_and(pl.program_id(0) == 0, pl.program_id(1) == 0))
    def _():
        wbf_ref[...] = w_ref[...].astype(BF16)

    a = a_ref[...]
    ts, d = a.shape
    sumsq = jnp.zeros((ts, 1), F32)
    for n0 in range(0, d, _EPILOGUE_LANES):
        ns = slice(n0, n0 + _EPILOGUE_LANES)
        acc = jnp.dot(a, wbf_ref[:, ns], preferred_element_type=F32)
        x1 = x_ref[0, :, ns] + gt_ref[0, :, ns] * acc
        x1_ref[0, :, ns] = x1
        sumsq = sumsq + jnp.sum(x1 * x1, axis=-1, keepdims=True)
    inv = lax.rsqrt(sumsq / d + EPS)
    y = x1_ref[0] * inv * g_ref[...]
    h_ref[0] = (y * (1.0 + sc_ref[0]) + sh_ref[0]).astype(h_ref.dtype)


def _mix_out(a, w, layer, x, mod3, gate_idx, norm_g, scale_idx, shift_idx, ts=512):
    b_sz, s_len, d = x.shape
    k = a.shape[1]
    assert d % _EPILOGUE_LANES == 0
    tiles = s_len // ts
    const2 = lambda b, s: (0, 0)
    mod_row = lambda idx: pl.BlockSpec((1, 1, d), lambda b, s: (b * N_MOD + idx, 0, 0))
    act = pl.BlockSpec((1, ts, d), lambda b, s: (b, s, 0))
    return pl.pallas_call(
        _mix_out_body,
        grid=(b_sz, tiles),
        in_specs=[
            pl.BlockSpec((ts, k), lambda b, s: (b * tiles + s, 0)),
            pl.BlockSpec((None, k, d), lambda b, s: (layer, 0, 0), pipeline_mode=pl.Buffered(1)),
            act,
            mod_row(gate_idx),
            pl.BlockSpec((1, d), const2),
            mod_row(scale_idx),
            mod_row(shift_idx),
        ],
        out_specs=[act, act],
        out_shape=[jax.ShapeDtypeStruct((b_sz, s_len, d), F32), jax.ShapeDtypeStruct((b_sz, s_len, d), BF16)],
        scratch_shapes=[pltpu.VMEM((k, d), BF16)],
        compiler_params=_params(2),
        name="mix_out",
    )(a, w, x, mod3, norm_g.reshape(1, d), mod3, mod3)


def _residual_proj_body(a_ref, w_ref, x_ref, gt_ref, o_ref, wbf_ref):
    _cast_weights_once([w_ref], [wbf_ref])
    a = a_ref[...]
    for n0 in range(0, o_ref.shape[1], _EPILOGUE_LANES):
        ns = slice(n0, n0 + _EPILOGUE_LANES)
        acc = jnp.dot(a, wbf_ref[:, ns], preferred_element_type=F32)
        o_ref[:, ns] = x_ref[:, ns] + gt_ref[0, :, ns] * acc


def _residual_proj(a, w, layer, x2d, mod3, gate_idx, rows_per_batch, tm, tn):
    m, k = a.shape
    n = w.shape[2]
    assert n % tn == 0 and tn % _EPILOGUE_LANES == 0 and rows_per_batch % tm == 0
    tiles_per_batch = rows_per_batch // tm
    return pl.pallas_call(
        _residual_proj_body,
        grid=(n // tn, m // tm),
        in_specs=[
            pl.BlockSpec((tm, k), lambda j, i: (i, 0)),
            pl.BlockSpec((None, k, tn), lambda j, i: (layer, 0, j), pipeline_mode=pl.Buffered(1)),
            pl.BlockSpec((tm, tn), lambda j, i: (i, j)),
            pl.BlockSpec((1, 1, tn), lambda j, i: ((i // tiles_per_batch) * N_MOD + gate_idx, 0, j)),
        ],
        out_specs=pl.BlockSpec((tm, tn), lambda j, i: (i, j)),
        out_shape=jax.ShapeDtypeStruct((m, n), F32),
        scratch_shapes=[pltpu.VMEM((k, tn), BF16)],
        compiler_params=_params(2),
        name="residual_proj",
    )(a, w, x2d, mod3)


def _ffn_in_body(a_ref, wg_ref, wu_ref, o_ref, wgbf_ref, wubf_ref):
    _cast_weights_once([wg_ref, wu_ref], [wgbf_ref, wubf_ref])
    for r0 in range(0, a_ref.shape[0], _MATMUL_ROWS):
        rs = slice(r0, r0 + _MATMUL_ROWS)
        a = a_ref[rs, :]
        gate = jnp.dot(a, wgbf_ref[...], preferred_element_type=F32)
        up = jnp.dot(a, wubf_ref[...], preferred_element_type=F32)
        o_ref[rs, :] = (gate * _sigmoid(gate) * up).astype(o_ref.dtype)


def _ffn_in(h, w_ffn_in, layer, tm=2048, tn=512):
    m, k = h.shape
    d_ff = w_ffn_in.shape[2] // 2
    up_blk = d_ff // tn
    return pl.pallas_call(
        _ffn_in_body,
        grid=(d_ff // tn, m // tm),
        in_specs=[
            pl.BlockSpec((tm, k), lambda j, i: (i, 0)),
            pl.BlockSpec((None, k, tn), lambda j, i: (layer, 0, j)),
            pl.BlockSpec((None, k, tn), lambda j, i: (layer, 0, up_blk + j)),
        ],
        out_specs=pl.BlockSpec((tm, tn), lambda j, i: (i, j)),
        out_shape=jax.ShapeDtypeStruct((m, d_ff), BF16),
        scratch_shapes=[pltpu.VMEM((k, tn), BF16), pltpu.VMEM((k, tn), BF16)],
        compiler_params=_params(2),
        name="ffn_in",
    )(h, w_ffn_in, w_ffn_in)


def _t5_causal_bucket(dist):
    n = jnp.maximum(dist, 0)
    nf = jnp.maximum(n, 1).astype(jnp.float32)
    large = MAX_EXACT + (jnp.log(nf / MAX_EXACT) / math.log(MAX_DISTANCE / MAX_EXACT)
                         * (NUM_BUCKETS - MAX_EXACT)).astype(jnp.int32)
    large = jnp.minimum(large, NUM_BUCKETS - 1)
    return jnp.where(n < MAX_EXACT, n, large)


def _band_buckets():
    q_off = jnp.arange(BLOCK)
    k_off = jnp.arange(2 * BLOCK)
    dist = q_off[:, None] + BLOCK - k_off[None, :]
    allowed = (dist >= 0) & (dist < WINDOW)
    return jnp.where(allowed, _t5_causal_bucket(dist), -1).astype(jnp.int32)


def _attn_build_bias(first_step, rb_ref, sink_ref, bucket_ref, bias_ref, *, n_q, n_kv):
    q_per_kv = n_q // n_kv

    @pl.when(first_step)
    def _():
        bucket = bucket_ref[...]
        col = lax.broadcasted_iota(jnp.int32, bucket.shape, 1)
        for slot in range(n_q):
            head = _orig_head(slot, n_kv, q_per_kv)
            tile = jnp.zeros(bucket.shape, F32)
            for b in range(NUM_BUCKETS):
                tile = jnp.where(bucket == b, rb_ref[b * n_q + head] * LOG2E, tile)
            tile = jnp.where(bucket < 0, -jnp.inf, tile)
            sink = sink_ref[head] * LOG2E
            bias_ref[0, slot] = jnp.where(col == 0, sink, tile)
            bias_ref[1, slot] = jnp.where(col == 0, sink, jnp.where(col < BLOCK, -jnp.inf, tile))


def _attn_block(seq_start, qmask_ref, krow_ref, q_ref, kc_ref, kp_ref, vc_ref, vp_ref, o_ref,
                bias_ref, lg_ref, p_ref, pv_ref, rs_ref, *, n_q, n_kv):
    q_per_kv = n_q // n_kv
    kv_w = n_kv * HEAD_DIM
    lane_slot = lax.broadcasted_iota(jnp.int32, (BLOCK, BLOCK), 1) // HEAD_DIM
    ones = jnp.ones((2 * BLOCK, BLOCK), BF16)
    bias_idx = jnp.where(seq_start, 1, 0)

    keys = jnp.concatenate([kp_ref[...], kc_ref[...]], axis=0) * krow_ref[...]
    vals = jnp.concatenate([vp_ref[...], vc_ref[...]], axis=0) * krow_ref[...]
    lhs = jnp.concatenate(
        [q_ref[:, j * kv_w:(j + 1) * kv_w] * qmask_ref[g]
         for j in range(q_per_kv) for g in range(n_kv)], axis=0)
    lg_ref[...] = lax.dot_general(lhs, keys, (((1,), (1,)), ((), ())),
                                  preferred_element_type=F32).reshape(n_q, BLOCK, 2 * BLOCK)
    for slot in range(n_q):
        logit = lg_ref[slot] + bias_ref[bias_idx, slot]
        m = jnp.max(logit, axis=-1, keepdims=True)
        p_ref[slot * BLOCK:(slot + 1) * BLOCK, :] = jnp.exp2(logit - m).astype(BF16)
    p = p_ref[...]
    pv_ref[...] = jnp.dot(p, vals, preferred_element_type=F32).reshape(n_q, BLOCK, kv_w)
    rs_ref[...] = jnp.dot(p, ones, preferred_element_type=F32).reshape(n_q, BLOCK, BLOCK)
    per_half = BLOCK // HEAD_DIM
    for j in range(q_per_kv):
        for half in range(kv_w // BLOCK):
            lanes = slice(half * BLOCK, (half + 1) * BLOCK)
            slots = [j * n_kv + half * per_half + i for i in range(per_half)]
            num = pv_ref[slots[-1], :, lanes]
            den = rs_ref[slots[-1]]
            for i in range(per_half - 2, -1, -1):
                num = jnp.where(lane_slot == i, pv_ref[slots[i], :, lanes], num)
                den = jnp.where(lane_slot == i, rs_ref[slots[i]], den)
            o_ref[:, j * kv_w + half * BLOCK:j * kv_w + (half + 1) * BLOCK] = (
                num * (1.0 / den)).astype(o_ref.dtype)


_CONV_HALO = 32
_CONV_LANES = 256
_SUBLANES = 8


def _conv_zero_history(seq_start, u_ref):
    @pl.when(seq_start)
    def _():
        u_ref[:_CONV_HALO, :] = jnp.zeros((_CONV_HALO, u_ref.shape[1]), F32)


def _conv_block(a_ref, g_ref, w_ref, cb_ref, lg_ref, lb_ref, o_ref, u_ref, sh_ref, y_ref):
    width = w_ref.shape[0]
    ts, channels = o_ref.shape
    u_ref[_CONV_HALO:, :] = a_ref[...].astype(F32) * _sigmoid(g_ref[...].astype(F32))
    span = sh_ref.shape[1]
    for r in range(1, _SUBLANES):
        sh_ref[r - 1] = u_ref[r:r + span, :]
    first = _CONV_HALO - (width - 1)
    for c0 in range(0, channels, _CONV_LANES):
        cs = slice(c0, c0 + _CONV_LANES)
        acc = jnp.broadcast_to(cb_ref[:, cs], (ts, _CONV_LANES))
        for j in range(width):
            tile, r = divmod(first + j, _SUBLANES)
            rows = slice(tile * _SUBLANES, tile * _SUBLANES + ts)
            taps = u_ref[rows, cs] if r == 0 else sh_ref[r - 1, rows, cs]
            acc = acc + taps * w_ref[j:j + 1, cs]
        y_ref[:, cs] = acc
    y = y_ref[...]
    mu = jnp.mean(y, axis=-1, keepdims=True)
    yc = y - mu
    var = jnp.mean(yc * yc, axis=-1, keepdims=True)
    z = yc * lax.rsqrt(var + EPS) * lg_ref[...] + lb_ref[...]
    o_ref[...] = (z * _sigmoid(z)).astype(o_ref.dtype)
    u_ref[:_CONV_HALO, :] = u_ref[ts:, :]


def _core_body(h_ref, w_ref,
               a_ref, g_ref, cw_ref, cb_ref, lg_ref, lb_ref,
               rb_ref, sink_ref, bucket_ref, qmask_ref, krow_ref, q_ref, kc_ref, kp_ref, vc_ref, vp_ref,
               gates_ref, yconv_ref, yattn_ref,
               wbf_ref, u_ref, sh_ref, y_ref, bias_ref, lgt_ref, p_ref, pv_ref, rs_ref,
               *, n_q, n_kv, blocks_per_seq):
    t = pl.program_id(0) * pl.num_programs(1) + pl.program_id(1)
    seq_start = t % blocks_per_seq == 0
    _cast_weights_once([w_ref], [wbf_ref])
    _conv_zero_history(seq_start, u_ref)
    _attn_build_bias(t == 0, rb_ref, sink_ref, bucket_ref, bias_ref, n_q=n_q, n_kv=n_kv)

    acc = jnp.dot(h_ref[...], wbf_ref[...], preferred_element_type=F32)
    gates_ref[...] = acc.astype(gates_ref.dtype)
    _conv_block(a_ref, g_ref, cw_ref, cb_ref, lg_ref, lb_ref, yconv_ref, u_ref, sh_ref, y_ref)
    _attn_block(seq_start, qmask_ref, krow_ref, q_ref, kc_ref, kp_ref, vc_ref, vp_ref, yattn_ref,
                bias_ref, lgt_ref, p_ref, pv_ref, rs_ref, n_q=n_q, n_kv=n_kv)


def _mixer_core(h, w_in, layer, gate_col, conv_in, qkv, conv_w, conv_b, ln_g, ln_b, rel_bias, sinks,
                s_len, n_q, n_kv, tn=512):
    m, k = h.shape
    n = w_in.shape[2] - gate_col
    width, channels = conv_w.shape
    attn_w, kv_w = n_q * HEAD_DIM, n_kv * HEAD_DIM
    n_blocks = m // BLOCK
    n_tiles = n // tn
    assert gate_col % tn == 0 and n % tn == 0 and n_blocks % n_tiles == 0
    assert s_len % BLOCK == 0 and width - 1 <= _CONV_HALO and _CONV_HALO % _SUBLANES == 0
    assert attn_w % kv_w == 0 and kv_w == 2 * BLOCK and BLOCK % HEAD_DIM == 0
    m_tiles = n_blocks // n_tiles
    tm = m // m_tiles
    col_blk0 = gate_col // tn
    k_blk = attn_w // kv_w
    v_blk = k_blk + 1

    step = lambda j, i: j * m_tiles + i
    prev_step = lambda j, i: jnp.maximum(step(j, i) - 1, 0)
    const2 = lambda j, i: (0, 0)
    row = lambda v: v.reshape(1, channels)
    lane_group = jnp.arange(kv_w) // HEAD_DIM
    qmask = jnp.broadcast_to((lane_group[None, :] == jnp.arange(n_kv)[:, None])[:, None, :],
                             (n_kv, BLOCK, kv_w)).astype(BF16)
    krow = jnp.broadcast_to((jnp.arange(2 * BLOCK) > 0)[:, None], (2 * BLOCK, kv_w)).astype(BF16)
    body = functools.partial(_core_body, n_q=n_q, n_kv=n_kv, blocks_per_seq=s_len // BLOCK)
    return pl.pallas_call(
        body,
        grid=(n_tiles, m_tiles),
        in_specs=[
            pl.BlockSpec((tm, k), lambda j, i: (i, 0)),
            pl.BlockSpec((None, k, tn), lambda j, i: (layer, 0, col_blk0 + j)),
            pl.BlockSpec((BLOCK, channels), lambda j, i: (step(j, i), 0)),
            pl.BlockSpec((BLOCK, channels), lambda j, i: (step(j, i), 1)),
            pl.BlockSpec((width, channels), const2),
            pl.BlockSpec((1, channels), const2),
            pl.BlockSpec((1, channels), const2),
            pl.BlockSpec((1, channels), const2),
            pl.BlockSpec(memory_space=pltpu.SMEM),
            pl.BlockSpec(memory_space=pltpu.SMEM),
            pl.BlockSpec((BLOCK, 2 * BLOCK), const2),
            pl.BlockSpec((n_kv, BLOCK, kv_w), lambda j, i: (0, 0, 0)),
            pl.BlockSpec((2 * BLOCK, kv_w), const2),
            pl.BlockSpec((BLOCK, attn_w), lambda j, i: (step(j, i), 0)),
            pl.BlockSpec((BLOCK, kv_w), lambda j, i: (step(j, i), k_blk)),
            pl.BlockSpec((BLOCK, kv_w), lambda j, i: (prev_step(j, i), k_blk)),
            pl.BlockSpec((BLOCK, kv_w), lambda j, i: (step(j, i), v_blk)),
            pl.BlockSpec((BLOCK, kv_w), lambda j, i: (prev_step(j, i), v_blk)),
        ],
        out_specs=[
            pl.BlockSpec((tm, tn), lambda j, i: (i, j)),
            pl.BlockSpec((BLOCK, channels), lambda j, i: (step(j, i), 0)),
            pl.BlockSpec((BLOCK, attn_w), lambda j, i: (step(j, i), 0)),
        ],
        out_shape=[
            jax.ShapeDtypeStruct((m, n), BF16),
            jax.ShapeDtypeStruct((m, channels), BF16),
            jax.ShapeDtypeStruct((m, attn_w), BF16),
        ],
        scratch_shapes=[
            pltpu.VMEM((k, tn), BF16),
            pltpu.VMEM((_CONV_HALO + BLOCK, channels), F32),
            pltpu.VMEM((_SUBLANES - 1, _CONV_HALO - _SUBLANES + BLOCK, channels), F32),
            pltpu.VMEM((BLOCK, channels), F32),
            pltpu.VMEM((2, n_q, BLOCK, 2 * BLOCK), F32),
            pltpu.VMEM((n_q, BLOCK, 2 * BLOCK), F32),
            pltpu.VMEM((n_q * BLOCK, 2 * BLOCK), BF16),
            pltpu.VMEM((n_q, BLOCK, kv_w), F32),
            pltpu.VMEM((n_q, BLOCK, BLOCK), F32),
        ],
        compiler_params=_params(2),
        name="mixer_core",
    )(h, w_in,
      conv_in, conv_in, conv_w, row(conv_b), row(ln_g), row(ln_b),
      rel_bias.astype(F32).reshape(-1), sinks.astype(F32), _band_buckets(), qmask, krow,
      qkv, qkv, qkv, qkv, qkv)


def kernel(x, c, w_ada, b_ada, norm_mix_g, w_in, q_norm_g, k_norm_g, attn_sinks, rel_bias,
           w_attn_out, conv_w, conv_b, conv_ln_g, conv_ln_b, w_conv_out, w_mix_out,
           norm_ffn_g, w_ffn_in, w_ffn_out):
    b_sz, s_len, d = x.shape
    depth = w_ada.shape[0]
    n_q = attn_sinks.shape[1]
    attn_w = w_attn_out.shape[1]
    channels = conv_w.shape[2]
    kv_w = (w_in.shape[2] - attn_w - 2 * channels - 2 * d) // 2
    n_kv = kv_w // HEAD_DIM
    conv_col = attn_w + 2 * kv_w
    gate_col = conv_col + 2 * channels
    m = b_sz * s_len

    for l in range(depth):
        mod = _ada(c, w_ada, b_ada[l], l)
        mod3 = mod.reshape(b_sz * N_MOD, 1, d)

        qkv, h = _qkv_proj(x, norm_mix_g[l], mod3, 1, 0, w_in, l, q_norm_g[l], k_norm_g[l], n_q, n_kv)
        h = h.reshape(m, d)
        conv_in = _proj(h, w_in, l, conv_col, 2 * channels, tm=2048, tn=512)
        gates, y_conv, y_attn = _mixer_core(h, w_in, l, gate_col, conv_in, qkv, conv_w[l], conv_b[l],
                                            conv_ln_g[l], conv_ln_b[l], rel_bias, attn_sinks[l],
                                            s_len, n_q, n_kv)
        merged = _merge(y_attn, y_conv, w_attn_out, w_conv_out, l, gates, 0, d, n_q, n_kv)
        x1, h = _mix_out(merged, w_mix_out, l, x, mod3, 2, norm_ffn_g[l], 4, 3)

        act = _ffn_in(h.reshape(m, d), w_ffn_in, l)
        x = _residual_proj(act, w_ffn_out, l, x1.reshape(m, d), mod3, 5, s_len,
                           tm=256, tn=1024).reshape(b_sz, s_len, d)
    return x
```

```python
import functools
import math

import jax
import jax.numpy as jnp
from jax import lax
from jax.experimental import pallas as pl
from jax.experimental.pallas import tpu as pltpu

F32 = jnp.float32
BF16 = jnp.bfloat16

HEAD_DIM = 64
WINDOW = 128
BLOCK = 128
NUM_BUCKETS = 32
MAX_EXACT = NUM_BUCKETS // 2
MAX_DISTANCE = 128
N_MOD = 6
EPS = 1e-6
LOG2E = math.log2(math.e)

V7X_VMEM_LIMIT_BYTES = 56 * 1024 * 1024


def _params(n_axes):
    return pltpu.CompilerParams(
        dimension_semantics=("arbitrary",) * n_axes,
        vmem_limit_bytes=V7X_VMEM_LIMIT_BYTES,
    )


def _sigmoid(x):
    return jax.nn.sigmoid(x)


def _orig_head(slot, n_kv, q_per_kv):
    return (slot % n_kv) * q_per_kv + slot // n_kv


def _ada_body(c_ref, w_ref, b_ref, o_ref):
    c = c_ref[...]
    act = (c * _sigmoid(c)).astype(BF16)
    o_ref[...] = jnp.dot(act, w_ref[...].astype(BF16), preferred_element_type=F32) + b_ref[...]


def _ada(c, w_ada, b_ada, layer, tn=1024):
    b_sz, d = c.shape
    n = w_ada.shape[2]
    rows = 8
    c_pad = jnp.pad(c, ((0, rows - b_sz), (0, 0)))
    out = pl.pallas_call(
        _ada_body,
        grid=(n // tn,),
        in_specs=[
            pl.BlockSpec((rows, d), lambda j: (0, 0)),
            pl.BlockSpec((None, d, tn), lambda j: (layer, 0, j)),
            pl.BlockSpec((1, tn), lambda j: (0, j)),
        ],
        out_specs=pl.BlockSpec((rows, tn), lambda j: (0, j)),
        out_shape=jax.ShapeDtypeStruct((rows, n), F32),
        compiler_params=_params(1),
        name="ada_mod",
    )(c_pad, w_ada, b_ada.reshape(1, n))
    return out[:b_sz]


def _norm_mod_body(x_ref, g_ref, sc_ref, sh_ref, o_ref):
    x = x_ref[0]
    ms = jnp.mean(x * x, axis=-1, keepdims=True)
    y = x * lax.rsqrt(ms + EPS) * g_ref[...]
    o_ref[0] = (y * (1.0 + sc_ref[0]) + sh_ref[0]).astype(o_ref.dtype)


def _qkv_body(x_ref, g_ref, sc_ref, sh_ref, w_ref, gq_ref, gk_ref, o_ref, h_ref, wbf_ref, seg_ref,
              *, n_q, n_kv, half):
    attn_w = n_q * HEAD_DIM
    kv_w = n_kv * HEAD_DIM
    _norm_mod_body(x_ref, g_ref, sc_ref, sh_ref, h_ref)

    @pl.when(jnp.logical_and(pl.program_id(0) == 0, pl.program_id(1) == 0))
    def _():
        for slot in range(n_q):
            src = _orig_head(slot, n_kv, n_q // n_kv) * HEAD_DIM
            wbf_ref[:, slot * HEAD_DIM:(slot + 1) * HEAD_DIM] = w_ref[:, src:src + HEAD_DIM].astype(BF16)
        wbf_ref[:, attn_w:] = w_ref[:, attn_w:].astype(BF16)
        r = lax.broadcasted_iota(jnp.int32, (half, half), 0) // HEAD_DIM
        c = lax.broadcasted_iota(jnp.int32, (half, half), 1) // HEAD_DIM
        seg_ref[...] = (r == c).astype(BF16)

    a = h_ref[0]

    def head_norm(acc, seg, gain):
        ss = jnp.dot((acc * acc).astype(BF16), seg, preferred_element_type=F32)
        return acc * lax.rsqrt(ss / HEAD_DIM + EPS) * gain

    for c0 in range(0, attn_w, half):
        acc = jnp.dot(a, wbf_ref[:, c0:c0 + half], preferred_element_type=F32)
        o_ref[:, c0:c0 + half] = head_norm(acc, seg_ref[...], gq_ref[...]).astype(o_ref.dtype)
    acc = jnp.dot(a, wbf_ref[:, attn_w:attn_w + kv_w], preferred_element_type=F32)
    o_ref[:, attn_w:attn_w + kv_w] = head_norm(acc, seg_ref[:kv_w, :kv_w], gk_ref[...]).astype(o_ref.dtype)
    acc = jnp.dot(a, wbf_ref[:, attn_w + kv_w:], preferred_element_type=F32)
    o_ref[:, attn_w + kv_w:] = acc.astype(o_ref.dtype)


def _qkv_proj(x, norm_g, mod3, scale_idx, shift_idx, w_in, layer, q_g, k_g, n_q, n_kv, ts=512, half=512):
    b_sz, s_len, d = x.shape
    attn_w, kv_w = n_q * HEAD_DIM, n_kv * HEAD_DIM
    width = attn_w + 2 * kv_w
    assert attn_w % half == 0 and kv_w <= half and half % HEAD_DIM == 0
    gq = jnp.tile(q_g * (HEAD_DIM ** -0.5 * LOG2E), half // HEAD_DIM).reshape(1, half)
    gk = jnp.tile(k_g, n_kv).reshape(1, kv_w)
    tiles = s_len // ts
    const2 = lambda b, s: (0, 0)
    body = functools.partial(_qkv_body, n_q=n_q, n_kv=n_kv, half=half)
    return pl.pallas_call(
        body,
        grid=(b_sz, tiles),
        in_specs=[
            pl.BlockSpec((1, ts, d), lambda b, s: (b, s, 0)),
            pl.BlockSpec((1, d), const2),
            pl.BlockSpec((1, 1, d), lambda b, s: (b * N_MOD + scale_idx, 0, 0)),
            pl.BlockSpec((1, 1, d), lambda b, s: (b * N_MOD + shift_idx, 0, 0)),
            pl.BlockSpec((None, d, width), lambda b, s: (layer, 0, 0), pipeline_mode=pl.Buffered(1)),
            pl.BlockSpec((1, half), const2),
            pl.BlockSpec((1, kv_w), const2),
        ],
        out_specs=[
            pl.BlockSpec((ts, width), lambda b, s: (b * tiles + s, 0)),
            pl.BlockSpec((1, ts, d), lambda b, s: (b, s, 0)),
        ],
        out_shape=[
            jax.ShapeDtypeStruct((b_sz * s_len, width), BF16),
            jax.ShapeDtypeStruct((b_sz, s_len, d), BF16),
        ],
        scratch_shapes=[pltpu.VMEM((d, width), BF16), pltpu.VMEM((half, half), BF16)],
        compiler_params=_params(2),
        name="qkv_proj",
    )(x, norm_g.reshape(1, d), mod3, mod3, w_in, gq, gk)


def _cast_weights_once(w_refs, wbf_refs):
    @pl.when(pl.program_id(1) == 0)
    def _():
        for w_ref, wbf_ref in zip(w_refs, wbf_refs):
            wbf_ref[...] = w_ref[...].astype(BF16)


def _proj_body(a_ref, w_ref, o_ref, wbf_ref):
    _cast_weights_once([w_ref], [wbf_ref])
    for r0 in range(0, a_ref.shape[0], _MATMUL_ROWS):
        rs = slice(r0, r0 + _MATMUL_ROWS)
        acc = jnp.dot(a_ref[rs, :], wbf_ref[...], preferred_element_type=F32)
        o_ref[rs, :] = acc.astype(o_ref.dtype)


def _proj(h, w, layer, col0, n, tm, tn):
    m, k = h.shape
    assert col0 % tn == 0 and n % tn == 0
    blk0 = col0 // tn
    return pl.pallas_call(
        _proj_body,
        grid=(n // tn, m // tm),
        in_specs=[
            pl.BlockSpec((tm, k), lambda j, i: (i, 0)),
            pl.BlockSpec((None, k, tn), lambda j, i: (layer, 0, blk0 + j)),
        ],
        out_specs=pl.BlockSpec((tm, tn), lambda j, i: (i, j)),
        out_shape=jax.ShapeDtypeStruct((m, n), BF16),
        scratch_shapes=[pltpu.VMEM((k, tn), BF16)],
        compiler_params=_params(2),
        name="proj",
    )(h, w)


_EPILOGUE_LANES = 512
_MATMUL_ROWS = 1024


def _merge_body(ya_ref, yc_ref, wa_ref, wc_ref, ga_ref, gc_ref, o_ref, wabf_ref, wcbf_ref,
                *, n_q, n_kv):
    @pl.when(pl.program_id(0) == 0)
    def _():
        for slot in range(n_q):
            src = _orig_head(slot, n_kv, n_q // n_kv) * HEAD_DIM
            wabf_ref[slot * HEAD_DIM:(slot + 1) * HEAD_DIM, :] = wa_ref[src:src + HEAD_DIM, :].astype(BF16)
        wcbf_ref[...] = wc_ref[...].astype(BF16)

    ya = ya_ref[...]
    yc = yc_ref[...]
    for n0 in range(0, o_ref.shape[1], _EPILOGUE_LANES):
        ns = slice(n0, n0 + _EPILOGUE_LANES)
        acc_a = jnp.dot(ya, wabf_ref[:, ns], preferred_element_type=F32)
        acc_c = jnp.dot(yc, wcbf_ref[:, ns], preferred_element_type=F32)
        ga = _sigmoid(ga_ref[:, ns].astype(F32))
        gc = _sigmoid(gc_ref[:, ns].astype(F32))
        o_ref[:, ns] = (ga * acc_a + gc * acc_c).astype(o_ref.dtype)


def _merge(attn, conv, w_attn_out, w_conv_out, layer, gates, ga_col, gc_col, n_q, n_kv, tm=512):
    m, ka = attn.shape
    kc = conv.shape[1]
    n = w_attn_out.shape[2]
    assert ga_col % n == 0 and gc_col % n == 0 and n % _EPILOGUE_LANES == 0
    ga_blk, gc_blk = ga_col // n, gc_col // n
    body = functools.partial(_merge_body, n_q=n_q, n_kv=n_kv)
    resident = pl.Buffered(1)
    return pl.pallas_call(
        body,
        grid=(m // tm,),
        in_specs=[
            pl.BlockSpec((tm, ka), lambda i: (i, 0)),
            pl.BlockSpec((tm, kc), lambda i: (i, 0)),
            pl.BlockSpec((None, ka, n), lambda i: (layer, 0, 0), pipeline_mode=resident),
            pl.BlockSpec((None, kc, n), lambda i: (layer, 0, 0), pipeline_mode=resident),
            pl.BlockSpec((tm, n), lambda i: (i, ga_blk)),
            pl.BlockSpec((tm, n), lambda i: (i, gc_blk)),
        ],
        out_specs=pl.BlockSpec((tm, n), lambda i: (i, 0)),
        out_shape=jax.ShapeDtypeStruct((m, n), BF16),
        scratch_shapes=[pltpu.VMEM((ka, n), BF16), pltpu.VMEM((kc, n), BF16)],
        compiler_params=_params(1),
        name="merge",
    )(attn, conv, w_attn_out, w_conv_out, gates, gates)


def _mix_out_body(a_ref, w_ref, x_ref, gt_ref, g_ref, sc_ref, sh_ref, x1_ref, h_ref, wbf_ref):
    @pl.when(jnp.logical_and(pl.program_id(0) == 0, pl.program_id(1) == 0))
    def _():
        wbf_ref[...] = w_ref[...].astype(BF16)

    a = a_ref[...]
    ts, d = a.shape
    sumsq = jnp.zeros((ts, 1), F32)
    for n0 in range(0, d, _EPILOGUE_LANES):
        ns = slice(n0, n0 + _EPILOGUE_LANES)
        acc = jnp.dot(a, wbf_ref[:, ns], preferred_element_type=F32)
        x1 = x_ref[0, :, ns] + gt_ref[0, :, ns] * acc
        x1_ref[0, :, ns] = x1
        sumsq = sumsq + jnp.sum(x1 * x1, axis=-1, keepdims=True)
    inv = lax.rsqrt(sumsq / d + EPS)
    y = x1_ref[0] * inv * g_ref[...]
    h_ref[0] = (y * (1.0 + sc_ref[0]) + sh_ref[0]).astype(h_ref.dtype)


def _mix_out(a, w, layer, x, mod3, gate_idx, norm_g, scale_idx, shift_idx, ts=512):
    b_sz, s_len, d = x.shape
    k = a.shape[1]
    assert d % _EPILOGUE_LANES == 0
    tiles = s_len // ts
    const2 = lambda b, s: (0, 0)
    mod_row = lambda idx: pl.BlockSpec((1, 1, d), lambda b, s: (b * N_MOD + idx, 0, 0))
    act = pl.BlockSpec((1, ts, d), lambda b, s: (b, s, 0))
    return pl.pallas_call(
        _mix_out_body,
        grid=(b_sz, tiles),
        in_specs=[
            pl.BlockSpec((ts, k), lambda b, s: (b * tiles + s, 0)),
            pl.BlockSpec((None, k, d), lambda b, s: (layer, 0, 0), pipeline_mode=pl.Buffered(1)),
            act,
            mod_row(gate_idx),
            pl.BlockSpec((1, d), const2),
            mod_row(scale_idx),
            mod_row(shift_idx),
        ],
        out_specs=[act, act],
        out_shape=[jax.ShapeDtypeStruct((b_sz, s_len, d), F32), jax.ShapeDtypeStruct((b_sz, s_len, d), BF16)],
        scratch_shapes=[pltpu.VMEM((k, d), BF16)],
        compiler_params=_params(2),
        name="mix_out",
    )(a, w, x, mod3, norm_g.reshape(1, d), mod3, mod3)


def _residual_proj_body(a_ref, w_ref, x_ref, gt_ref, o_ref):
    a = a_ref[...]
    for n0 in range(0, o_ref.shape[1], _EPILOGUE_LANES):
        ns = slice(n0, n0 + _EPILOGUE_LANES)
        acc = jnp.dot(a, w_ref[:, ns], preferred_element_type=F32)
        o_ref[:, ns] = x_ref[:, ns] + gt_ref[0, :, ns] * acc


def _residual_proj(a, w_bf, x2d, mod3, gate_idx, rows_per_batch, tm, tn):
    m, k = a.shape
    n = w_bf.shape[1]
    assert n % tn == 0 and tn % _EPILOGUE_LANES == 0 and rows_per_batch % tm == 0
    tiles_per_batch = rows_per_batch // tm
    return pl.pallas_call(
        _residual_proj_body,
        grid=(n // tn, m // tm),
        in_specs=[
            pl.BlockSpec((tm, k), lambda j, i: (i, 0)),
            pl.BlockSpec((k, tn), lambda j, i: (0, j)),
            pl.BlockSpec((tm, tn), lambda j, i: (i, j)),
            pl.BlockSpec((1, 1, tn), lambda j, i: ((i // tiles_per_batch) * N_MOD + gate_idx, 0, j)),
        ],
        out_specs=pl.BlockSpec((tm, tn), lambda j, i: (i, j)),
        out_shape=jax.ShapeDtypeStruct((m, n), F32),
        compiler_params=_params(2),
        name="residual_proj",
    )(a, w_bf, x2d, mod3)


def _ffn_in_body(a_ref, wg_ref, wu_ref, wo_ref, o_ref, wobf_ref, wgbf_ref, wubf_ref):
    _cast_weights_once([wg_ref, wu_ref], [wgbf_ref, wubf_ref])
    wobf_ref[...] = wo_ref[...].astype(BF16)
    for r0 in range(0, a_ref.shape[0], _MATMUL_ROWS):
        rs = slice(r0, r0 + _MATMUL_ROWS)
        a = a_ref[rs, :]
        gate = jnp.dot(a, wgbf_ref[...], preferred_element_type=F32)
        up = jnp.dot(a, wubf_ref[...], preferred_element_type=F32)
        o_ref[rs, :] = (gate * _sigmoid(gate) * up).astype(o_ref.dtype)


def _ffn_in(h, w_ffn_in, w_ffn_out, layer, tm=2048, tn=512):
    m, k = h.shape
    d_ff = w_ffn_in.shape[2] // 2
    d_out = w_ffn_out.shape[2]
    up_blk = d_ff // tn
    m_tiles = m // tm
    steps = (d_ff // tn) * m_tiles
    assert d_ff % steps == 0
    slab = d_ff // steps
    step = lambda j, i: j * m_tiles + i
    return pl.pallas_call(
        _ffn_in_body,
        grid=(d_ff // tn, m_tiles),
        in_specs=[
            pl.BlockSpec((tm, k), lambda j, i: (i, 0)),
            pl.BlockSpec((None, k, tn), lambda j, i: (layer, 0, j)),
            pl.BlockSpec((None, k, tn), lambda j, i: (layer, 0, up_blk + j)),
            pl.BlockSpec((None, slab, d_out), lambda j, i: (layer, step(j, i), 0)),
        ],
        out_specs=[
            pl.BlockSpec((tm, tn), lambda j, i: (i, j)),
            pl.BlockSpec((slab, d_out), lambda j, i: (step(j, i), 0)),
        ],
        out_shape=[jax.ShapeDtypeStruct((m, d_ff), BF16), jax.ShapeDtypeStruct((d_ff, d_out), BF16)],
        scratch_shapes=[pltpu.VMEM((k, tn), BF16), pltpu.VMEM((k, tn), BF16)],
        compiler_params=_params(2),
        name="ffn_in",
    )(h, w_ffn_in, w_ffn_in, w_ffn_out)


def _t5_causal_bucket(dist):
    n = jnp.maximum(dist, 0)
    nf = jnp.maximum(n, 1).astype(jnp.float32)
    large = MAX_EXACT + (jnp.log(nf / MAX_EXACT) / math.log(MAX_DISTANCE / MAX_EXACT)
                         * (NUM_BUCKETS - MAX_EXACT)).astype(jnp.int32)
    large = jnp.minimum(large, NUM_BUCKETS - 1)
    return jnp.where(n < MAX_EXACT, n, large)


def _band_buckets():
    q_off = jnp.arange(BLOCK)
    k_off = jnp.arange(2 * BLOCK)
    dist = q_off[:, None] + BLOCK - k_off[None, :]
    allowed = (dist >= 0) & (dist < WINDOW)
    return jnp.where(allowed, _t5_causal_bucket(dist), -1).astype(jnp.int32)


def _attn_build_bias(first_step, rb_ref, sink_ref, bucket_ref, bias_ref, *, n_q, n_kv):
    q_per_kv = n_q // n_kv

    @pl.when(first_step)
    def _():
        bucket = bucket_ref[...]
        col = lax.broadcasted_iota(jnp.int32, bucket.shape, 1)
        for slot in range(n_q):
            head = _orig_head(slot, n_kv, q_per_kv)
            tile = jnp.zeros(bucket.shape, F32)
            for b in range(NUM_BUCKETS):
                tile = jnp.where(bucket == b, rb_ref[b * n_q + head] * LOG2E, tile)
            tile = jnp.where(bucket < 0, -jnp.inf, tile)
            sink = sink_ref[head] * LOG2E
            bias_ref[0, slot] = jnp.where(col == 0, sink, tile)
            bias_ref[1, slot] = jnp.where(col == 0, sink, jnp.where(col < BLOCK, -jnp.inf, tile))


def _attn_block(seq_start, qmask_ref, krow_ref, q_ref, kc_ref, kp_ref, vc_ref, vp_ref, o_ref,
                bias_ref, lg_ref, p_ref, pv_ref, rs_ref, *, n_q, n_kv):
    q_per_kv = n_q // n_kv
    kv_w = n_kv * HEAD_DIM
    lane_slot = lax.broadcasted_iota(jnp.int32, (BLOCK, BLOCK), 1) // HEAD_DIM
    ones = jnp.ones((2 * BLOCK, BLOCK), BF16)
    bias_idx = jnp.where(seq_start, 1, 0)

    keys = jnp.concatenate([kp_ref[...], kc_ref[...]], axis=0) * krow_ref[...]
    vals = jnp.concatenate([vp_ref[...], vc_ref[...]], axis=0) * krow_ref[...]
    lhs = jnp.concatenate(
        [q_ref[:, j * kv_w:(j + 1) * kv_w] * qmask_ref[g]
         for j in range(q_per_kv) for g in range(n_kv)], axis=0)
    lg_ref[...] = lax.dot_general(lhs, keys, (((1,), (1,)), ((), ())),
                                  preferred_element_type=F32).reshape(n_q, BLOCK, 2 * BLOCK)
    for slot in range(n_q):
        logit = lg_ref[slot] + bias_ref[bias_idx, slot]
        m = jnp.max(logit, axis=-1, keepdims=True)
        p_ref[slot * BLOCK:(slot + 1) * BLOCK, :] = jnp.exp2(logit - m).astype(BF16)
    p = p_ref[...]
    pv_ref[...] = jnp.dot(p, vals, preferred_element_type=F32).reshape(n_q, BLOCK, kv_w)
    rs_ref[...] = jnp.dot(p, ones, preferred_element_type=F32).reshape(n_q, BLOCK, BLOCK)
    per_half = BLOCK // HEAD_DIM
    for j in range(q_per_kv):
        for half in range(kv_w // BLOCK):
            lanes = slice(half * BLOCK, (half + 1) * BLOCK)
            slots = [j * n_kv + half * per_half + i for i in range(per_half)]
            num = pv_ref[slots[-1], :, lanes]
            den = rs_ref[slots[-1]]
            for i in range(per_half - 2, -1, -1):
                num = jnp.where(lane_slot == i, pv_ref[slots[i], :, lanes], num)
                den = jnp.where(lane_slot == i, rs_ref[slots[i]], den)
            o_ref[:, j * kv_w + half * BLOCK:j * kv_w + (half + 1) * BLOCK] = (
                num * (1.0 / den)).astype(o_ref.dtype)


_CONV_HALO = 32
_CONV_LANES = 256
_SUBLANES = 8


def _conv_zero_history(seq_start, u_ref):
    @pl.when(seq_start)
    def _():
        u_ref[:_CONV_HALO, :] = jnp.zeros((_CONV_HALO, u_ref.shape[1]), F32)


def _conv_block(a_ref, g_ref, w_ref, cb_ref, lg_ref, lb_ref, o_ref, u_ref, sh_ref, y_ref):
    width = w_ref.shape[0]
    ts, channels = o_ref.shape
    u_ref[_CONV_HALO:, :] = a_ref[...].astype(F32) * _sigmoid(g_ref[...].astype(F32))
    span = sh_ref.shape[1]
    for r in range(1, _SUBLANES):
        sh_ref[r - 1] = u_ref[r:r + span, :]
    first = _CONV_HALO - (width - 1)
    for c0 in range(0, channels, _CONV_LANES):
        cs = slice(c0, c0 + _CONV_LANES)
        acc = jnp.broadcast_to(cb_ref[:, cs], (ts, _CONV_LANES))
        for j in range(width):
            tile, r = divmod(first + j, _SUBLANES)
            rows = slice(tile * _SUBLANES, tile * _SUBLANES + ts)
            taps = u_ref[rows, cs] if r == 0 else sh_ref[r - 1, rows, cs]
            acc = acc + taps * w_ref[j:j + 1, cs]
        y_ref[:, cs] = acc
    y = y_ref[...]
    mu = jnp.mean(y, axis=-1, keepdims=True)
    yc = y - mu
    var = jnp.mean(yc * yc, axis=-1, keepdims=True)
    z = yc * lax.rsqrt(var + EPS) * lg_ref[...] + lb_ref[...]
    o_ref[...] = (z * _sigmoid(z)).astype(o_ref.dtype)
    u_ref[:_CONV_HALO, :] = u_ref[ts:, :]


def _core_body(h_ref, w_ref,
               a_ref, g_ref, cw_ref, cb_ref, lg_ref, lb_ref,
               rb_ref, sink_ref, bucket_ref, qmask_ref, krow_ref, q_ref, kc_ref, kp_ref, vc_ref, vp_ref,
               gates_ref, yconv_ref, yattn_ref,
               wbf_ref, u_ref, sh_ref, y_ref, bias_ref, lgt_ref, p_ref, pv_ref, rs_ref,
               *, n_q, n_kv, blocks_per_seq):
    t = pl.program_id(0) * pl.num_programs(1) + pl.program_id(1)
    seq_start = t % blocks_per_seq == 0
    _cast_weights_once([w_ref], [wbf_ref])
    _conv_zero_history(seq_start, u_ref)
    _attn_build_bias(t == 0, rb_ref, sink_ref, bucket_ref, bias_ref, n_q=n_q, n_kv=n_kv)

    acc = jnp.dot(h_ref[...], wbf_ref[...], preferred_element_type=F32)
    gates_ref[...] = acc.astype(gates_ref.dtype)
    _conv_block(a_ref, g_ref, cw_ref, cb_ref, lg_ref, lb_ref, yconv_ref, u_ref, sh_ref, y_ref)
    _attn_block(seq_start, qmask_ref, krow_ref, q_ref, kc_ref, kp_ref, vc_ref, vp_ref, yattn_ref,
                bias_ref, lgt_ref, p_ref, pv_ref, rs_ref, n_q=n_q, n_kv=n_kv)


def _mixer_core(h, w_in, layer, gate_col, conv_in, qkv, conv_w, conv_b, ln_g, ln_b, rel_bias, sinks,
                s_len, n_q, n_kv, tn=512):
    m, k = h.shape
    n = w_in.shape[2] - gate_col
    width, channels = conv_w.shape
    attn_w, kv_w = n_q * HEAD_DIM, n_kv * HEAD_DIM
    n_blocks = m // BLOCK
    n_tiles = n // tn
    assert gate_col % tn == 0 and n % tn == 0 and n_blocks % n_tiles == 0
    assert s_len % BLOCK == 0 and width - 1 <= _CONV_HALO and _CONV_HALO % _SUBLANES == 0
    assert attn_w % kv_w == 0 and kv_w == 2 * BLOCK and BLOCK % HEAD_DIM == 0
    m_tiles = n_blocks // n_tiles
    tm = m // m_tiles
    col_blk0 = gate_col // tn
    k_blk = attn_w // kv_w
    v_blk = k_blk + 1

    step = lambda j, i: j * m_tiles + i
    prev_step = lambda j, i: jnp.maximum(step(j, i) - 1, 0)
    const2 = lambda j, i: (0, 0)
    row = lambda v: v.reshape(1, channels)
    lane_group = jnp.arange(kv_w) // HEAD_DIM
    qmask = jnp.broadcast_to((lane_group[None, :] == jnp.arange(n_kv)[:, None])[:, None, :],
                             (n_kv, BLOCK, kv_w)).astype(BF16)
    krow = jnp.broadcast_to((jnp.arange(2 * BLOCK) > 0)[:, None], (2 * BLOCK, kv_w)).astype(BF16)
    body = functools.partial(_core_body, n_q=n_q, n_kv=n_kv, blocks_per_seq=s_len // BLOCK)
    return pl.pallas_call(
        body,
        grid=(n_tiles, m_tiles),
        in_specs=[
            pl.BlockSpec((tm, k), lambda j, i: (i, 0)),
            pl.BlockSpec((None, k, tn), lambda j, i: (layer, 0, col_blk0 + j)),
            pl.BlockSpec((BLOCK, channels), lambda j, i: (step(j, i), 0)),
            pl.BlockSpec((BLOCK, channels), lambda j, i: (step(j, i), 1)),
            pl.BlockSpec((width, channels), const2),
            pl.BlockSpec((1, channels), const2),
            pl.BlockSpec((1, channels), const2),
            pl.BlockSpec((1, channels), const2),
            pl.BlockSpec(memory_space=pltpu.SMEM),
            pl.BlockSpec(memory_space=pltpu.SMEM),
            pl.BlockSpec((BLOCK, 2 * BLOCK), const2),
            pl.BlockSpec((n_kv, BLOCK, kv_w), lambda j, i: (0, 0, 0)),
            pl.BlockSpec((2 * BLOCK, kv_w), const2),
            pl.BlockSpec((BLOCK, attn_w), lambda j, i: (step(j, i), 0)),
            pl.BlockSpec((BLOCK, kv_w), lambda j, i: (step(j, i), k_blk)),
            pl.BlockSpec((BLOCK, kv_w), lambda j, i: (prev_step(j, i), k_blk)),
            pl.BlockSpec((BLOCK, kv_w), lambda j, i: (step(j, i), v_blk)),
            pl.BlockSpec((BLOCK, kv_w), lambda j, i: (prev_step(j, i), v_blk)),
        ],
        out_specs=[
            pl.BlockSpec((tm, tn), lambda j, i: (i, j)),
            pl.BlockSpec((BLOCK, channels), lambda j, i: (step(j, i), 0)),
            pl.BlockSpec((BLOCK, attn_w), lambda j, i: (step(j, i), 0)),
        ],
        out_shape=[
            jax.ShapeDtypeStruct((m, n), BF16),
            jax.ShapeDtypeStruct((m, channels), BF16),
            jax.ShapeDtypeStruct((m, attn_w), BF16),
        ],
        scratch_shapes=[
            pltpu.VMEM((k, tn), BF16),
            pltpu.VMEM((_CONV_HALO + BLOCK, channels), F32),
            pltpu.VMEM((_SUBLANES - 1, _CONV_HALO - _SUBLANES + BLOCK, channels), F32),
            pltpu.VMEM((BLOCK, channels), F32),
            pltpu.VMEM((2, n_q, BLOCK, 2 * BLOCK), F32),
            pltpu.VMEM((n_q, BLOCK, 2 * BLOCK), F32),
            pltpu.VMEM((n_q * BLOCK, 2 * BLOCK), BF16),
            pltpu.VMEM((n_q, BLOCK, kv_w), F32),
            pltpu.VMEM((n_q, BLOCK, BLOCK), F32),
        ],
        compiler_params=_params(2),
        name="mixer_core",
    )(h, w_in,
      conv_in, conv_in, conv_w, row(conv_b), row(ln_g), row(ln_b),
      rel_bias.astype(F32).reshape(-1), sinks.astype(F32), _band_buckets(), qmask, krow,
      qkv, qkv, qkv, qkv, qkv)


def kernel(x, c, w_ada, b_ada, norm_mix_g, w_in, q_norm_g, k_norm_g, attn_sinks, rel_bias,
           w_attn_out, conv_w, conv_b, conv_ln_g, conv_ln_b, w_conv_out, w_mix_out,
           norm_ffn_g, w_ffn_in, w_ffn_out):
    b_sz, s_len, d = x.shape
    depth = w_ada.shape[0]
    n_q = attn_sinks.shape[1]
    attn_w = w_attn_out.shape[1]
    channels = conv_w.shape[2]
    kv_w = (w_in.shape[2] - attn_w - 2 * channels - 2 * d) // 2
    n_kv = kv_w // HEAD_DIM
    conv_col = attn_w + 2 * kv_w
    gate_col = conv_col + 2 * channels
    m = b_sz * s_len

    for l in range(depth):
        mod = _ada(c, w_ada, b_ada[l], l)
        mod3 = mod.reshape(b_sz * N_MOD, 1, d)

        qkv, h = _qkv_proj(x, norm_mix_g[l], mod3, 1, 0, w_in, l, q_norm_g[l], k_norm_g[l], n_q, n_kv)
        h = h.reshape(m, d)
        conv_in = _proj(h, w_in, l, conv_col, 2 * channels, tm=2048, tn=512)
        gates, y_conv, y_attn = _mixer_core(h, w_in, l, gate_col, conv_in, qkv, conv_w[l], conv_b[l],
                                            conv_ln_g[l], conv_ln_b[l], rel_bias, attn_sinks[l],
                                            s_len, n_q, n_kv)
        merged = _merge(y_attn, y_conv, w_attn_out, w_conv_out, l, gates, 0, d, n_q, n_kv)
        x1, h = _mix_out(merged, w_mix_out, l, x, mod3, 2, norm_ffn_g[l], 4, 3)

        act, wo_bf = _ffn_in(h.reshape(m, d), w_ffn_in, w_ffn_out, l)
        x = _residual_proj(act, wo_bf, x1.reshape(m, d), mod3, 5, s_len,
                           tm=512, tn=1024).reshape(b_sz, s_len, d)
    return x
```

```python
import functools
import math

import jax
import jax.numpy as jnp
from jax import lax
from jax.experimental import pallas as pl
from jax.experimental.pallas import tpu as pltpu

F32 = jnp.float32
BF16 = jnp.bfloat16

HEAD_DIM = 64
WINDOW = 128
BLOCK = 128
NUM_BUCKETS = 32
MAX_EXACT = NUM_BUCKETS // 2
MAX_DISTANCE = 128
N_MOD = 6
EPS = 1e-6
LOG2E = math.log2(math.e)

V7X_VMEM_LIMIT_BYTES = 56 * 1024 * 1024


def _params(n_axes):
    return pltpu.CompilerParams(
        dimension_semantics=("arbitrary",) * n_axes,
        vmem_limit_bytes=V7X_VMEM_LIMIT_BYTES,
    )


def _sigmoid(x):
    return jax.nn.sigmoid(x)


def _orig_head(slot, n_kv, q_per_kv):
    return (slot % n_kv) * q_per_kv + slot // n_kv


def _ada_body(c_ref, w_ref, b_ref, o_ref):
    c = c_ref[...]
    act = (c * _sigmoid(c)).astype(BF16)
    o_ref[...] = jnp.dot(act, w_ref[...].astype(BF16), preferred_element_type=F32) + b_ref[...]


def _ada(c, w_ada, b_ada, layer, tn=1024):
    b_sz, d = c.shape
    n = w_ada.shape[2]
    rows = 8
    c_pad = jnp.pad(c, ((0, rows - b_sz), (0, 0)))
    out = pl.pallas_call(
        _ada_body,
        grid=(n // tn,),
        in_specs=[
            pl.BlockSpec((rows, d), lambda j: (0, 0)),
            pl.BlockSpec((None, d, tn), lambda j: (layer, 0, j)),
            pl.BlockSpec((1, tn), lambda j: (0, j)),
        ],
        out_specs=pl.BlockSpec((rows, tn), lambda j: (0, j)),
        out_shape=jax.ShapeDtypeStruct((rows, n), F32),
        compiler_params=_params(1),
        name="ada_mod",
    )(c_pad, w_ada, b_ada.reshape(1, n))
    return out[:b_sz]


def _norm_mod_body(x_ref, g_ref, sc_ref, sh_ref, o_ref):
    x = x_ref[0]
    ms = jnp.mean(x * x, axis=-1, keepdims=True)
    gain = g_ref[...] * (1.0 + sc_ref[0])
    o_ref[0] = (x * lax.rsqrt(ms + EPS) * gain + sh_ref[0]).astype(o_ref.dtype)


def _qkv_body(x_ref, g_ref, sc_ref, sh_ref, w_ref, gq_ref, gk_ref, o_ref, h_ref, wbf_ref, seg_ref,
              *, n_q, n_kv, half):
    attn_w = n_q * HEAD_DIM
    kv_w = n_kv * HEAD_DIM
    _norm_mod_body(x_ref, g_ref, sc_ref, sh_ref, h_ref)

    @pl.when(jnp.logical_and(pl.program_id(0) == 0, pl.program_id(1) == 0))
    def _():
        for slot in range(n_q):
            src = _orig_head(slot, n_kv, n_q // n_kv) * HEAD_DIM
            wbf_ref[:, slot * HEAD_DIM:(slot + 1) * HEAD_DIM] = w_ref[:, src:src + HEAD_DIM].astype(BF16)
        wbf_ref[:, attn_w:] = w_ref[:, attn_w:].astype(BF16)
        r = lax.broadcasted_iota(jnp.int32, (half, half), 0) // HEAD_DIM
        c = lax.broadcasted_iota(jnp.int32, (half, half), 1) // HEAD_DIM
        seg_ref[...] = (r == c).astype(BF16)

    a = h_ref[0]

    def head_norm(acc, seg, gain):
        ss = jnp.dot((acc * acc).astype(BF16), seg, preferred_element_type=F32)
        return acc * lax.rsqrt(ss / HEAD_DIM + EPS) * gain

    for c0 in range(0, attn_w, half):
        acc = jnp.dot(a, wbf_ref[:, c0:c0 + half], preferred_element_type=F32)
        o_ref[:, c0:c0 + half] = head_norm(acc, seg_ref[...], gq_ref[...]).astype(o_ref.dtype)
    acc = jnp.dot(a, wbf_ref[:, attn_w:attn_w + kv_w], preferred_element_type=F32)
    o_ref[:, attn_w:attn_w + kv_w] = head_norm(acc, seg_ref[:kv_w, :kv_w], gk_ref[...]).astype(o_ref.dtype)
    acc = jnp.dot(a, wbf_ref[:, attn_w + kv_w:], preferred_element_type=F32)
    o_ref[:, attn_w + kv_w:] = acc.astype(o_ref.dtype)


def _qkv_proj(x, norm_g, mod3, scale_idx, shift_idx, w_in, layer, q_g, k_g, n_q, n_kv, ts=512, half=512):
    b_sz, s_len, d = x.shape
    attn_w, kv_w = n_q * HEAD_DIM, n_kv * HEAD_DIM
    width = attn_w + 2 * kv_w
    assert attn_w % half == 0 and kv_w <= half and half % HEAD_DIM == 0
    gq = jnp.tile(q_g * (HEAD_DIM ** -0.5 * LOG2E), half // HEAD_DIM).reshape(1, half)
    gk = jnp.tile(k_g, n_kv).reshape(1, kv_w)
    tiles = s_len // ts
    const2 = lambda b, s: (0, 0)
    body = functools.partial(_qkv_body, n_q=n_q, n_kv=n_kv, half=half)
    return pl.pallas_call(
        body,
        grid=(b_sz, tiles),
        in_specs=[
            pl.BlockSpec((1, ts, d), lambda b, s: (b, s, 0)),
            pl.BlockSpec((1, d), const2),
            pl.BlockSpec((1, 1, d), lambda b, s: (b * N_MOD + scale_idx, 0, 0)),
            pl.BlockSpec((1, 1, d), lambda b, s: (b * N_MOD + shift_idx, 0, 0)),
            pl.BlockSpec((None, d, width), lambda b, s: (layer, 0, 0), pipeline_mode=pl.Buffered(1)),
            pl.BlockSpec((1, half), const2),
            pl.BlockSpec((1, kv_w), const2),
        ],
        out_specs=[
            pl.BlockSpec((ts, width), lambda b, s: (b * tiles + s, 0)),
            pl.BlockSpec((1, ts, d), lambda b, s: (b, s, 0)),
        ],
        out_shape=[
            jax.ShapeDtypeStruct((b_sz * s_len, width), BF16),
            jax.ShapeDtypeStruct((b_sz, s_len, d), BF16),
        ],
        scratch_shapes=[pltpu.VMEM((d, width), BF16), pltpu.VMEM((half, half), BF16)],
        compiler_params=_params(2),
        name="qkv_proj",
    )(x, norm_g.reshape(1, d), mod3, mod3, w_in, gq, gk)


def _cast_weights_once(w_refs, wbf_refs):
    @pl.when(pl.program_id(1) == 0)
    def _():
        for w_ref, wbf_ref in zip(w_refs, wbf_refs):
            wbf_ref[...] = w_ref[...].astype(BF16)


def _proj_body(a_ref, w_ref, *refs):
    n_jobs = (len(refs) - 2) // 2
    job_in, o_ref, job_out, wbf_ref = refs[:n_jobs], refs[n_jobs], refs[n_jobs + 1:-1], refs[-1]
    _cast_weights_once([w_ref], [wbf_ref])
    for src_ref, dst_ref in zip(job_in, job_out):
        dst_ref[...] = src_ref[...].astype(BF16)
    for r0 in range(0, a_ref.shape[0], _MATMUL_ROWS):
        rs = slice(r0, r0 + _MATMUL_ROWS)
        acc = jnp.dot(a_ref[rs, :], wbf_ref[...], preferred_element_type=F32)
        o_ref[rs, :] = acc.astype(o_ref.dtype)


def _proj(h, w, layer, col0, n, tm, tn, cast_jobs=()):
    m, k = h.shape
    assert col0 % tn == 0 and n % tn == 0
    blk0 = col0 // tn
    m_tiles = m // tm
    steps = (n // tn) * m_tiles
    step = lambda j, i: j * m_tiles + i
    job_in, job_out, job_shapes = [], [], []
    for wj, row_block_of_step in cast_jobs:
        rows, cols = wj.shape[1:]
        assert rows % steps == 0
        slab = rows // steps
        job_in.append(pl.BlockSpec((None, slab, cols),
                                   lambda j, i, f=row_block_of_step: (layer, f(step(j, i)), 0)))
        job_out.append(pl.BlockSpec((slab, cols), lambda j, i: (step(j, i), 0)))
        job_shapes.append(jax.ShapeDtypeStruct((rows, cols), BF16))
    return pl.pallas_call(
        _proj_body,
        grid=(n // tn, m_tiles),
        in_specs=[
            pl.BlockSpec((tm, k), lambda j, i: (i, 0)),
            pl.BlockSpec((None, k, tn), lambda j, i: (layer, 0, blk0 + j)),
        ] + job_in,
        out_specs=[pl.BlockSpec((tm, tn), lambda j, i: (i, j))] + job_out,
        out_shape=[jax.ShapeDtypeStruct((m, n), BF16)] + job_shapes,
        scratch_shapes=[pltpu.VMEM((k, tn), BF16)],
        compiler_params=_params(2),
        name="proj",
    )(h, w, *[wj for wj, _ in cast_jobs])


_EPILOGUE_LANES = 512
_MATMUL_ROWS = 1024


def _merge_body(ya_ref, yc_ref, wa_ref, wc_ref, ga_ref, gc_ref, o_ref):
    ya = ya_ref[...]
    yc = yc_ref[...]
    for n0 in range(0, o_ref.shape[1], _EPILOGUE_LANES):
        ns = slice(n0, n0 + _EPILOGUE_LANES)
        acc_a = jnp.dot(ya, wa_ref[:, ns], preferred_element_type=F32)
        acc_c = jnp.dot(yc, wc_ref[:, ns], preferred_element_type=F32)
        ga = _sigmoid(ga_ref[:, ns].astype(F32))
        gc = _sigmoid(gc_ref[:, ns].astype(F32))
        o_ref[:, ns] = (ga * acc_a + gc * acc_c).astype(o_ref.dtype)


def _merge(attn, conv, wa_bf, wc_bf, gates, ga_col, gc_col, tm=512):
    m, ka = attn.shape
    kc = conv.shape[1]
    n = wa_bf.shape[1]
    assert ga_col % n == 0 and gc_col % n == 0 and n % _EPILOGUE_LANES == 0
    ga_blk, gc_blk = ga_col // n, gc_col // n
    resident = pl.Buffered(1)
    return pl.pallas_call(
        _merge_body,
        grid=(m // tm,),
        in_specs=[
            pl.BlockSpec((tm, ka), lambda i: (i, 0)),
            pl.BlockSpec((tm, kc), lambda i: (i, 0)),
            pl.BlockSpec((ka, n), lambda i: (0, 0), pipeline_mode=resident),
            pl.BlockSpec((kc, n), lambda i: (0, 0), pipeline_mode=resident),
            pl.BlockSpec((tm, n), lambda i: (i, ga_blk)),
            pl.BlockSpec((tm, n), lambda i: (i, gc_blk)),
        ],
        out_specs=pl.BlockSpec((tm, n), lambda i: (i, 0)),
        out_shape=jax.ShapeDtypeStruct((m, n), BF16),
        compiler_params=_params(1),
        name="merge",
    )(attn, conv, wa_bf, wc_bf, gates, gates)


def _mix_out_body(a_ref, w_ref, x_ref, gt_ref, g_ref, sc_ref, sh_ref, x1_ref, h_ref):
    a = a_ref[...]
    ts, d = a.shape
    sumsq = jnp.zeros((ts, 1), F32)
    for n0 in range(0, d, _EPILOGUE_LANES):
        ns = slice(n0, n0 + _EPILOGUE_LANES)
        acc = jnp.dot(a, w_ref[:, ns], preferred_element_type=F32)
        x1 = x_ref[0, :, ns] + gt_ref[0, :, ns] * acc
        x1_ref[0, :, ns] = x1
        sumsq = sumsq + jnp.sum(x1 * x1, axis=-1, keepdims=True)
    inv = lax.rsqrt(sumsq / d + EPS)
    gain = g_ref[...] * (1.0 + sc_ref[0])
    h_ref[0] = (x1_ref[0] * inv * gain + sh_ref[0]).astype(h_ref.dtype)


def _mix_out(a, w_bf, x, mod3, gate_idx, norm_g, scale_idx, shift_idx, ts=512):
    b_sz, s_len, d = x.shape
    k = a.shape[1]
    assert d % _EPILOGUE_LANES == 0
    tiles = s_len // ts
    const2 = lambda b, s: (0, 0)
    mod_row = lambda idx: pl.BlockSpec((1, 1, d), lambda b, s: (b * N_MOD + idx, 0, 0))
    act = pl.BlockSpec((1, ts, d), lambda b, s: (b, s, 0))
    return pl.pallas_call(
        _mix_out_body,
        grid=(b_sz, tiles),
        in_specs=[
            pl.BlockSpec((ts, k), lambda b, s: (b * tiles + s, 0)),
            pl.BlockSpec((k, d), const2, pipeline_mode=pl.Buffered(1)),
            act,
            mod_row(gate_idx),
            pl.BlockSpec((1, d), const2),
            mod_row(scale_idx),
            mod_row(shift_idx),
        ],
        out_specs=[act, act],
        out_shape=[jax.ShapeDtypeStruct((b_sz, s_len, d), F32), jax.ShapeDtypeStruct((b_sz, s_len, d), BF16)],
        compiler_params=_params(2),
        name="mix_out",
    )(a, w_bf, x, mod3, norm_g.reshape(1, d), mod3, mod3)


def _residual_proj_body(a_ref, w_ref, x_ref, gt_ref, o_ref):
    a = a_ref[...]
    for n0 in range(0, o_ref.shape[1], _EPILOGUE_LANES):
        ns = slice(n0, n0 + _EPILOGUE_LANES)
        acc = jnp.dot(a, w_ref[:, ns], preferred_element_type=F32)
        o_ref[:, ns] = x_ref[:, ns] + gt_ref[0, :, ns] * acc


def _residual_proj(a, w_bf, x2d, mod3, gate_idx, rows_per_batch, tm, tn):
    m, k = a.shape
    n = w_bf.shape[1]
    assert n % tn == 0 and tn % _EPILOGUE_LANES == 0 and rows_per_batch % tm == 0
    tiles_per_batch = rows_per_batch // tm
    return pl.pallas_call(
        _residual_proj_body,
        grid=(n // tn, m // tm),
        in_specs=[
            pl.BlockSpec((tm, k), lambda j, i: (i, 0)),
            pl.BlockSpec((k, tn), lambda j, i: (0, j)),
            pl.BlockSpec((tm, tn), lambda j, i: (i, j)),
            pl.BlockSpec((1, 1, tn), lambda j, i: ((i // tiles_per_batch) * N_MOD + gate_idx, 0, j)),
        ],
        out_specs=pl.BlockSpec((tm, tn), lambda j, i: (i, j)),
        out_shape=jax.ShapeDtypeStruct((m, n), F32),
        compiler_params=_params(2),
        name="residual_proj",
    )(a, w_bf, x2d, mod3)


def _ffn_in_body(a_ref, wg_ref, wu_ref, wo_ref, o_ref, wobf_ref, wgbf_ref, wubf_ref):
    _cast_weights_once([wg_ref, wu_ref], [wgbf_ref, wubf_ref])
    wobf_ref[...] = wo_ref[...].astype(BF16)
    for r0 in range(0, a_ref.shape[0], _MATMUL_ROWS):
        rs = slice(r0, r0 + _MATMUL_ROWS)
        a = a_ref[rs, :]
        gate = jnp.dot(a, wgbf_ref[...], preferred_element_type=F32)
        up = jnp.dot(a, wubf_ref[...], preferred_element_type=F32)
        o_ref[rs, :] = (gate * _sigmoid(gate) * up).astype(o_ref.dtype)


def _ffn_in(h, w_ffn_in, w_ffn_out, layer, tm=2048, tn=512):
    m, k = h.shape
    d_ff = w_ffn_in.shape[2] // 2
    d_out = w_ffn_out.shape[2]
    up_blk = d_ff // tn
    m_tiles = m // tm
    steps = (d_ff // tn) * m_tiles
    assert d_ff % steps == 0
    slab = d_ff // steps
    step = lambda j, i: j * m_tiles + i
    return pl.pallas_call(
        _ffn_in_body,
        grid=(d_ff // tn, m_tiles),
        in_specs=[
            pl.BlockSpec((tm, k), lambda j, i: (i, 0)),
            pl.BlockSpec((None, k, tn), lambda j, i: (layer, 0, j)),
            pl.BlockSpec((None, k, tn), lambda j, i: (layer, 0, up_blk + j)),
            pl.BlockSpec((None, slab, d_out), lambda j, i: (layer, step(j, i), 0)),
        ],
        out_specs=[
            pl.BlockSpec((tm, tn), lambda j, i: (i, j)),
            pl.BlockSpec((slab, d_out), lambda j, i: (step(j, i), 0)),
        ],
        out_shape=[jax.ShapeDtypeStruct((m, d_ff), BF16), jax.ShapeDtypeStruct((d_ff, d_out), BF16)],
        scratch_shapes=[pltpu.VMEM((k, tn), BF16), pltpu.VMEM((k, tn), BF16)],
        compiler_params=_params(2),
        name="ffn_in",
    )(h, w_ffn_in, w_ffn_in, w_ffn_out)


def _t5_causal_bucket(dist):
    n = jnp.maximum(dist, 0)
    nf = jnp.maximum(n, 1).astype(jnp.float32)
    large = MAX_EXACT + (jnp.log(nf / MAX_EXACT) / math.log(MAX_DISTANCE / MAX_EXACT)
                         * (NUM_BUCKETS - MAX_EXACT)).astype(jnp.int32)
    large = jnp.minimum(large, NUM_BUCKETS - 1)
    return jnp.where(n < MAX_EXACT, n, large)


def _band_buckets():
    q_off = jnp.arange(BLOCK)
    k_off = jnp.arange(2 * BLOCK)
    dist = q_off[:, None] + BLOCK - k_off[None, :]
    allowed = (dist >= 0) & (dist < WINDOW)
    return jnp.where(allowed, _t5_causal_bucket(dist), -1).astype(jnp.int32)


def _attn_build_bias(first_step, rb_ref, sink_ref, bucket_ref, bias_ref, *, n_q, n_kv):
    q_per_kv = n_q // n_kv

    @pl.when(first_step)
    def _():
        bucket = bucket_ref[...]
        col = lax.broadcasted_iota(jnp.int32, bucket.shape, 1)
        for slot in range(n_q):
            head = _orig_head(slot, n_kv, q_per_kv)
            tile = jnp.zeros(bucket.shape, F32)
            for b in range(NUM_BUCKETS):
                tile = jnp.where(bucket == b, rb_ref[b * n_q + head] * LOG2E, tile)
            tile = jnp.where(bucket < 0, -jnp.inf, tile)
            sink = sink_ref[head] * LOG2E
            bias_ref[0, slot] = jnp.where(col == 0, sink, tile)
            bias_ref[1, slot] = jnp.where(col == 0, sink, jnp.where(col < BLOCK, -jnp.inf, tile))


def _attn_block(seq_start, qmask_ref, krow_ref, q_ref, kc_ref, kp_ref, vc_ref, vp_ref, o_ref,
                bias_ref, lg_ref, p_ref, pv_ref, rs_ref, *, n_q, n_kv):
    q_per_kv = n_q // n_kv
    kv_w = n_kv * HEAD_DIM
    lane_slot = lax.broadcasted_iota(jnp.int32, (BLOCK, BLOCK), 1) // HEAD_DIM
    ones = jnp.ones((2 * BLOCK, BLOCK), BF16)
    bias_idx = jnp.where(seq_start, 1, 0)

    keys = jnp.concatenate([kp_ref[...], kc_ref[...]], axis=0) * krow_ref[...]
    vals = jnp.concatenate([vp_ref[...], vc_ref[...]], axis=0) * krow_ref[...]
    lhs = jnp.concatenate(
        [q_ref[:, j * kv_w:(j + 1) * kv_w] * qmask_ref[g]
         for j in range(q_per_kv) for g in range(n_kv)], axis=0)
    lg_ref[...] = lax.dot_general(lhs, keys, (((1,), (1,)), ((), ())),
                                  preferred_element_type=F32).reshape(n_q, BLOCK, 2 * BLOCK)
    for slot in range(n_q):
        logit = lg_ref[slot] + bias_ref[bias_idx, slot]
        m = jnp.max(logit, axis=-1, keepdims=True)
        p_ref[slot * BLOCK:(slot + 1) * BLOCK, :] = jnp.exp2(logit - m).astype(BF16)
    p = p_ref[...]
    pv_ref[...] = jnp.dot(p, vals, preferred_element_type=F32).reshape(n_q, BLOCK, kv_w)
    rs_ref[...] = jnp.dot(p, ones, preferred_element_type=F32).reshape(n_q, BLOCK, BLOCK)
    per_half = BLOCK // HEAD_DIM
    for j in range(q_per_kv):
        for half in range(kv_w // BLOCK):
            lanes = slice(half * BLOCK, (half + 1) * BLOCK)
            slots = [j * n_kv + half * per_half + i for i in range(per_half)]
            num = pv_ref[slots[-1], :, lanes]
            den = rs_ref[slots[-1]]
            for i in range(per_half - 2, -1, -1):
                num = jnp.where(lane_slot == i, pv_ref[slots[i], :, lanes], num)
                den = jnp.where(lane_slot == i, rs_ref[slots[i]], den)
            o_ref[:, j * kv_w + half * BLOCK:j * kv_w + (half + 1) * BLOCK] = (
                num * (1.0 / den)).astype(o_ref.dtype)


_CONV_HALO = 32
_CONV_LANES = 256
_SUBLANES = 8


def _conv_zero_history(seq_start, u_ref):
    @pl.when(seq_start)
    def _():
        u_ref[:_CONV_HALO, :] = jnp.zeros((_CONV_HALO, u_ref.shape[1]), F32)


def _conv_block(a_ref, g_ref, w_ref, cb_ref, lg_ref, lb_ref, o_ref, u_ref, sh_ref, y_ref):
    width = w_ref.shape[0]
    ts, channels = o_ref.shape
    u_ref[_CONV_HALO:, :] = a_ref[...].astype(F32) * _sigmoid(g_ref[...].astype(F32))
    span = sh_ref.shape[1]
    for r in range(1, _SUBLANES):
        sh_ref[r - 1] = u_ref[r:r + span, :]
    first = _CONV_HALO - (width - 1)
    for c0 in range(0, channels, _CONV_LANES):
        cs = slice(c0, c0 + _CONV_LANES)
        acc = jnp.broadcast_to(cb_ref[:, cs], (ts, _CONV_LANES))
        for j in range(width):
            tile, r = divmod(first + j, _SUBLANES)
            rows = slice(tile * _SUBLANES, tile * _SUBLANES + ts)
            taps = u_ref[rows, cs] if r == 0 else sh_ref[r - 1, rows, cs]
            acc = acc + taps * w_ref[j:j + 1, cs]
        y_ref[:, cs] = acc
    y = y_ref[...]
    mu = jnp.mean(y, axis=-1, keepdims=True)
    yc = y - mu
    var = jnp.mean(yc * yc, axis=-1, keepdims=True)
    z = yc * lax.rsqrt(var + EPS) * lg_ref[...] + lb_ref[...]
    o_ref[...] = (z * _sigmoid(z)).astype(o_ref.dtype)
    u_ref[:_CONV_HALO, :] = u_ref[ts:, :]


def _core_body(h_ref, w_ref,
               a_ref, g_ref, cw_ref, cb_ref, lg_ref, lb_ref,
               rb_ref, sink_ref, bucket_ref, qmask_ref, krow_ref, q_ref, kc_ref, kp_ref, vc_ref, vp_ref,
               gates_ref, yconv_ref, yattn_ref,
               wbf_ref, u_ref, sh_ref, y_ref, bias_ref, lgt_ref, p_ref, pv_ref, rs_ref,
               *, n_q, n_kv, blocks_per_seq):
    t = pl.program_id(0) * pl.num_programs(1) + pl.program_id(1)
    seq_start = t % blocks_per_seq == 0
    _cast_weights_once([w_ref], [wbf_ref])
    _conv_zero_history(seq_start, u_ref)
    _attn_build_bias(t == 0, rb_ref, sink_ref, bucket_ref, bias_ref, n_q=n_q, n_kv=n_kv)

    acc = jnp.dot(h_ref[...], wbf_ref[...], preferred_element_type=F32)
    gates_ref[...] = acc.astype(gates_ref.dtype)
    _conv_block(a_ref, g_ref, cw_ref, cb_ref, lg_ref, lb_ref, yconv_ref, u_ref, sh_ref, y_ref)
    _attn_block(seq_start, qmask_ref, krow_ref, q_ref, kc_ref, kp_ref, vc_ref, vp_ref, yattn_ref,
                bias_ref, lgt_ref, p_ref, pv_ref, rs_ref, n_q=n_q, n_kv=n_kv)


def _mixer_core(h, w_in, layer, gate_col, conv_in, qkv, conv_w, conv_b, ln_g, ln_b, rel_bias, sinks,
                s_len, n_q, n_kv, tn=512):
    m, k = h.shape
    n = w_in.shape[2] - gate_col
    width, channels = conv_w.shape
    attn_w, kv_w = n_q * HEAD_DIM, n_kv * HEAD_DIM
    n_blocks = m // BLOCK
    n_tiles = n // tn
    assert gate_col % tn == 0 and n % tn == 0 and n_blocks % n_tiles == 0
    assert s_len % BLOCK == 0 and width - 1 <= _CONV_HALO and _CONV_HALO % _SUBLANES == 0
    assert attn_w % kv_w == 0 and kv_w == 2 * BLOCK and BLOCK % HEAD_DIM == 0
    m_tiles = n_blocks // n_tiles
    tm = m // m_tiles
    col_blk0 = gate_col // tn
    k_blk = attn_w // kv_w
    v_blk = k_blk + 1

    step = lambda j, i: j * m_tiles + i
    prev_step = lambda j, i: jnp.maximum(step(j, i) - 1, 0)
    const2 = lambda j, i: (0, 0)
    row = lambda v: v.reshape(1, channels)
    lane_group = jnp.arange(kv_w) // HEAD_DIM
    qmask = jnp.broadcast_to((lane_group[None, :] == jnp.arange(n_kv)[:, None])[:, None, :],
                             (n_kv, BLOCK, kv_w)).astype(BF16)
    krow = jnp.broadcast_to((jnp.arange(2 * BLOCK) > 0)[:, None], (2 * BLOCK, kv_w)).astype(BF16)
    body = functools.partial(_core_body, n_q=n_q, n_kv=n_kv, blocks_per_seq=s_len // BLOCK)
    return pl.pallas_call(
        body,
        grid=(n_tiles, m_tiles),
        in_specs=[
            pl.BlockSpec((tm, k), lambda j, i: (i, 0)),
            pl.BlockSpec((None, k, tn), lambda j, i: (layer, 0, col_blk0 + j)),
            pl.BlockSpec((BLOCK, channels), lambda j, i: (step(j, i), 0)),
            pl.BlockSpec((BLOCK, channels), lambda j, i: (step(j, i), 1)),
            pl.BlockSpec((width, channels), const2),
            pl.BlockSpec((1, channels), const2),
            pl.BlockSpec((1, channels), const2),
            pl.BlockSpec((1, channels), const2),
            pl.BlockSpec(memory_space=pltpu.SMEM),
            pl.BlockSpec(memory_space=pltpu.SMEM),
            pl.BlockSpec((BLOCK, 2 * BLOCK), const2),
            pl.BlockSpec((n_kv, BLOCK, kv_w), lambda j, i: (0, 0, 0)),
            pl.BlockSpec((2 * BLOCK, kv_w), const2),
            pl.BlockSpec((BLOCK, attn_w), lambda j, i: (step(j, i), 0)),
            pl.BlockSpec((BLOCK, kv_w), lambda j, i: (step(j, i), k_blk)),
            pl.BlockSpec((BLOCK, kv_w), lambda j, i: (prev_step(j, i), k_blk)),
            pl.BlockSpec((BLOCK, kv_w), lambda j, i: (step(j, i), v_blk)),
            pl.BlockSpec((BLOCK, kv_w), lambda j, i: (prev_step(j, i), v_blk)),
        ],
        out_specs=[
            pl.BlockSpec((tm, tn), lambda j, i: (i, j)),
            pl.BlockSpec((BLOCK, channels), lambda j, i: (step(j, i), 0)),
            pl.BlockSpec((BLOCK, attn_w), lambda j, i: (step(j, i), 0)),
        ],
        out_shape=[
            jax.ShapeDtypeStruct((m, n), BF16),
            jax.ShapeDtypeStruct((m, channels), BF16),
            jax.ShapeDtypeStruct((m, attn_w), BF16),
        ],
        scratch_shapes=[
            pltpu.VMEM((k, tn), BF16),
            pltpu.VMEM((_CONV_HALO + BLOCK, channels), F32),
            pltpu.VMEM((_SUBLANES - 1, _CONV_HALO - _SUBLANES + BLOCK, channels), F32),
            pltpu.VMEM((BLOCK, channels), F32),
            pltpu.VMEM((2, n_q, BLOCK, 2 * BLOCK), F32),
            pltpu.VMEM((n_q, BLOCK, 2 * BLOCK), F32),
            pltpu.VMEM((n_q * BLOCK, 2 * BLOCK), BF16),
            pltpu.VMEM((n_q, BLOCK, kv_w), F32),
            pltpu.VMEM((n_q, BLOCK, BLOCK), F32),
        ],
        compiler_params=_params(2),
        name="mixer_core",
    )(h, w_in,
      conv_in, conv_in, conv_w, row(conv_b), row(ln_g), row(ln_b),
      rel_bias.astype(F32).reshape(-1), sinks.astype(F32), _band_buckets(), qmask, krow,
      qkv, qkv, qkv, qkv, qkv)


def kernel(x, c, w_ada, b_ada, norm_mix_g, w_in, q_norm_g, k_norm_g, attn_sinks, rel_bias,
           w_attn_out, conv_w, conv_b, conv_ln_g, conv_ln_b, w_conv_out, w_mix_out,
           norm_ffn_g, w_ffn_in, w_ffn_out):
    b_sz, s_len, d = x.shape
    depth = w_ada.shape[0]
    n_q = attn_sinks.shape[1]
    attn_w = w_attn_out.shape[1]
    channels = conv_w.shape[2]
    kv_w = (w_in.shape[2] - attn_w - 2 * channels - 2 * d) // 2
    n_kv = kv_w // HEAD_DIM
    conv_col = attn_w + 2 * kv_w
    gate_col = conv_col + 2 * channels
    m = b_sz * s_len

    for l in range(depth):
        mod = _ada(c, w_ada, b_ada[l], l)
        mod3 = mod.reshape(b_sz * N_MOD, 1, d)

        qkv, h = _qkv_proj(x, norm_mix_g[l], mod3, 1, 0, w_in, l, q_norm_g[l], k_norm_g[l], n_q, n_kv)
        h = h.reshape(m, d)
        proj_tm, proj_tn = 2048, 512
        assert (2 * channels // proj_tn) * (m // proj_tm) == n_q
        slot_head = lambda slot: _orig_head(slot, n_kv, n_q // n_kv)
        same = lambda step: step
        conv_in, wa_bf, wc_bf, wmix_bf = _proj(
            h, w_in, l, conv_col, 2 * channels, proj_tm, proj_tn,
            cast_jobs=((w_attn_out, slot_head), (w_conv_out, same), (w_mix_out, same)))
        gates, y_conv, y_attn = _mixer_core(h, w_in, l, gate_col, conv_in, qkv, conv_w[l], conv_b[l],
                                            conv_ln_g[l], conv_ln_b[l], rel_bias, attn_sinks[l],
                                            s_len, n_q, n_kv)
        merged = _merge(y_attn, y_conv, wa_bf, wc_bf, gates, 0, d)
        x1, h = _mix_out(merged, wmix_bf, x, mod3, 2, norm_ffn_g[l], 4, 3)

        act, wo_bf = _ffn_in(h.reshape(m, d), w_ffn_in, w_ffn_out, l)
        x = _residual_proj(act, wo_bf, x1.reshape(m, d), mod3, 5, s_len,
                           tm=512, tn=1024).reshape(b_sz, s_len, d)
    return x
```

```python
import functools
import math

import jax
import jax.numpy as jnp
from jax import lax
from jax.experimental import pallas as pl
from jax.experimental.pallas import tpu as pltpu

F32 = jnp.float32
BF16 = jnp.bfloat16

HEAD_DIM = 64
WINDOW = 128
BLOCK = 128
NUM_BUCKETS = 32
MAX_EXACT = NUM_BUCKETS // 2
MAX_DISTANCE = 128
N_MOD = 6
EPS = 1e-6
LOG2E = math.log2(math.e)

V7X_VMEM_LIMIT_BYTES = 56 * 1024 * 1024


def _params(n_axes):
    return pltpu.CompilerParams(
        dimension_semantics=("arbitrary",) * n_axes,
        vmem_limit_bytes=V7X_VMEM_LIMIT_BYTES,
    )


def _sigmoid(x):
    return jax.nn.sigmoid(x)


def _orig_head(slot, n_kv, q_per_kv):
    return (slot % n_kv) * q_per_kv + slot // n_kv


def _ada_body(c_ref, w_ref, b_ref, o_ref):
    c = c_ref[...]
    act = (c * _sigmoid(c)).astype(BF16)
    o_ref[...] = jnp.dot(act, w_ref[...].astype(BF16), preferred_element_type=F32) + b_ref[...]


def _ada(c, w_ada, b_ada, layer, tn=1024):
    b_sz, d = c.shape
    n = w_ada.shape[2]
    rows = 8
    c_pad = jnp.pad(c, ((0, rows - b_sz), (0, 0)))
    out = pl.pallas_call(
        _ada_body,
        grid=(n // tn,),
        in_specs=[
            pl.BlockSpec((rows, d), lambda j: (0, 0)),
            pl.BlockSpec((None, d, tn), lambda j: (layer, 0, j)),
            pl.BlockSpec((1, tn), lambda j: (0, j)),
        ],
        out_specs=pl.BlockSpec((rows, tn), lambda j: (0, j)),
        out_shape=jax.ShapeDtypeStruct((rows, n), F32),
        compiler_params=_params(1),
        name="ada_mod",
    )(c_pad, w_ada, b_ada.reshape(1, n))
    return out[:b_sz]


def _norm_mod_body(x_ref, g_ref, sc_ref, sh_ref, o_ref):
    x = x_ref[0]
    ms = jnp.mean(x * x, axis=-1, keepdims=True)
    gain = g_ref[...] * (1.0 + sc_ref[0])
    o_ref[0] = (x * lax.rsqrt(ms + EPS) * gain + sh_ref[0]).astype(o_ref.dtype)


def _qkv_body(x_ref, g_ref, sc_ref, sh_ref, w_ref, gq_ref, gk_ref, o_ref, h_ref, wbf_ref, seg_ref,
              *, n_q, n_kv, half):
    attn_w = n_q * HEAD_DIM
    kv_w = n_kv * HEAD_DIM
    _norm_mod_body(x_ref, g_ref, sc_ref, sh_ref, h_ref)

    @pl.when(jnp.logical_and(pl.program_id(0) == 0, pl.program_id(1) == 0))
    def _():
        for slot in range(n_q):
            src = _orig_head(slot, n_kv, n_q // n_kv) * HEAD_DIM
            wbf_ref[:, slot * HEAD_DIM:(slot + 1) * HEAD_DIM] = w_ref[:, src:src + HEAD_DIM].astype(BF16)
        wbf_ref[:, attn_w:] = w_ref[:, attn_w:].astype(BF16)
        r = lax.broadcasted_iota(jnp.int32, (half, half), 0) // HEAD_DIM
        c = lax.broadcasted_iota(jnp.int32, (half, half), 1) // HEAD_DIM
        seg_ref[...] = (r == c).astype(BF16)

    a = h_ref[0]

    def head_norm(acc, seg, gain):
        ss = jnp.dot((acc * acc).astype(BF16), seg, preferred_element_type=F32)
        return acc * lax.rsqrt(ss / HEAD_DIM + EPS) * gain

    for c0 in range(0, attn_w, half):
        acc = jnp.dot(a, wbf_ref[:, c0:c0 + half], preferred_element_type=F32)
        o_ref[:, c0:c0 + half] = head_norm(acc, seg_ref[...], gq_ref[...]).astype(o_ref.dtype)
    acc = jnp.dot(a, wbf_ref[:, attn_w:attn_w + kv_w], preferred_element_type=F32)
    o_ref[:, attn_w:attn_w + kv_w] = head_norm(acc, seg_ref[:kv_w, :kv_w], gk_ref[...]).astype(o_ref.dtype)
    acc = jnp.dot(a, wbf_ref[:, attn_w + kv_w:], preferred_element_type=F32)
    o_ref[:, attn_w + kv_w:] = acc.astype(o_ref.dtype)


def _qkv_proj(x, norm_g, mod3, scale_idx, shift_idx, w_in, layer, q_g, k_g, n_q, n_kv, ts=512, half=512):
    b_sz, s_len, d = x.shape
    attn_w, kv_w = n_q * HEAD_DIM, n_kv * HEAD_DIM
    width = attn_w + 2 * kv_w
    assert attn_w % half == 0 and kv_w <= half and half % HEAD_DIM == 0
    gq = jnp.tile(q_g * (HEAD_DIM ** -0.5 * LOG2E), half // HEAD_DIM).reshape(1, half)
    gk = jnp.tile(k_g, n_kv).reshape(1, kv_w)
    tiles = s_len // ts
    const2 = lambda b, s: (0, 0)
    body = functools.partial(_qkv_body, n_q=n_q, n_kv=n_kv, half=half)
    return pl.pallas_call(
        body,
        grid=(b_sz, tiles),
        in_specs=[
            pl.BlockSpec((1, ts, d), lambda b, s: (b, s, 0)),
            pl.BlockSpec((1, d), const2),
            pl.BlockSpec((1, 1, d), lambda b, s: (b * N_MOD + scale_idx, 0, 0)),
            pl.BlockSpec((1, 1, d), lambda b, s: (b * N_MOD + shift_idx, 0, 0)),
            pl.BlockSpec((None, d, width), lambda b, s: (layer, 0, 0), pipeline_mode=pl.Buffered(1)),
            pl.BlockSpec((1, half), const2),
            pl.BlockSpec((1, kv_w), const2),
        ],
        out_specs=[
            pl.BlockSpec((ts, width), lambda b, s: (b * tiles + s, 0)),
            pl.BlockSpec((1, ts, d), lambda b, s: (b, s, 0)),
        ],
        out_shape=[
            jax.ShapeDtypeStruct((b_sz * s_len, width), BF16),
            jax.ShapeDtypeStruct((b_sz, s_len, d), BF16),
        ],
        scratch_shapes=[pltpu.VMEM((d, width), BF16), pltpu.VMEM((half, half), BF16)],
        compiler_params=_params(2),
        name="qkv_proj",
    )(x, norm_g.reshape(1, d), mod3, mod3, w_in, gq, gk)


def _cast_weights_once(w_refs, wbf_refs):
    @pl.when(pl.program_id(1) == 0)
    def _():
        for w_ref, wbf_ref in zip(w_refs, wbf_refs):
            wbf_ref[...] = w_ref[...].astype(BF16)


def _glu_proj_body(h_ref, wa_ref, wg_ref, *refs):
    n_jobs = (len(refs) - 3) // 2
    job_in, o_ref, job_out = refs[:n_jobs], refs[n_jobs], refs[n_jobs + 1:-2]
    wabf_ref, wgbf_ref = refs[-2:]
    _cast_weights_once([wa_ref, wg_ref], [wabf_ref, wgbf_ref])
    for src_ref, dst_ref in zip(job_in, job_out):
        dst_ref[...] = src_ref[...].astype(BF16)
    for r0 in range(0, h_ref.shape[0], _MATMUL_ROWS):
        rs = slice(r0, r0 + _MATMUL_ROWS)
        h = h_ref[rs, :]
        a = jnp.dot(h, wabf_ref[...], preferred_element_type=F32)
        g = jnp.dot(h, wgbf_ref[...], preferred_element_type=F32)
        o_ref[rs, :] = (a * _sigmoid(g)).astype(o_ref.dtype)


def _glu_proj(h, w, layer, a_col, g_col, n, tm, tn, cast_jobs=()):
    m, k = h.shape
    assert a_col % tn == 0 and g_col % tn == 0 and n % tn == 0
    a_blk0, g_blk0 = a_col // tn, g_col // tn
    m_tiles = m // tm
    steps = (n // tn) * m_tiles
    step = lambda j, i: j * m_tiles + i
    job_in, job_out, job_shapes = [], [], []
    for wj, row_block_of_step in cast_jobs:
        rows, cols = wj.shape[1:]
        assert rows % steps == 0
        slab = rows // steps
        job_in.append(pl.BlockSpec((None, slab, cols),
                                   lambda j, i, f=row_block_of_step: (layer, f(step(j, i)), 0)))
        job_out.append(pl.BlockSpec((slab, cols), lambda j, i: (step(j, i), 0)))
        job_shapes.append(jax.ShapeDtypeStruct((rows, cols), BF16))
    return pl.pallas_call(
        _glu_proj_body,
        grid=(n // tn, m_tiles),
        in_specs=[
            pl.BlockSpec((tm, k), lambda j, i: (i, 0)),
            pl.BlockSpec((None, k, tn), lambda j, i: (layer, 0, a_blk0 + j)),
            pl.BlockSpec((None, k, tn), lambda j, i: (layer, 0, g_blk0 + j)),
        ] + job_in,
        out_specs=[pl.BlockSpec((tm, tn), lambda j, i: (i, j))] + job_out,
        out_shape=[jax.ShapeDtypeStruct((m, n), BF16)] + job_shapes,
        scratch_shapes=[pltpu.VMEM((k, tn), BF16), pltpu.VMEM((k, tn), BF16)],
        compiler_params=_params(2),
        name="glu_proj",
    )(h, w, w, *[wj for wj, _ in cast_jobs])


_EPILOGUE_LANES = 512
_MATMUL_ROWS = 1024


def _merge_body(ya_ref, yc_ref, wa_ref, wc_ref, ga_ref, gc_ref, o_ref):
    ya = ya_ref[...]
    yc = yc_ref[...]
    for n0 in range(0, o_ref.shape[1], _EPILOGUE_LANES):
        ns = slice(n0, n0 + _EPILOGUE_LANES)
        acc_a = jnp.dot(ya, wa_ref[:, ns], preferred_element_type=F32)
        acc_c = jnp.dot(yc, wc_ref[:, ns], preferred_element_type=F32)
        ga = _sigmoid(ga_ref[:, ns].astype(F32))
        gc = _sigmoid(gc_ref[:, ns].astype(F32))
        o_ref[:, ns] = (ga * acc_a + gc * acc_c).astype(o_ref.dtype)


def _merge(attn, conv, wa_bf, wc_bf, gates, ga_col, gc_col, tm=512):
    m, ka = attn.shape
    kc = conv.shape[1]
    n = wa_bf.shape[1]
    assert ga_col % n == 0 and gc_col % n == 0 and n % _EPILOGUE_LANES == 0
    ga_blk, gc_blk = ga_col // n, gc_col // n
    resident = pl.Buffered(1)
    return pl.pallas_call(
        _merge_body,
        grid=(m // tm,),
        in_specs=[
            pl.BlockSpec((tm, ka), lambda i: (i, 0)),
            pl.BlockSpec((tm, kc), lambda i: (i, 0)),
            pl.BlockSpec((ka, n), lambda i: (0, 0), pipeline_mode=resident),
            pl.BlockSpec((kc, n), lambda i: (0, 0), pipeline_mode=resident),
            pl.BlockSpec((tm, n), lambda i: (i, ga_blk)),
            pl.BlockSpec((tm, n), lambda i: (i, gc_blk)),
        ],
        out_specs=pl.BlockSpec((tm, n), lambda i: (i, 0)),
        out_shape=jax.ShapeDtypeStruct((m, n), BF16),
        compiler_params=_params(1),
        name="merge",
    )(attn, conv, wa_bf, wc_bf, gates, gates)


def _mix_out_body(a_ref, w_ref, x_ref, gt_ref, g_ref, sc_ref, sh_ref, x1_ref, h_ref):
    a = a_ref[...]
    ts, d = a.shape
    sumsq = jnp.zeros((ts, 1), F32)
    for n0 in range(0, d, _EPILOGUE_LANES):
        ns = slice(n0, n0 + _EPILOGUE_LANES)
        acc = jnp.dot(a, w_ref[:, ns], preferred_element_type=F32)
        x1 = x_ref[0, :, ns] + gt_ref[0, :, ns] * acc
        x1_ref[0, :, ns] = x1
        sumsq = sumsq + jnp.sum(x1 * x1, axis=-1, keepdims=True)
    inv = lax.rsqrt(sumsq / d + EPS)
    gain = g_ref[...] * (1.0 + sc_ref[0])
    h_ref[0] = (x1_ref[0] * inv * gain + sh_ref[0]).astype(h_ref.dtype)


def _mix_out(a, w_bf, x, mod3, gate_idx, norm_g, scale_idx, shift_idx, ts=512):
    b_sz, s_len, d = x.shape
    k = a.shape[1]
    assert d % _EPILOGUE_LANES == 0
    tiles = s_len // ts
    const2 = lambda b, s: (0, 0)
    mod_row = lambda idx: pl.BlockSpec((1, 1, d), lambda b, s: (b * N_MOD + idx, 0, 0))
    act = pl.BlockSpec((1, ts, d), lambda b, s: (b, s, 0))
    return pl.pallas_call(
        _mix_out_body,
        grid=(b_sz, tiles),
        in_specs=[
            pl.BlockSpec((ts, k), lambda b, s: (b * tiles + s, 0)),
            pl.BlockSpec((k, d), const2, pipeline_mode=pl.Buffered(1)),
            act,
            mod_row(gate_idx),
            pl.BlockSpec((1, d), const2),
            mod_row(scale_idx),
            mod_row(shift_idx),
        ],
        out_specs=[act, act],
        out_shape=[jax.ShapeDtypeStruct((b_sz, s_len, d), F32), jax.ShapeDtypeStruct((b_sz, s_len, d), BF16)],
        compiler_params=_params(2),
        name="mix_out",
    )(a, w_bf, x, mod3, norm_g.reshape(1, d), mod3, mod3)


def _residual_proj_body(a_ref, w_ref, x_ref, gt_ref, o_ref):
    a = a_ref[...]
    for n0 in range(0, o_ref.shape[1], _EPILOGUE_LANES):
        ns = slice(n0, n0 + _EPILOGUE_LANES)
        acc = jnp.dot(a, w_ref[:, ns], preferred_element_type=F32)
        o_ref[:, ns] = x_ref[:, ns] + gt_ref[0, :, ns] * acc


def _residual_proj(a, w_bf, x2d, mod3, gate_idx, rows_per_batch, tm, tn):
    m, k = a.shape
    n = w_bf.shape[1]
    assert n % tn == 0 and tn % _EPILOGUE_LANES == 0 and rows_per_batch % tm == 0
    tiles_per_batch = rows_per_batch // tm
    return pl.pallas_call(
        _residual_proj_body,
        grid=(n // tn, m // tm),
        in_specs=[
            pl.BlockSpec((tm, k), lambda j, i: (i, 0)),
            pl.BlockSpec((k, tn), lambda j, i: (0, j)),
            pl.BlockSpec((tm, tn), lambda j, i: (i, j)),
            pl.BlockSpec((1, 1, tn), lambda j, i: ((i // tiles_per_batch) * N_MOD + gate_idx, 0, j)),
        ],
        out_specs=pl.BlockSpec((tm, tn), lambda j, i: (i, j)),
        out_shape=jax.ShapeDtypeStruct((m, n), F32),
        compiler_params=_params(2),
        name="residual_proj",
    )(a, w_bf, x2d, mod3)


def _ffn_in_body(a_ref, wg_ref, wu_ref, wo_ref, o_ref, wobf_ref, wgbf_ref, wubf_ref):
    _cast_weights_once([wg_ref, wu_ref], [wgbf_ref, wubf_ref])
    wobf_ref[...] = wo_ref[...].astype(BF16)
    for r0 in range(0, a_ref.shape[0], _MATMUL_ROWS):
        rs = slice(r0, r0 + _MATMUL_ROWS)
        a = a_ref[rs, :]
        gate = jnp.dot(a, wgbf_ref[...], preferred_element_type=F32)
        up = jnp.dot(a, wubf_ref[...], preferred_element_type=F32)
        o_ref[rs, :] = (gate * _sigmoid(gate) * up).astype(o_ref.dtype)


def _ffn_in(h, w_ffn_in, w_ffn_out, layer, tm=2048, tn=512):
    m, k = h.shape
    d_ff = w_ffn_in.shape[2] // 2
    d_out = w_ffn_out.shape[2]
    up_blk = d_ff // tn
    m_tiles = m // tm
    steps = (d_ff // tn) * m_tiles
    assert d_ff % steps == 0
    slab = d_ff // steps
    step = lambda j, i: j * m_tiles + i
    return pl.pallas_call(
        _ffn_in_body,
        grid=(d_ff // tn, m_tiles),
        in_specs=[
            pl.BlockSpec((tm, k), lambda j, i: (i, 0)),
            pl.BlockSpec((None, k, tn), lambda j, i: (layer, 0, j)),
            pl.BlockSpec((None, k, tn), lambda j, i: (layer, 0, up_blk + j)),
            pl.BlockSpec((None, slab, d_out), lambda j, i: (layer, step(j, i), 0)),
        ],
        out_specs=[
            pl.BlockSpec((tm, tn), lambda j, i: (i, j)),
            pl.BlockSpec((slab, d_out), lambda j, i: (step(j, i), 0)),
        ],
        out_shape=[jax.ShapeDtypeStruct((m, d_ff), BF16), jax.ShapeDtypeStruct((d_ff, d_out), BF16)],
        scratch_shapes=[pltpu.VMEM((k, tn), BF16), pltpu.VMEM((k, tn), BF16)],
        compiler_params=_params(2),
        name="ffn_in",
    )(h, w_ffn_in, w_ffn_in, w_ffn_out)


def _t5_causal_bucket(dist):
    n = jnp.maximum(dist, 0)
    nf = jnp.maximum(n, 1).astype(jnp.float32)
    large = MAX_EXACT + (jnp.log(nf / MAX_EXACT) / math.log(MAX_DISTANCE / MAX_EXACT)
                         * (NUM_BUCKETS - MAX_EXACT)).astype(jnp.int32)
    large = jnp.minimum(large, NUM_BUCKETS - 1)
    return jnp.where(n < MAX_EXACT, n, large)


def _band_buckets():
    q_off = jnp.arange(BLOCK)
    k_off = jnp.arange(2 * BLOCK)
    dist = q_off[:, None] + BLOCK - k_off[None, :]
    allowed = (dist >= 0) & (dist < WINDOW)
    return jnp.where(allowed, _t5_causal_bucket(dist), -1).astype(jnp.int32)


def _attn_build_bias(first_step, rb_ref, sink_ref, bucket_ref, bias_ref, *, n_q, n_kv):
    q_per_kv = n_q // n_kv

    @pl.when(first_step)
    def _():
        bucket = bucket_ref[...]
        col = lax.broadcasted_iota(jnp.int32, bucket.shape, 1)
        for slot in range(n_q):
            head = _orig_head(slot, n_kv, q_per_kv)
            tile = jnp.zeros(bucket.shape, F32)
            for b in range(NUM_BUCKETS):
                tile = jnp.where(bucket == b, rb_ref[b * n_q + head] * LOG2E, tile)
            tile = jnp.where(bucket < 0, -jnp.inf, tile)
            sink = sink_ref[head] * LOG2E
            bias_ref[0, slot] = jnp.where(col == 0, sink, tile)
            bias_ref[1, slot] = jnp.where(col == 0, sink, jnp.where(col < BLOCK, -jnp.inf, tile))


def _attn_block(seq_start, qmask_ref, krow_ref, q_ref, kc_ref, kp_ref, vc_ref, vp_ref, o_ref,
                bias_ref, lg_ref, p_ref, pv_ref, rs_ref, *, n_q, n_kv):
    q_per_kv = n_q // n_kv
    kv_w = n_kv * HEAD_DIM
    lane_slot = lax.broadcasted_iota(jnp.int32, (BLOCK, BLOCK), 1) // HEAD_DIM
    ones = jnp.ones((2 * BLOCK, BLOCK), BF16)
    bias_idx = jnp.where(seq_start, 1, 0)

    keys = jnp.concatenate([kp_ref[...], kc_ref[...]], axis=0) * krow_ref[...]
    vals = jnp.concatenate([vp_ref[...], vc_ref[...]], axis=0) * krow_ref[...]
    lhs = jnp.concatenate(
        [q_ref[:, j * kv_w:(j + 1) * kv_w] * qmask_ref[g]
         for j in range(q_per_kv) for g in range(n_kv)], axis=0)
    lg_ref[...] = lax.dot_general(lhs, keys, (((1,), (1,)), ((), ())),
                                  preferred_element_type=F32).reshape(n_q, BLOCK, 2 * BLOCK)
    for slot in range(n_q):
        logit = lg_ref[slot] + bias_ref[bias_idx, slot]
        m = jnp.max(logit, axis=-1, keepdims=True)
        p_ref[slot * BLOCK:(slot + 1) * BLOCK, :] = jnp.exp2(logit - m).astype(BF16)
    p = p_ref[...]
    pv_ref[...] = jnp.dot(p, vals, preferred_element_type=F32).reshape(n_q, BLOCK, kv_w)
    rs_ref[...] = jnp.dot(p, ones, preferred_element_type=F32).reshape(n_q, BLOCK, BLOCK)
    per_half = BLOCK // HEAD_DIM
    for j in range(q_per_kv):
        for half in range(kv_w // BLOCK):
            lanes = slice(half * BLOCK, (half + 1) * BLOCK)
            slots = [j * n_kv + half * per_half + i for i in range(per_half)]
            num = pv_ref[slots[-1], :, lanes]
            den = rs_ref[slots[-1]]
            for i in range(per_half - 2, -1, -1):
                num = jnp.where(lane_slot == i, pv_ref[slots[i], :, lanes], num)
                den = jnp.where(lane_slot == i, rs_ref[slots[i]], den)
            o_ref[:, j * kv_w + half * BLOCK:j * kv_w + (half + 1) * BLOCK] = (
                num * (1.0 / den)).astype(o_ref.dtype)


_CONV_HALO = 32
_CONV_LANES = 256
_SUBLANES = 8


def _conv_zero_history(seq_start, u_ref):
    @pl.when(seq_start)
    def _():
        u_ref[:_CONV_HALO, :] = jnp.zeros((_CONV_HALO, u_ref.shape[1]), F32)


def _conv_block(uin_ref, w_ref, cb_ref, lg_ref, lb_ref, o_ref, u_ref, sh_ref, y_ref):
    width = w_ref.shape[0]
    ts, channels = o_ref.shape
    u_ref[_CONV_HALO:, :] = uin_ref[...].astype(F32)
    span = sh_ref.shape[1]
    for r in range(1, _SUBLANES):
        sh_ref[r - 1] = u_ref[r:r + span, :]
    first = _CONV_HALO - (width - 1)
    for c0 in range(0, channels, _CONV_LANES):
        cs = slice(c0, c0 + _CONV_LANES)
        acc = jnp.broadcast_to(cb_ref[:, cs], (ts, _CONV_LANES))
        for j in range(width):
            tile, r = divmod(first + j, _SUBLANES)
            rows = slice(tile * _SUBLANES, tile * _SUBLANES + ts)
            taps = u_ref[rows, cs] if r == 0 else sh_ref[r - 1, rows, cs]
            acc = acc + taps * w_ref[j:j + 1, cs]
        y_ref[:, cs] = acc
    y = y_ref[...]
    mu = jnp.mean(y, axis=-1, keepdims=True)
    yc = y - mu
    var = jnp.mean(yc * yc, axis=-1, keepdims=True)
    z = yc * lax.rsqrt(var + EPS) * lg_ref[...] + lb_ref[...]
    o_ref[...] = (z * _sigmoid(z)).astype(o_ref.dtype)
    u_ref[:_CONV_HALO, :] = u_ref[ts:, :]


def _core_body(h_ref, w_ref,
               uin_ref, cw_ref, cb_ref, lg_ref, lb_ref,
               rb_ref, sink_ref, bucket_ref, qmask_ref, krow_ref, q_ref, kc_ref, kp_ref, vc_ref, vp_ref,
               gates_ref, yconv_ref, yattn_ref,
               wbf_ref, u_ref, sh_ref, y_ref, bias_ref, lgt_ref, p_ref, pv_ref, rs_ref,
               *, n_q, n_kv, blocks_per_seq):
    t = pl.program_id(0) * pl.num_programs(1) + pl.program_id(1)
    seq_start = t % blocks_per_seq == 0
    _cast_weights_once([w_ref], [wbf_ref])
    _conv_zero_history(seq_start, u_ref)
    _attn_build_bias(t == 0, rb_ref, sink_ref, bucket_ref, bias_ref, n_q=n_q, n_kv=n_kv)

    acc = jnp.dot(h_ref[...], wbf_ref[...], preferred_element_type=F32)
    gates_ref[...] = acc.astype(gates_ref.dtype)
    _conv_block(uin_ref, cw_ref, cb_ref, lg_ref, lb_ref, yconv_ref, u_ref, sh_ref, y_ref)
    _attn_block(seq_start, qmask_ref, krow_ref, q_ref, kc_ref, kp_ref, vc_ref, vp_ref, yattn_ref,
                bias_ref, lgt_ref, p_ref, pv_ref, rs_ref, n_q=n_q, n_kv=n_kv)


def _mixer_core(h, w_in, layer, gate_col, conv_in, qkv, conv_w, conv_b, ln_g, ln_b, rel_bias, sinks,
                s_len, n_q, n_kv, tn=512):
    m, k = h.shape
    n = w_in.shape[2] - gate_col
    width, channels = conv_w.shape
    attn_w, kv_w = n_q * HEAD_DIM, n_kv * HEAD_DIM
    n_blocks = m // BLOCK
    n_tiles = n // tn
    assert gate_col % tn == 0 and n % tn == 0 and n_blocks % n_tiles == 0
    assert s_len % BLOCK == 0 and width - 1 <= _CONV_HALO and _CONV_HALO % _SUBLANES == 0
    assert attn_w % kv_w == 0 and kv_w == 2 * BLOCK and BLOCK % HEAD_DIM == 0
    m_tiles = n_blocks // n_tiles
    tm = m // m_tiles
    col_blk0 = gate_col // tn
    k_blk = attn_w // kv_w
    v_blk = k_blk + 1

    step = lambda j, i: j * m_tiles + i
    prev_step = lambda j, i: jnp.maximum(step(j, i) - 1, 0)
    const2 = lambda j, i: (0, 0)
    row = lambda v: v.reshape(1, channels)
    lane_group = jnp.arange(kv_w) // HEAD_DIM
    qmask = jnp.broadcast_to((lane_group[None, :] == jnp.arange(n_kv)[:, None])[:, None, :],
                             (n_kv, BLOCK, kv_w)).astype(BF16)
    krow = jnp.broadcast_to((jnp.arange(2 * BLOCK) > 0)[:, None], (2 * BLOCK, kv_w)).astype(BF16)
    body = functools.partial(_core_body, n_q=n_q, n_kv=n_kv, blocks_per_seq=s_len // BLOCK)
    return pl.pallas_call(
        body,
        grid=(n_tiles, m_tiles),
        in_specs=[
            pl.BlockSpec((tm, k), lambda j, i: (i, 0)),
            pl.BlockSpec((None, k, tn), lambda j, i: (layer, 0, col_blk0 + j)),
            pl.BlockSpec((BLOCK, channels), lambda j, i: (step(j, i), 0)),
            pl.BlockSpec((width, channels), const2),
            pl.BlockSpec((1, channels), const2),
            pl.BlockSpec((1, channels), const2),
            pl.BlockSpec((1, channels), const2),
            pl.BlockSpec(memory_space=pltpu.SMEM),
            pl.BlockSpec(memory_space=pltpu.SMEM),
            pl.BlockSpec((BLOCK, 2 * BLOCK), const2),
            pl.BlockSpec((n_kv, BLOCK, kv_w), lambda j, i: (0, 0, 0)),
            pl.BlockSpec((2 * BLOCK, kv_w), const2),
            pl.BlockSpec((BLOCK, attn_w), lambda j, i: (step(j, i), 0)),
            pl.BlockSpec((BLOCK, kv_w), lambda j, i: (step(j, i), k_blk)),
            pl.BlockSpec((BLOCK, kv_w), lambda j, i: (prev_step(j, i), k_blk)),
            pl.BlockSpec((BLOCK, kv_w), lambda j, i: (step(j, i), v_blk)),
            pl.BlockSpec((BLOCK, kv_w), lambda j, i: (prev_step(j, i), v_blk)),
        ],
        out_specs=[
            pl.BlockSpec((tm, tn), lambda j, i: (i, j)),
            pl.BlockSpec((BLOCK, channels), lambda j, i: (step(j, i), 0)),
            pl.BlockSpec((BLOCK, attn_w), lambda j, i: (step(j, i), 0)),
        ],
        out_shape=[
            jax.ShapeDtypeStruct((m, n), BF16),
            jax.ShapeDtypeStruct((m, channels), BF16),
            jax.ShapeDtypeStruct((m, attn_w), BF16),
        ],
        scratch_shapes=[
            pltpu.VMEM((k, tn), BF16),
            pltpu.VMEM((_CONV_HALO + BLOCK, channels), F32),
            pltpu.VMEM((_SUBLANES - 1, _CONV_HALO - _SUBLANES + BLOCK, channels), F32),
            pltpu.VMEM((BLOCK, channels), F32),
            pltpu.VMEM((2, n_q, BLOCK, 2 * BLOCK), F32),
            pltpu.VMEM((n_q, BLOCK, 2 * BLOCK), F32),
            pltpu.VMEM((n_q * BLOCK, 2 * BLOCK), BF16),
            pltpu.VMEM((n_q, BLOCK, kv_w), F32),
            pltpu.VMEM((n_q, BLOCK, BLOCK), F32),
        ],
        compiler_params=_params(2),
        name="mixer_core",
    )(h, w_in,
      conv_in, conv_w, row(conv_b), row(ln_g), row(ln_b),
      rel_bias.astype(F32).reshape(-1), sinks.astype(F32), _band_buckets(), qmask, krow,
      qkv, qkv, qkv, qkv, qkv)


def kernel(x, c, w_ada, b_ada, norm_mix_g, w_in, q_norm_g, k_norm_g, attn_sinks, rel_bias,
           w_attn_out, conv_w, conv_b, conv_ln_g, conv_ln_b, w_conv_out, w_mix_out,
           norm_ffn_g, w_ffn_in, w_ffn_out):
    b_sz, s_len, d = x.shape
    depth = w_ada.shape[0]
    n_q = attn_sinks.shape[1]
    attn_w = w_attn_out.shape[1]
    channels = conv_w.shape[2]
    kv_w = (w_in.shape[2] - attn_w - 2 * channels - 2 * d) // 2
    n_kv = kv_w // HEAD_DIM
    conv_col = attn_w + 2 * kv_w
    gate_col = conv_col + 2 * channels
    m = b_sz * s_len

    for l in range(depth):
        mod = _ada(c, w_ada, b_ada[l], l)
        mod3 = mod.reshape(b_sz * N_MOD, 1, d)

        qkv, h = _qkv_proj(x, norm_mix_g[l], mod3, 1, 0, w_in, l, q_norm_g[l], k_norm_g[l], n_q, n_kv)
        h = h.reshape(m, d)
        proj_tm, proj_tn = 2048, 256
        assert (channels // proj_tn) * (m // proj_tm) == n_q
        slot_head = lambda slot: _orig_head(slot, n_kv, n_q // n_kv)
        same = lambda step: step
        conv_in, wa_bf, wc_bf, wmix_bf = _glu_proj(
            h, w_in, l, conv_col, conv_col + channels, channels, proj_tm, proj_tn,
            cast_jobs=((w_attn_out, slot_head), (w_conv_out, same), (w_mix_out, same)))
        gates, y_conv, y_attn = _mixer_core(h, w_in, l, gate_col, conv_in, qkv, conv_w[l], conv_b[l],
                                            conv_ln_g[l], conv_ln_b[l], rel_bias, attn_sinks[l],
                                            s_len, n_q, n_kv)
        merged = _merge(y_attn, y_conv, wa_bf, wc_bf, gates, 0, d)
        x1, h = _mix_out(merged, wmix_bf, x, mod3, 2, norm_ffn_g[l], 4, 3)

        act, wo_bf = _ffn_in(h.reshape(m, d), w_ffn_in, w_ffn_out, l)
        x = _residual_proj(act, wo_bf, x1.reshape(m, d), mod3, 5, s_len,
                           tm=512, tn=1024).reshape(b_sz, s_len, d)
    return x
```

```python
import functools
import math

import jax
import jax.numpy as jnp
from jax import lax
from jax.experimental import pallas as pl
from jax.experimental.pallas import tpu as pltpu

F32 = jnp.float32
BF16 = jnp.bfloat16

HEAD_DIM = 64
WINDOW = 128
BLOCK = 128
NUM_BUCKETS = 32
MAX_EXACT = NUM_BUCKETS // 2
MAX_DISTANCE = 128
N_MOD = 6
EPS = 1e-6
LOG2E = math.log2(math.e)

V7X_VMEM_LIMIT_BYTES = 56 * 1024 * 1024


def _params(n_axes):
    return pltpu.CompilerParams(
        dimension_semantics=("arbitrary",) * n_axes,
        vmem_limit_bytes=V7X_VMEM_LIMIT_BYTES,
    )


def _sigmoid(x):
    return jax.nn.sigmoid(x)


def _orig_head(slot, n_kv, q_per_kv):
    return (slot % n_kv) * q_per_kv + slot // n_kv


def _ada_body(c_ref, w_ref, b_ref, o_ref):
    c = c_ref[...]
    act = (c * _sigmoid(c)).astype(BF16)
    o_ref[...] = jnp.dot(act, w_ref[...].astype(BF16), preferred_element_type=F32) + b_ref[...]


def _ada(c, w_ada, b_ada, layer, tn=1024):
    b_sz, d = c.shape
    n = w_ada.shape[2]
    rows = 8
    c_pad = jnp.pad(c, ((0, rows - b_sz), (0, 0)))
    out = pl.pallas_call(
        _ada_body,
        grid=(n // tn,),
        in_specs=[
            pl.BlockSpec((rows, d), lambda j: (0, 0)),
            pl.BlockSpec((None, d, tn), lambda j: (layer, 0, j)),
            pl.BlockSpec((1, tn), lambda j: (0, j)),
        ],
        out_specs=pl.BlockSpec((rows, tn), lambda j: (0, j)),
        out_shape=jax.ShapeDtypeStruct((rows, n), F32),
        compiler_params=_params(1),
        name="ada_mod",
    )(c_pad, w_ada, b_ada.reshape(1, n))
    return out[:b_sz]


def _norm_mod_body(x_ref, g_ref, sc_ref, sh_ref, o_ref):
    x = x_ref[0]
    ms = jnp.mean(x * x, axis=-1, keepdims=True)
    gain = g_ref[...] * (1.0 + sc_ref[0])
    o_ref[0] = (x * lax.rsqrt(ms + EPS) * gain + sh_ref[0]).astype(o_ref.dtype)


def _qkv_body(x_ref, g_ref, sc_ref, sh_ref, w_ref, gq_ref, gk_ref, o_ref, h_ref, wbf_ref, seg_ref,
              *, n_q, n_kv, half):
    attn_w = n_q * HEAD_DIM
    kv_w = n_kv * HEAD_DIM
    _norm_mod_body(x_ref, g_ref, sc_ref, sh_ref, h_ref)

    @pl.when(jnp.logical_and(pl.program_id(0) == 0, pl.program_id(1) == 0))
    def _():
        for slot in range(n_q):
            src = _orig_head(slot, n_kv, n_q // n_kv) * HEAD_DIM
            wbf_ref[:, slot * HEAD_DIM:(slot + 1) * HEAD_DIM] = w_ref[:, src:src + HEAD_DIM].astype(BF16)
        wbf_ref[:, attn_w:] = w_ref[:, attn_w:].astype(BF16)
        r = lax.broadcasted_iota(jnp.int32, (half, half), 0) // HEAD_DIM
        c = lax.broadcasted_iota(jnp.int32, (half, half), 1) // HEAD_DIM
        seg_ref[...] = (r == c).astype(BF16)

    a = h_ref[0]

    def head_norm(acc, seg, gain):
        ss = jnp.dot((acc * acc).astype(BF16), seg, preferred_element_type=F32)
        return acc * lax.rsqrt(ss / HEAD_DIM + EPS) * gain

    for c0 in range(0, attn_w, half):
        acc = jnp.dot(a, wbf_ref[:, c0:c0 + half], preferred_element_type=F32)
        o_ref[:, c0:c0 + half] = head_norm(acc, seg_ref[...], gq_ref[...]).astype(o_ref.dtype)
    acc = jnp.dot(a, wbf_ref[:, attn_w:], preferred_element_type=F32)
    o_ref[:, attn_w:attn_w + kv_w] = head_norm(acc[:, :kv_w], seg_ref[:kv_w, :kv_w],
                                               gk_ref[...]).astype(o_ref.dtype)
    o_ref[:, attn_w + kv_w:] = acc[:, kv_w:].astype(o_ref.dtype)


def _qkv_proj(x, norm_g, mod3, scale_idx, shift_idx, w_in, layer, q_g, k_g, n_q, n_kv, ts=512, half=512):
    b_sz, s_len, d = x.shape
    attn_w, kv_w = n_q * HEAD_DIM, n_kv * HEAD_DIM
    width = attn_w + 2 * kv_w
    assert attn_w % half == 0 and kv_w <= half and half % HEAD_DIM == 0
    gq = jnp.tile(q_g * (HEAD_DIM ** -0.5 * LOG2E), half // HEAD_DIM).reshape(1, half)
    gk = jnp.tile(k_g, n_kv).reshape(1, kv_w)
    tiles = s_len // ts
    const2 = lambda b, s: (0, 0)
    body = functools.partial(_qkv_body, n_q=n_q, n_kv=n_kv, half=half)
    return pl.pallas_call(
        body,
        grid=(b_sz, tiles),
        in_specs=[
            pl.BlockSpec((1, ts, d), lambda b, s: (b, s, 0)),
            pl.BlockSpec((1, d), const2),
            pl.BlockSpec((1, 1, d), lambda b, s: (b * N_MOD + scale_idx, 0, 0)),
            pl.BlockSpec((1, 1, d), lambda b, s: (b * N_MOD + shift_idx, 0, 0)),
            pl.BlockSpec((None, d, width), lambda b, s: (layer, 0, 0), pipeline_mode=pl.Buffered(1)),
            pl.BlockSpec((1, half), const2),
            pl.BlockSpec((1, kv_w), const2),
        ],
        out_specs=[
            pl.BlockSpec((ts, width), lambda b, s: (b * tiles + s, 0)),
            pl.BlockSpec((1, ts, d), lambda b, s: (b, s, 0)),
        ],
        out_shape=[
            jax.ShapeDtypeStruct((b_sz * s_len, width), BF16),
            jax.ShapeDtypeStruct((b_sz, s_len, d), BF16),
        ],
        scratch_shapes=[pltpu.VMEM((d, width), BF16), pltpu.VMEM((half, half), BF16)],
        compiler_params=_params(2),
        name="qkv_proj",
    )(x, norm_g.reshape(1, d), mod3, mod3, w_in, gq, gk)


def _cast_weights_once(w_refs, wbf_refs):
    @pl.when(pl.program_id(1) == 0)
    def _():
        for w_ref, wbf_ref in zip(w_refs, wbf_refs):
            wbf_ref[...] = w_ref[...].astype(BF16)


def _glu_proj_body(h_ref, wa_ref, wg_ref, *refs):
    n_jobs = (len(refs) - 3) // 2
    job_in, o_ref, job_out = refs[:n_jobs], refs[n_jobs], refs[n_jobs + 1:-2]
    wabf_ref, wgbf_ref = refs[-2:]
    _cast_weights_once([wa_ref, wg_ref], [wabf_ref, wgbf_ref])
    for src_ref, dst_ref in zip(job_in, job_out):
        dst_ref[...] = src_ref[...].astype(BF16)
    for r0 in range(0, h_ref.shape[0], _MATMUL_ROWS):
        rs = slice(r0, r0 + _MATMUL_ROWS)
        h = h_ref[rs, :]
        a = jnp.dot(h, wabf_ref[...], preferred_element_type=F32)
        g = jnp.dot(h, wgbf_ref[...], preferred_element_type=F32)
        o_ref[rs, :] = (a * _sigmoid(g)).astype(o_ref.dtype)


def _glu_proj(h, w, layer, a_col, g_col, n, tm, tn, cast_jobs=()):
    m, k = h.shape
    assert a_col % tn == 0 and g_col % tn == 0 and n % tn == 0
    a_blk0, g_blk0 = a_col // tn, g_col // tn
    m_tiles = m // tm
    steps = (n // tn) * m_tiles
    step = lambda j, i: j * m_tiles + i
    job_in, job_out, job_shapes = [], [], []
    for wj, row_block_of_step in cast_jobs:
        rows, cols = wj.shape[1:]
        assert rows % steps == 0
        slab = rows // steps
        job_in.append(pl.BlockSpec((None, slab, cols),
                                   lambda j, i, f=row_block_of_step: (layer, f(step(j, i)), 0)))
        job_out.append(pl.BlockSpec((slab, cols), lambda j, i: (step(j, i), 0)))
        job_shapes.append(jax.ShapeDtypeStruct((rows, cols), BF16))
    return pl.pallas_call(
        _glu_proj_body,
        grid=(n // tn, m_tiles),
        in_specs=[
            pl.BlockSpec((tm, k), lambda j, i: (i, 0)),
            pl.BlockSpec((None, k, tn), lambda j, i: (layer, 0, a_blk0 + j)),
            pl.BlockSpec((None, k, tn), lambda j, i: (layer, 0, g_blk0 + j)),
        ] + job_in,
        out_specs=[pl.BlockSpec((tm, tn), lambda j, i: (i, j))] + job_out,
        out_shape=[jax.ShapeDtypeStruct((m, n), BF16)] + job_shapes,
        scratch_shapes=[pltpu.VMEM((k, tn), BF16), pltpu.VMEM((k, tn), BF16)],
        compiler_params=_params(2),
        name="glu_proj",
    )(h, w, w, *[wj for wj, _ in cast_jobs])


_EPILOGUE_LANES = 512
_MATMUL_ROWS = 1024


def _merge_body(ya_ref, yc_ref, wa_ref, wc_ref, ga_ref, gc_ref, o_ref):
    ya = ya_ref[...]
    yc = yc_ref[...]
    for n0 in range(0, o_ref.shape[1], _EPILOGUE_LANES):
        ns = slice(n0, n0 + _EPILOGUE_LANES)
        acc_a = jnp.dot(ya, wa_ref[:, ns], preferred_element_type=F32)
        acc_c = jnp.dot(yc, wc_ref[:, ns], preferred_element_type=F32)
        ga = _sigmoid(ga_ref[:, ns].astype(F32))
        gc = _sigmoid(gc_ref[:, ns].astype(F32))
        o_ref[:, ns] = (ga * acc_a + gc * acc_c).astype(o_ref.dtype)


def _merge(attn, conv, wa_bf, wc_bf, gates, ga_col, gc_col, tm=512):
    m, ka = attn.shape
    kc = conv.shape[1]
    n = wa_bf.shape[1]
    assert ga_col % n == 0 and gc_col % n == 0 and n % _EPILOGUE_LANES == 0
    ga_blk, gc_blk = ga_col // n, gc_col // n
    resident = pl.Buffered(1)
    return pl.pallas_call(
        _merge_body,
        grid=(m // tm,),
        in_specs=[
            pl.BlockSpec((tm, ka), lambda i: (i, 0)),
            pl.BlockSpec((tm, kc), lambda i: (i, 0)),
            pl.BlockSpec((ka, n), lambda i: (0, 0), pipeline_mode=resident),
            pl.BlockSpec((kc, n), lambda i: (0, 0), pipeline_mode=resident),
            pl.BlockSpec((tm, n), lambda i: (i, ga_blk)),
            pl.BlockSpec((tm, n), lambda i: (i, gc_blk)),
        ],
        out_specs=pl.BlockSpec((tm, n), lambda i: (i, 0)),
        out_shape=jax.ShapeDtypeStruct((m, n), BF16),
        compiler_params=_params(1),
        name="merge",
    )(attn, conv, wa_bf, wc_bf, gates, gates)


def _mix_out_body(a_ref, w_ref, x_ref, gt_ref, g_ref, sc_ref, sh_ref, x1_ref, h_ref):
    a = a_ref[...]
    ts, d = a.shape
    sumsq = jnp.zeros((ts, 1), F32)
    for n0 in range(0, d, _EPILOGUE_LANES):
        ns = slice(n0, n0 + _EPILOGUE_LANES)
        acc = jnp.dot(a, w_ref[:, ns], preferred_element_type=F32)
        x1 = x_ref[0, :, ns] + gt_ref[0, :, ns] * acc
        x1_ref[0, :, ns] = x1
        sumsq = sumsq + jnp.sum(x1 * x1, axis=-1, keepdims=True)
    inv = lax.rsqrt(sumsq / d + EPS)
    gain = g_ref[...] * (1.0 + sc_ref[0])
    h_ref[0] = (x1_ref[0] * inv * gain + sh_ref[0]).astype(h_ref.dtype)


def _mix_out(a, w_bf, x, mod3, gate_idx, norm_g, scale_idx, shift_idx, ts=512):
    b_sz, s_len, d = x.shape
    k = a.shape[1]
    assert d % _EPILOGUE_LANES == 0
    tiles = s_len // ts
    const2 = lambda b, s: (0, 0)
    mod_row = lambda idx: pl.BlockSpec((1, 1, d), lambda b, s: (b * N_MOD + idx, 0, 0))
    act = pl.BlockSpec((1, ts, d), lambda b, s: (b, s, 0))
    return pl.pallas_call(
        _mix_out_body,
        grid=(b_sz, tiles),
        in_specs=[
            pl.BlockSpec((ts, k), lambda b, s: (b * tiles + s, 0)),
            pl.BlockSpec((k, d), const2, pipeline_mode=pl.Buffered(1)),
            act,
            mod_row(gate_idx),
            pl.BlockSpec((1, d), const2),
            mod_row(scale_idx),
            mod_row(shift_idx),
        ],
        out_specs=[act, act],
        out_shape=[jax.ShapeDtypeStruct((b_sz, s_len, d), F32), jax.ShapeDtypeStruct((b_sz, s_len, d), BF16)],
        compiler_params=_params(2),
        name="mix_out",
    )(a, w_bf, x, mod3, norm_g.reshape(1, d), mod3, mod3)


def _residual_proj_body(a_ref, w_ref, x_ref, gt_ref, o_ref):
    a = a_ref[...]
    for n0 in range(0, o_ref.shape[1], _EPILOGUE_LANES):
        ns = slice(n0, n0 + _EPILOGUE_LANES)
        acc = jnp.dot(a, w_ref[:, ns], preferred_element_type=F32)
        o_ref[:, ns] = x_ref[:, ns] + gt_ref[0, :, ns] * acc


def _residual_proj(a, w_bf, x2d, mod3, gate_idx, rows_per_batch, tm, tn):
    m, k = a.shape
    n = w_bf.shape[1]
    assert n % tn == 0 and tn % _EPILOGUE_LANES == 0 and rows_per_batch % tm == 0
    tiles_per_batch = rows_per_batch // tm
    return pl.pallas_call(
        _residual_proj_body,
        grid=(n // tn, m // tm),
        in_specs=[
            pl.BlockSpec((tm, k), lambda j, i: (i, 0)),
            pl.BlockSpec((k, tn), lambda j, i: (0, j)),
            pl.BlockSpec((tm, tn), lambda j, i: (i, j)),
            pl.BlockSpec((1, 1, tn), lambda j, i: ((i // tiles_per_batch) * N_MOD + gate_idx, 0, j)),
        ],
        out_specs=pl.BlockSpec((tm, tn), lambda j, i: (i, j)),
        out_shape=jax.ShapeDtypeStruct((m, n), F32),
        compiler_params=_params(2),
        name="residual_proj",
    )(a, w_bf, x2d, mod3)


def _ffn_in_body(a_ref, wg_ref, wu_ref, wo_ref, o_ref, wobf_ref, wgbf_ref, wubf_ref):
    _cast_weights_once([wg_ref, wu_ref], [wgbf_ref, wubf_ref])
    wobf_ref[...] = wo_ref[...].astype(BF16)
    for r0 in range(0, a_ref.shape[0], _MATMUL_ROWS):
        rs = slice(r0, r0 + _MATMUL_ROWS)
        a = a_ref[rs, :]
        gate = jnp.dot(a, wgbf_ref[...], preferred_element_type=F32)
        up = jnp.dot(a, wubf_ref[...], preferred_element_type=F32)
        o_ref[rs, :] = (gate * _sigmoid(gate) * up).astype(o_ref.dtype)


def _ffn_in(h, w_ffn_in, w_ffn_out, layer, tm=2048, tn=512):
    m, k = h.shape
    d_ff = w_ffn_in.shape[2] // 2
    d_out = w_ffn_out.shape[2]
    up_blk = d_ff // tn
    m_tiles = m // tm
    steps = (d_ff // tn) * m_tiles
    assert d_ff % steps == 0
    slab = d_ff // steps
    step = lambda j, i: j * m_tiles + i
    return pl.pallas_call(
        _ffn_in_body,
        grid=(d_ff // tn, m_tiles),
        in_specs=[
            pl.BlockSpec((tm, k), lambda j, i: (i, 0)),
            pl.BlockSpec((None, k, tn), lambda j, i: (layer, 0, j)),
            pl.BlockSpec((None, k, tn), lambda j, i: (layer, 0, up_blk + j)),
            pl.BlockSpec((None, slab, d_out), lambda j, i: (layer, step(j, i), 0)),
        ],
        out_specs=[
            pl.BlockSpec((tm, tn), lambda j, i: (i, j)),
            pl.BlockSpec((slab, d_out), lambda j, i: (step(j, i), 0)),
        ],
        out_shape=[jax.ShapeDtypeStruct((m, d_ff), BF16), jax.ShapeDtypeStruct((d_ff, d_out), BF16)],
        scratch_shapes=[pltpu.VMEM((k, tn), BF16), pltpu.VMEM((k, tn), BF16)],
        compiler_params=_params(2),
        name="ffn_in",
    )(h, w_ffn_in, w_ffn_in, w_ffn_out)


def _t5_causal_bucket(dist):
    n = jnp.maximum(dist, 0)
    nf = jnp.maximum(n, 1).astype(jnp.float32)
    large = MAX_EXACT + (jnp.log(nf / MAX_EXACT) / math.log(MAX_DISTANCE / MAX_EXACT)
                         * (NUM_BUCKETS - MAX_EXACT)).astype(jnp.int32)
    large = jnp.minimum(large, NUM_BUCKETS - 1)
    return jnp.where(n < MAX_EXACT, n, large)


def _band_buckets():
    q_off = jnp.arange(BLOCK)
    k_off = jnp.arange(2 * BLOCK)
    dist = q_off[:, None] + BLOCK - k_off[None, :]
    allowed = (dist >= 0) & (dist < WINDOW)
    return jnp.where(allowed, _t5_causal_bucket(dist), -1).astype(jnp.int32)


def _attn_build_bias(first_step, rb_ref, sink_ref, bucket_ref, bias_ref, *, n_q, n_kv):
    q_per_kv = n_q // n_kv

    @pl.when(first_step)
    def _():
        bucket = bucket_ref[...]
        col = lax.broadcasted_iota(jnp.int32, bucket.shape, 1)
        for slot in range(n_q):
            head = _orig_head(slot, n_kv, q_per_kv)
            tile = jnp.zeros(bucket.shape, F32)
            for b in range(NUM_BUCKETS):
                tile = jnp.where(bucket == b, rb_ref[b * n_q + head] * LOG2E, tile)
            tile = jnp.where(bucket < 0, -jnp.inf, tile)
            sink = sink_ref[head] * LOG2E
            bias_ref[0, slot] = jnp.where(col == 0, sink, tile)
            bias_ref[1, slot] = jnp.where(col == 0, sink, jnp.where(col < BLOCK, -jnp.inf, tile))


def _attn_block(seq_start, qmask_ref, krow_ref, q_ref, kc_ref, kp_ref, vc_ref, vp_ref, o_ref,
                bias_ref, lg_ref, p_ref, pv_ref, rs_ref, *, n_q, n_kv):
    q_per_kv = n_q // n_kv
    kv_w = n_kv * HEAD_DIM
    lane_slot = lax.broadcasted_iota(jnp.int32, (BLOCK, BLOCK), 1) // HEAD_DIM
    ones = jnp.ones((2 * BLOCK, BLOCK), BF16)
    bias_idx = jnp.where(seq_start, 1, 0)

    keys = jnp.concatenate([kp_ref[...], kc_ref[...]], axis=0) * krow_ref[...]
    vals = jnp.concatenate([vp_ref[...], vc_ref[...]], axis=0) * krow_ref[...]
    lhs = jnp.concatenate(
        [q_ref[:, j * kv_w:(j + 1) * kv_w] * qmask_ref[g]
         for j in range(q_per_kv) for g in range(n_kv)], axis=0)
    lg_ref[...] = lax.dot_general(lhs, keys, (((1,), (1,)), ((), ())),
                                  preferred_element_type=F32).reshape(n_q, BLOCK, 2 * BLOCK)
    for slot in range(n_q):
        logit = lg_ref[slot] + bias_ref[bias_idx, slot]
        m = jnp.max(logit, axis=-1, keepdims=True)
        p_ref[slot * BLOCK:(slot + 1) * BLOCK, :] = jnp.exp2(logit - m).astype(BF16)
    p = p_ref[...]
    pv_ref[...] = jnp.dot(p, vals, preferred_element_type=F32).reshape(n_q, BLOCK, kv_w)
    rs_ref[...] = jnp.dot(p, ones, preferred_element_type=F32).reshape(n_q, BLOCK, BLOCK)
    per_half = BLOCK // HEAD_DIM
    for j in range(q_per_kv):
        for half in range(kv_w // BLOCK):
            lanes = slice(half * BLOCK, (half + 1) * BLOCK)
            slots = [j * n_kv + half * per_half + i for i in range(per_half)]
            num = pv_ref[slots[-1], :, lanes]
            den = rs_ref[slots[-1]]
            for i in range(per_half - 2, -1, -1):
                num = jnp.where(lane_slot == i, pv_ref[slots[i], :, lanes], num)
                den = jnp.where(lane_slot == i, rs_ref[slots[i]], den)
            o_ref[:, j * kv_w + half * BLOCK:j * kv_w + (half + 1) * BLOCK] = (
                num * (1.0 / den)).astype(o_ref.dtype)


_CONV_HALO = 32
_CONV_LANES = 256
_SUBLANES = 8


def _conv_zero_history(seq_start, u_ref):
    @pl.when(seq_start)
    def _():
        u_ref[:_CONV_HALO, :] = jnp.zeros((_CONV_HALO, u_ref.shape[1]), F32)


def _conv_block(uin_ref, w_ref, cb_ref, lg_ref, lb_ref, o_ref, u_ref, sh_ref, y_ref):
    width = w_ref.shape[0]
    ts, channels = o_ref.shape
    u_ref[_CONV_HALO:, :] = uin_ref[...].astype(F32)
    span = sh_ref.shape[1]
    for r in range(1, _SUBLANES):
        sh_ref[r - 1] = u_ref[r:r + span, :]
    first = _CONV_HALO - (width - 1)
    for c0 in range(0, channels, _CONV_LANES):
        cs = slice(c0, c0 + _CONV_LANES)
        acc = jnp.broadcast_to(cb_ref[:, cs], (ts, _CONV_LANES))
        for j in range(width):
            tile, r = divmod(first + j, _SUBLANES)
            rows = slice(tile * _SUBLANES, tile * _SUBLANES + ts)
            taps = u_ref[rows, cs] if r == 0 else sh_ref[r - 1, rows, cs]
            acc = acc + taps * w_ref[j:j + 1, cs]
        y_ref[:, cs] = acc
    y = y_ref[...]
    mu = jnp.mean(y, axis=-1, keepdims=True)
    yc = y - mu
    var = jnp.mean(yc * yc, axis=-1, keepdims=True)
    z = yc * lax.rsqrt(var + EPS) * lg_ref[...] + lb_ref[...]
    o_ref[...] = (z * _sigmoid(z)).astype(o_ref.dtype)
    u_ref[:_CONV_HALO, :] = u_ref[ts:, :]


def _core_body(h_ref, w_ref,
               uin_ref, cw_ref, cb_ref, lg_ref, lb_ref,
               rb_ref, sink_ref, bucket_ref, qmask_ref, krow_ref, q_ref, kc_ref, kp_ref, vc_ref, vp_ref,
               gates_ref, yconv_ref, yattn_ref,
               wbf_ref, u_ref, sh_ref, y_ref, bias_ref, lgt_ref, p_ref, pv_ref, rs_ref,
               *, n_q, n_kv, blocks_per_seq):
    t = pl.program_id(0) * pl.num_programs(1) + pl.program_id(1)
    seq_start = t % blocks_per_seq == 0
    _cast_weights_once([w_ref], [wbf_ref])
    _conv_zero_history(seq_start, u_ref)
    _attn_build_bias(t == 0, rb_ref, sink_ref, bucket_ref, bias_ref, n_q=n_q, n_kv=n_kv)

    acc = jnp.dot(h_ref[...], wbf_ref[...], preferred_element_type=F32)
    gates_ref[...] = acc.astype(gates_ref.dtype)
    _conv_block(uin_ref, cw_ref, cb_ref, lg_ref, lb_ref, yconv_ref, u_ref, sh_ref, y_ref)
    _attn_block(seq_start, qmask_ref, krow_ref, q_ref, kc_ref, kp_ref, vc_ref, vp_ref, yattn_ref,
                bias_ref, lgt_ref, p_ref, pv_ref, rs_ref, n_q=n_q, n_kv=n_kv)


def _mixer_core(h, w_in, layer, gate_col, conv_in, qkv, conv_w, conv_b, ln_g, ln_b, rel_bias, sinks,
                s_len, n_q, n_kv, tn=512):
    m, k = h.shape
    n = w_in.shape[2] - gate_col
    width, channels = conv_w.shape
    attn_w, kv_w = n_q * HEAD_DIM, n_kv * HEAD_DIM
    n_blocks = m // BLOCK
    n_tiles = n // tn
    assert gate_col % tn == 0 and n % tn == 0 and n_blocks % n_tiles == 0
    assert s_len % BLOCK == 0 and width - 1 <= _CONV_HALO and _CONV_HALO % _SUBLANES == 0
    assert attn_w % kv_w == 0 and kv_w == 2 * BLOCK and BLOCK % HEAD_DIM == 0
    m_tiles = n_blocks // n_tiles
    tm = m // m_tiles
    col_blk0 = gate_col // tn
    k_blk = attn_w // kv_w
    v_blk = k_blk + 1

    step = lambda j, i: j * m_tiles + i
    prev_step = lambda j, i: jnp.maximum(step(j, i) - 1, 0)
    const2 = lambda j, i: (0, 0)
    row = lambda v: v.reshape(1, channels)
    lane_group = jnp.arange(kv_w) // HEAD_DIM
    qmask = jnp.broadcast_to((lane_group[None, :] == jnp.arange(n_kv)[:, None])[:, None, :],
                             (n_kv, BLOCK, kv_w)).astype(BF16)
    krow = jnp.broadcast_to((jnp.arange(2 * BLOCK) > 0)[:, None], (2 * BLOCK, kv_w)).astype(BF16)
    body = functools.partial(_core_body, n_q=n_q, n_kv=n_kv, blocks_per_seq=s_len // BLOCK)
    return pl.pallas_call(
        body,
        grid=(n_tiles, m_tiles),
        in_specs=[
            pl.BlockSpec((tm, k), lambda j, i: (i, 0)),
            pl.BlockSpec((None, k, tn), lambda j, i: (layer, 0, col_blk0 + j)),
            pl.BlockSpec((BLOCK, channels), lambda j, i: (step(j, i), 0)),
            pl.BlockSpec((width, channels), const2),
            pl.BlockSpec((1, channels), const2),
            pl.BlockSpec((1, channels), const2),
            pl.BlockSpec((1, channels), const2),
            pl.BlockSpec(memory_space=pltpu.SMEM),
            pl.BlockSpec(memory_space=pltpu.SMEM),
            pl.BlockSpec((BLOCK, 2 * BLOCK), const2),
            pl.BlockSpec((n_kv, BLOCK, kv_w), lambda j, i: (0, 0, 0)),
            pl.BlockSpec((2 * BLOCK, kv_w), const2),
            pl.BlockSpec((BLOCK, attn_w), lambda j, i: (step(j, i), 0)),
            pl.BlockSpec((BLOCK, kv_w), lambda j, i: (step(j, i), k_blk)),
            pl.BlockSpec((BLOCK, kv_w), lambda j, i: (prev_step(j, i), k_blk)),
            pl.BlockSpec((BLOCK, kv_w), lambda j, i: (step(j, i), v_blk)),
            pl.BlockSpec((BLOCK, kv_w), lambda j, i: (prev_step(j, i), v_blk)),
        ],
        out_specs=[
            pl.BlockSpec((tm, tn), lambda j, i: (i, j)),
            pl.BlockSpec((BLOCK, channels), lambda j, i: (step(j, i), 0)),
            pl.BlockSpec((BLOCK, attn_w), lambda j, i: (step(j, i), 0)),
        ],
        out_shape=[
            jax.ShapeDtypeStruct((m, n), BF16),
            jax.ShapeDtypeStruct((m, channels), BF16),
            jax.ShapeDtypeStruct((m, attn_w), BF16),
        ],
        scratch_shapes=[
            pltpu.VMEM((k, tn), BF16),
            pltpu.VMEM((_CONV_HALO + BLOCK, channels), F32),
            pltpu.VMEM((_SUBLANES - 1, _CONV_HALO - _SUBLANES + BLOCK, channels), F32),
            pltpu.VMEM((BLOCK, channels), F32),
            pltpu.VMEM((2, n_q, BLOCK, 2 * BLOCK), F32),
            pltpu.VMEM((n_q, BLOCK, 2 * BLOCK), F32),
            pltpu.VMEM((n_q * BLOCK, 2 * BLOCK), BF16),
            pltpu.VMEM((n_q, BLOCK, kv_w), F32),
            pltpu.VMEM((n_q, BLOCK, BLOCK), F32),
        ],
        compiler_params=_params(2),
        name="mixer_core",
    )(h, w_in,
      conv_in, conv_w, row(conv_b), row(ln_g), row(ln_b),
      rel_bias.astype(F32).reshape(-1), sinks.astype(F32), _band_buckets(), qmask, krow,
      qkv, qkv, qkv, qkv, qkv)


def kernel(x, c, w_ada, b_ada, norm_mix_g, w_in, q_norm_g, k_norm_g, attn_sinks, rel_bias,
           w_attn_out, conv_w, conv_b, conv_ln_g, conv_ln_b, w_conv_out, w_mix_out,
           norm_ffn_g, w_ffn_in, w_ffn_out):
    b_sz, s_len, d = x.shape
    depth = w_ada.shape[0]
    n_q = attn_sinks.shape[1]
    attn_w = w_attn_out.shape[1]
    channels = conv_w.shape[2]
    kv_w = (w_in.shape[2] - attn_w - 2 * channels - 2 * d) // 2
    n_kv = kv_w // HEAD_DIM
    conv_col = attn_w + 2 * kv_w
    gate_col = conv_col + 2 * channels
    m = b_sz * s_len

    for l in range(depth):
        mod = _ada(c, w_ada, b_ada[l], l)
        mod3 = mod.reshape(b_sz * N_MOD, 1, d)

        qkv, h = _qkv_proj(x, norm_mix_g[l], mod3, 1, 0, w_in, l, q_norm_g[l], k_norm_g[l], n_q, n_kv)
        h = h.reshape(m, d)
        proj_tm, proj_tn = 2048, 256
        assert (channels // proj_tn) * (m // proj_tm) == n_q
        slot_head = lambda slot: _orig_head(slot, n_kv, n_q // n_kv)
        same = lambda step: step
        conv_in, wa_bf, wc_bf, wmix_bf = _glu_proj(
            h, w_in, l, conv_col, conv_col + channels, channels, proj_tm, proj_tn,
            cast_jobs=((w_attn_out, slot_head), (w_conv_out, same), (w_mix_out, same)))
        gates, y_conv, y_attn = _mixer_core(h, w_in, l, gate_col, conv_in, qkv, conv_w[l], conv_b[l],
                                            conv_ln_g[l], conv_ln_b[l], rel_bias, attn_sinks[l],
                                            s_len, n_q, n_kv)
        merged = _merge(y_attn, y_conv, wa_bf, wc_bf, gates, 0, d)
        x1, h = _mix_out(merged, wmix_bf, x, mod3, 2, norm_ffn_g[l], 4, 3)

        act, wo_bf = _ffn_in(h.reshape(m, d), w_ffn_in, w_ffn_out, l)
        x = _residual_proj(act, wo_bf, x1.reshape(m, d), mod3, 5, s_len,
                           tm=512, tn=1024).reshape(b_sz, s_len, d)
    return x
```

```python
import functools
import math

import jax
import jax.numpy as jnp
from jax import lax
from jax.experimental import pallas as pl
from jax.experimental.pallas import tpu as pltpu

F32 = jnp.float32
BF16 = jnp.bfloat16

HEAD_DIM = 64
WINDOW = 128
BLOCK = 128
NUM_BUCKETS = 32
MAX_EXACT = NUM_BUCKETS // 2
MAX_DISTANCE = 128
N_MOD = 6
EPS = 1e-6
LOG2E = math.log2(math.e)

V7X_VMEM_LIMIT_BYTES = 56 * 1024 * 1024


def _params(n_axes):
    return pltpu.CompilerParams(
        dimension_semantics=("arbitrary",) * n_axes,
        vmem_limit_bytes=V7X_VMEM_LIMIT_BYTES,
    )


def _sigmoid(x):
    return jax.nn.sigmoid(x)


def _orig_head(slot, n_kv, q_per_kv):
    return (slot % n_kv) * q_per_kv + slot // n_kv


def _ada_body(c_ref, w_ref, b_ref, o_ref):
    c = c_ref[...]
    act = (c * _sigmoid(c)).astype(BF16)
    o_ref[...] = jnp.dot(act, w_ref[...].astype(BF16), preferred_element_type=F32) + b_ref[...]


def _ada(c, w_ada, b_ada, layer, tn=1024):
    b_sz, d = c.shape
    n = w_ada.shape[2]
    rows = 8
    c_pad = jnp.pad(c, ((0, rows - b_sz), (0, 0)))
    out = pl.pallas_call(
        _ada_body,
        grid=(n // tn,),
        in_specs=[
            pl.BlockSpec((rows, d), lambda j: (0, 0)),
            pl.BlockSpec((None, d, tn), lambda j: (layer, 0, j)),
            pl.BlockSpec((1, tn), lambda j: (0, j)),
        ],
        out_specs=pl.BlockSpec((rows, tn), lambda j: (0, j)),
        out_shape=jax.ShapeDtypeStruct((rows, n), F32),
        compiler_params=_params(1),
        name="ada_mod",
    )(c_pad, w_ada, b_ada.reshape(1, n))
    return out[:b_sz]


def _norm_mod_body(x_ref, g_ref, sc_ref, sh_ref, o_ref):
    x = x_ref[0]
    ms = jnp.mean(x * x, axis=-1, keepdims=True)
    gain = g_ref[...] * (1.0 + sc_ref[0])
    o_ref[0] = (x * lax.rsqrt(ms + EPS) * gain + sh_ref[0]).astype(o_ref.dtype)


def _qkv_body(x_ref, g_ref, sc_ref, sh_ref, w_ref, gq_ref, gk_ref, o_ref, h_ref, wbf_ref, seg_ref,
              *, n_q, n_kv, half):
    attn_w = n_q * HEAD_DIM
    kv_w = n_kv * HEAD_DIM
    _norm_mod_body(x_ref, g_ref, sc_ref, sh_ref, h_ref)

    @pl.when(jnp.logical_and(pl.program_id(0) == 0, pl.program_id(1) == 0))
    def _():
        for slot in range(n_q):
            src = _orig_head(slot, n_kv, n_q // n_kv) * HEAD_DIM
            wbf_ref[:, slot * HEAD_DIM:(slot + 1) * HEAD_DIM] = w_ref[:, src:src + HEAD_DIM].astype(BF16)
        wbf_ref[:, attn_w:] = w_ref[:, attn_w:].astype(BF16)
        r = lax.broadcasted_iota(jnp.int32, (half, half), 0) // HEAD_DIM
        c = lax.broadcasted_iota(jnp.int32, (half, half), 1) // HEAD_DIM
        seg_ref[...] = (r == c).astype(BF16)

    a = h_ref[0]

    def head_norm(acc, seg, gain):
        ss = jnp.dot((acc * acc).astype(BF16), seg, preferred_element_type=F32)
        return acc * lax.rsqrt(ss / HEAD_DIM + EPS) * gain

    for c0 in range(0, attn_w, half):
        acc = jnp.dot(a, wbf_ref[:, c0:c0 + half], preferred_element_type=F32)
        o_ref[:, c0:c0 + half] = head_norm(acc, seg_ref[...], gq_ref[...]).astype(o_ref.dtype)
    acc = jnp.dot(a, wbf_ref[:, attn_w:], preferred_element_type=F32)
    o_ref[:, attn_w:attn_w + kv_w] = head_norm(acc[:, :kv_w], seg_ref[:kv_w, :kv_w],
                                               gk_ref[...]).astype(o_ref.dtype)
    o_ref[:, attn_w + kv_w:] = acc[:, kv_w:].astype(o_ref.dtype)


def _qkv_proj(x, norm_g, mod3, scale_idx, shift_idx, w_in, layer, q_g, k_g, n_q, n_kv, ts=512, half=512):
    b_sz, s_len, d = x.shape
    attn_w, kv_w = n_q * HEAD_DIM, n_kv * HEAD_DIM
    width = attn_w + 2 * kv_w
    assert attn_w % half == 0 and kv_w <= half and half % HEAD_DIM == 0
    gq = jnp.tile(q_g * (HEAD_DIM ** -0.5 * LOG2E), half // HEAD_DIM).reshape(1, half)
    gk = jnp.tile(k_g, n_kv).reshape(1, kv_w)
    tiles = s_len // ts
    const2 = lambda b, s: (0, 0)
    body = functools.partial(_qkv_body, n_q=n_q, n_kv=n_kv, half=half)
    return pl.pallas_call(
        body,
        grid=(b_sz, tiles),
        in_specs=[
            pl.BlockSpec((1, ts, d), lambda b, s: (b, s, 0)),
            pl.BlockSpec((1, d), const2),
            pl.BlockSpec((1, 1, d), lambda b, s: (b * N_MOD + scale_idx, 0, 0)),
            pl.BlockSpec((1, 1, d), lambda b, s: (b * N_MOD + shift_idx, 0, 0)),
            pl.BlockSpec((None, d, width), lambda b, s: (layer, 0, 0), pipeline_mode=pl.Buffered(1)),
            pl.BlockSpec((1, half), const2),
            pl.BlockSpec((1, kv_w), const2),
        ],
        out_specs=[
            pl.BlockSpec((ts, width), lambda b, s: (b * tiles + s, 0)),
            pl.BlockSpec((1, ts, d), lambda b, s: (b, s, 0)),
        ],
        out_shape=[
            jax.ShapeDtypeStruct((b_sz * s_len, width), BF16),
            jax.ShapeDtypeStruct((b_sz, s_len, d), BF16),
        ],
        scratch_shapes=[pltpu.VMEM((d, width), BF16), pltpu.VMEM((half, half), BF16)],
        compiler_params=_params(2),
        name="qkv_proj",
    )(x, norm_g.reshape(1, d), mod3, mod3, w_in, gq, gk)


def _cast_weights_once(w_refs, wbf_refs):
    @pl.when(pl.program_id(1) == 0)
    def _():
        for w_ref, wbf_ref in zip(w_refs, wbf_refs):
            wbf_ref[...] = w_ref[...].astype(BF16)


def _glu_proj_body(h_ref, wa_ref, wg_ref, *refs):
    n_jobs = (len(refs) - 3) // 2
    job_in, o_ref, job_out = refs[:n_jobs], refs[n_jobs], refs[n_jobs + 1:-2]
    wabf_ref, wgbf_ref = refs[-2:]
    _cast_weights_once([wa_ref, wg_ref], [wabf_ref, wgbf_ref])
    for src_ref, dst_ref in zip(job_in, job_out):
        dst_ref[...] = src_ref[...].astype(BF16)
    for r0 in range(0, h_ref.shape[0], _MATMUL_ROWS):
        rs = slice(r0, r0 + _MATMUL_ROWS)
        h = h_ref[rs, :]
        a = jnp.dot(h, wabf_ref[...], preferred_element_type=F32)
        g = jnp.dot(h, wgbf_ref[...], preferred_element_type=F32)
        o_ref[rs, :] = (a * _sigmoid(g)).astype(o_ref.dtype)


def _glu_proj(h, w, layer, a_col, g_col, n, tm, tn, cast_jobs=()):
    m, k = h.shape
    assert a_col % tn == 0 and g_col % tn == 0 and n % tn == 0
    a_blk0, g_blk0 = a_col // tn, g_col // tn
    m_tiles = m // tm
    steps = (n // tn) * m_tiles
    step = lambda j, i: j * m_tiles + i
    job_in, job_out, job_shapes = [], [], []
    for wj, row_block_of_step in cast_jobs:
        rows, cols = wj.shape[1:]
        assert rows % steps == 0
        slab = rows // steps
        job_in.append(pl.BlockSpec((None, slab, cols),
                                   lambda j, i, f=row_block_of_step: (layer, f(step(j, i)), 0)))
        job_out.append(pl.BlockSpec((slab, cols), lambda j, i: (step(j, i), 0)))
        job_shapes.append(jax.ShapeDtypeStruct((rows, cols), BF16))
    return pl.pallas_call(
        _glu_proj_body,
        grid=(n // tn, m_tiles),
        in_specs=[
            pl.BlockSpec((tm, k), lambda j, i: (i, 0)),
            pl.BlockSpec((None, k, tn), lambda j, i: (layer, 0, a_blk0 + j)),
            pl.BlockSpec((None, k, tn), lambda j, i: (layer, 0, g_blk0 + j)),
        ] + job_in,
        out_specs=[pl.BlockSpec((tm, tn), lambda j, i: (i, j))] + job_out,
        out_shape=[jax.ShapeDtypeStruct((m, n), BF16)] + job_shapes,
        scratch_shapes=[pltpu.VMEM((k, tn), BF16), pltpu.VMEM((k, tn), BF16)],
        compiler_params=_params(2),
        name="glu_proj",
    )(h, w, w, *[wj for wj, _ in cast_jobs])


_EPILOGUE_LANES = 512
_MATMUL_ROWS = 1024


def _merge_body(ya_ref, yc_ref, wa_ref, wc_ref, ga_ref, gc_ref, o_ref):
    ya = ya_ref[...]
    yc = yc_ref[...]
    for n0 in range(0, o_ref.shape[1], _EPILOGUE_LANES):
        ns = slice(n0, n0 + _EPILOGUE_LANES)
        acc_a = jnp.dot(ya, wa_ref[:, ns], preferred_element_type=F32)
        acc_c = jnp.dot(yc, wc_ref[:, ns], preferred_element_type=F32)
        ga = _sigmoid(ga_ref[:, ns].astype(F32))
        gc = _sigmoid(gc_ref[:, ns].astype(F32))
        o_ref[:, ns] = (ga * acc_a + gc * acc_c).astype(o_ref.dtype)


def _merge(attn, conv, wa_bf, wc_bf, gates, ga_col, gc_col, tm=512):
    m, ka = attn.shape
    kc = conv.shape[1]
    n = wa_bf.shape[1]
    assert ga_col % n == 0 and gc_col % n == 0 and n % _EPILOGUE_LANES == 0
    ga_blk, gc_blk = ga_col // n, gc_col // n
    resident = pl.Buffered(1)
    return pl.pallas_call(
        _merge_body,
        grid=(m // tm,),
        in_specs=[
            pl.BlockSpec((tm, ka), lambda i: (i, 0)),
            pl.BlockSpec((tm, kc), lambda i: (i, 0)),
            pl.BlockSpec((ka, n), lambda i: (0, 0), pipeline_mode=resident),
            pl.BlockSpec((kc, n), lambda i: (0, 0), pipeline_mode=resident),
            pl.BlockSpec((tm, n), lambda i: (i, ga_blk)),
            pl.BlockSpec((tm, n), lambda i: (i, gc_blk)),
        ],
        out_specs=pl.BlockSpec((tm, n), lambda i: (i, 0)),
        out_shape=jax.ShapeDtypeStruct((m, n), BF16),
        compiler_params=_params(1),
        name="merge",
    )(attn, conv, wa_bf, wc_bf, gates, gates)


def _mix_out_body(a_ref, w_ref, x_ref, gt_ref, g_ref, sc_ref, sh_ref, x1_ref, h_ref):
    a = a_ref[...]
    ts, d = a.shape
    sumsq = jnp.zeros((ts, 1), F32)
    for n0 in range(0, d, _EPILOGUE_LANES):
        ns = slice(n0, n0 + _EPILOGUE_LANES)
        acc = jnp.dot(a, w_ref[:, ns], preferred_element_type=F32)
        x1 = x_ref[0, :, ns] + gt_ref[0, :, ns] * acc
        x1_ref[0, :, ns] = x1
        sumsq = sumsq + jnp.sum(x1 * x1, axis=-1, keepdims=True)
    inv = lax.rsqrt(sumsq / d + EPS)
    gain = g_ref[...] * (1.0 + sc_ref[0])
    h_ref[0] = (x1_ref[0] * inv * gain + sh_ref[0]).astype(h_ref.dtype)


def _mix_out(a, w_bf, x, mod3, gate_idx, norm_g, scale_idx, shift_idx, ts=512):
    b_sz, s_len, d = x.shape
    k = a.shape[1]
    assert d % _EPILOGUE_LANES == 0
    tiles = s_len // ts
    const2 = lambda b, s: (0, 0)
    mod_row = lambda idx: pl.BlockSpec((1, 1, d), lambda b, s: (b * N_MOD + idx, 0, 0))
    act = pl.BlockSpec((1, ts, d), lambda b, s: (b, s, 0))
    return pl.pallas_call(
        _mix_out_body,
        grid=(b_sz, tiles),
        in_specs=[
            pl.BlockSpec((ts, k), lambda b, s: (b * tiles + s, 0)),
            pl.BlockSpec((k, d), const2, pipeline_mode=pl.Buffered(1)),
            act,
            mod_row(gate_idx),
            pl.BlockSpec((1, d), const2),
            mod_row(scale_idx),
            mod_row(shift_idx),
        ],
        out_specs=[act, act],
        out_shape=[jax.ShapeDtypeStruct((b_sz, s_len, d), F32), jax.ShapeDtypeStruct((b_sz, s_len, d), BF16)],
        compiler_params=_params(2),
        name="mix_out",
    )(a, w_bf, x, mod3, norm_g.reshape(1, d), mod3, mod3)


def _residual_proj_body(a_ref, w_ref, x_ref, gt_ref, o_ref):
    a = a_ref[...]
    for n0 in range(0, o_ref.shape[1], _EPILOGUE_LANES):
        ns = slice(n0, n0 + _EPILOGUE_LANES)
        acc = jnp.dot(a, w_ref[:, ns], preferred_element_type=F32)
        o_ref[:, ns] = x_ref[:, ns] + gt_ref[0, :, ns] * acc


def _residual_proj(a, w_bf, x2d, mod3, gate_idx, rows_per_batch, tm, tn):
    m, k = a.shape
    n = w_bf.shape[1]
    assert n % tn == 0 and tn % _EPILOGUE_LANES == 0 and rows_per_batch % tm == 0
    tiles_per_batch = rows_per_batch // tm
    return pl.pallas_call(
        _residual_proj_body,
        grid=(n // tn, m // tm),
        in_specs=[
            pl.BlockSpec((tm, k), lambda j, i: (i, 0)),
            pl.BlockSpec((k, tn), lambda j, i: (0, j)),
            pl.BlockSpec((tm, tn), lambda j, i: (i, j)),
            pl.BlockSpec((1, 1, tn), lambda j, i: ((i // tiles_per_batch) * N_MOD + gate_idx, 0, j)),
        ],
        out_specs=pl.BlockSpec((tm, tn), lambda j, i: (i, j)),
        out_shape=jax.ShapeDtypeStruct((m, n), F32),
        compiler_params=_params(2),
        name="residual_proj",
    )(a, w_bf, x2d, mod3)


def _ffn_in_body(a_ref, wg_ref, wu_ref, wo_ref, o_ref, wobf_ref, wgbf_ref, wubf_ref):
    _cast_weights_once([wg_ref, wu_ref], [wgbf_ref, wubf_ref])
    wobf_ref[...] = wo_ref[...].astype(BF16)
    for r0 in range(0, a_ref.shape[0], _MATMUL_ROWS):
        rs = slice(r0, r0 + _MATMUL_ROWS)
        a = a_ref[rs, :]
        gate = jnp.dot(a, wgbf_ref[...], preferred_element_type=F32)
        up = jnp.dot(a, wubf_ref[...], preferred_element_type=F32)
        o_ref[rs, :] = (gate * _sigmoid(gate) * up).astype(o_ref.dtype)


def _ffn_in(h, w_ffn_in, w_ffn_out, layer, tm=2048, tn=512):
    m, k = h.shape
    d_ff = w_ffn_in.shape[2] // 2
    d_out = w_ffn_out.shape[2]
    up_blk = d_ff // tn
    m_tiles = m // tm
    steps = (d_ff // tn) * m_tiles
    assert d_ff % steps == 0
    slab = d_ff // steps
    step = lambda j, i: j * m_tiles + i
    return pl.pallas_call(
        _ffn_in_body,
        grid=(d_ff // tn, m_tiles),
        in_specs=[
            pl.BlockSpec((tm, k), lambda j, i: (i, 0)),
            pl.BlockSpec((None, k, tn), lambda j, i: (layer, 0, j)),
            pl.BlockSpec((None, k, tn), lambda j, i: (layer, 0, up_blk + j)),
            pl.BlockSpec((None, slab, d_out), lambda j, i: (layer, step(j, i), 0)),
        ],
        out_specs=[
            pl.BlockSpec((tm, tn), lambda j, i: (i, j)),
            pl.BlockSpec((slab, d_out), lambda j, i: (step(j, i), 0)),
        ],
        out_shape=[jax.ShapeDtypeStruct((m, d_ff), BF16), jax.ShapeDtypeStruct((d_ff, d_out), BF16)],
        scratch_shapes=[pltpu.VMEM((k, tn), BF16), pltpu.VMEM((k, tn), BF16)],
        compiler_params=_params(2),
        name="ffn_in",
    )(h, w_ffn_in, w_ffn_in, w_ffn_out)


def _t5_causal_bucket(dist):
    n = jnp.maximum(dist, 0)
    nf = jnp.maximum(n, 1).astype(jnp.float32)
    large = MAX_EXACT + (jnp.log(nf / MAX_EXACT) / math.log(MAX_DISTANCE / MAX_EXACT)
                         * (NUM_BUCKETS - MAX_EXACT)).astype(jnp.int32)
    large = jnp.minimum(large, NUM_BUCKETS - 1)
    return jnp.where(n < MAX_EXACT, n, large)


def _band_buckets():
    q_off = jnp.arange(BLOCK)
    k_off = jnp.arange(2 * BLOCK)
    dist = q_off[:, None] + BLOCK - k_off[None, :]
    allowed = (dist >= 0) & (dist < WINDOW)
    return jnp.where(allowed, _t5_causal_bucket(dist), -1).astype(jnp.int32)


def _attn_build_bias(first_step, rb_ref, sink_ref, bucket_ref, bias_ref, *, n_q, n_kv):
    q_per_kv = n_q // n_kv

    @pl.when(first_step)
    def _():
        bucket = bucket_ref[...]
        col = lax.broadcasted_iota(jnp.int32, bucket.shape, 1)
        for slot in range(n_q):
            head = _orig_head(slot, n_kv, q_per_kv)
            tile = jnp.zeros(bucket.shape, F32)
            for b in range(NUM_BUCKETS):
                tile = jnp.where(bucket == b, rb_ref[b * n_q + head] * LOG2E, tile)
            tile = jnp.where(bucket < 0, -jnp.inf, tile)
            sink = sink_ref[head] * LOG2E
            bias_ref[0, slot] = jnp.where(col == 0, sink, tile)
            bias_ref[1, slot] = jnp.where(col == 0, sink, jnp.where(col < BLOCK, -jnp.inf, tile))


def _attn_block(seq_start, qmask_ref, krow_ref, q_ref, kc_ref, kp_ref, vc_ref, vp_ref, o_ref,
                bias_ref, lg_ref, p_ref, pv_ref, rs_ref, *, n_q, n_kv):
    q_per_kv = n_q // n_kv
    kv_w = n_kv * HEAD_DIM
    lane_slot = lax.broadcasted_iota(jnp.int32, (BLOCK, BLOCK), 1) // HEAD_DIM
    ones = jnp.ones((2 * BLOCK, BLOCK), BF16)
    bias_idx = jnp.where(seq_start, 1, 0)

    keys = jnp.concatenate([kp_ref[...], kc_ref[...]], axis=0) * krow_ref[...]
    vals = jnp.concatenate([vp_ref[...], vc_ref[...]], axis=0) * krow_ref[...]
    lhs = jnp.concatenate(
        [q_ref[:, j * kv_w:(j + 1) * kv_w] * qmask_ref[g]
         for j in range(q_per_kv) for g in range(n_kv)], axis=0)
    lg_ref[...] = lax.dot_general(lhs, keys, (((1,), (1,)), ((), ())),
                                  preferred_element_type=F32).reshape(n_q, BLOCK, 2 * BLOCK)
    for slot in range(n_q):
        logit = lg_ref[slot] + bias_ref[bias_idx, slot]
        m = jnp.max(logit, axis=-1, keepdims=True)
        p_ref[slot * BLOCK:(slot + 1) * BLOCK, :] = jnp.exp2(logit - m).astype(BF16)
    p = p_ref[...]
    pv_ref[...] = jnp.dot(p, vals, preferred_element_type=F32).reshape(n_q, BLOCK, kv_w)
    rs_ref[...] = jnp.dot(p, ones, preferred_element_type=F32).reshape(n_q, BLOCK, BLOCK)
    per_half = BLOCK // HEAD_DIM
    for j in range(q_per_kv):
        for half in range(kv_w // BLOCK):
            lanes = slice(half * BLOCK, (half + 1) * BLOCK)
            slots = [j * n_kv + half * per_half + i for i in range(per_half)]
            num = pv_ref[slots[-1], :, lanes]
            den = rs_ref[slots[-1]]
            for i in range(per_half - 2, -1, -1):
                num = jnp.where(lane_slot == i, pv_ref[slots[i], :, lanes], num)
                den = jnp.where(lane_slot == i, rs_ref[slots[i]], den)
            o_ref[:, j * kv_w + half * BLOCK:j * kv_w + (half + 1) * BLOCK] = (
                num * (1.0 / den)).astype(o_ref.dtype)


_CONV_HALO = 32
_CONV_LANES = 256
_SUBLANES = 8


def _conv_zero_history(seq_start, u_ref):
    @pl.when(seq_start)
    def _():
        u_ref[:_CONV_HALO, :] = jnp.zeros((_CONV_HALO, u_ref.shape[1]), F32)


def _conv_block(uin_ref, w_ref, cb_ref, lg_ref, lb_ref, o_ref, u_ref, sh_ref, y_ref):
    width = w_ref.shape[0]
    ts, channels = o_ref.shape
    u_ref[_CONV_HALO:, :] = uin_ref[...].astype(F32)
    span = sh_ref.shape[1]
    for r in range(1, _SUBLANES):
        sh_ref[r - 1] = u_ref[r:r + span, :]
    first = _CONV_HALO - (width - 1)
    for c0 in range(0, channels, _CONV_LANES):
        cs = slice(c0, c0 + _CONV_LANES)
        acc = jnp.broadcast_to(cb_ref[:, cs], (ts, _CONV_LANES))
        for j in range(width):
            tile, r = divmod(first + j, _SUBLANES)
            rows = slice(tile * _SUBLANES, tile * _SUBLANES + ts)
            taps = u_ref[rows, cs] if r == 0 else sh_ref[r - 1, rows, cs]
            acc = acc + taps * w_ref[j:j + 1, cs]
        y_ref[:, cs] = acc
    y = y_ref[...]
    mu = jnp.mean(y, axis=-1, keepdims=True)
    yc = y - mu
    var = jnp.mean(yc * yc, axis=-1, keepdims=True)
    z = yc * lax.rsqrt(var + EPS) * lg_ref[...] + lb_ref[...]
    o_ref[...] = (z * _sigmoid(z)).astype(o_ref.dtype)
    u_ref[:_CONV_HALO, :] = u_ref[ts:, :]


def _core_body(h_ref, w_ref,
               uin_ref, cw_ref, cb_ref, lg_ref, lb_ref,
               rb_ref, sink_ref, bucket_ref, qmask_ref, krow_ref, q_ref, kc_ref, kp_ref, vc_ref, vp_ref,
               gates_ref, yconv_ref, yattn_ref,
               wbf_ref, u_ref, sh_ref, y_ref, bias_ref, lgt_ref, p_ref, pv_ref, rs_ref,
               *, n_q, n_kv, blocks_per_seq):
    t = pl.program_id(0) * pl.num_programs(1) + pl.program_id(1)
    seq_start = t % blocks_per_seq == 0
    _cast_weights_once([w_ref], [wbf_ref])
    _conv_zero_history(seq_start, u_ref)
    _attn_build_bias(t == 0, rb_ref, sink_ref, bucket_ref, bias_ref, n_q=n_q, n_kv=n_kv)

    acc = jnp.dot(h_ref[...], wbf_ref[...], preferred_element_type=F32)
    gates_ref[...] = acc.astype(gates_ref.dtype)
    _conv_block(uin_ref, cw_ref, cb_ref, lg_ref, lb_ref, yconv_ref, u_ref, sh_ref, y_ref)
    _attn_block(seq_start, qmask_ref, krow_ref, q_ref, kc_ref, kp_ref, vc_ref, vp_ref, yattn_ref,
                bias_ref, lgt_ref, p_ref, pv_ref, rs_ref, n_q=n_q, n_kv=n_kv)


def _mixer_core(h, w_in, layer, gate_col, conv_in, qkv, conv_w, conv_b, ln_g, ln_b, rel_bias, sinks,
                s_len, n_q, n_kv, tn=512):
    m, k = h.shape
    n = w_in.shape[2] - gate_col
    width, channels = conv_w.shape
    attn_w, kv_w = n_q * HEAD_DIM, n_kv * HEAD_DIM
    n_blocks = m // BLOCK
    n_tiles = n // tn
    assert gate_col % tn == 0 and n % tn == 0 and n_blocks % n_tiles == 0
    assert s_len % BLOCK == 0 and width - 1 <= _CONV_HALO and _CONV_HALO % _SUBLANES == 0
    assert attn_w % kv_w == 0 and kv_w == 2 * BLOCK and BLOCK % HEAD_DIM == 0
    m_tiles = n_blocks // n_tiles
    tm = m // m_tiles
    col_blk0 = gate_col // tn
    k_blk = attn_w // kv_w
    v_blk = k_blk + 1

    step = lambda j, i: j * m_tiles + i
    prev_step = lambda j, i: jnp.maximum(step(j, i) - 1, 0)
    const2 = lambda j, i: (0, 0)
    row = lambda v: v.reshape(1, channels)
    lane_group = jnp.arange(kv_w) // HEAD_DIM
    qmask = jnp.broadcast_to((lane_group[None, :] == jnp.arange(n_kv)[:, None])[:, None, :],
                             (n_kv, BLOCK, kv_w)).astype(BF16)
    krow = jnp.broadcast_to((jnp.arange(2 * BLOCK) > 0)[:, None], (2 * BLOCK, kv_w)).astype(BF16)
    body = functools.partial(_core_body, n_q=n_q, n_kv=n_kv, blocks_per_seq=s_len // BLOCK)
    return pl.pallas_call(
        body,
        grid=(n_tiles, m_tiles),
        in_specs=[
            pl.BlockSpec((tm, k), lambda j, i: (i, 0)),
            pl.BlockSpec((None, k, tn), lambda j, i: (layer, 0, col_blk0 + j)),
            pl.BlockSpec((BLOCK, channels), lambda j, i: (step(j, i), 0)),
            pl.BlockSpec((width, channels), const2),
            pl.BlockSpec((1, channels), const2),
            pl.BlockSpec((1, channels), const2),
            pl.BlockSpec((1, channels), const2),
            pl.BlockSpec(memory_space=pltpu.SMEM),
            pl.BlockSpec(memory_space=pltpu.SMEM),
            pl.BlockSpec((BLOCK, 2 * BLOCK), const2),
            pl.BlockSpec((n_kv, BLOCK, kv_w), lambda j, i: (0, 0, 0)),
            pl.BlockSpec((2 * BLOCK, kv_w), const2),
            pl.BlockSpec((BLOCK, attn_w), lambda j, i: (step(j, i), 0)),
            pl.BlockSpec((BLOCK, kv_w), lambda j, i: (step(j, i), k_blk)),
            pl.BlockSpec((BLOCK, kv_w), lambda j, i: (prev_step(j, i), k_blk)),
            pl.BlockSpec((BLOCK, kv_w), lambda j, i: (step(j, i), v_blk)),
            pl.BlockSpec((BLOCK, kv_w), lambda j, i: (prev_step(j, i), v_blk)),
        ],
        out_specs=[
            pl.BlockSpec((tm, tn), lambda j, i: (i, j)),
            pl.BlockSpec((BLOCK, channels), lambda j, i: (step(j, i), 0)),
            pl.BlockSpec((BLOCK, attn_w), lambda j, i: (step(j, i), 0)),
        ],
        out_shape=[
            jax.ShapeDtypeStruct((m, n), BF16),
            jax.ShapeDtypeStruct((m, channels), BF16),
            jax.ShapeDtypeStruct((m, attn_w), BF16),
        ],
        scratch_shapes=[
            pltpu.VMEM((k, tn), BF16),
            pltpu.VMEM((_CONV_HALO + BLOCK, channels), F32),
            pltpu.VMEM((_SUBLANES - 1, _CONV_HALO - _SUBLANES + BLOCK, channels), F32),
            pltpu.VMEM((BLOCK, channels), F32),
            pltpu.VMEM((2, n_q, BLOCK, 2 * BLOCK), F32),
            pltpu.VMEM((n_q, BLOCK, 2 * BLOCK), F32),
            pltpu.VMEM((n_q * BLOCK, 2 * BLOCK), BF16),
            pltpu.VMEM((n_q, BLOCK, kv_w), F32),
            pltpu.VMEM((n_q, BLOCK, BLOCK), F32),
        ],
        compiler_params=_params(2),
        name="mixer_core",
    )(h, w_in,
      conv_in, conv_w, row(conv_b), row(ln_g), row(ln_b),
      rel_bias.astype(F32).reshape(-1), sinks.astype(F32), _band_buckets(), qmask, krow,
      qkv, qkv, qkv, qkv, qkv)


def kernel(x, c, w_ada, b_ada, norm_mix_g, w_in, q_norm_g, k_norm_g, attn_sinks, rel_bias,
           w_attn_out, conv_w, conv_b, conv_ln_g, conv_ln_b, w_conv_out, w_mix_out,
           norm_ffn_g, w_ffn_in, w_ffn_out):
    b_sz, s_len, d = x.shape
    depth = w_ada.shape[0]
    n_q = attn_sinks.shape[1]
    attn_w = w_attn_out.shape[1]
    channels = conv_w.shape[2]
    kv_w = (w_in.shape[2] - attn_w - 2 * channels - 2 * d) // 2
    n_kv = kv_w // HEAD_DIM
    conv_col = attn_w + 2 * kv_w
    gate_col = conv_col + 2 * channels
    m = b_sz * s_len

    for l in range(depth):
        mod = _ada(c, w_ada, b_ada[l], l)
        mod3 = mod.reshape(b_sz * N_MOD, 1, d)

        qkv, h = _qkv_proj(x, norm_mix_g[l], mod3, 1, 0, w_in, l, q_norm_g[l], k_norm_g[l], n_q, n_kv)
        h = h.reshape(m, d)
        proj_tm, proj_tn = 1024, 512
        assert (channels // proj_tn) * (m // proj_tm) == n_q
        slot_head = lambda slot: _orig_head(slot, n_kv, n_q // n_kv)
        same = lambda step: step
        conv_in, wa_bf, wc_bf, wmix_bf = _glu_proj(
            h, w_in, l, conv_col, conv_col + channels, channels, proj_tm, proj_tn,
            cast_jobs=((w_attn_out, slot_head), (w_conv_out, same), (w_mix_out, same)))
        gates, y_conv, y_attn = _mixer_core(h, w_in, l, gate_col, conv_in, qkv, conv_w[l], conv_b[l],
                                            conv_ln_g[l], conv_ln_b[l], rel_bias, attn_sinks[l],
                                            s_len, n_q, n_kv)
        merged = _merge(y_attn, y_conv, wa_bf, wc_bf, gates, 0, d)
        x1, h = _mix_out(merged, wmix_bf, x, mod3, 2, norm_ffn_g[l], 4, 3)

        act, wo_bf = _ffn_in(h.reshape(m, d), w_ffn_in, w_ffn_out, l)
        x = _residual_proj(act, wo_bf, x1.reshape(m, d), mod3, 5, s_len,
                           tm=512, tn=1024).reshape(b_sz, s_len, d)
    return x
```

```python
import functools
import math

import jax
import jax.numpy as jnp
from jax import lax
from jax.experimental import pallas as pl
from jax.experimental.pallas import tpu as pltpu

F32 = jnp.float32
BF16 = jnp.bfloat16

HEAD_DIM = 64
WINDOW = 128
BLOCK = 128
NUM_BUCKETS = 32
MAX_EXACT = NUM_BUCKETS // 2
MAX_DISTANCE = 128
N_MOD = 6
EPS = 1e-6
LOG2E = math.log2(math.e)

V7X_VMEM_LIMIT_BYTES = 56 * 1024 * 1024


def _params(n_axes):
    return pltpu.CompilerParams(
        dimension_semantics=("arbitrary",) * n_axes,
        vmem_limit_bytes=V7X_VMEM_LIMIT_BYTES,
    )


def _sigmoid(x):
    return jax.nn.sigmoid(x)


def _orig_head(slot, n_kv, q_per_kv):
    return (slot % n_kv) * q_per_kv + slot // n_kv


def _ada_body(c_ref, w_ref, b_ref, o_ref):
    c = c_ref[...]
    act = (c * _sigmoid(c)).astype(BF16)
    o_ref[...] = jnp.dot(act, w_ref[...].astype(BF16), preferred_element_type=F32) + b_ref[...]


def _ada(c, w_ada, b_ada, layer, tn=1024):
    b_sz, d = c.shape
    n = w_ada.shape[2]
    rows = 8
    c_pad = jnp.pad(c, ((0, rows - b_sz), (0, 0)))
    out = pl.pallas_call(
        _ada_body,
        grid=(n // tn,),
        in_specs=[
            pl.BlockSpec((rows, d), lambda j: (0, 0)),
            pl.BlockSpec((None, d, tn), lambda j: (layer, 0, j)),
            pl.BlockSpec((1, tn), lambda j: (0, j)),
        ],
        out_specs=pl.BlockSpec((rows, tn), lambda j: (0, j)),
        out_shape=jax.ShapeDtypeStruct((rows, n), F32),
        compiler_params=_params(1),
        name="ada_mod",
    )(c_pad, w_ada, b_ada.reshape(1, n))
    return out[:b_sz]


def _norm_mod_body(x_ref, g_ref, sc_ref, sh_ref, o_ref):
    x = x_ref[0]
    ms = jnp.mean(x * x, axis=-1, keepdims=True)
    gain = g_ref[...] * (1.0 + sc_ref[0])
    o_ref[0] = (x * lax.rsqrt(ms + EPS) * gain + sh_ref[0]).astype(o_ref.dtype)


def _qkv_body(x_ref, g_ref, sc_ref, sh_ref, w_ref, gq_ref, gk_ref, o_ref, h_ref, wbf_ref, seg_ref,
              *, n_q, n_kv, half):
    attn_w = n_q * HEAD_DIM
    kv_w = n_kv * HEAD_DIM
    _norm_mod_body(x_ref, g_ref, sc_ref, sh_ref, h_ref)

    @pl.when(jnp.logical_and(pl.program_id(0) == 0, pl.program_id(1) == 0))
    def _():
        for slot in range(n_q):
            src = _orig_head(slot, n_kv, n_q // n_kv) * HEAD_DIM
            wbf_ref[:, slot * HEAD_DIM:(slot + 1) * HEAD_DIM] = w_ref[:, src:src + HEAD_DIM].astype(BF16)
        wbf_ref[:, attn_w:] = w_ref[:, attn_w:].astype(BF16)
        r = lax.broadcasted_iota(jnp.int32, (half, half), 0) // HEAD_DIM
        c = lax.broadcasted_iota(jnp.int32, (half, half), 1) // HEAD_DIM
        seg_ref[...] = (r == c).astype(BF16)

    a = h_ref[0]

    def head_norm(acc, seg, gain):
        ss = jnp.dot((acc * acc).astype(BF16), seg, preferred_element_type=F32)
        return acc * lax.rsqrt(ss / HEAD_DIM + EPS) * gain

    for c0 in range(0, attn_w, half):
        acc = jnp.dot(a, wbf_ref[:, c0:c0 + half], preferred_element_type=F32)
        o_ref[:, c0:c0 + half] = head_norm(acc, seg_ref[...], gq_ref[...]).astype(o_ref.dtype)
    acc = jnp.dot(a, wbf_ref[:, attn_w:], preferred_element_type=F32)
    o_ref[:, attn_w:attn_w + kv_w] = head_norm(acc[:, :kv_w], seg_ref[:kv_w, :kv_w],
                                               gk_ref[...]).astype(o_ref.dtype)
    o_ref[:, attn_w + kv_w:] = acc[:, kv_w:].astype(o_ref.dtype)


def _qkv_proj(x, norm_g, mod3, scale_idx, shift_idx, w_in, layer, q_g, k_g, n_q, n_kv, ts=512, half=512):
    b_sz, s_len, d = x.shape
    attn_w, kv_w = n_q * HEAD_DIM, n_kv * HEAD_DIM
    width = attn_w + 2 * kv_w
    assert attn_w % half == 0 and kv_w <= half and half % HEAD_DIM == 0
    gq = jnp.tile(q_g * (HEAD_DIM ** -0.5 * LOG2E), half // HEAD_DIM).reshape(1, half)
    gk = jnp.tile(k_g, n_kv).reshape(1, kv_w)
    tiles = s_len // ts
    const2 = lambda b, s: (0, 0)
    body = functools.partial(_qkv_body, n_q=n_q, n_kv=n_kv, half=half)
    return pl.pallas_call(
        body,
        grid=(b_sz, tiles),
        in_specs=[
            pl.BlockSpec((1, ts, d), lambda b, s: (b, s, 0)),
            pl.BlockSpec((1, d), const2),
            pl.BlockSpec((1, 1, d), lambda b, s: (b * N_MOD + scale_idx, 0, 0)),
            pl.BlockSpec((1, 1, d), lambda b, s: (b * N_MOD + shift_idx, 0, 0)),
            pl.BlockSpec((None, d, width), lambda b, s: (layer, 0, 0), pipeline_mode=pl.Buffered(1)),
            pl.BlockSpec((1, half), const2),
            pl.BlockSpec((1, kv_w), const2),
        ],
        out_specs=[
            pl.BlockSpec((ts, width), lambda b, s: (b * tiles + s, 0)),
            pl.BlockSpec((1, ts, d), lambda b, s: (b, s, 0)),
        ],
        out_shape=[
            jax.ShapeDtypeStruct((b_sz * s_len, width), BF16),
            jax.ShapeDtypeStruct((b_sz, s_len, d), BF16),
        ],
        scratch_shapes=[pltpu.VMEM((d, width), BF16), pltpu.VMEM((half, half), BF16)],
        compiler_params=_params(2),
        name="qkv_proj",
    )(x, norm_g.reshape(1, d), mod3, mod3, w_in, gq, gk)


def _cast_weights_once(w_refs, wbf_refs):
    @pl.when(pl.program_id(1) == 0)
    def _():
        for w_ref, wbf_ref in zip(w_refs, wbf_refs):
            wbf_ref[...] = w_ref[...].astype(BF16)


def _glu_proj_body(h_ref, wa_ref, wg_ref, *refs):
    n_jobs = (len(refs) - 3) // 2
    job_in, o_ref, job_out = refs[:n_jobs], refs[n_jobs], refs[n_jobs + 1:-2]
    wabf_ref, wgbf_ref = refs[-2:]
    _cast_weights_once([wa_ref, wg_ref], [wabf_ref, wgbf_ref])
    for src_ref, dst_ref in zip(job_in, job_out):
        dst_ref[...] = src_ref[...].astype(BF16)
    for r0 in range(0, h_ref.shape[0], _MATMUL_ROWS):
        rs = slice(r0, r0 + _MATMUL_ROWS)
        h = h_ref[rs, :]
        a = jnp.dot(h, wabf_ref[...], preferred_element_type=F32)
        g = jnp.dot(h, wgbf_ref[...], preferred_element_type=F32)
        o_ref[rs, :] = (a * _sigmoid(g)).astype(o_ref.dtype)


def _glu_proj(h, w, layer, a_col, g_col, n, tm, tn, cast_jobs=()):
    m, k = h.shape
    assert a_col % tn == 0 and g_col % tn == 0 and n % tn == 0
    a_blk0, g_blk0 = a_col // tn, g_col // tn
    m_tiles = m // tm
    steps = (n // tn) * m_tiles
    step = lambda j, i: j * m_tiles + i
    job_in, job_out, job_shapes = [], [], []
    for wj, row_block_of_step in cast_jobs:
        rows, cols = wj.shape[1:]
        assert rows % steps == 0
        slab = rows // steps
        job_in.append(pl.BlockSpec((None, slab, cols),
                                   lambda j, i, f=row_block_of_step: (layer, f(step(j, i)), 0)))
        job_out.append(pl.BlockSpec((slab, cols), lambda j, i: (step(j, i), 0)))
        job_shapes.append(jax.ShapeDtypeStruct((rows, cols), BF16))
    return pl.pallas_call(
        _glu_proj_body,
        grid=(n // tn, m_tiles),
        in_specs=[
            pl.BlockSpec((tm, k), lambda j, i: (i, 0)),
            pl.BlockSpec((None, k, tn), lambda j, i: (layer, 0, a_blk0 + j)),
            pl.BlockSpec((None, k, tn), lambda j, i: (layer, 0, g_blk0 + j)),
        ] + job_in,
        out_specs=[pl.BlockSpec((tm, tn), lambda j, i: (i, j))] + job_out,
        out_shape=[jax.ShapeDtypeStruct((m, n), BF16)] + job_shapes,
        scratch_shapes=[pltpu.VMEM((k, tn), BF16), pltpu.VMEM((k, tn), BF16)],
        compiler_params=_params(2),
        name="glu_proj",
    )(h, w, w, *[wj for wj, _ in cast_jobs])


_EPILOGUE_LANES = 512
_MATMUL_ROWS = 1024


def _merge_body(ya_ref, yc_ref, wa_ref, wc_ref, ga_ref, gc_ref, o_ref):
    ya = ya_ref[...]
    yc = yc_ref[...]
    for n0 in range(0, o_ref.shape[1], _EPILOGUE_LANES):
        ns = slice(n0, n0 + _EPILOGUE_LANES)
        acc_a = jnp.dot(ya, wa_ref[:, ns], preferred_element_type=F32)
        acc_c = jnp.dot(yc, wc_ref[:, ns], preferred_element_type=F32)
        ga = _sigmoid(ga_ref[:, ns].astype(F32))
        gc = _sigmoid(gc_ref[:, ns].astype(F32))
        o_ref[:, ns] = (ga * acc_a + gc * acc_c).astype(o_ref.dtype)


def _merge(attn, conv, wa_bf, wc_bf, gates, ga_col, gc_col, tm=512):
    m, ka = attn.shape
    kc = conv.shape[1]
    n = wa_bf.shape[1]
    assert ga_col % n == 0 and gc_col % n == 0 and n % _EPILOGUE_LANES == 0
    ga_blk, gc_blk = ga_col // n, gc_col // n
    resident = pl.Buffered(1)
    return pl.pallas_call(
        _merge_body,
        grid=(m // tm,),
        in_specs=[
            pl.BlockSpec((tm, ka), lambda i: (i, 0)),
            pl.BlockSpec((tm, kc), lambda i: (i, 0)),
            pl.BlockSpec((ka, n), lambda i: (0, 0), pipeline_mode=resident),
            pl.BlockSpec((kc, n), lambda i: (0, 0), pipeline_mode=resident),
            pl.BlockSpec((tm, n), lambda i: (i, ga_blk)),
            pl.BlockSpec((tm, n), lambda i: (i, gc_blk)),
        ],
        out_specs=pl.BlockSpec((tm, n), lambda i: (i, 0)),
        out_shape=jax.ShapeDtypeStruct((m, n), BF16),
        compiler_params=_params(1),
        name="merge",
    )(attn, conv, wa_bf, wc_bf, gates, gates)


def _mix_out_body(a_ref, w_ref, x_ref, gt_ref, g_ref, sc_ref, sh_ref, x1_ref, h_ref):
    a = a_ref[...]
    ts, d = a.shape
    sumsq = jnp.zeros((ts, 1), F32)
    for n0 in range(0, d, _EPILOGUE_LANES):
        ns = slice(n0, n0 + _EPILOGUE_LANES)
        acc = jnp.dot(a, w_ref[:, ns], preferred_element_type=F32)
        x1 = x_ref[0, :, ns] + gt_ref[0, :, ns] * acc
        x1_ref[0, :, ns] = x1
        sumsq = sumsq + jnp.sum(x1 * x1, axis=-1, keepdims=True)
    inv = lax.rsqrt(sumsq / d + EPS)
    gain = g_ref[...] * (1.0 + sc_ref[0])
    h_ref[0] = (x1_ref[0] * inv * gain + sh_ref[0]).astype(h_ref.dtype)


def _mix_out(a, w_bf, x, mod3, gate_idx, norm_g, scale_idx, shift_idx, ts=512):
    b_sz, s_len, d = x.shape
    k = a.shape[1]
    assert d % _EPILOGUE_LANES == 0
    tiles = s_len // ts
    const2 = lambda b, s: (0, 0)
    mod_row = lambda idx: pl.BlockSpec((1, 1, d), lambda b, s: (b * N_MOD + idx, 0, 0))
    act = pl.BlockSpec((1, ts, d), lambda b, s: (b, s, 0))
    return pl.pallas_call(
        _mix_out_body,
        grid=(b_sz, tiles),
        in_specs=[
            pl.BlockSpec((ts, k), lambda b, s: (b * tiles + s, 0)),
            pl.BlockSpec((k, d), const2, pipeline_mode=pl.Buffered(1)),
            act,
            mod_row(gate_idx),
            pl.BlockSpec((1, d), const2),
            mod_row(scale_idx),
            mod_row(shift_idx),
        ],
        out_specs=[act, act],
        out_shape=[jax.ShapeDtypeStruct((b_sz, s_len, d), F32), jax.ShapeDtypeStruct((b_sz, s_len, d), BF16)],
        compiler_params=_params(2),
        name="mix_out",
    )(a, w_bf, x, mod3, norm_g.reshape(1, d), mod3, mod3)


def _residual_proj_body(a_ref, w_ref, x_ref, gt_ref, o_ref):
    a = a_ref[...]
    for n0 in range(0, o_ref.shape[1], _EPILOGUE_LANES):
        ns = slice(n0, n0 + _EPILOGUE_LANES)
        acc = jnp.dot(a, w_ref[:, ns], preferred_element_type=F32)
        o_ref[:, ns] = x_ref[:, ns] + gt_ref[0, :, ns] * acc


def _residual_proj(a, w_bf, x2d, mod3, gate_idx, rows_per_batch, tm, tn):
    m, k = a.shape
    n = w_bf.shape[1]
    assert n % tn == 0 and tn % _EPILOGUE_LANES == 0 and rows_per_batch % tm == 0
    tiles_per_batch = rows_per_batch // tm
    return pl.pallas_call(
        _residual_proj_body,
        grid=(n // tn, m // tm),
        in_specs=[
            pl.BlockSpec((tm, k), lambda j, i: (i, 0)),
            pl.BlockSpec((k, tn), lambda j, i: (0, j)),
            pl.BlockSpec((tm, tn), lambda j, i: (i, j)),
            pl.BlockSpec((1, 1, tn), lambda j, i: ((i // tiles_per_batch) * N_MOD + gate_idx, 0, j)),
        ],
        out_specs=pl.BlockSpec((tm, tn), lambda j, i: (i, j)),
        out_shape=jax.ShapeDtypeStruct((m, n), F32),
        compiler_params=_params(2),
        name="residual_proj",
    )(a, w_bf, x2d, mod3)


def _ffn_in_body(a_ref, wg_ref, wu_ref, wo_ref, o_ref, wobf_ref, wgbf_ref, wubf_ref):
    _cast_weights_once([wg_ref, wu_ref], [wgbf_ref, wubf_ref])
    wobf_ref[...] = wo_ref[...].astype(BF16)
    for r0 in range(0, a_ref.shape[0], _MATMUL_ROWS):
        rs = slice(r0, r0 + _MATMUL_ROWS)
        a = a_ref[rs, :]
        gate = jnp.dot(a, wgbf_ref[...], preferred_element_type=F32)
        up = jnp.dot(a, wubf_ref[...], preferred_element_type=F32)
        o_ref[rs, :] = (gate * _sigmoid(gate) * up).astype(o_ref.dtype)


def _ffn_in(h, w_ffn_in, w_ffn_out, layer, tm=2048, tn=512):
    m, k = h.shape
    d_ff = w_ffn_in.shape[2] // 2
    d_out = w_ffn_out.shape[2]
    up_blk = d_ff // tn
    m_tiles = m // tm
    steps = (d_ff // tn) * m_tiles
    assert d_ff % steps == 0
    slab = d_ff // steps
    step = lambda j, i: j * m_tiles + i
    return pl.pallas_call(
        _ffn_in_body,
        grid=(d_ff // tn, m_tiles),
        in_specs=[
            pl.BlockSpec((tm, k), lambda j, i: (i, 0)),
            pl.BlockSpec((None, k, tn), lambda j, i: (layer, 0, j)),
            pl.BlockSpec((None, k, tn), lambda j, i: (layer, 0, up_blk + j)),
            pl.BlockSpec((None, slab, d_out), lambda j, i: (layer, step(j, i), 0)),
        ],
        out_specs=[
            pl.BlockSpec((tm, tn), lambda j, i: (i, j)),
            pl.BlockSpec((slab, d_out), lambda j, i: (step(j, i), 0)),
        ],
        out_shape=[jax.ShapeDtypeStruct((m, d_ff), BF16), jax.ShapeDtypeStruct((d_ff, d_out), BF16)],
        scratch_shapes=[pltpu.VMEM((k, tn), BF16), pltpu.VMEM((k, tn), BF16)],
        compiler_params=_params(2),
        name="ffn_in",
    )(h, w_ffn_in, w_ffn_in, w_ffn_out)


def _t5_causal_bucket(dist):
    n = jnp.maximum(dist, 0)
    nf = jnp.maximum(n, 1).astype(jnp.float32)
    large = MAX_EXACT + (jnp.log(nf / MAX_EXACT) / math.log(MAX_DISTANCE / MAX_EXACT)
                         * (NUM_BUCKETS - MAX_EXACT)).astype(jnp.int32)
    large = jnp.minimum(large, NUM_BUCKETS - 1)
    return jnp.where(n < MAX_EXACT, n, large)


def _band_buckets():
    q_off = jnp.arange(BLOCK)
    k_off = jnp.arange(2 * BLOCK)
    dist = q_off[:, None] + BLOCK - k_off[None, :]
    allowed = (dist >= 0) & (dist < WINDOW)
    return jnp.where(allowed, _t5_causal_bucket(dist), -1).astype(jnp.int32)


def _attn_build_bias(first_step, rb_ref, sink_ref, bucket_ref, bias_ref, *, n_q, n_kv):
    q_per_kv = n_q // n_kv

    @pl.when(first_step)
    def _():
        bucket = bucket_ref[...]
        col = lax.broadcasted_iota(jnp.int32, bucket.shape, 1)
        for slot in range(n_q):
            head = _orig_head(slot, n_kv, q_per_kv)
            tile = jnp.zeros(bucket.shape, F32)
            for b in range(NUM_BUCKETS):
                tile = jnp.where(bucket == b, rb_ref[b * n_q + head] * LOG2E, tile)
            tile = jnp.where(bucket < 0, -jnp.inf, tile)
            sink = sink_ref[head] * LOG2E
            bias_ref[0, slot] = jnp.where(col == 0, sink, tile)
            bias_ref[1, slot] = jnp.where(col == 0, sink, jnp.where(col < BLOCK, -jnp.inf, tile))


def _attn_block(seq_start, qmask_ref, krow_ref, q_ref, kc_ref, kp_ref, vc_ref, vp_ref, o_ref,
                bias_ref, lg_ref, p_ref, pv_ref, rs_ref, *, n_q, n_kv):
    q_per_kv = n_q // n_kv
    kv_w = n_kv * HEAD_DIM
    lane_slot = lax.broadcasted_iota(jnp.int32, (BLOCK, BLOCK), 1) // HEAD_DIM
    ones = jnp.ones((2 * BLOCK, BLOCK), BF16)
    bias_idx = jnp.where(seq_start, 1, 0)

    keys = jnp.concatenate([kp_ref[...], kc_ref[...]], axis=0) * krow_ref[...]
    vals = jnp.concatenate([vp_ref[...], vc_ref[...]], axis=0) * krow_ref[...]
    lhs = jnp.concatenate(
        [q_ref[:, j * kv_w:(j + 1) * kv_w] * qmask_ref[g]
         for j in range(q_per_kv) for g in range(n_kv)], axis=0)
    lg_ref[...] = lax.dot_general(lhs, keys, (((1,), (1,)), ((), ())),
                                  preferred_element_type=F32).reshape(n_q, BLOCK, 2 * BLOCK)
    for slot in range(n_q):
        logit = lg_ref[slot] + bias_ref[bias_idx, slot]
        m = jnp.max(logit, axis=-1, keepdims=True)
        p_ref[slot * BLOCK:(slot + 1) * BLOCK, :] = jnp.exp2(logit - m).astype(BF16)
    p = p_ref[...]
    pv_ref[...] = jnp.dot(p, vals, preferred_element_type=F32).reshape(n_q, BLOCK, kv_w)
    rs_ref[...] = jnp.dot(p, ones, preferred_element_type=F32).reshape(n_q, BLOCK, BLOCK)
    per_half = BLOCK // HEAD_DIM
    for j in range(q_per_kv):
        for half in range(kv_w // BLOCK):
            lanes = slice(half * BLOCK, (half + 1) * BLOCK)
            slots = [j * n_kv + half * per_half + i for i in range(per_half)]
            num = pv_ref[slots[-1], :, lanes]
            den = rs_ref[slots[-1]]
            for i in range(per_half - 2, -1, -1):
                num = jnp.where(lane_slot == i, pv_ref[slots[i], :, lanes], num)
                den = jnp.where(lane_slot == i, rs_ref[slots[i]], den)
            o_ref[:, j * kv_w + half * BLOCK:j * kv_w + (half + 1) * BLOCK] = (
                num * (1.0 / den)).astype(o_ref.dtype)


_CONV_HALO = 32
_CONV_LANES = 256
_SUBLANES = 8


def _conv_zero_history(seq_start, u_ref):
    @pl.when(seq_start)
    def _():
        u_ref[:_CONV_HALO, :] = jnp.zeros((_CONV_HALO, u_ref.shape[1]), u_ref.dtype)


def _conv_shift_matrix(rows):
    i = jnp.arange(_SUBLANES * rows) % rows
    r = jnp.arange(_SUBLANES * rows) // rows
    return (jnp.arange(rows)[None, :] == (i + r)[:, None]).astype(BF16)


def _conv_block(uin_ref, shift_ref, w_ref, cb_ref, lg_ref, lb_ref, o_ref, u_ref, sh_ref, y_ref):
    width = w_ref.shape[0]
    ts, channels = o_ref.shape
    u_ref[_CONV_HALO:, :] = uin_ref[...]
    sh_ref[...] = jnp.dot(shift_ref[...], u_ref[...], preferred_element_type=F32).reshape(sh_ref.shape)
    first = _CONV_HALO - (width - 1)
    for c0 in range(0, channels, _CONV_LANES):
        cs = slice(c0, c0 + _CONV_LANES)
        acc = jnp.broadcast_to(cb_ref[:, cs], (ts, _CONV_LANES))
        for j in range(width):
            tile, r = divmod(first + j, _SUBLANES)
            rows = slice(tile * _SUBLANES, tile * _SUBLANES + ts)
            acc = acc + sh_ref[r, rows, cs] * w_ref[j:j + 1, cs]
        y_ref[:, cs] = acc
    y = y_ref[...]
    mu = jnp.mean(y, axis=-1, keepdims=True)
    yc = y - mu
    var = jnp.mean(yc * yc, axis=-1, keepdims=True)
    z = yc * lax.rsqrt(var + EPS) * lg_ref[...] + lb_ref[...]
    o_ref[...] = (z * _sigmoid(z)).astype(o_ref.dtype)
    u_ref[:_CONV_HALO, :] = u_ref[ts:, :]


def _core_body(h_ref, w_ref,
               uin_ref, shift_ref, cw_ref, cb_ref, lg_ref, lb_ref,
               rb_ref, sink_ref, bucket_ref, qmask_ref, krow_ref, q_ref, kc_ref, kp_ref, vc_ref, vp_ref,
               gates_ref, yconv_ref, yattn_ref,
               wbf_ref, u_ref, sh_ref, y_ref, bias_ref, lgt_ref, p_ref, pv_ref, rs_ref,
               *, n_q, n_kv, blocks_per_seq):
    t = pl.program_id(0) * pl.num_programs(1) + pl.program_id(1)
    seq_start = t % blocks_per_seq == 0
    _cast_weights_once([w_ref], [wbf_ref])
    _conv_zero_history(seq_start, u_ref)
    _attn_build_bias(t == 0, rb_ref, sink_ref, bucket_ref, bias_ref, n_q=n_q, n_kv=n_kv)

    _conv_block(uin_ref, shift_ref, cw_ref, cb_ref, lg_ref, lb_ref, yconv_ref, u_ref, sh_ref, y_ref)
    acc = jnp.dot(h_ref[...], wbf_ref[...], preferred_element_type=F32)
    gates_ref[...] = acc.astype(gates_ref.dtype)
    _attn_block(seq_start, qmask_ref, krow_ref, q_ref, kc_ref, kp_ref, vc_ref, vp_ref, yattn_ref,
                bias_ref, lgt_ref, p_ref, pv_ref, rs_ref, n_q=n_q, n_kv=n_kv)


def _mixer_core(h, w_in, layer, gate_col, conv_in, qkv, conv_w, conv_b, ln_g, ln_b, rel_bias, sinks,
                s_len, n_q, n_kv, tn=512):
    m, k = h.shape
    n = w_in.shape[2] - gate_col
    width, channels = conv_w.shape
    attn_w, kv_w = n_q * HEAD_DIM, n_kv * HEAD_DIM
    n_blocks = m // BLOCK
    n_tiles = n // tn
    assert gate_col % tn == 0 and n % tn == 0 and n_blocks % n_tiles == 0
    assert s_len % BLOCK == 0 and width - 1 <= _CONV_HALO and _CONV_HALO % _SUBLANES == 0
    assert attn_w % kv_w == 0 and kv_w == 2 * BLOCK and BLOCK % HEAD_DIM == 0
    m_tiles = n_blocks // n_tiles
    tm = m // m_tiles
    col_blk0 = gate_col // tn
    k_blk = attn_w // kv_w
    v_blk = k_blk + 1

    step = lambda j, i: j * m_tiles + i
    prev_step = lambda j, i: jnp.maximum(step(j, i) - 1, 0)
    const2 = lambda j, i: (0, 0)
    row = lambda v: v.reshape(1, channels)
    lane_group = jnp.arange(kv_w) // HEAD_DIM
    qmask = jnp.broadcast_to((lane_group[None, :] == jnp.arange(n_kv)[:, None])[:, None, :],
                             (n_kv, BLOCK, kv_w)).astype(BF16)
    krow = jnp.broadcast_to((jnp.arange(2 * BLOCK) > 0)[:, None], (2 * BLOCK, kv_w)).astype(BF16)
    body = functools.partial(_core_body, n_q=n_q, n_kv=n_kv, blocks_per_seq=s_len // BLOCK)
    return pl.pallas_call(
        body,
        grid=(n_tiles, m_tiles),
        in_specs=[
            pl.BlockSpec((tm, k), lambda j, i: (i, 0)),
            pl.BlockSpec((None, k, tn), lambda j, i: (layer, 0, col_blk0 + j)),
            pl.BlockSpec((BLOCK, channels), lambda j, i: (step(j, i), 0)),
            pl.BlockSpec((_SUBLANES * (_CONV_HALO + BLOCK), _CONV_HALO + BLOCK), const2),
            pl.BlockSpec((width, channels), const2),
            pl.BlockSpec((1, channels), const2),
            pl.BlockSpec((1, channels), const2),
            pl.BlockSpec((1, channels), const2),
            pl.BlockSpec(memory_space=pltpu.SMEM),
            pl.BlockSpec(memory_space=pltpu.SMEM),
            pl.BlockSpec((BLOCK, 2 * BLOCK), const2),
            pl.BlockSpec((n_kv, BLOCK, kv_w), lambda j, i: (0, 0, 0)),
            pl.BlockSpec((2 * BLOCK, kv_w), const2),
            pl.BlockSpec((BLOCK, attn_w), lambda j, i: (step(j, i), 0)),
            pl.BlockSpec((BLOCK, kv_w), lambda j, i: (step(j, i), k_blk)),
            pl.BlockSpec((BLOCK, kv_w), lambda j, i: (prev_step(j, i), k_blk)),
            pl.BlockSpec((BLOCK, kv_w), lambda j, i: (step(j, i), v_blk)),
            pl.BlockSpec((BLOCK, kv_w), lambda j, i: (prev_step(j, i), v_blk)),
        ],
        out_specs=[
            pl.BlockSpec((tm, tn), lambda j, i: (i, j)),
            pl.BlockSpec((BLOCK, channels), lambda j, i: (step(j, i), 0)),
            pl.BlockSpec((BLOCK, attn_w), lambda j, i: (step(j, i), 0)),
        ],
        out_shape=[
            jax.ShapeDtypeStruct((m, n), BF16),
            jax.ShapeDtypeStruct((m, channels), BF16),
            jax.ShapeDtypeStruct((m, attn_w), BF16),
        ],
        scratch_shapes=[
            pltpu.VMEM((k, tn), BF16),
            pltpu.VMEM((_CONV_HALO + BLOCK, channels), BF16),
            pltpu.VMEM((_SUBLANES, _CONV_HALO + BLOCK, channels), F32),
            pltpu.VMEM((BLOCK, channels), F32),
            pltpu.VMEM((2, n_q, BLOCK, 2 * BLOCK), F32),
            pltpu.VMEM((n_q, BLOCK, 2 * BLOCK), F32),
            pltpu.VMEM((n_q * BLOCK, 2 * BLOCK), BF16),
            pltpu.VMEM((n_q, BLOCK, kv_w), F32),
            pltpu.VMEM((n_q, BLOCK, BLOCK), F32),
        ],
        compiler_params=_params(2),
        name="mixer_core",
    )(h, w_in,
      conv_in, _conv_shift_matrix(_CONV_HALO + BLOCK), conv_w, row(conv_b), row(ln_g), row(ln_b),
      rel_bias.astype(F32).reshape(-1), sinks.astype(F32), _band_buckets(), qmask, krow,
      qkv, qkv, qkv, qkv, qkv)


def kernel(x, c, w_ada, b_ada, norm_mix_g, w_in, q_norm_g, k_norm_g, attn_sinks, rel_bias,
           w_attn_out, conv_w, conv_b, conv_ln_g, conv_ln_b, w_conv_out, w_mix_out,
           norm_ffn_g, w_ffn_in, w_ffn_out):
    b_sz, s_len, d = x.shape
    depth = w_ada.shape[0]
    n_q = attn_sinks.shape[1]
    attn_w = w_attn_out.shape[1]
    channels = conv_w.shape[2]
    kv_w = (w_in.shape[2] - attn_w - 2 * channels - 2 * d) // 2
    n_kv = kv_w // HEAD_DIM
    conv_col = attn_w + 2 * kv_w
    gate_col = conv_col + 2 * channels
    m = b_sz * s_len

    for l in range(depth):
        mod = _ada(c, w_ada, b_ada[l], l)
        mod3 = mod.reshape(b_sz * N_MOD, 1, d)

        qkv, h = _qkv_proj(x, norm_mix_g[l], mod3, 1, 0, w_in, l, q_norm_g[l], k_norm_g[l], n_q, n_kv)
        h = h.reshape(m, d)
        proj_tm, proj_tn = 2048, 256
        assert (channels // proj_tn) * (m // proj_tm) == n_q
        slot_head = lambda slot: _orig_head(slot, n_kv, n_q // n_kv)
        same = lambda step: step
        conv_in, wa_bf, wc_bf, wmix_bf = _glu_proj(
            h, w_in, l, conv_col, conv_col + channels, channels, proj_tm, proj_tn,
            cast_jobs=((w_attn_out, slot_head), (w_conv_out, same), (w_mix_out, same)))
        gates, y_conv, y_attn = _mixer_core(h, w_in, l, gate_col, conv_in, qkv, conv_w[l], conv_b[l],
                                            conv_ln_g[l], conv_ln_b[l], rel_bias, attn_sinks[l],
                                            s_len, n_q, n_kv)
        merged = _merge(y_attn, y_conv, wa_bf, wc_bf, gates, 0, d)
        x1, h = _mix_out(merged, wmix_bf, x, mod3, 2, norm_ffn_g[l], 4, 3)

        act, wo_bf = _ffn_in(h.reshape(m, d), w_ffn_in, w_ffn_out, l)
        x = _residual_proj(act, wo_bf, x1.reshape(m, d), mod3, 5, s_len,
                           tm=512, tn=1024).reshape(b_sz, s_len, d)
    return x
```

```python
import functools
import math

import jax
import jax.numpy as jnp
from jax import lax
from jax.experimental import pallas as pl
from jax.experimental.pallas import tpu as pltpu

F32 = jnp.float32
BF16 = jnp.bfloat16

HEAD_DIM = 64
WINDOW = 128
BLOCK = 128
NUM_BUCKETS = 32
MAX_EXACT = NUM_BUCKETS // 2
MAX_DISTANCE = 128
N_MOD = 6
EPS = 1e-6
LOG2E = math.log2(math.e)

V7X_VMEM_LIMIT_BYTES = 56 * 1024 * 1024


def _params(n_axes):
    return pltpu.CompilerParams(
        dimension_semantics=("arbitrary",) * n_axes,
        vmem_limit_bytes=V7X_VMEM_LIMIT_BYTES,
    )


def _sigmoid(x):
    return jax.nn.sigmoid(x)


def _orig_head(slot, n_kv, q_per_kv):
    return (slot % n_kv) * q_per_kv + slot // n_kv


def _ada_body(c_ref, w_ref, b_ref, o_ref):
    c = c_ref[...]
    act = (c * _sigmoid(c)).astype(BF16)
    o_ref[...] = jnp.dot(act, w_ref[...].astype(BF16), preferred_element_type=F32) + b_ref[...]


def _ada(c, w_ada, b_ada, layer, tn=1024):
    b_sz, d = c.shape
    n = w_ada.shape[2]
    rows = 8
    c_pad = jnp.pad(c, ((0, rows - b_sz), (0, 0)))
    out = pl.pallas_call(
        _ada_body,
        grid=(n // tn,),
        in_specs=[
            pl.BlockSpec((rows, d), lambda j: (0, 0)),
            pl.BlockSpec((None, d, tn), lambda j: (layer, 0, j)),
            pl.BlockSpec((1, tn), lambda j: (0, j)),
        ],
        out_specs=pl.BlockSpec((rows, tn), lambda j: (0, j)),
        out_shape=jax.ShapeDtypeStruct((rows, n), F32),
        compiler_params=_params(1),
        name="ada_mod",
    )(c_pad, w_ada, b_ada.reshape(1, n))
    return out[:b_sz]


def _norm_mod_body(x_ref, g_ref, sc_ref, sh_ref, o_ref):
    x = x_ref[0]
    ms = jnp.mean(x * x, axis=-1, keepdims=True)
    gain = g_ref[...] * (1.0 + sc_ref[0])
    o_ref[0] = (x * lax.rsqrt(ms + EPS) * gain + sh_ref[0]).astype(o_ref.dtype)


def _qkv_body(x_ref, g_ref, sc_ref, sh_ref, w_ref, gq_ref, gk_ref, o_ref, h_ref, wbf_ref, seg_ref,
              *, n_q, n_kv, half):
    attn_w = n_q * HEAD_DIM
    kv_w = n_kv * HEAD_DIM
    _norm_mod_body(x_ref, g_ref, sc_ref, sh_ref, h_ref)

    @pl.when(jnp.logical_and(pl.program_id(0) == 0, pl.program_id(1) == 0))
    def _():
        for slot in range(n_q):
            src = _orig_head(slot, n_kv, n_q // n_kv) * HEAD_DIM
            wbf_ref[:, slot * HEAD_DIM:(slot + 1) * HEAD_DIM] = w_ref[:, src:src + HEAD_DIM].astype(BF16)
        wbf_ref[:, attn_w:] = w_ref[:, attn_w:].astype(BF16)
        r = lax.broadcasted_iota(jnp.int32, (half, half), 0) // HEAD_DIM
        c = lax.broadcasted_iota(jnp.int32, (half, half), 1) // HEAD_DIM
        seg_ref[...] = (r == c).astype(BF16)

    a = h_ref[0]

    def head_norm(acc, seg, gain):
        ss = jnp.dot((acc * acc).astype(BF16), seg, preferred_element_type=F32)
        return acc * lax.rsqrt(ss / HEAD_DIM + EPS) * gain

    for c0 in range(0, attn_w, half):
        acc = jnp.dot(a, wbf_ref[:, c0:c0 + half], preferred_element_type=F32)
        o_ref[:, c0:c0 + half] = head_norm(acc, seg_ref[...], gq_ref[...]).astype(o_ref.dtype)
    acc = jnp.dot(a, wbf_ref[:, attn_w:], preferred_element_type=F32)
    o_ref[:, attn_w:attn_w + kv_w] = head_norm(acc[:, :kv_w], seg_ref[:kv_w, :kv_w],
                                               gk_ref[...]).astype(o_ref.dtype)
    o_ref[:, attn_w + kv_w:] = acc[:, kv_w:].astype(o_ref.dtype)


def _qkv_proj(x, norm_g, mod3, scale_idx, shift_idx, w_in, layer, q_g, k_g, n_q, n_kv, ts=512, half=512):
    b_sz, s_len, d = x.shape
    attn_w, kv_w = n_q * HEAD_DIM, n_kv * HEAD_DIM
    width = attn_w + 2 * kv_w
    assert attn_w % half == 0 and kv_w <= half and half % HEAD_DIM == 0
    gq = jnp.tile(q_g * (HEAD_DIM ** -0.5 * LOG2E), half // HEAD_DIM).reshape(1, half)
    gk = jnp.tile(k_g, n_kv).reshape(1, kv_w)
    tiles = s_len // ts
    const2 = lambda b, s: (0, 0)
    body = functools.partial(_qkv_body, n_q=n_q, n_kv=n_kv, half=half)
    return pl.pallas_call(
        body,
        grid=(b_sz, tiles),
        in_specs=[
            pl.BlockSpec((1, ts, d), lambda b, s: (b, s, 0)),
            pl.BlockSpec((1, d), const2),
            pl.BlockSpec((1, 1, d), lambda b, s: (b * N_MOD + scale_idx, 0, 0)),
            pl.BlockSpec((1, 1, d), lambda b, s: (b * N_MOD + shift_idx, 0, 0)),
            pl.BlockSpec((None, d, width), lambda b, s: (layer, 0, 0), pipeline_mode=pl.Buffered(1)),
            pl.BlockSpec((1, half), const2),
            pl.BlockSpec((1, kv_w), const2),
        ],
        out_specs=[
            pl.BlockSpec((ts, width), lambda b, s: (b * tiles + s, 0)),
            pl.BlockSpec((1, ts, d), lambda b, s: (b, s, 0)),
        ],
        out_shape=[
            jax.ShapeDtypeStruct((b_sz * s_len, width), BF16),
            jax.ShapeDtypeStruct((b_sz, s_len, d), BF16),
        ],
        scratch_shapes=[pltpu.VMEM((d, width), BF16), pltpu.VMEM((half, half), BF16)],
        compiler_params=_params(2),
        name="qkv_proj",
    )(x, norm_g.reshape(1, d), mod3, mod3, w_in, gq, gk)


def _cast_weights_once(w_refs, wbf_refs):
    @pl.when(pl.program_id(1) == 0)
    def _():
        for w_ref, wbf_ref in zip(w_refs, wbf_refs):
            wbf_ref[...] = w_ref[...].astype(BF16)


def _glu_proj_body(h_ref, wa_ref, wg_ref, *refs):
    n_jobs = (len(refs) - 3) // 2
    job_in, o_ref, job_out = refs[:n_jobs], refs[n_jobs], refs[n_jobs + 1:-2]
    wabf_ref, wgbf_ref = refs[-2:]
    _cast_weights_once([wa_ref, wg_ref], [wabf_ref, wgbf_ref])
    for src_ref, dst_ref in zip(job_in, job_out):
        dst_ref[...] = src_ref[...].astype(BF16)
    for r0 in range(0, h_ref.shape[0], _MATMUL_ROWS):
        rs = slice(r0, r0 + _MATMUL_ROWS)
        h = h_ref[rs, :]
        a = jnp.dot(h, wabf_ref[...], preferred_element_type=F32)
        g = jnp.dot(h, wgbf_ref[...], preferred_element_type=F32)
        o_ref[rs, :] = (a * _sigmoid(g)).astype(o_ref.dtype)


def _glu_proj(h, w, layer, a_col, g_col, n, tm, tn, cast_jobs=()):
    m, k = h.shape
    assert a_col % tn == 0 and g_col % tn == 0 and n % tn == 0
    a_blk0, g_blk0 = a_col // tn, g_col // tn
    m_tiles = m // tm
    steps = (n // tn) * m_tiles
    step = lambda j, i: j * m_tiles + i
    job_in, job_out, job_shapes = [], [], []
    for wj, row_block_of_step in cast_jobs:
        rows, cols = wj.shape[1:]
        assert rows % steps == 0
        slab = rows // steps
        job_in.append(pl.BlockSpec((None, slab, cols),
                                   lambda j, i, f=row_block_of_step: (layer, f(step(j, i)), 0)))
        job_out.append(pl.BlockSpec((slab, cols), lambda j, i: (step(j, i), 0)))
        job_shapes.append(jax.ShapeDtypeStruct((rows, cols), BF16))
    return pl.pallas_call(
        _glu_proj_body,
        grid=(n // tn, m_tiles),
        in_specs=[
            pl.BlockSpec((tm, k), lambda j, i: (i, 0)),
            pl.BlockSpec((None, k, tn), lambda j, i: (layer, 0, a_blk0 + j)),
            pl.BlockSpec((None, k, tn), lambda j, i: (layer, 0, g_blk0 + j)),
        ] + job_in,
        out_specs=[pl.BlockSpec((tm, tn), lambda j, i: (i, j))] + job_out,
        out_shape=[jax.ShapeDtypeStruct((m, n), BF16)] + job_shapes,
        scratch_shapes=[pltpu.VMEM((k, tn), BF16), pltpu.VMEM((k, tn), BF16)],
        compiler_params=_params(2),
        name="glu_proj",
    )(h, w, w, *[wj for wj, _ in cast_jobs])


_EPILOGUE_LANES = 512
_MATMUL_ROWS = 1024


def _merge_body(ya_ref, yc_ref, wa_ref, wc_ref, ga_ref, gc_ref, o_ref):
    ya = ya_ref[...]
    yc = yc_ref[...]
    for n0 in range(0, o_ref.shape[1], _EPILOGUE_LANES):
        ns = slice(n0, n0 + _EPILOGUE_LANES)
        acc_a = jnp.dot(ya, wa_ref[:, ns], preferred_element_type=F32)
        acc_c = jnp.dot(yc, wc_ref[:, ns], preferred_element_type=F32)
        ga = _sigmoid(ga_ref[:, ns].astype(F32))
        gc = _sigmoid(gc_ref[:, ns].astype(F32))
        o_ref[:, ns] = (ga * acc_a + gc * acc_c).astype(o_ref.dtype)


def _merge(attn, conv, wa_bf, wc_bf, gates, ga_col, gc_col, tm=512):
    m, ka = attn.shape
    kc = conv.shape[1]
    n = wa_bf.shape[1]
    assert ga_col % n == 0 and gc_col % n == 0 and n % _EPILOGUE_LANES == 0
    ga_blk, gc_blk = ga_col // n, gc_col // n
    resident = pl.Buffered(1)
    return pl.pallas_call(
        _merge_body,
        grid=(m // tm,),
        in_specs=[
            pl.BlockSpec((tm, ka), lambda i: (i, 0)),
            pl.BlockSpec((tm, kc), lambda i: (i, 0)),
            pl.BlockSpec((ka, n), lambda i: (0, 0), pipeline_mode=resident),
            pl.BlockSpec((kc, n), lambda i: (0, 0), pipeline_mode=resident),
            pl.BlockSpec((tm, n), lambda i: (i, ga_blk)),
            pl.BlockSpec((tm, n), lambda i: (i, gc_blk)),
        ],
        out_specs=pl.BlockSpec((tm, n), lambda i: (i, 0)),
        out_shape=jax.ShapeDtypeStruct((m, n), BF16),
        compiler_params=_params(1),
        name="merge",
    )(attn, conv, wa_bf, wc_bf, gates, gates)


def _mix_out_body(a_ref, w_ref, x_ref, gt_ref, g_ref, sc_ref, sh_ref, x1_ref, h_ref):
    a = a_ref[...]
    ts, d = a.shape
    sumsq = jnp.zeros((ts, 1), F32)
    for n0 in range(0, d, _EPILOGUE_LANES):
        ns = slice(n0, n0 + _EPILOGUE_LANES)
        acc = jnp.dot(a, w_ref[:, ns], preferred_element_type=F32)
        x1 = x_ref[0, :, ns] + gt_ref[0, :, ns] * acc
        x1_ref[0, :, ns] = x1
        sumsq = sumsq + jnp.sum(x1 * x1, axis=-1, keepdims=True)
    inv = lax.rsqrt(sumsq / d + EPS)
    gain = g_ref[...] * (1.0 + sc_ref[0])
    h_ref[0] = (x1_ref[0] * inv * gain + sh_ref[0]).astype(h_ref.dtype)


def _mix_out(a, w_bf, x, mod3, gate_idx, norm_g, scale_idx, shift_idx, ts=512):
    b_sz, s_len, d = x.shape
    k = a.shape[1]
    assert d % _EPILOGUE_LANES == 0
    tiles = s_len // ts
    const2 = lambda b, s: (0, 0)
    mod_row = lambda idx: pl.BlockSpec((1, 1, d), lambda b, s: (b * N_MOD + idx, 0, 0))
    act = pl.BlockSpec((1, ts, d), lambda b, s: (b, s, 0))
    return pl.pallas_call(
        _mix_out_body,
        grid=(b_sz, tiles),
        in_specs=[
            pl.BlockSpec((ts, k), lambda b, s: (b * tiles + s, 0)),
            pl.BlockSpec((k, d), const2, pipeline_mode=pl.Buffered(1)),
            act,
            mod_row(gate_idx),
            pl.BlockSpec((1, d), const2),
            mod_row(scale_idx),
            mod_row(shift_idx),
        ],
        out_specs=[act, act],
        out_shape=[jax.ShapeDtypeStruct((b_sz, s_len, d), F32), jax.ShapeDtypeStruct((b_sz, s_len, d), BF16)],
        compiler_params=_params(2),
        name="mix_out",
    )(a, w_bf, x, mod3, norm_g.reshape(1, d), mod3, mod3)


def _residual_proj_body(a_ref, w_ref, x_ref, gt_ref, o_ref):
    a = a_ref[...]
    for n0 in range(0, o_ref.shape[1], _EPILOGUE_LANES):
        ns = slice(n0, n0 + _EPILOGUE_LANES)
        acc = jnp.dot(a, w_ref[:, ns], preferred_element_type=F32)
        o_ref[:, ns] = x_ref[:, ns] + gt_ref[0, :, ns] * acc


def _residual_proj(a, w_bf, x2d, mod3, gate_idx, rows_per_batch, tm, tn):
    m, k = a.shape
    n = w_bf.shape[1]
    assert n % tn == 0 and tn % _EPILOGUE_LANES == 0 and rows_per_batch % tm == 0
    tiles_per_batch = rows_per_batch // tm
    return pl.pallas_call(
        _residual_proj_body,
        grid=(n // tn, m // tm),
        in_specs=[
            pl.BlockSpec((tm, k), lambda j, i: (i, 0)),
            pl.BlockSpec((k, tn), lambda j, i: (0, j)),
            pl.BlockSpec((tm, tn), lambda j, i: (i, j)),
            pl.BlockSpec((1, 1, tn), lambda j, i: ((i // tiles_per_batch) * N_MOD + gate_idx, 0, j)),
        ],
        out_specs=pl.BlockSpec((tm, tn), lambda j, i: (i, j)),
        out_shape=jax.ShapeDtypeStruct((m, n), F32),
        compiler_params=_params(2),
        name="residual_proj",
    )(a, w_bf, x2d, mod3)


def _ffn_in_body(a_ref, wg_ref, wu_ref, wo_ref, o_ref, wobf_ref, wgbf_ref, wubf_ref):
    _cast_weights_once([wg_ref, wu_ref], [wgbf_ref, wubf_ref])
    wobf_ref[...] = wo_ref[...].astype(BF16)
    for r0 in range(0, a_ref.shape[0], _MATMUL_ROWS):
        rs = slice(r0, r0 + _MATMUL_ROWS)
        a = a_ref[rs, :]
        gate = jnp.dot(a, wgbf_ref[...], preferred_element_type=F32)
        up = jnp.dot(a, wubf_ref[...], preferred_element_type=F32)
        o_ref[rs, :] = (gate * _sigmoid(gate) * up).astype(o_ref.dtype)


def _ffn_in(h, w_ffn_in, w_ffn_out, layer, tm=2048, tn=512):
    m, k = h.shape
    d_ff = w_ffn_in.shape[2] // 2
    d_out = w_ffn_out.shape[2]
    up_blk = d_ff // tn
    m_tiles = m // tm
    steps = (d_ff // tn) * m_tiles
    assert d_ff % steps == 0
    slab = d_ff // steps
    step = lambda j, i: j * m_tiles + i
    return pl.pallas_call(
        _ffn_in_body,
        grid=(d_ff // tn, m_tiles),
        in_specs=[
            pl.BlockSpec((tm, k), lambda j, i: (i, 0)),
            pl.BlockSpec((None, k, tn), lambda j, i: (layer, 0, j)),
            pl.BlockSpec((None, k, tn), lambda j, i: (layer, 0, up_blk + j)),
            pl.BlockSpec((None, slab, d_out), lambda j, i: (layer, step(j, i), 0)),
        ],
        out_specs=[
            pl.BlockSpec((tm, tn), lambda j, i: (i, j)),
            pl.BlockSpec((slab, d_out), lambda j, i: (step(j, i), 0)),
        ],
        out_shape=[jax.ShapeDtypeStruct((m, d_ff), BF16), jax.ShapeDtypeStruct((d_ff, d_out), BF16)],
        scratch_shapes=[pltpu.VMEM((k, tn), BF16), pltpu.VMEM((k, tn), BF16)],
        compiler_params=_params(2),
        name="ffn_in",
    )(h, w_ffn_in, w_ffn_in, w_ffn_out)


def _t5_causal_bucket(dist):
    n = jnp.maximum(dist, 0)
    nf = jnp.maximum(n, 1).astype(jnp.float32)
    large = MAX_EXACT + (jnp.log(nf / MAX_EXACT) / math.log(MAX_DISTANCE / MAX_EXACT)
                         * (NUM_BUCKETS - MAX_EXACT)).astype(jnp.int32)
    large = jnp.minimum(large, NUM_BUCKETS - 1)
    return jnp.where(n < MAX_EXACT, n, large)


def _band_buckets():
    q_off = jnp.arange(BLOCK)
    k_off = jnp.arange(2 * BLOCK)
    dist = q_off[:, None] + BLOCK - k_off[None, :]
    allowed = (dist >= 0) & (dist < WINDOW)
    return jnp.where(allowed, _t5_causal_bucket(dist), -1).astype(jnp.int32)


def _attn_build_bias(first_step, rb_ref, sink_ref, bucket_ref, bias_ref, *, n_q, n_kv):
    q_per_kv = n_q // n_kv

    @pl.when(first_step)
    def _():
        bucket = bucket_ref[...]
        col = lax.broadcasted_iota(jnp.int32, bucket.shape, 1)
        for slot in range(n_q):
            head = _orig_head(slot, n_kv, q_per_kv)
            tile = jnp.zeros(bucket.shape, F32)
            for b in range(NUM_BUCKETS):
                tile = jnp.where(bucket == b, rb_ref[b * n_q + head] * LOG2E, tile)
            tile = jnp.where(bucket < 0, -jnp.inf, tile)
            sink = sink_ref[head] * LOG2E
            bias_ref[0, slot] = jnp.where(col == 0, sink, tile)
            bias_ref[1, slot] = jnp.where(col == 0, sink, jnp.where(col < BLOCK, -jnp.inf, tile))


def _attn_block(seq_start, qmask_ref, krow_ref, q_ref, kc_ref, kp_ref, vc_ref, vp_ref, o_ref,
                bias_ref, lg_ref, p_ref, pv_ref, rs_ref, *, n_q, n_kv):
    q_per_kv = n_q // n_kv
    kv_w = n_kv * HEAD_DIM
    lane_slot = lax.broadcasted_iota(jnp.int32, (BLOCK, BLOCK), 1) // HEAD_DIM
    ones = jnp.ones((2 * BLOCK, BLOCK), BF16)
    bias_idx = jnp.where(seq_start, 1, 0)

    keys = jnp.concatenate([kp_ref[...], kc_ref[...]], axis=0) * krow_ref[...]
    vals = jnp.concatenate([vp_ref[...], vc_ref[...]], axis=0) * krow_ref[...]
    lhs = jnp.concatenate(
        [q_ref[:, j * kv_w:(j + 1) * kv_w] * qmask_ref[g]
         for j in range(q_per_kv) for g in range(n_kv)], axis=0)
    lg_ref[...] = lax.dot_general(lhs, keys, (((1,), (1,)), ((), ())),
                                  preferred_element_type=F32).reshape(n_q, BLOCK, 2 * BLOCK)
    for slot in range(n_q):
        logit = lg_ref[slot] + bias_ref[bias_idx, slot]
        m = jnp.max(logit, axis=-1, keepdims=True)
        p_ref[slot * BLOCK:(slot + 1) * BLOCK, :] = jnp.exp2(logit - m).astype(BF16)
    p = p_ref[...]
    pv_ref[...] = jnp.dot(p, vals, preferred_element_type=F32).reshape(n_q, BLOCK, kv_w)
    rs_ref[...] = jnp.dot(p, ones, preferred_element_type=F32).reshape(n_q, BLOCK, BLOCK)
    per_half = BLOCK // HEAD_DIM
    for j in range(q_per_kv):
        for half in range(kv_w // BLOCK):
            lanes = slice(half * BLOCK, (half + 1) * BLOCK)
            slots = [j * n_kv + half * per_half + i for i in range(per_half)]
            num = pv_ref[slots[-1], :, lanes]
            den = rs_ref[slots[-1]]
            for i in range(per_half - 2, -1, -1):
                num = jnp.where(lane_slot == i, pv_ref[slots[i], :, lanes], num)
                den = jnp.where(lane_slot == i, rs_ref[slots[i]], den)
            o_ref[:, j * kv_w + half * BLOCK:j * kv_w + (half + 1) * BLOCK] = (
                num * (1.0 / den)).astype(o_ref.dtype)


_CONV_HALO = 32
_CONV_LANES = 256
_SUBLANES = 8


def _conv_zero_history(seq_start, u_ref):
    @pl.when(seq_start)
    def _():
        u_ref[:_CONV_HALO, :] = jnp.zeros((_CONV_HALO, u_ref.shape[1]), u_ref.dtype)


def _conv_shift_matrix(rows):
    i = jnp.arange(_SUBLANES * rows) % rows
    r = jnp.arange(_SUBLANES * rows) // rows
    return (jnp.arange(rows)[None, :] == (i + r)[:, None]).astype(BF16)


def _conv_block(uin_ref, shift_ref, w_ref, cb_ref, lg_ref, lb_ref, o_ref, u_ref, sh_ref, y_ref):
    width = w_ref.shape[0]
    ts, channels = o_ref.shape
    u_ref[_CONV_HALO:, :] = uin_ref[...]
    sh_ref[...] = jnp.dot(shift_ref[...], u_ref[...], preferred_element_type=F32).reshape(sh_ref.shape)
    first = _CONV_HALO - (width - 1)
    unknown_zero = jnp.minimum(pl.program_id(0), 0)
    for c0 in range(0, channels, _CONV_LANES):
        cs = slice(c0, c0 + _CONV_LANES)
        acc = jnp.broadcast_to(cb_ref[:, cs], (ts, _CONV_LANES))
        for j in range(width):
            tile, r = divmod(first + j, _SUBLANES)
            rows = pl.ds(pl.multiple_of(tile * _SUBLANES + unknown_zero, _SUBLANES), ts)
            acc = acc + sh_ref[r, rows, cs] * w_ref[j:j + 1, cs]
        y_ref[:, cs] = acc
    y = y_ref[...]
    mu = jnp.mean(y, axis=-1, keepdims=True)
    yc = y - mu
    var = jnp.mean(yc * yc, axis=-1, keepdims=True)
    z = yc * lax.rsqrt(var + EPS) * lg_ref[...] + lb_ref[...]
    o_ref[...] = (z * _sigmoid(z)).astype(o_ref.dtype)
    u_ref[:_CONV_HALO, :] = u_ref[ts:, :]


def _core_body(h_ref, w_ref,
               uin_ref, shift_ref, cw_ref, cb_ref, lg_ref, lb_ref,
               rb_ref, sink_ref, bucket_ref, qmask_ref, krow_ref, q_ref, kc_ref, kp_ref, vc_ref, vp_ref,
               gates_ref, yconv_ref, yattn_ref,
               wbf_ref, u_ref, sh_ref, y_ref, bias_ref, lgt_ref, p_ref, pv_ref, rs_ref,
               *, n_q, n_kv, blocks_per_seq):
    t = pl.program_id(0) * pl.num_programs(1) + pl.program_id(1)
    seq_start = t % blocks_per_seq == 0
    _cast_weights_once([w_ref], [wbf_ref])
    _conv_zero_history(seq_start, u_ref)
    _attn_build_bias(t == 0, rb_ref, sink_ref, bucket_ref, bias_ref, n_q=n_q, n_kv=n_kv)

    _conv_block(uin_ref, shift_ref, cw_ref, cb_ref, lg_ref, lb_ref, yconv_ref, u_ref, sh_ref, y_ref)
    acc = jnp.dot(h_ref[...], wbf_ref[...], preferred_element_type=F32)
    gates_ref[...] = acc.astype(gates_ref.dtype)
    _attn_block(seq_start, qmask_ref, krow_ref, q_ref, kc_ref, kp_ref, vc_ref, vp_ref, yattn_ref,
                bias_ref, lgt_ref, p_ref, pv_ref, rs_ref, n_q=n_q, n_kv=n_kv)


def _mixer_core(h, w_in, layer, gate_col, conv_in, qkv, conv_w, conv_b, ln_g, ln_b, rel_bias, sinks,
                s_len, n_q, n_kv, tn=512):
    m, k = h.shape
    n = w_in.shape[2] - gate_col
    width, channels = conv_w.shape
    attn_w, kv_w = n_q * HEAD_DIM, n_kv * HEAD_DIM
    n_blocks = m // BLOCK
    n_tiles = n // tn
    assert gate_col % tn == 0 and n % tn == 0 and n_blocks % n_tiles == 0
    assert s_len % BLOCK == 0 and width - 1 <= _CONV_HALO and _CONV_HALO % _SUBLANES == 0
    assert attn_w % kv_w == 0 and kv_w == 2 * BLOCK and BLOCK % HEAD_DIM == 0
    m_tiles = n_blocks // n_tiles
    tm = m // m_tiles
    col_blk0 = gate_col // tn
    k_blk = attn_w // kv_w
    v_blk = k_blk + 1

    step = lambda j, i: j * m_tiles + i
    prev_step = lambda j, i: jnp.maximum(step(j, i) - 1, 0)
    const2 = lambda j, i: (0, 0)
    row = lambda v: v.reshape(1, channels)
    lane_group = jnp.arange(kv_w) // HEAD_DIM
    qmask = jnp.broadcast_to((lane_group[None, :] == jnp.arange(n_kv)[:, None])[:, None, :],
                             (n_kv, BLOCK, kv_w)).astype(BF16)
    krow = jnp.broadcast_to((jnp.arange(2 * BLOCK) > 0)[:, None], (2 * BLOCK, kv_w)).astype(BF16)
    body = functools.partial(_core_body, n_q=n_q, n_kv=n_kv, blocks_per_seq=s_len // BLOCK)
    return pl.pallas_call(
        body,
        grid=(n_tiles, m_tiles),
        in_specs=[
            pl.BlockSpec((tm, k), lambda j, i: (i, 0)),
            pl.BlockSpec((None, k, tn), lambda j, i: (layer, 0, col_blk0 + j)),
            pl.BlockSpec((BLOCK, channels), lambda j, i: (step(j, i), 0)),
            pl.BlockSpec((_SUBLANES * (_CONV_HALO + BLOCK), _CONV_HALO + BLOCK), const2),
            pl.BlockSpec((width, channels), const2),
            pl.BlockSpec((1, channels), const2),
            pl.BlockSpec((1, channels), const2),
            pl.BlockSpec((1, channels), const2),
            pl.BlockSpec(memory_space=pltpu.SMEM),
            pl.BlockSpec(memory_space=pltpu.SMEM),
            pl.BlockSpec((BLOCK, 2 * BLOCK), const2),
            pl.BlockSpec((n_kv, BLOCK, kv_w), lambda j, i: (0, 0, 0)),
            pl.BlockSpec((2 * BLOCK, kv_w), const2),
            pl.BlockSpec((BLOCK, attn_w), lambda j, i: (step(j, i), 0)),
            pl.BlockSpec((BLOCK, kv_w), lambda j, i: (step(j, i), k_blk)),
            pl.BlockSpec((BLOCK, kv_w), lambda j, i: (prev_step(j, i), k_blk)),
            pl.BlockSpec((BLOCK, kv_w), lambda j, i: (step(j, i), v_blk)),
            pl.BlockSpec((BLOCK, kv_w), lambda j, i: (prev_step(j, i), v_blk)),
        ],
        out_specs=[
            pl.BlockSpec((tm, tn), lambda j, i: (i, j)),
            pl.BlockSpec((BLOCK, channels), lambda j, i: (step(j, i), 0)),
            pl.BlockSpec((BLOCK, attn_w), lambda j, i: (step(j, i), 0)),
        ],
        out_shape=[
            jax.ShapeDtypeStruct((m, n), BF16),
            jax.ShapeDtypeStruct((m, channels), BF16),
            jax.ShapeDtypeStruct((m, attn_w), BF16),
        ],
        scratch_shapes=[
            pltpu.VMEM((k, tn), BF16),
            pltpu.VMEM((_CONV_HALO + BLOCK, channels), BF16),
            pltpu.VMEM((_SUBLANES, _CONV_HALO + BLOCK, channels), F32),
            pltpu.VMEM((BLOCK, channels), F32),
            pltpu.VMEM((2, n_q, BLOCK, 2 * BLOCK), F32),
            pltpu.VMEM((n_q, BLOCK, 2 * BLOCK), F32),
            pltpu.VMEM((n_q * BLOCK, 2 * BLOCK), BF16),
            pltpu.VMEM((n_q, BLOCK, kv_w), F32),
            pltpu.VMEM((n_q, BLOCK, BLOCK), F32),
        ],
        compiler_params=_params(2),
        name="mixer_core",
    )(h, w_in,
      conv_in, _conv_shift_matrix(_CONV_HALO + BLOCK), conv_w, row(conv_b), row(ln_g), row(ln_b),
      rel_bias.astype(F32).reshape(-1), sinks.astype(F32), _band_buckets(), qmask, krow,
      qkv, qkv, qkv, qkv, qkv)


def kernel(x, c, w_ada, b_ada, norm_mix_g, w_in, q_norm_g, k_norm_g, attn_sinks, rel_bias,
           w_attn_out, conv_w, conv_b, conv_ln_g, conv_ln_b, w_conv_out, w_mix_out,
           norm_ffn_g, w_ffn_in, w_ffn_out):
    b_sz, s_len, d = x.shape
    depth = w_ada.shape[0]
    n_q = attn_sinks.shape[1]
    attn_w = w_attn_out.shape[1]
    channels = conv_w.shape[2]
    kv_w = (w_in.shape[2] - attn_w - 2 * channels - 2 * d) // 2
    n_kv = kv_w // HEAD_DIM
    conv_col = attn_w + 2 * kv_w
    gate_col = conv_col + 2 * channels
    m = b_sz * s_len

    for l in range(depth):
        mod = _ada(c, w_ada, b_ada[l], l)
        mod3 = mod.reshape(b_sz * N_MOD, 1, d)

        qkv, h = _qkv_proj(x, norm_mix_g[l], mod3, 1, 0, w_in, l, q_norm_g[l], k_norm_g[l], n_q, n_kv)
        h = h.reshape(m, d)
        proj_tm, proj_tn = 2048, 256
        assert (channels // proj_tn) * (m // proj_tm) == n_q
        slot_head = lambda slot: _orig_head(slot, n_kv, n_q // n_kv)
        same = lambda step: step
        conv_in, wa_bf, wc_bf, wmix_bf = _glu_proj(
            h, w_in, l, conv_col, conv_col + channels, channels, proj_tm, proj_tn,
            cast_jobs=((w_attn_out, slot_head), (w_conv_out, same), (w_mix_out, same)))
        gates, y_conv, y_attn = _mixer_core(h, w_in, l, gate_col, conv_in, qkv, conv_w[l], conv_b[l],
                                            conv_ln_g[l], conv_ln_b[l], rel_bias, attn_sinks[l],
                                            s_len, n_q, n_kv)
        merged = _merge(y_attn, y_conv, wa_bf, wc_bf, gates, 0, d)
        x1, h = _mix_out(merged, wmix_bf, x, mod3, 2, norm_ffn_g[l], 4, 3)

        act, wo_bf = _ffn_in(h.reshape(m, d), w_ffn_in, w_ffn_out, l)
        x = _residual_proj(act, wo_bf, x1.reshape(m, d), mod3, 5, s_len,
                           tm=512, tn=1024).reshape(b_sz, s_len, d)
    return x
```

```python
import functools
import math

import jax
import jax.numpy as jnp
from jax import lax
from jax.experimental import pallas as pl
from jax.experimental.pallas import tpu as pltpu

F32 = jnp.float32
BF16 = jnp.bfloat16

HEAD_DIM = 64
WINDOW = 128
BLOCK = 128
NUM_BUCKETS = 32
MAX_EXACT = NUM_BUCKETS // 2
MAX_DISTANCE = 128
N_MOD = 6
EPS = 1e-6
LOG2E = math.log2(math.e)

V7X_VMEM_LIMIT_BYTES = 56 * 1024 * 1024


def _params(n_axes):
    return pltpu.CompilerParams(
        dimension_semantics=("arbitrary",) * n_axes,
        vmem_limit_bytes=V7X_VMEM_LIMIT_BYTES,
    )


def _sigmoid(x):
    return jax.nn.sigmoid(x)


def _orig_head(slot, n_kv, q_per_kv):
    return (slot % n_kv) * q_per_kv + slot // n_kv


def _ada_body(c_ref, w_ref, b_ref, o_ref):
    c = c_ref[...]
    act = (c * _sigmoid(c)).astype(BF16)
    o_ref[...] = jnp.dot(act, w_ref[...].astype(BF16), preferred_element_type=F32) + b_ref[...]


def _ada(c, w_ada, b_ada, layer, tn=1024):
    b_sz, d = c.shape
    n = w_ada.shape[2]
    rows = 8
    c_pad = jnp.pad(c, ((0, rows - b_sz), (0, 0)))
    out = pl.pallas_call(
        _ada_body,
        grid=(n // tn,),
        in_specs=[
            pl.BlockSpec((rows, d), lambda j: (0, 0)),
            pl.BlockSpec((None, d, tn), lambda j: (layer, 0, j)),
            pl.BlockSpec((1, tn), lambda j: (0, j)),
        ],
        out_specs=pl.BlockSpec((rows, tn), lambda j: (0, j)),
        out_shape=jax.ShapeDtypeStruct((rows, n), F32),
        compiler_params=_params(1),
        name="ada_mod",
    )(c_pad, w_ada, b_ada.reshape(1, n))
    return out[:b_sz]


def _norm_mod_body(x_ref, g_ref, sc_ref, sh_ref, o_ref):
    x = x_ref[0]
    ms = jnp.mean(x * x, axis=-1, keepdims=True)
    gain = g_ref[...] * (1.0 + sc_ref[0])
    o_ref[0] = (x * lax.rsqrt(ms + EPS) * gain + sh_ref[0]).astype(o_ref.dtype)


def _qkv_body(x_ref, g_ref, sc_ref, sh_ref, w_ref, gq_ref, gk_ref, o_ref, h_ref, wbf_ref, seg_ref,
              *, n_q, n_kv, half):
    attn_w = n_q * HEAD_DIM
    kv_w = n_kv * HEAD_DIM

    @pl.when(jnp.logical_and(pl.program_id(0) == 0, pl.program_id(1) == 0))
    def _():
        for slot in range(n_q):
            src = _orig_head(slot, n_kv, n_q // n_kv) * HEAD_DIM
            wbf_ref[:, slot * HEAD_DIM:(slot + 1) * HEAD_DIM] = w_ref[:, src:src + HEAD_DIM].astype(BF16)
        wbf_ref[:, attn_w:] = w_ref[:, attn_w:].astype(BF16)
        r = lax.broadcasted_iota(jnp.int32, (half, half), 0) // HEAD_DIM
        c = lax.broadcasted_iota(jnp.int32, (half, half), 1) // HEAD_DIM
        seg_ref[...] = (r == c).astype(BF16)

    def head_norm(acc, seg, gain):
        ss = jnp.dot((acc * acc).astype(BF16), seg, preferred_element_type=F32)
        return acc * lax.rsqrt(ss / HEAD_DIM + EPS) * gain

    rows = x_ref.shape[1] // 2
    for r0 in (0, rows):
        rs = slice(r0, r0 + rows)
        x = x_ref[0, rs, :]
        ms = jnp.mean(x * x, axis=-1, keepdims=True)
        gain = g_ref[...] * (1.0 + sc_ref[0])
        a = (x * lax.rsqrt(ms + EPS) * gain + sh_ref[0]).astype(BF16)
        h_ref[0, rs, :] = a
        for c0 in range(0, attn_w, half):
            acc = jnp.dot(a, wbf_ref[:, c0:c0 + half], preferred_element_type=F32)
            o_ref[rs, c0:c0 + half] = head_norm(acc, seg_ref[...], gq_ref[...]).astype(o_ref.dtype)
        acc = jnp.dot(a, wbf_ref[:, attn_w:], preferred_element_type=F32)
        o_ref[rs, attn_w:attn_w + kv_w] = head_norm(acc[:, :kv_w], seg_ref[:kv_w, :kv_w],
                                                    gk_ref[...]).astype(o_ref.dtype)
        o_ref[rs, attn_w + kv_w:] = acc[:, kv_w:].astype(o_ref.dtype)


def _qkv_proj(x, norm_g, mod3, scale_idx, shift_idx, w_in, layer, q_g, k_g, n_q, n_kv, ts=512, half=512):
    b_sz, s_len, d = x.shape
    attn_w, kv_w = n_q * HEAD_DIM, n_kv * HEAD_DIM
    width = attn_w + 2 * kv_w
    assert attn_w % half == 0 and kv_w <= half and half % HEAD_DIM == 0
    gq = jnp.tile(q_g * (HEAD_DIM ** -0.5 * LOG2E), half // HEAD_DIM).reshape(1, half)
    gk = jnp.tile(k_g, n_kv).reshape(1, kv_w)
    tiles = s_len // ts
    const2 = lambda b, s: (0, 0)
    body = functools.partial(_qkv_body, n_q=n_q, n_kv=n_kv, half=half)
    return pl.pallas_call(
        body,
        grid=(b_sz, tiles),
        in_specs=[
            pl.BlockSpec((1, ts, d), lambda b, s: (b, s, 0)),
            pl.BlockSpec((1, d), const2),
            pl.BlockSpec((1, 1, d), lambda b, s: (b * N_MOD + scale_idx, 0, 0)),
            pl.BlockSpec((1, 1, d), lambda b, s: (b * N_MOD + shift_idx, 0, 0)),
            pl.BlockSpec((None, d, width), lambda b, s: (layer, 0, 0), pipeline_mode=pl.Buffered(1)),
            pl.BlockSpec((1, half), const2),
            pl.BlockSpec((1, kv_w), const2),
        ],
        out_specs=[
            pl.BlockSpec((ts, width), lambda b, s: (b * tiles + s, 0)),
            pl.BlockSpec((1, ts, d), lambda b, s: (b, s, 0)),
        ],
        out_shape=[
            jax.ShapeDtypeStruct((b_sz * s_len, width), BF16),
            jax.ShapeDtypeStruct((b_sz, s_len, d), BF16),
        ],
        scratch_shapes=[pltpu.VMEM((d, width), BF16), pltpu.VMEM((half, half), BF16)],
        compiler_params=_params(2),
        name="qkv_proj",
    )(x, norm_g.reshape(1, d), mod3, mod3, w_in, gq, gk)


def _cast_weights_once(w_refs, wbf_refs):
    @pl.when(pl.program_id(1) == 0)
    def _():
        for w_ref, wbf_ref in zip(w_refs, wbf_refs):
            wbf_ref[...] = w_ref[...].astype(BF16)


def _glu_proj_body(h_ref, wa_ref, wg_ref, *refs):
    n_jobs = (len(refs) - 3) // 2
    job_in, o_ref, job_out = refs[:n_jobs], refs[n_jobs], refs[n_jobs + 1:-2]
    wabf_ref, wgbf_ref = refs[-2:]
    _cast_weights_once([wa_ref, wg_ref], [wabf_ref, wgbf_ref])
    for src_ref, dst_ref in zip(job_in, job_out):
        dst_ref[...] = src_ref[...].astype(BF16)
    for r0 in range(0, h_ref.shape[0], _MATMUL_ROWS):
        rs = slice(r0, r0 + _MATMUL_ROWS)
        h = h_ref[rs, :]
        a = jnp.dot(h, wabf_ref[...], preferred_element_type=F32)
        g = jnp.dot(h, wgbf_ref[...], preferred_element_type=F32)
        o_ref[rs, :] = (a * _sigmoid(g)).astype(o_ref.dtype)


def _glu_proj(h, w, layer, a_col, g_col, n, tm, tn, cast_jobs=()):
    m, k = h.shape
    assert a_col % tn == 0 and g_col % tn == 0 and n % tn == 0
    a_blk0, g_blk0 = a_col // tn, g_col // tn
    m_tiles = m // tm
    steps = (n // tn) * m_tiles
    step = lambda j, i: j * m_tiles + i
    job_in, job_out, job_shapes = [], [], []
    for wj, row_block_of_step in cast_jobs:
        rows, cols = wj.shape[1:]
        assert rows % steps == 0
        slab = rows // steps
        job_in.append(pl.BlockSpec((None, slab, cols),
                                   lambda j, i, f=row_block_of_step: (layer, f(step(j, i)), 0)))
        job_out.append(pl.BlockSpec((slab, cols), lambda j, i: (step(j, i), 0)))
        job_shapes.append(jax.ShapeDtypeStruct((rows, cols), BF16))
    return pl.pallas_call(
        _glu_proj_body,
        grid=(n // tn, m_tiles),
        in_specs=[
            pl.BlockSpec((tm, k), lambda j, i: (i, 0)),
            pl.BlockSpec((None, k, tn), lambda j, i: (layer, 0, a_blk0 + j)),
            pl.BlockSpec((None, k, tn), lambda j, i: (layer, 0, g_blk0 + j)),
        ] + job_in,
        out_specs=[pl.BlockSpec((tm, tn), lambda j, i: (i, j))] + job_out,
        out_shape=[jax.ShapeDtypeStruct((m, n), BF16)] + job_shapes,
        scratch_shapes=[pltpu.VMEM((k, tn), BF16), pltpu.VMEM((k, tn), BF16)],
        compiler_params=_params(2),
        name="glu_proj",
    )(h, w, w, *[wj for wj, _ in cast_jobs])


_EPILOGUE_LANES = 512
_MATMUL_ROWS = 1024


def _merge_body(ya_ref, yc_ref, wa_ref, wc_ref, ga_ref, gc_ref, o_ref):
    ya = ya_ref[...]
    yc = yc_ref[...]
    for n0 in range(0, o_ref.shape[1], _EPILOGUE_LANES):
        ns = slice(n0, n0 + _EPILOGUE_LANES)
        acc_a = jnp.dot(ya, wa_ref[:, ns], preferred_element_type=F32)
        acc_c = jnp.dot(yc, wc_ref[:, ns], preferred_element_type=F32)
        ga = _sigmoid(ga_ref[:, ns].astype(F32))
        gc = _sigmoid(gc_ref[:, ns].astype(F32))
        o_ref[:, ns] = (ga * acc_a + gc * acc_c).astype(o_ref.dtype)


def _merge(attn, conv, wa_bf, wc_bf, gates, ga_col, gc_col, tm=512):
    m, ka = attn.shape
    kc = conv.shape[1]
    n = wa_bf.shape[1]
    assert ga_col % n == 0 and gc_col % n == 0 and n % _EPILOGUE_LANES == 0
    ga_blk, gc_blk = ga_col // n, gc_col // n
    resident = pl.Buffered(1)
    return pl.pallas_call(
        _merge_body,
        grid=(m // tm,),
        in_specs=[
            pl.BlockSpec((tm, ka), lambda i: (i, 0)),
            pl.BlockSpec((tm, kc), lambda i: (i, 0)),
            pl.BlockSpec((ka, n), lambda i: (0, 0), pipeline_mode=resident),
            pl.BlockSpec((kc, n), lambda i: (0, 0), pipeline_mode=resident),
            pl.BlockSpec((tm, n), lambda i: (i, ga_blk)),
            pl.BlockSpec((tm, n), lambda i: (i, gc_blk)),
        ],
        out_specs=pl.BlockSpec((tm, n), lambda i: (i, 0)),
        out_shape=jax.ShapeDtypeStruct((m, n), BF16),
        compiler_params=_params(1),
        name="merge",
    )(attn, conv, wa_bf, wc_bf, gates, gates)


def _mix_out_body(a_ref, w_ref, x_ref, gt_ref, g_ref, sc_ref, sh_ref, x1_ref, h_ref):
    a = a_ref[...]
    ts, d = a.shape
    sumsq = jnp.zeros((ts, 1), F32)
    for n0 in range(0, d, _EPILOGUE_LANES):
        ns = slice(n0, n0 + _EPILOGUE_LANES)
        acc = jnp.dot(a, w_ref[:, ns], preferred_element_type=F32)
        x1 = x_ref[0, :, ns] + gt_ref[0, :, ns] * acc
        x1_ref[0, :, ns] = x1
        sumsq = sumsq + jnp.sum(x1 * x1, axis=-1, keepdims=True)
    inv = lax.rsqrt(sumsq / d + EPS)
    gain = g_ref[...] * (1.0 + sc_ref[0])
    h_ref[0] = (x1_ref[0] * inv * gain + sh_ref[0]).astype(h_ref.dtype)


def _mix_out(a, w_bf, x, mod3, gate_idx, norm_g, scale_idx, shift_idx, ts=512):
    b_sz, s_len, d = x.shape
    k = a.shape[1]
    assert d % _EPILOGUE_LANES == 0
    tiles = s_len // ts
    const2 = lambda b, s: (0, 0)
    mod_row = lambda idx: pl.BlockSpec((1, 1, d), lambda b, s: (b * N_MOD + idx, 0, 0))
    act = pl.BlockSpec((1, ts, d), lambda b, s: (b, s, 0))
    return pl.pallas_call(
        _mix_out_body,
        grid=(b_sz, tiles),
        in_specs=[
            pl.BlockSpec((ts, k), lambda b, s: (b * tiles + s, 0)),
            pl.BlockSpec((k, d), const2, pipeline_mode=pl.Buffered(1)),
            act,
            mod_row(gate_idx),
            pl.BlockSpec((1, d), const2),
            mod_row(scale_idx),
            mod_row(shift_idx),
        ],
        out_specs=[act, act],
        out_shape=[jax.ShapeDtypeStruct((b_sz, s_len, d), F32), jax.ShapeDtypeStruct((b_sz, s_len, d), BF16)],
        compiler_params=_params(2),
        name="mix_out",
    )(a, w_bf, x, mod3, norm_g.reshape(1, d), mod3, mod3)


def _residual_proj_body(a_ref, w_ref, x_ref, gt_ref, o_ref):
    a = a_ref[...]
    for n0 in range(0, o_ref.shape[1], _EPILOGUE_LANES):
        ns = slice(n0, n0 + _EPILOGUE_LANES)
        acc = jnp.dot(a, w_ref[:, ns], preferred_element_type=F32)
        o_ref[:, ns] = x_ref[:, ns] + gt_ref[0, :, ns] * acc


def _residual_proj(a, w_bf, x2d, mod3, gate_idx, rows_per_batch, tm, tn):
    m, k = a.shape
    n = w_bf.shape[1]
    assert n % tn == 0 and tn % _EPILOGUE_LANES == 0 and rows_per_batch % tm == 0
    tiles_per_batch = rows_per_batch // tm
    return pl.pallas_call(
        _residual_proj_body,
        grid=(n // tn, m // tm),
        in_specs=[
            pl.BlockSpec((tm, k), lambda j, i: (i, 0)),
            pl.BlockSpec((k, tn), lambda j, i: (0, j)),
            pl.BlockSpec((tm, tn), lambda j, i: (i, j)),
            pl.BlockSpec((1, 1, tn), lambda j, i: ((i // tiles_per_batch) * N_MOD + gate_idx, 0, j)),
        ],
        out_specs=pl.BlockSpec((tm, tn), lambda j, i: (i, j)),
        out_shape=jax.ShapeDtypeStruct((m, n), F32),
        compiler_params=_params(2),
        name="residual_proj",
    )(a, w_bf, x2d, mod3)


def _ffn_in_body(a_ref, wg_ref, wu_ref, wo_ref, o_ref, wobf_ref, wgbf_ref, wubf_ref):
    _cast_weights_once([wg_ref, wu_ref], [wgbf_ref, wubf_ref])
    wobf_ref[...] = wo_ref[...].astype(BF16)
    for r0 in range(0, a_ref.shape[0], _MATMUL_ROWS):
        rs = slice(r0, r0 + _MATMUL_ROWS)
        a = a_ref[rs, :]
        gate = jnp.dot(a, wgbf_ref[...], preferred_element_type=F32)
        up = jnp.dot(a, wubf_ref[...], preferred_element_type=F32)
        o_ref[rs, :] = (gate * _sigmoid(gate) * up).astype(o_ref.dtype)


def _ffn_in(h, w_ffn_in, w_ffn_out, layer, tm=2048, tn=512):
    m, k = h.shape
    d_ff = w_ffn_in.shape[2] // 2
    d_out = w_ffn_out.shape[2]
    up_blk = d_ff // tn
    m_tiles = m // tm
    steps = (d_ff // tn) * m_tiles
    assert d_ff % steps == 0
    slab = d_ff // steps
    step = lambda j, i: j * m_tiles + i
    return pl.pallas_call(
        _ffn_in_body,
        grid=(d_ff // tn, m_tiles),
        in_specs=[
            pl.BlockSpec((tm, k), lambda j, i: (i, 0)),
            pl.BlockSpec((None, k, tn), lambda j, i: (layer, 0, j)),
            pl.BlockSpec((None, k, tn), lambda j, i: (layer, 0, up_blk + j)),
            pl.BlockSpec((None, slab, d_out), lambda j, i: (layer, step(j, i), 0)),
        ],
        out_specs=[
            pl.BlockSpec((tm, tn), lambda j, i: (i, j)),
            pl.BlockSpec((slab, d_out), lambda j, i: (step(j, i), 0)),
        ],
        out_shape=[jax.ShapeDtypeStruct((m, d_ff), BF16), jax.ShapeDtypeStruct((d_ff, d_out), BF16)],
        scratch_shapes=[pltpu.VMEM((k, tn), BF16), pltpu.VMEM((k, tn), BF16)],
        compiler_params=_params(2),
        name="ffn_in",
    )(h, w_ffn_in, w_ffn_in, w_ffn_out)


def _t5_causal_bucket(dist):
    n = jnp.maximum(dist, 0)
    nf = jnp.maximum(n, 1).astype(jnp.float32)
    large = MAX_EXACT + (jnp.log(nf / MAX_EXACT) / math.log(MAX_DISTANCE / MAX_EXACT)
                         * (NUM_BUCKETS - MAX_EXACT)).astype(jnp.int32)
    large = jnp.minimum(large, NUM_BUCKETS - 1)
    return jnp.where(n < MAX_EXACT, n, large)


def _band_buckets():
    q_off = jnp.arange(BLOCK)
    k_off = jnp.arange(2 * BLOCK)
    dist = q_off[:, None] + BLOCK - k_off[None, :]
    allowed = (dist >= 0) & (dist < WINDOW)
    return jnp.where(allowed, _t5_causal_bucket(dist), -1).astype(jnp.int32)


def _attn_build_bias(first_step, rb_ref, sink_ref, bucket_ref, bias_ref, *, n_q, n_kv):
    q_per_kv = n_q // n_kv

    @pl.when(first_step)
    def _():
        bucket = bucket_ref[...]
        col = lax.broadcasted_iota(jnp.int32, bucket.shape, 1)
        for slot in range(n_q):
            head = _orig_head(slot, n_kv, q_per_kv)
            tile = jnp.zeros(bucket.shape, F32)
            for b in range(NUM_BUCKETS):
                tile = jnp.where(bucket == b, rb_ref[b * n_q + head] * LOG2E, tile)
            tile = jnp.where(bucket < 0, -jnp.inf, tile)
            sink = sink_ref[head] * LOG2E
            bias_ref[0, slot] = jnp.where(col == 0, sink, tile)
            bias_ref[1, slot] = jnp.where(col == 0, sink, jnp.where(col < BLOCK, -jnp.inf, tile))


def _attn_block(seq_start, qmask_ref, krow_ref, q_ref, kc_ref, kp_ref, vc_ref, vp_ref, o_ref,
                bias_ref, lg_ref, p_ref, pv_ref, rs_ref, *, n_q, n_kv):
    q_per_kv = n_q // n_kv
    kv_w = n_kv * HEAD_DIM
    lane_slot = lax.broadcasted_iota(jnp.int32, (BLOCK, BLOCK), 1) // HEAD_DIM
    ones = jnp.ones((2 * BLOCK, BLOCK), BF16)
    bias_idx = jnp.where(seq_start, 1, 0)

    keys = jnp.concatenate([kp_ref[...], kc_ref[...]], axis=0) * krow_ref[...]
    vals = jnp.concatenate([vp_ref[...], vc_ref[...]], axis=0) * krow_ref[...]
    lhs = jnp.concatenate(
        [q_ref[:, j * kv_w:(j + 1) * kv_w] * qmask_ref[g]
         for j in range(q_per_kv) for g in range(n_kv)], axis=0)
    lg_ref[...] = lax.dot_general(lhs, keys, (((1,), (1,)), ((), ())),
                                  preferred_element_type=F32).reshape(n_q, BLOCK, 2 * BLOCK)
    for slot in range(n_q):
        logit = lg_ref[slot] + bias_ref[bias_idx, slot]
        m = jnp.max(logit, axis=-1, keepdims=True)
        p_ref[slot * BLOCK:(slot + 1) * BLOCK, :] = jnp.exp2(logit - m).astype(BF16)
    p = p_ref[...]
    pv_ref[...] = jnp.dot(p, vals, preferred_element_type=F32).reshape(n_q, BLOCK, kv_w)
    rs_ref[...] = jnp.dot(p, ones, preferred_element_type=F32).reshape(n_q, BLOCK, BLOCK)
    per_half = BLOCK // HEAD_DIM
    for j in range(q_per_kv):
        for half in range(kv_w // BLOCK):
            lanes = slice(half * BLOCK, (half + 1) * BLOCK)
            slots = [j * n_kv + half * per_half + i for i in range(per_half)]
            num = pv_ref[slots[-1], :, lanes]
            den = rs_ref[slots[-1]]
            for i in range(per_half - 2, -1, -1):
                num = jnp.where(lane_slot == i, pv_ref[slots[i], :, lanes], num)
                den = jnp.where(lane_slot == i, rs_ref[slots[i]], den)
            o_ref[:, j * kv_w + half * BLOCK:j * kv_w + (half + 1) * BLOCK] = (
                num * (1.0 / den)).astype(o_ref.dtype)


_CONV_HALO = 32
_CONV_LANES = 256
_SUBLANES = 8


def _conv_zero_history(seq_start, u_ref):
    @pl.when(seq_start)
    def _():
        u_ref[:_CONV_HALO, :] = jnp.zeros((_CONV_HALO, u_ref.shape[1]), u_ref.dtype)


def _conv_shift_matrix(rows):
    i = jnp.arange(_SUBLANES * rows) % rows
    r = jnp.arange(_SUBLANES * rows) // rows
    return (jnp.arange(rows)[None, :] == (i + r)[:, None]).astype(BF16)


def _conv_block(uin_ref, shift_ref, w_ref, cb_ref, lg_ref, lb_ref, o_ref, u_ref, sh_ref, y_ref):
    width = w_ref.shape[0]
    ts, channels = o_ref.shape
    u_ref[_CONV_HALO:, :] = uin_ref[...]
    sh_ref[...] = jnp.dot(shift_ref[...], u_ref[...], preferred_element_type=F32).reshape(sh_ref.shape)
    first = _CONV_HALO - (width - 1)
    unknown_zero = jnp.minimum(pl.program_id(0), 0)
    for c0 in range(0, channels, _CONV_LANES):
        cs = slice(c0, c0 + _CONV_LANES)
        acc = jnp.broadcast_to(cb_ref[:, cs], (ts, _CONV_LANES))
        for j in range(width):
            tile, r = divmod(first + j, _SUBLANES)
            rows = pl.ds(pl.multiple_of(tile * _SUBLANES + unknown_zero, _SUBLANES), ts)
            acc = acc + sh_ref[r, rows, cs] * w_ref[j:j + 1, cs]
        y_ref[:, cs] = acc
    y = y_ref[...]
    mu = jnp.mean(y, axis=-1, keepdims=True)
    yc = y - mu
    var = jnp.mean(yc * yc, axis=-1, keepdims=True)
    z = yc * lax.rsqrt(var + EPS) * lg_ref[...] + lb_ref[...]
    o_ref[...] = (z * _sigmoid(z)).astype(o_ref.dtype)
    u_ref[:_CONV_HALO, :] = u_ref[ts:, :]


def _core_body(h_ref, w_ref,
               uin_ref, shift_ref, cw_ref, cb_ref, lg_ref, lb_ref,
               rb_ref, sink_ref, bucket_ref, qmask_ref, krow_ref, q_ref, kc_ref, kp_ref, vc_ref, vp_ref,
               gates_ref, yconv_ref, yattn_ref,
               wbf_ref, u_ref, sh_ref, y_ref, bias_ref, lgt_ref, p_ref, pv_ref, rs_ref,
               *, n_q, n_kv, blocks_per_seq):
    t = pl.program_id(0) * pl.num_programs(1) + pl.program_id(1)
    seq_start = t % blocks_per_seq == 0
    _cast_weights_once([w_ref], [wbf_ref])
    _conv_zero_history(seq_start, u_ref)
    _attn_build_bias(t == 0, rb_ref, sink_ref, bucket_ref, bias_ref, n_q=n_q, n_kv=n_kv)

    _conv_block(uin_ref, shift_ref, cw_ref, cb_ref, lg_ref, lb_ref, yconv_ref, u_ref, sh_ref, y_ref)
    acc = jnp.dot(h_ref[...], wbf_ref[...], preferred_element_type=F32)
    gates_ref[...] = acc.astype(gates_ref.dtype)
    _attn_block(seq_start, qmask_ref, krow_ref, q_ref, kc_ref, kp_ref, vc_ref, vp_ref, yattn_ref,
                bias_ref, lgt_ref, p_ref, pv_ref, rs_ref, n_q=n_q, n_kv=n_kv)


def _mixer_core(h, w_in, layer, gate_col, conv_in, qkv, conv_w, conv_b, ln_g, ln_b, rel_bias, sinks,
                s_len, n_q, n_kv, tn=512):
    m, k = h.shape
    n = w_in.shape[2] - gate_col
    width, channels = conv_w.shape
    attn_w, kv_w = n_q * HEAD_DIM, n_kv * HEAD_DIM
    n_blocks = m // BLOCK
    n_tiles = n // tn
    assert gate_col % tn == 0 and n % tn == 0 and n_blocks % n_tiles == 0
    assert s_len % BLOCK == 0 and width - 1 <= _CONV_HALO and _CONV_HALO % _SUBLANES == 0
    assert attn_w % kv_w == 0 and kv_w == 2 * BLOCK and BLOCK % HEAD_DIM == 0
    m_tiles = n_blocks // n_tiles
    tm = m // m_tiles
    col_blk0 = gate_col // tn
    k_blk = attn_w // kv_w
    v_blk = k_blk + 1

    step = lambda j, i: j * m_tiles + i
    prev_step = lambda j, i: jnp.maximum(step(j, i) - 1, 0)
    const2 = lambda j, i: (0, 0)
    row = lambda v: v.reshape(1, channels)
    lane_group = jnp.arange(kv_w) // HEAD_DIM
    qmask = jnp.broadcast_to((lane_group[None, :] == jnp.arange(n_kv)[:, None])[:, None, :],
                             (n_kv, BLOCK, kv_w)).astype(BF16)
    krow = jnp.broadcast_to((jnp.arange(2 * BLOCK) > 0)[:, None], (2 * BLOCK, kv_w)).astype(BF16)
    body = functools.partial(_core_body, n_q=n_q, n_kv=n_kv, blocks_per_seq=s_len // BLOCK)
    return pl.pallas_call(
        body,
        grid=(n_tiles, m_tiles),
        in_specs=[
            pl.BlockSpec((tm, k), lambda j, i: (i, 0)),
            pl.BlockSpec((None, k, tn), lambda j, i: (layer, 0, col_blk0 + j)),
            pl.BlockSpec((BLOCK, channels), lambda j, i: (step(j, i), 0)),
            pl.BlockSpec((_SUBLANES * (_CONV_HALO + BLOCK), _CONV_HALO + BLOCK), const2),
            pl.BlockSpec((width, channels), const2),
            pl.BlockSpec((1, channels), const2),
            pl.BlockSpec((1, channels), const2),
            pl.BlockSpec((1, channels), const2),
            pl.BlockSpec(memory_space=pltpu.SMEM),
            pl.BlockSpec(memory_space=pltpu.SMEM),
            pl.BlockSpec((BLOCK, 2 * BLOCK), const2),
            pl.BlockSpec((n_kv, BLOCK, kv_w), lambda j, i: (0, 0, 0)),
            pl.BlockSpec((2 * BLOCK, kv_w), const2),
            pl.BlockSpec((BLOCK, attn_w), lambda j, i: (step(j, i), 0)),
            pl.BlockSpec((BLOCK, kv_w), lambda j, i: (step(j, i), k_blk)),
            pl.BlockSpec((BLOCK, kv_w), lambda j, i: (prev_step(j, i), k_blk)),
            pl.BlockSpec((BLOCK, kv_w), lambda j, i: (step(j, i), v_blk)),
            pl.BlockSpec((BLOCK, kv_w), lambda j, i: (prev_step(j, i), v_blk)),
        ],
        out_specs=[
            pl.BlockSpec((tm, tn), lambda j, i: (i, j)),
            pl.BlockSpec((BLOCK, channels), lambda j, i: (step(j, i), 0)),
            pl.BlockSpec((BLOCK, attn_w), lambda j, i: (step(j, i), 0)),
        ],
        out_shape=[
            jax.ShapeDtypeStruct((m, n), BF16),
            jax.ShapeDtypeStruct((m, channels), BF16),
            jax.ShapeDtypeStruct((m, attn_w), BF16),
        ],
        scratch_shapes=[
            pltpu.VMEM((k, tn), BF16),
            pltpu.VMEM((_CONV_HALO + BLOCK, channels), BF16),
            pltpu.VMEM((_SUBLANES, _CONV_HALO + BLOCK, channels), F32),
            pltpu.VMEM((BLOCK, channels), F32),
            pltpu.VMEM((2, n_q, BLOCK, 2 * BLOCK), F32),
            pltpu.VMEM((n_q, BLOCK, 2 * BLOCK), F32),
            pltpu.VMEM((n_q * BLOCK, 2 * BLOCK), BF16),
            pltpu.VMEM((n_q, BLOCK, kv_w), F32),
            pltpu.VMEM((n_q, BLOCK, BLOCK), F32),
        ],
        compiler_params=_params(2),
        name="mixer_core",
    )(h, w_in,
      conv_in, _conv_shift_matrix(_CONV_HALO + BLOCK), conv_w, row(conv_b), row(ln_g), row(ln_b),
      rel_bias.astype(F32).reshape(-1), sinks.astype(F32), _band_buckets(), qmask, krow,
      qkv, qkv, qkv, qkv, qkv)


def kernel(x, c, w_ada, b_ada, norm_mix_g, w_in, q_norm_g, k_norm_g, attn_sinks, rel_bias,
           w_attn_out, conv_w, conv_b, conv_ln_g, conv_ln_b, w_conv_out, w_mix_out,
           norm_ffn_g, w_ffn_in, w_ffn_out):
    b_sz, s_len, d = x.shape
    depth = w_ada.shape[0]
    n_q = attn_sinks.shape[1]
    attn_w = w_attn_out.shape[1]
    channels = conv_w.shape[2]
    kv_w = (w_in.shape[2] - attn_w - 2 * channels - 2 * d) // 2
    n_kv = kv_w // HEAD_DIM
    conv_col = attn_w + 2 * kv_w
    gate_col = conv_col + 2 * channels
    m = b_sz * s_len

    for l in range(depth):
        mod = _ada(c, w_ada, b_ada[l], l)
        mod3 = mod.reshape(b_sz * N_MOD, 1, d)

        qkv, h = _qkv_proj(x, norm_mix_g[l], mod3, 1, 0, w_in, l, q_norm_g[l], k_norm_g[l], n_q, n_kv)
        h = h.reshape(m, d)
        proj_tm, proj_tn = 2048, 256
        assert (channels // proj_tn) * (m // proj_tm) == n_q
        slot_head = lambda slot: _orig_head(slot, n_kv, n_q // n_kv)
        same = lambda step: step
        conv_in, wa_bf, wc_bf, wmix_bf = _glu_proj(
            h, w_in, l, conv_col, conv_col + channels, channels, proj_tm, proj_tn,
            cast_jobs=((w_attn_out, slot_head), (w_conv_out, same), (w_mix_out, same)))
        gates, y_conv, y_attn = _mixer_core(h, w_in, l, gate_col, conv_in, qkv, conv_w[l], conv_b[l],
                                            conv_ln_g[l], conv_ln_b[l], rel_bias, attn_sinks[l],
                                            s_len, n_q, n_kv)
        merged = _merge(y_attn, y_conv, wa_bf, wc_bf, gates, 0, d)
        x1, h = _mix_out(merged, wmix_bf, x, mod3, 2, norm_ffn_g[l], 4, 3)

        act, wo_bf = _ffn_in(h.reshape(m, d), w_ffn_in, w_ffn_out, l)
        x = _residual_proj(act, wo_bf, x1.reshape(m, d), mod3, 5, s_len,
                           tm=512, tn=1024).reshape(b_sz, s_len, d)
    return x
```

```python
import functools
import math

import jax
import jax.numpy as jnp
from jax import lax
from jax.experimental import pallas as pl
from jax.experimental.pallas import tpu as pltpu

F32 = jnp.float32
BF16 = jnp.bfloat16

HEAD_DIM = 64
WINDOW = 128
BLOCK = 128
NUM_BUCKETS = 32
MAX_EXACT = NUM_BUCKETS // 2
MAX_DISTANCE = 128
N_MOD = 6
EPS = 1e-6
LOG2E = math.log2(math.e)

V7X_VMEM_LIMIT_BYTES = 56 * 1024 * 1024


def _params(n_axes):
    return pltpu.CompilerParams(
        dimension_semantics=("arbitrary",) * n_axes,
        vmem_limit_bytes=V7X_VMEM_LIMIT_BYTES,
    )


def _sigmoid(x):
    return jax.nn.sigmoid(x)


def _orig_head(slot, n_kv, q_per_kv):
    return (slot % n_kv) * q_per_kv + slot // n_kv


def _ada_body(c_ref, w_ref, b_ref, o_ref):
    c = c_ref[...]
    act = (c * _sigmoid(c)).astype(BF16)
    o_ref[...] = jnp.dot(act, w_ref[...].astype(BF16), preferred_element_type=F32) + b_ref[...]


def _ada(c, w_ada, b_ada, layer, tn=1024):
    b_sz, d = c.shape
    n = w_ada.shape[2]
    rows = 8
    c_pad = jnp.pad(c, ((0, rows - b_sz), (0, 0)))
    out = pl.pallas_call(
        _ada_body,
        grid=(n // tn,),
        in_specs=[
            pl.BlockSpec((rows, d), lambda j: (0, 0)),
            pl.BlockSpec((None, d, tn), lambda j: (layer, 0, j)),
            pl.BlockSpec((1, tn), lambda j: (0, j)),
        ],
        out_specs=pl.BlockSpec((rows, tn), lambda j: (0, j)),
        out_shape=jax.ShapeDtypeStruct((rows, n), F32),
        compiler_params=_params(1),
        name="ada_mod",
    )(c_pad, w_ada, b_ada.reshape(1, n))
    return out[:b_sz]


def _norm_mod_body(x_ref, g_ref, sc_ref, sh_ref, o_ref):
    x = x_ref[0]
    ms = jnp.mean(x * x, axis=-1, keepdims=True)
    gain = g_ref[...] * (1.0 + sc_ref[0])
    o_ref[0] = (x * lax.rsqrt(ms + EPS) * gain + sh_ref[0]).astype(o_ref.dtype)


def _qkv_body(x_ref, g_ref, sc_ref, sh_ref, w_ref, gq_ref, gk_ref, o_ref, h_ref, wbf_ref, seg_ref,
              *, n_q, n_kv, half):
    attn_w = n_q * HEAD_DIM
    kv_w = n_kv * HEAD_DIM
    _norm_mod_body(x_ref, g_ref, sc_ref, sh_ref, h_ref)

    @pl.when(jnp.logical_and(pl.program_id(0) == 0, pl.program_id(1) == 0))
    def _():
        for slot in range(n_q):
            src = _orig_head(slot, n_kv, n_q // n_kv) * HEAD_DIM
            wbf_ref[:, slot * HEAD_DIM:(slot + 1) * HEAD_DIM] = w_ref[:, src:src + HEAD_DIM].astype(BF16)
        wbf_ref[:, attn_w:] = w_ref[:, attn_w:].astype(BF16)
        r = lax.broadcasted_iota(jnp.int32, (half, half), 0) // HEAD_DIM
        c = lax.broadcasted_iota(jnp.int32, (half, half), 1) // HEAD_DIM
        seg_ref[...] = (r == c).astype(BF16)

    a = h_ref[0]

    def head_norm(acc, seg, gain):
        ss = jnp.dot((acc * acc).astype(BF16), seg, preferred_element_type=F32)
        return acc * lax.rsqrt(ss / HEAD_DIM + EPS) * gain

    for c0 in range(0, attn_w, half):
        acc = jnp.dot(a, wbf_ref[:, c0:c0 + half], preferred_element_type=F32)
        o_ref[:, c0:c0 + half] = head_norm(acc, seg_ref[...], gq_ref[...]).astype(o_ref.dtype)
    acc = jnp.dot(a, wbf_ref[:, attn_w:], preferred_element_type=F32)
    o_ref[:, attn_w:attn_w + kv_w] = head_norm(acc[:, :kv_w], seg_ref[:kv_w, :kv_w],
                                               gk_ref[...]).astype(o_ref.dtype)
    o_ref[:, attn_w + kv_w:] = acc[:, kv_w:].astype(o_ref.dtype)


def _qkv_proj(x, norm_g, mod3, scale_idx, shift_idx, w_in, layer, q_g, k_g, n_q, n_kv, ts=512, half=512):
    b_sz, s_len, d = x.shape
    attn_w, kv_w = n_q * HEAD_DIM, n_kv * HEAD_DIM
    width = attn_w + 2 * kv_w
    assert attn_w % half == 0 and kv_w <= half and half % HEAD_DIM == 0
    gq = jnp.tile(q_g * (HEAD_DIM ** -0.5 * LOG2E), half // HEAD_DIM).reshape(1, half)
    gk = jnp.tile(k_g, n_kv).reshape(1, kv_w)
    tiles = s_len // ts
    const2 = lambda b, s: (0, 0)
    body = functools.partial(_qkv_body, n_q=n_q, n_kv=n_kv, half=half)
    return pl.pallas_call(
        body,
        grid=(b_sz, tiles),
        in_specs=[
            pl.BlockSpec((1, ts, d), lambda b, s: (b, s, 0)),
            pl.BlockSpec((1, d), const2),
            pl.BlockSpec((1, 1, d), lambda b, s: (b * N_MOD + scale_idx, 0, 0)),
            pl.BlockSpec((1, 1, d), lambda b, s: (b * N_MOD + shift_idx, 0, 0)),
            pl.BlockSpec((None, d, width), lambda b, s: (layer, 0, 0), pipeline_mode=pl.Buffered(1)),
            pl.BlockSpec((1, half), const2),
            pl.BlockSpec((1, kv_w), const2),
        ],
        out_specs=[
            pl.BlockSpec((ts, width), lambda b, s: (b * tiles + s, 0)),
            pl.BlockSpec((1, ts, d), lambda b, s: (b, s, 0)),
        ],
        out_shape=[
            jax.ShapeDtypeStruct((b_sz * s_len, width), BF16),
            jax.ShapeDtypeStruct((b_sz, s_len, d), BF16),
        ],
        scratch_shapes=[pltpu.VMEM((d, width), BF16), pltpu.VMEM((half, half), BF16)],
        compiler_params=_params(2),
        name="qkv_proj",
    )(x, norm_g.reshape(1, d), mod3, mod3, w_in, gq, gk)


def _cast_weights_once(w_refs, wbf_refs):
    @pl.when(pl.program_id(1) == 0)
    def _():
        for w_ref, wbf_ref in zip(w_refs, wbf_refs):
            wbf_ref[...] = w_ref[...].astype(BF16)


def _glu_proj_body(h_ref, wa_ref, wg_ref, *refs):
    n_jobs = (len(refs) - 3) // 2
    job_in, o_ref, job_out = refs[:n_jobs], refs[n_jobs], refs[n_jobs + 1:-2]
    wabf_ref, wgbf_ref = refs[-2:]
    _cast_weights_once([wa_ref, wg_ref], [wabf_ref, wgbf_ref])
    for src_ref, dst_ref in zip(job_in, job_out):
        dst_ref[...] = src_ref[...].astype(BF16)
    for r0 in range(0, h_ref.shape[0], _MATMUL_ROWS):
        rs = slice(r0, r0 + _MATMUL_ROWS)
        h = h_ref[rs, :]
        a = jnp.dot(h, wabf_ref[...], preferred_element_type=F32)
        g = jnp.dot(h, wgbf_ref[...], preferred_element_type=F32)
        o_ref[rs, :] = (a * _sigmoid(g)).astype(o_ref.dtype)


def _glu_proj(h, w, layer, a_col, g_col, n, tm, tn, cast_jobs=()):
    m, k = h.shape
    assert a_col % tn == 0 and g_col % tn == 0 and n % tn == 0
    a_blk0, g_blk0 = a_col // tn, g_col // tn
    m_tiles = m // tm
    steps = (n // tn) * m_tiles
    step = lambda j, i: j * m_tiles + i
    job_in, job_out, job_shapes = [], [], []
    for wj, row_block_of_step in cast_jobs:
        rows, cols = wj.shape[1:]
        assert rows % steps == 0
        slab = rows // steps
        job_in.append(pl.BlockSpec((None, slab, cols),
                                   lambda j, i, f=row_block_of_step: (layer, f(step(j, i)), 0)))
        job_out.append(pl.BlockSpec((slab, cols), lambda j, i: (step(j, i), 0)))
        job_shapes.append(jax.ShapeDtypeStruct((rows, cols), BF16))
    return pl.pallas_call(
        _glu_proj_body,
        grid=(n // tn, m_tiles),
        in_specs=[
            pl.BlockSpec((tm, k), lambda j, i: (i, 0)),
            pl.BlockSpec((None, k, tn), lambda j, i: (layer, 0, a_blk0 + j)),
            pl.BlockSpec((None, k, tn), lambda j, i: (layer, 0, g_blk0 + j)),
        ] + job_in,
        out_specs=[pl.BlockSpec((tm, tn), lambda j, i: (i, j))] + job_out,
        out_shape=[jax.ShapeDtypeStruct((m, n), BF16)] + job_shapes,
        scratch_shapes=[pltpu.VMEM((k, tn), BF16), pltpu.VMEM((k, tn), BF16)],
        compiler_params=_params(2),
        name="glu_proj",
    )(h, w, w, *[wj for wj, _ in cast_jobs])


_EPILOGUE_LANES = 512
_MATMUL_ROWS = 1024


def _merge_mix_body(ya_ref, yc_ref, wa_ref, wc_ref, ga_ref, gc_ref, w_ref, x_ref, gt_ref, g_ref, sc_ref, sh_ref,
                    x1_ref, h_ref, merged_ref):
    d = w_ref.shape[1]
    ya = ya_ref[...]
    yc = yc_ref[...]
    for n0 in range(0, d, _EPILOGUE_LANES):
        ns = slice(n0, n0 + _EPILOGUE_LANES)
        acc_a = jnp.dot(ya, wa_ref[:, ns], preferred_element_type=F32)
        acc_c = jnp.dot(yc, wc_ref[:, ns], preferred_element_type=F32)
        ga = _sigmoid(ga_ref[:, ns].astype(F32))
        gc = _sigmoid(gc_ref[:, ns].astype(F32))
        merged_ref[:, ns] = (ga * acc_a + gc * acc_c).astype(merged_ref.dtype)
    a = merged_ref[...]
    sumsq = jnp.zeros((a.shape[0], 1), F32)
    for n0 in range(0, d, _EPILOGUE_LANES):
        ns = slice(n0, n0 + _EPILOGUE_LANES)
        acc = jnp.dot(a, w_ref[:, ns], preferred_element_type=F32)
        x1 = x_ref[0, :, ns] + gt_ref[0, :, ns] * acc
        x1_ref[0, :, ns] = x1
        sumsq = sumsq + jnp.sum(x1 * x1, axis=-1, keepdims=True)
    inv = lax.rsqrt(sumsq / d + EPS)
    gain = g_ref[...] * (1.0 + sc_ref[0])
    h_ref[0] = (x1_ref[0] * inv * gain + sh_ref[0]).astype(h_ref.dtype)


def _merge_mix(attn, conv, wa_bf, wc_bf, gates, ga_col, gc_col, wmix_bf, x, mod3, gate_idx, norm_g,
               scale_idx, shift_idx, ts=512):
    b_sz, s_len, d = x.shape
    ka, kc = attn.shape[1], conv.shape[1]
    assert d % _EPILOGUE_LANES == 0 and ga_col % d == 0 and gc_col % d == 0
    tiles = s_len // ts
    const2 = lambda b, s: (0, 0)
    rows = lambda width, col_blk: pl.BlockSpec((ts, width), lambda b, s: (b * tiles + s, col_blk))
    mod_row = lambda idx: pl.BlockSpec((1, 1, d), lambda b, s: (b * N_MOD + idx, 0, 0))
    act = pl.BlockSpec((1, ts, d), lambda b, s: (b, s, 0))
    resident = pl.Buffered(1)
    return pl.pallas_call(
        _merge_mix_body,
        grid=(b_sz, tiles),
        in_specs=[
            rows(ka, 0),
            rows(kc, 0),
            pl.BlockSpec((ka, d), const2, pipeline_mode=resident),
            pl.BlockSpec((kc, d), const2, pipeline_mode=resident),
            rows(d, ga_col // d),
            rows(d, gc_col // d),
            pl.BlockSpec((d, d), const2, pipeline_mode=resident),
            act,
            mod_row(gate_idx),
            pl.BlockSpec((1, d), const2),
            mod_row(scale_idx),
            mod_row(shift_idx),
        ],
        out_specs=[act, act],
        out_shape=[jax.ShapeDtypeStruct((b_sz, s_len, d), F32), jax.ShapeDtypeStruct((b_sz, s_len, d), BF16)],
        scratch_shapes=[pltpu.VMEM((ts, d), BF16)],
        compiler_params=_params(2),
        name="merge_mix",
    )(attn, conv, wa_bf, wc_bf, gates, gates, wmix_bf, x, mod3, norm_g.reshape(1, d), mod3, mod3)


def _residual_proj_body(a_ref, w_ref, x_ref, gt_ref, o_ref):
    a = a_ref[...]
    for n0 in range(0, o_ref.shape[1], _EPILOGUE_LANES):
        ns = slice(n0, n0 + _EPILOGUE_LANES)
        acc = jnp.dot(a, w_ref[:, ns], preferred_element_type=F32)
        o_ref[:, ns] = x_ref[:, ns] + gt_ref[0, :, ns] * acc


def _residual_proj(a, w_bf, x2d, mod3, gate_idx, rows_per_batch, tm, tn):
    m, k = a.shape
    n = w_bf.shape[1]
    assert n % tn == 0 and tn % _EPILOGUE_LANES == 0 and rows_per_batch % tm == 0
    tiles_per_batch = rows_per_batch // tm
    return pl.pallas_call(
        _residual_proj_body,
        grid=(n // tn, m // tm),
        in_specs=[
            pl.BlockSpec((tm, k), lambda j, i: (i, 0)),
            pl.BlockSpec((k, tn), lambda j, i: (0, j)),
            pl.BlockSpec((tm, tn), lambda j, i: (i, j)),
            pl.BlockSpec((1, 1, tn), lambda j, i: ((i // tiles_per_batch) * N_MOD + gate_idx, 0, j)),
        ],
        out_specs=pl.BlockSpec((tm, tn), lambda j, i: (i, j)),
        out_shape=jax.ShapeDtypeStruct((m, n), F32),
        compiler_params=_params(2),
        name="residual_proj",
    )(a, w_bf, x2d, mod3)


def _ffn_in_body(a_ref, wg_ref, wu_ref, wo_ref, o_ref, wobf_ref, wgbf_ref, wubf_ref):
    _cast_weights_once([wg_ref, wu_ref], [wgbf_ref, wubf_ref])
    wobf_ref[...] = wo_ref[...].astype(BF16)
    for r0 in range(0, a_ref.shape[0], _MATMUL_ROWS):
        rs = slice(r0, r0 + _MATMUL_ROWS)
        a = a_ref[rs, :]
        gate = jnp.dot(a, wgbf_ref[...], preferred_element_type=F32)
        up = jnp.dot(a, wubf_ref[...], preferred_element_type=F32)
        o_ref[rs, :] = (gate * _sigmoid(gate) * up).astype(o_ref.dtype)


def _ffn_in(h, w_ffn_in, w_ffn_out, layer, tm=2048, tn=512):
    m, k = h.shape
    d_ff = w_ffn_in.shape[2] // 2
    d_out = w_ffn_out.shape[2]
    up_blk = d_ff // tn
    m_tiles = m // tm
    steps = (d_ff // tn) * m_tiles
    assert d_ff % steps == 0
    slab = d_ff // steps
    step = lambda j, i: j * m_tiles + i
    return pl.pallas_call(
        _ffn_in_body,
        grid=(d_ff // tn, m_tiles),
        in_specs=[
            pl.BlockSpec((tm, k), lambda j, i: (i, 0)),
            pl.BlockSpec((None, k, tn), lambda j, i: (layer, 0, j)),
            pl.BlockSpec((None, k, tn), lambda j, i: (layer, 0, up_blk + j)),
            pl.BlockSpec((None, slab, d_out), lambda j, i: (layer, step(j, i), 0)),
        ],
        out_specs=[
            pl.BlockSpec((tm, tn), lambda j, i: (i, j)),
            pl.BlockSpec((slab, d_out), lambda j, i: (step(j, i), 0)),
        ],
        out_shape=[jax.ShapeDtypeStruct((m, d_ff), BF16), jax.ShapeDtypeStruct((d_ff, d_out), BF16)],
        scratch_shapes=[pltpu.VMEM((k, tn), BF16), pltpu.VMEM((k, tn), BF16)],
        compiler_params=_params(2),
        name="ffn_in",
    )(h, w_ffn_in, w_ffn_in, w_ffn_out)


def _t5_causal_bucket(dist):
    n = jnp.maximum(dist, 0)
    nf = jnp.maximum(n, 1).astype(jnp.float32)
    large = MAX_EXACT + (jnp.log(nf / MAX_EXACT) / math.log(MAX_DISTANCE / MAX_EXACT)
                         * (NUM_BUCKETS - MAX_EXACT)).astype(jnp.int32)
    large = jnp.minimum(large, NUM_BUCKETS - 1)
    return jnp.where(n < MAX_EXACT, n, large)


def _band_buckets():
    q_off = jnp.arange(BLOCK)
    k_off = jnp.arange(2 * BLOCK)
    dist = q_off[:, None] + BLOCK - k_off[None, :]
    allowed = (dist >= 0) & (dist < WINDOW)
    return jnp.where(allowed, _t5_causal_bucket(dist), -1).astype(jnp.int32)


def _attn_build_bias(first_step, rb_ref, sink_ref, bucket_ref, bias_ref, *, n_q, n_kv):
    q_per_kv = n_q // n_kv

    @pl.when(first_step)
    def _():
        bucket = bucket_ref[...]
        col = lax.broadcasted_iota(jnp.int32, bucket.shape, 1)
        for slot in range(n_q):
            head = _orig_head(slot, n_kv, q_per_kv)
            tile = jnp.zeros(bucket.shape, F32)
            for b in range(NUM_BUCKETS):
                tile = jnp.where(bucket == b, rb_ref[b * n_q + head] * LOG2E, tile)
            tile = jnp.where(bucket < 0, -jnp.inf, tile)
            sink = sink_ref[head] * LOG2E
            bias_ref[0, slot] = jnp.where(col == 0, sink, tile)
            bias_ref[1, slot] = jnp.where(col == 0, sink, jnp.where(col < BLOCK, -jnp.inf, tile))


def _attn_block(seq_start, qmask_ref, krow_ref, q_ref, kc_ref, kp_ref, vc_ref, vp_ref, o_ref,
                bias_ref, lg_ref, p_ref, pv_ref, rs_ref, *, n_q, n_kv):
    q_per_kv = n_q // n_kv
    kv_w = n_kv * HEAD_DIM
    lane_slot = lax.broadcasted_iota(jnp.int32, (BLOCK, BLOCK), 1) // HEAD_DIM
    ones = jnp.ones((2 * BLOCK, BLOCK), BF16)
    bias_idx = jnp.where(seq_start, 1, 0)

    keys = jnp.concatenate([kp_ref[...], kc_ref[...]], axis=0) * krow_ref[...]
    vals = jnp.concatenate([vp_ref[...], vc_ref[...]], axis=0) * krow_ref[...]
    lhs = jnp.concatenate(
        [q_ref[:, j * kv_w:(j + 1) * kv_w] * qmask_ref[g]
         for j in range(q_per_kv) for g in range(n_kv)], axis=0)
    lg_ref[...] = lax.dot_general(lhs, keys, (((1,), (1,)), ((), ())),
                                  preferred_element_type=F32).reshape(n_q, BLOCK, 2 * BLOCK)
    for slot in range(n_q):
        logit = lg_ref[slot] + bias_ref[bias_idx, slot]
        m = jnp.max(logit, axis=-1, keepdims=True)
        p_ref[slot * BLOCK:(slot + 1) * BLOCK, :] = jnp.exp2(logit - m).astype(BF16)
    p = p_ref[...]
    pv_ref[...] = jnp.dot(p, vals, preferred_element_type=F32).reshape(n_q, BLOCK, kv_w)
    rs_ref[...] = jnp.dot(p, ones, preferred_element_type=F32).reshape(n_q, BLOCK, BLOCK)
    per_half = BLOCK // HEAD_DIM
    for j in range(q_per_kv):
        for half in range(kv_w // BLOCK):
            lanes = slice(half * BLOCK, (half + 1) * BLOCK)
            slots = [j * n_kv + half * per_half + i for i in range(per_half)]
            num = pv_ref[slots[-1], :, lanes]
            den = rs_ref[slots[-1]]
            for i in range(per_half - 2, -1, -1):
                num = jnp.where(lane_slot == i, pv_ref[slots[i], :, lanes], num)
                den = jnp.where(lane_slot == i, rs_ref[slots[i]], den)
            o_ref[:, j * kv_w + half * BLOCK:j * kv_w + (half + 1) * BLOCK] = (
                num * (1.0 / den)).astype(o_ref.dtype)


_CONV_HALO = 32
_CONV_LANES = 256
_SUBLANES = 8


def _conv_zero_history(seq_start, u_ref):
    @pl.when(seq_start)
    def _():
        u_ref[:_CONV_HALO, :] = jnp.zeros((_CONV_HALO, u_ref.shape[1]), u_ref.dtype)


def _conv_shift_matrix(rows):
    i = jnp.arange(_SUBLANES * rows) % rows
    r = jnp.arange(_SUBLANES * rows) // rows
    return (jnp.arange(rows)[None, :] == (i + r)[:, None]).astype(BF16)


def _conv_block(uin_ref, shift_ref, w_ref, cb_ref, lg_ref, lb_ref, o_ref, u_ref, sh_ref, y_ref):
    width = w_ref.shape[0]
    ts, channels = o_ref.shape
    u_ref[_CONV_HALO:, :] = uin_ref[...]
    sh_ref[...] = jnp.dot(shift_ref[...], u_ref[...], preferred_element_type=F32).reshape(sh_ref.shape)
    first = _CONV_HALO - (width - 1)
    unknown_zero = jnp.minimum(pl.program_id(0), 0)
    for c0 in range(0, channels, _CONV_LANES):
        cs = slice(c0, c0 + _CONV_LANES)
        acc = jnp.broadcast_to(cb_ref[:, cs], (ts, _CONV_LANES))
        for j in range(width):
            tile, r = divmod(first + j, _SUBLANES)
            rows = pl.ds(pl.multiple_of(tile * _SUBLANES + unknown_zero, _SUBLANES), ts)
            acc = acc + sh_ref[r, rows, cs] * w_ref[j:j + 1, cs]
        y_ref[:, cs] = acc
    y = y_ref[...]
    mu = jnp.mean(y, axis=-1, keepdims=True)
    yc = y - mu
    var = jnp.mean(yc * yc, axis=-1, keepdims=True)
    z = yc * lax.rsqrt(var + EPS) * lg_ref[...] + lb_ref[...]
    o_ref[...] = (z * _sigmoid(z)).astype(o_ref.dtype)
    u_ref[:_CONV_HALO, :] = u_ref[ts:, :]


def _core_body(h_ref, w_ref,
               uin_ref, shift_ref, cw_ref, cb_ref, lg_ref, lb_ref,
               rb_ref, sink_ref, bucket_ref, qmask_ref, krow_ref, q_ref, kc_ref, kp_ref, vc_ref, vp_ref,
               gates_ref, yconv_ref, yattn_ref,
               wbf_ref, u_ref, sh_ref, y_ref, bias_ref, lgt_ref, p_ref, pv_ref, rs_ref,
               *, n_q, n_kv, blocks_per_seq):
    t = pl.program_id(0) * pl.num_programs(1) + pl.program_id(1)
    seq_start = t % blocks_per_seq == 0
    _cast_weights_once([w_ref], [wbf_ref])
    _conv_zero_history(seq_start, u_ref)
    _attn_build_bias(t == 0, rb_ref, sink_ref, bucket_ref, bias_ref, n_q=n_q, n_kv=n_kv)

    _conv_block(uin_ref, shift_ref, cw_ref, cb_ref, lg_ref, lb_ref, yconv_ref, u_ref, sh_ref, y_ref)
    acc = jnp.dot(h_ref[...], wbf_ref[...], preferred_element_type=F32)
    gates_ref[...] = acc.astype(gates_ref.dtype)
    _attn_block(seq_start, qmask_ref, krow_ref, q_ref, kc_ref, kp_ref, vc_ref, vp_ref, yattn_ref,
                bias_ref, lgt_ref, p_ref, pv_ref, rs_ref, n_q=n_q, n_kv=n_kv)


def _mixer_core(h, w_in, layer, gate_col, conv_in, qkv, conv_w, conv_b, ln_g, ln_b, rel_bias, sinks,
                s_len, n_q, n_kv, tn=512):
    m, k = h.shape
    n = w_in.shape[2] - gate_col
    width, channels = conv_w.shape
    attn_w, kv_w = n_q * HEAD_DIM, n_kv * HEAD_DIM
    n_blocks = m // BLOCK
    n_tiles = n // tn
    assert gate_col % tn == 0 and n % tn == 0 and n_blocks % n_tiles == 0
    assert s_len % BLOCK == 0 and width - 1 <= _CONV_HALO and _CONV_HALO % _SUBLANES == 0
    assert attn_w % kv_w == 0 and kv_w == 2 * BLOCK and BLOCK % HEAD_DIM == 0
    m_tiles = n_blocks // n_tiles
    tm = m // m_tiles
    col_blk0 = gate_col // tn
    k_blk = attn_w // kv_w
    v_blk = k_blk + 1

    step = lambda j, i: j * m_tiles + i
    prev_step = lambda j, i: jnp.maximum(step(j, i) - 1, 0)
    const2 = lambda j, i: (0, 0)
    row = lambda v: v.reshape(1, channels)
    lane_group = jnp.arange(kv_w) // HEAD_DIM
    qmask = jnp.broadcast_to((lane_group[None, :] == jnp.arange(n_kv)[:, None])[:, None, :],
                             (n_kv, BLOCK, kv_w)).astype(BF16)
    krow = jnp.broadcast_to((jnp.arange(2 * BLOCK) > 0)[:, None], (2 * BLOCK, kv_w)).astype(BF16)
    body = functools.partial(_core_body, n_q=n_q, n_kv=n_kv, blocks_per_seq=s_len // BLOCK)
    return pl.pallas_call(
        body,
        grid=(n_tiles, m_tiles),
        in_specs=[
            pl.BlockSpec((tm, k), lambda j, i: (i, 0)),
            pl.BlockSpec((None, k, tn), lambda j, i: (layer, 0, col_blk0 + j)),
            pl.BlockSpec((BLOCK, channels), lambda j, i: (step(j, i), 0)),
            pl.BlockSpec((_SUBLANES * (_CONV_HALO + BLOCK), _CONV_HALO + BLOCK), const2),
            pl.BlockSpec((width, channels), const2),
            pl.BlockSpec((1, channels), const2),
            pl.BlockSpec((1, channels), const2),
            pl.BlockSpec((1, channels), const2),
            pl.BlockSpec(memory_space=pltpu.SMEM),
            pl.BlockSpec(memory_space=pltpu.SMEM),
            pl.BlockSpec((BLOCK, 2 * BLOCK), const2),
            pl.BlockSpec((n_kv, BLOCK, kv_w), lambda j, i: (0, 0, 0)),
            pl.BlockSpec((2 * BLOCK, kv_w), const2),
            pl.BlockSpec((BLOCK, attn_w), lambda j, i: (step(j, i), 0)),
            pl.BlockSpec((BLOCK, kv_w), lambda j, i: (step(j, i), k_blk)),
            pl.BlockSpec((BLOCK, kv_w), lambda j, i: (prev_step(j, i), k_blk)),
            pl.BlockSpec((BLOCK, kv_w), lambda j, i: (step(j, i), v_blk)),
            pl.BlockSpec((BLOCK, kv_w), lambda j, i: (prev_step(j, i), v_blk)),
        ],
        out_specs=[
            pl.BlockSpec((tm, tn), lambda j, i: (i, j)),
            pl.BlockSpec((BLOCK, channels), lambda j, i: (step(j, i), 0)),
            pl.BlockSpec((BLOCK, attn_w), lambda j, i: (step(j, i), 0)),
        ],
        out_shape=[
            jax.ShapeDtypeStruct((m, n), BF16),
            jax.ShapeDtypeStruct((m, channels), BF16),
            jax.ShapeDtypeStruct((m, attn_w), BF16),
        ],
        scratch_shapes=[
            pltpu.VMEM((k, tn), BF16),
            pltpu.VMEM((_CONV_HALO + BLOCK, channels), BF16),
            pltpu.VMEM((_SUBLANES, _CONV_HALO + BLOCK, channels), F32),
            pltpu.VMEM((BLOCK, channels), F32),
            pltpu.VMEM((2, n_q, BLOCK, 2 * BLOCK), F32),
            pltpu.VMEM((n_q, BLOCK, 2 * BLOCK), F32),
            pltpu.VMEM((n_q * BLOCK, 2 * BLOCK), BF16),
            pltpu.VMEM((n_q, BLOCK, kv_w), F32),
            pltpu.VMEM((n_q, BLOCK, BLOCK), F32),
        ],
        compiler_params=_params(2),
        name="mixer_core",
    )(h, w_in,
      conv_in, _conv_shift_matrix(_CONV_HALO + BLOCK), conv_w, row(conv_b), row(ln_g), row(ln_b),
      rel_bias.astype(F32).reshape(-1), sinks.astype(F32), _band_buckets(), qmask, krow,
      qkv, qkv, qkv, qkv, qkv)


def kernel(x, c, w_ada, b_ada, norm_mix_g, w_in, q_norm_g, k_norm_g, attn_sinks, rel_bias,
           w_attn_out, conv_w, conv_b, conv_ln_g, conv_ln_b, w_conv_out, w_mix_out,
           norm_ffn_g, w_ffn_in, w_ffn_out):
    b_sz, s_len, d = x.shape
    depth = w_ada.shape[0]
    n_q = attn_sinks.shape[1]
    attn_w = w_attn_out.shape[1]
    channels = conv_w.shape[2]
    kv_w = (w_in.shape[2] - attn_w - 2 * channels - 2 * d) // 2
    n_kv = kv_w // HEAD_DIM
    conv_col = attn_w + 2 * kv_w
    gate_col = conv_col + 2 * channels
    m = b_sz * s_len

    for l in range(depth):
        mod = _ada(c, w_ada, b_ada[l], l)
        mod3 = mod.reshape(b_sz * N_MOD, 1, d)

        qkv, h = _qkv_proj(x, norm_mix_g[l], mod3, 1, 0, w_in, l, q_norm_g[l], k_norm_g[l], n_q, n_kv)
        h = h.reshape(m, d)
        proj_tm, proj_tn = 2048, 256
        assert (channels // proj_tn) * (m // proj_tm) == n_q
        slot_head = lambda slot: _orig_head(slot, n_kv, n_q // n_kv)
        same = lambda step: step
        conv_in, wa_bf, wc_bf, wmix_bf = _glu_proj(
            h, w_in, l, conv_col, conv_col + channels, channels, proj_tm, proj_tn,
            cast_jobs=((w_attn_out, slot_head), (w_conv_out, same), (w_mix_out, same)))
        gates, y_conv, y_attn = _mixer_core(h, w_in, l, gate_col, conv_in, qkv, conv_w[l], conv_b[l],
                                            conv_ln_g[l], conv_ln_b[l], rel_bias, attn_sinks[l],
                                            s_len, n_q, n_kv)
        x1, h = _merge_mix(y_attn, y_conv, wa_bf, wc_bf, gates, 0, d, wmix_bf, x, mod3, 2,
                           norm_ffn_g[l], 4, 3)

        act, wo_bf = _ffn_in(h.reshape(m, d), w_ffn_in, w_ffn_out, l)
        x = _residual_proj(act, wo_bf, x1.reshape(m, d), mod3, 5, s_len,
                           tm=512, tn=1024).reshape(b_sz, s_len, d)
    return x
```

```python
import functools
import math

import jax
import jax.numpy as jnp
from jax import lax
from jax.experimental import pallas as pl
from jax.experimental.pallas import tpu as pltpu

F32 = jnp.float32
BF16 = jnp.bfloat16

HEAD_DIM = 64
WINDOW = 128
BLOCK = 128
NUM_BUCKETS = 32
MAX_EXACT = NUM_BUCKETS // 2
MAX_DISTANCE = 128
N_MOD = 6
N_MOD_EARLY = 2
EPS = 1e-6
LOG2E = math.log2(math.e)

V7X_VMEM_LIMIT_BYTES = 56 * 1024 * 1024


def _params(n_axes):
    return pltpu.CompilerParams(
        dimension_semantics=("arbitrary",) * n_axes,
        vmem_limit_bytes=V7X_VMEM_LIMIT_BYTES,
    )


def _sigmoid(x):
    return jax.nn.sigmoid(x)


def _orig_head(slot, n_kv, q_per_kv):
    return (slot % n_kv) * q_per_kv + slot // n_kv


def _ada_body(c_ref, w_ref, b_ref, o_ref):
    c = c_ref[...]
    act = (c * _sigmoid(c)).astype(BF16)
    o_ref[...] = jnp.dot(act, w_ref[...].astype(BF16), preferred_element_type=F32) + b_ref[...]


def _ada(c_pad, w_ada, b_row, layer, n, tn=1024):
    rows, d = c_pad.shape
    assert n % tn == 0
    return pl.pallas_call(
        _ada_body,
        grid=(n // tn,),
        in_specs=[
            pl.BlockSpec((rows, d), lambda j: (0, 0)),
            pl.BlockSpec((None, d, tn), lambda j: (layer, 0, j)),
            pl.BlockSpec((1, tn), lambda j: (0, j)),
        ],
        out_specs=pl.BlockSpec((rows, tn), lambda j: (0, j)),
        out_shape=jax.ShapeDtypeStruct((rows, n), F32),
        compiler_params=_params(1),
        name="ada_mod",
    )(c_pad, w_ada, b_row)


def _norm_mod_body(x_ref, g_ref, sc_ref, sh_ref, o_ref):
    x = x_ref[0]
    ms = jnp.mean(x * x, axis=-1, keepdims=True)
    gain = g_ref[...] * (1.0 + sc_ref[0])
    o_ref[0] = (x * lax.rsqrt(ms + EPS) * gain + sh_ref[0]).astype(o_ref.dtype)


def _qkv_body(x_ref, g_ref, sc_ref, sh_ref, w_ref, gq_ref, gk_ref, o_ref, h_ref, wbf_ref, seg_ref,
              *, n_q, n_kv, half):
    attn_w = n_q * HEAD_DIM
    kv_w = n_kv * HEAD_DIM
    _norm_mod_body(x_ref, g_ref, sc_ref, sh_ref, h_ref)

    @pl.when(jnp.logical_and(pl.program_id(0) == 0, pl.program_id(1) == 0))
    def _():
        for slot in range(n_q):
            src = _orig_head(slot, n_kv, n_q // n_kv) * HEAD_DIM
            wbf_ref[:, slot * HEAD_DIM:(slot + 1) * HEAD_DIM] = w_ref[:, src:src + HEAD_DIM].astype(BF16)
        wbf_ref[:, attn_w:] = w_ref[:, attn_w:].astype(BF16)
        r = lax.broadcasted_iota(jnp.int32, (half, half), 0) // HEAD_DIM
        c = lax.broadcasted_iota(jnp.int32, (half, half), 1) // HEAD_DIM
        seg_ref[...] = (r == c).astype(BF16)

    a = h_ref[0]

    def head_norm(acc, seg, gain):
        ss = jnp.dot((acc * acc).astype(BF16), seg, preferred_element_type=F32)
        return acc * lax.rsqrt(ss / HEAD_DIM + EPS) * gain

    for c0 in range(0, attn_w, half):
        acc = jnp.dot(a, wbf_ref[:, c0:c0 + half], preferred_element_type=F32)
        o_ref[:, c0:c0 + half] = head_norm(acc, seg_ref[...], gq_ref[...]).astype(o_ref.dtype)
    acc = jnp.dot(a, wbf_ref[:, attn_w:], preferred_element_type=F32)
    o_ref[:, attn_w:attn_w + kv_w] = head_norm(acc[:, :kv_w], seg_ref[:kv_w, :kv_w],
                                               gk_ref[...]).astype(o_ref.dtype)
    o_ref[:, attn_w + kv_w:] = acc[:, kv_w:].astype(o_ref.dtype)


def _qkv_proj(x, norm_g, mod3, scale_idx, shift_idx, w_in, layer, q_g, k_g, n_q, n_kv, ts=512, half=512):
    b_sz, s_len, d = x.shape
    attn_w, kv_w = n_q * HEAD_DIM, n_kv * HEAD_DIM
    width = attn_w + 2 * kv_w
    assert attn_w % half == 0 and kv_w <= half and half % HEAD_DIM == 0
    gq = jnp.tile(q_g * (HEAD_DIM ** -0.5 * LOG2E), half // HEAD_DIM).reshape(1, half)
    gk = jnp.tile(k_g, n_kv).reshape(1, kv_w)
    tiles = s_len // ts
    mod_rows = mod3.shape[0] // b_sz
    const2 = lambda b, s: (0, 0)
    body = functools.partial(_qkv_body, n_q=n_q, n_kv=n_kv, half=half)
    return pl.pallas_call(
        body,
        grid=(b_sz, tiles),
        in_specs=[
            pl.BlockSpec((1, ts, d), lambda b, s: (b, s, 0)),
            pl.BlockSpec((1, d), const2),
            pl.BlockSpec((1, 1, d), lambda b, s: (b * mod_rows + scale_idx, 0, 0)),
            pl.BlockSpec((1, 1, d), lambda b, s: (b * mod_rows + shift_idx, 0, 0)),
            pl.BlockSpec((None, d, width), lambda b, s: (layer, 0, 0), pipeline_mode=pl.Buffered(1)),
            pl.BlockSpec((1, half), const2),
            pl.BlockSpec((1, kv_w), const2),
        ],
        out_specs=[
            pl.BlockSpec((ts, width), lambda b, s: (b * tiles + s, 0)),
            pl.BlockSpec((1, ts, d), lambda b, s: (b, s, 0)),
        ],
        out_shape=[
            jax.ShapeDtypeStruct((b_sz * s_len, width), BF16),
            jax.ShapeDtypeStruct((b_sz, s_len, d), BF16),
        ],
        scratch_shapes=[pltpu.VMEM((d, width), BF16), pltpu.VMEM((half, half), BF16)],
        compiler_params=_params(2),
        name="qkv_proj",
    )(x, norm_g.reshape(1, d), mod3, mod3, w_in, gq, gk)


def _cast_weights_once(w_refs, wbf_refs):
    @pl.when(pl.program_id(1) == 0)
    def _():
        for w_ref, wbf_ref in zip(w_refs, wbf_refs):
            wbf_ref[...] = w_ref[...].astype(BF16)


def _glu_proj_body(h_ref, wa_ref, wg_ref, c_ref, wada_ref, bada_ref, *refs):
    n_jobs = (len(refs) - 4) // 2
    job_in, o_ref, mod_ref, job_out = refs[:n_jobs], refs[n_jobs], refs[n_jobs + 1], refs[n_jobs + 2:-2]
    wabf_ref, wgbf_ref = refs[-2:]
    _cast_weights_once([wa_ref, wg_ref], [wabf_ref, wgbf_ref])
    _ada_body(c_ref, wada_ref, bada_ref, mod_ref)
    for src_ref, dst_ref in zip(job_in, job_out):
        dst_ref[...] = src_ref[...].astype(BF16)
    for r0 in range(0, h_ref.shape[0], _MATMUL_ROWS):
        rs = slice(r0, r0 + _MATMUL_ROWS)
        h = h_ref[rs, :]
        a = jnp.dot(h, wabf_ref[...], preferred_element_type=F32)
        g = jnp.dot(h, wgbf_ref[...], preferred_element_type=F32)
        o_ref[rs, :] = (a * _sigmoid(g)).astype(o_ref.dtype)


def _glu_proj(h, w, layer, a_col, g_col, n, tm, tn, c_pad, w_ada, b_row, mod_col, cast_jobs=()):
    m, k = h.shape
    assert a_col % tn == 0 and g_col % tn == 0 and n % tn == 0
    a_blk0, g_blk0 = a_col // tn, g_col // tn
    m_tiles = m // tm
    steps = (n // tn) * m_tiles
    step = lambda j, i: j * m_tiles + i
    mod_n = w_ada.shape[2] - mod_col
    assert mod_n % steps == 0 and mod_col % (mod_n // steps) == 0
    mod_slab = mod_n // steps
    mod_blk0 = mod_col // mod_slab
    c_rows, d_c = c_pad.shape
    job_in, job_out, job_shapes = [], [], []
    for wj, row_block_of_step in cast_jobs:
        rows, cols = wj.shape[1:]
        assert rows % steps == 0
        slab = rows // steps
        job_in.append(pl.BlockSpec((None, slab, cols),
                                   lambda j, i, f=row_block_of_step: (layer, f(step(j, i)), 0)))
        job_out.append(pl.BlockSpec((slab, cols), lambda j, i: (step(j, i), 0)))
        job_shapes.append(jax.ShapeDtypeStruct((rows, cols), BF16))
    return pl.pallas_call(
        _glu_proj_body,
        grid=(n // tn, m_tiles),
        in_specs=[
            pl.BlockSpec((tm, k), lambda j, i: (i, 0)),
            pl.BlockSpec((None, k, tn), lambda j, i: (layer, 0, a_blk0 + j)),
            pl.BlockSpec((None, k, tn), lambda j, i: (layer, 0, g_blk0 + j)),
            pl.BlockSpec((c_rows, d_c), lambda j, i: (0, 0)),
            pl.BlockSpec((None, d_c, mod_slab), lambda j, i: (layer, 0, mod_blk0 + step(j, i))),
            pl.BlockSpec((1, mod_slab), lambda j, i: (0, mod_blk0 + step(j, i))),
        ] + job_in,
        out_specs=[pl.BlockSpec((tm, tn), lambda j, i: (i, j)),
                   pl.BlockSpec((c_rows, mod_slab), lambda j, i: (0, step(j, i)))] + job_out,
        out_shape=[jax.ShapeDtypeStruct((m, n), BF16),
                   jax.ShapeDtypeStruct((c_rows, mod_n), F32)] + job_shapes,
        scratch_shapes=[pltpu.VMEM((k, tn), BF16), pltpu.VMEM((k, tn), BF16)],
        compiler_params=_params(2),
        name="glu_proj",
    )(h, w, w, c_pad, w_ada, b_row, *[wj for wj, _ in cast_jobs])


_EPILOGUE_LANES = 512
_MATMUL_ROWS = 1024


def _merge_mix_body(ya_ref, yc_ref, wa_ref, wc_ref, ga_ref, gc_ref, w_ref, x_ref, gt_ref, g_ref, sc_ref, sh_ref,
                    x1_ref, h_ref, merged_ref):
    d = w_ref.shape[1]
    ya = ya_ref[...]
    yc = yc_ref[...]
    for n0 in range(0, d, _EPILOGUE_LANES):
        ns = slice(n0, n0 + _EPILOGUE_LANES)
        acc_a = jnp.dot(ya, wa_ref[:, ns], preferred_element_type=F32)
        acc_c = jnp.dot(yc, wc_ref[:, ns], preferred_element_type=F32)
        ga = _sigmoid(ga_ref[:, ns].astype(F32))
        gc = _sigmoid(gc_ref[:, ns].astype(F32))
        merged_ref[:, ns] = (ga * acc_a + gc * acc_c).astype(merged_ref.dtype)
    a = merged_ref[...]
    sumsq = jnp.zeros((a.shape[0], 1), F32)
    for n0 in range(0, d, _EPILOGUE_LANES):
        ns = slice(n0, n0 + _EPILOGUE_LANES)
        acc = jnp.dot(a, w_ref[:, ns], preferred_element_type=F32)
        x1 = x_ref[0, :, ns] + gt_ref[0, :, ns] * acc
        x1_ref[0, :, ns] = x1
        sumsq = sumsq + jnp.sum(x1 * x1, axis=-1, keepdims=True)
    inv = lax.rsqrt(sumsq / d + EPS)
    gain = g_ref[...] * (1.0 + sc_ref[0])
    h_ref[0] = (x1_ref[0] * inv * gain + sh_ref[0]).astype(h_ref.dtype)


def _merge_mix(attn, conv, wa_bf, wc_bf, gates, ga_col, gc_col, wmix_bf, x, mod3, gate_idx, norm_g,
               scale_idx, shift_idx, ts=512):
    b_sz, s_len, d = x.shape
    ka, kc = attn.shape[1], conv.shape[1]
    assert d % _EPILOGUE_LANES == 0 and ga_col % d == 0 and gc_col % d == 0
    tiles = s_len // ts
    const2 = lambda b, s: (0, 0)
    rows = lambda width, col_blk: pl.BlockSpec((ts, width), lambda b, s: (b * tiles + s, col_blk))
    mod_rows = mod3.shape[0] // b_sz
    mod_row = lambda idx: pl.BlockSpec((1, 1, d), lambda b, s: (b * mod_rows + idx, 0, 0))
    act = pl.BlockSpec((1, ts, d), lambda b, s: (b, s, 0))
    resident = pl.Buffered(1)
    return pl.pallas_call(
        _merge_mix_body,
        grid=(b_sz, tiles),
        in_specs=[
            rows(ka, 0),
            rows(kc, 0),
            pl.BlockSpec((ka, d), const2, pipeline_mode=resident),
            pl.BlockSpec((kc, d), const2, pipeline_mode=resident),
            rows(d, ga_col // d),
            rows(d, gc_col // d),
            pl.BlockSpec((d, d), const2, pipeline_mode=resident),
            act,
            mod_row(gate_idx),
            pl.BlockSpec((1, d), const2),
            mod_row(scale_idx),
            mod_row(shift_idx),
        ],
        out_specs=[act, act],
        out_shape=[jax.ShapeDtypeStruct((b_sz, s_len, d), F32), jax.ShapeDtypeStruct((b_sz, s_len, d), BF16)],
        scratch_shapes=[pltpu.VMEM((ts, d), BF16)],
        compiler_params=_params(2),
        name="merge_mix",
    )(attn, conv, wa_bf, wc_bf, gates, gates, wmix_bf, x, mod3, norm_g.reshape(1, d), mod3, mod3)


def _residual_proj_body(a_ref, w_ref, x_ref, gt_ref, o_ref):
    a = a_ref[...]
    for n0 in range(0, o_ref.shape[1], _EPILOGUE_LANES):
        ns = slice(n0, n0 + _EPILOGUE_LANES)
        acc = jnp.dot(a, w_ref[:, ns], preferred_element_type=F32)
        o_ref[:, ns] = x_ref[:, ns] + gt_ref[0, :, ns] * acc


def _residual_proj(a, w_bf, x2d, mod3, gate_idx, rows_per_batch, tm, tn):
    m, k = a.shape
    n = w_bf.shape[1]
    assert n % tn == 0 and tn % _EPILOGUE_LANES == 0 and rows_per_batch % tm == 0
    tiles_per_batch = rows_per_batch // tm
    mod_rows = mod3.shape[0] // (m // rows_per_batch)
    return pl.pallas_call(
        _residual_proj_body,
        grid=(n // tn, m // tm),
        in_specs=[
            pl.BlockSpec((tm, k), lambda j, i: (i, 0)),
            pl.BlockSpec((k, tn), lambda j, i: (0, j)),
            pl.BlockSpec((tm, tn), lambda j, i: (i, j)),
            pl.BlockSpec((1, 1, tn), lambda j, i: ((i // tiles_per_batch) * mod_rows + gate_idx, 0, j)),
        ],
        out_specs=pl.BlockSpec((tm, tn), lambda j, i: (i, j)),
        out_shape=jax.ShapeDtypeStruct((m, n), F32),
        compiler_params=_params(2),
        name="residual_proj",
    )(a, w_bf, x2d, mod3)


def _ffn_in_body(a_ref, wg_ref, wu_ref, wo_ref, o_ref, wobf_ref, wgbf_ref, wubf_ref):
    _cast_weights_once([wg_ref, wu_ref], [wgbf_ref, wubf_ref])
    wobf_ref[...] = wo_ref[...].astype(BF16)
    for r0 in range(0, a_ref.shape[0], _MATMUL_ROWS):
        rs = slice(r0, r0 + _MATMUL_ROWS)
        a = a_ref[rs, :]
        gate = jnp.dot(a, wgbf_ref[...], preferred_element_type=F32)
        up = jnp.dot(a, wubf_ref[...], preferred_element_type=F32)
        o_ref[rs, :] = (gate * _sigmoid(gate) * up).astype(o_ref.dtype)


def _ffn_in(h, w_ffn_in, w_ffn_out, layer, tm=2048, tn=512):
    m, k = h.shape
    d_ff = w_ffn_in.shape[2] // 2
    d_out = w_ffn_out.shape[2]
    up_blk = d_ff // tn
    m_tiles = m // tm
    steps = (d_ff // tn) * m_tiles
    assert d_ff % steps == 0
    slab = d_ff // steps
    step = lambda j, i: j * m_tiles + i
    return pl.pallas_call(
        _ffn_in_body,
        grid=(d_ff // tn, m_tiles),
        in_specs=[
            pl.BlockSpec((tm, k), lambda j, i: (i, 0)),
            pl.BlockSpec((None, k, tn), lambda j, i: (layer, 0, j)),
            pl.BlockSpec((None, k, tn), lambda j, i: (layer, 0, up_blk + j)),
            pl.BlockSpec((None, slab, d_out), lambda j, i: (layer, step(j, i), 0)),
        ],
        out_specs=[
            pl.BlockSpec((tm, tn), lambda j, i: (i, j)),
            pl.BlockSpec((slab, d_out), lambda j, i: (step(j, i), 0)),
        ],
        out_shape=[jax.ShapeDtypeStruct((m, d_ff), BF16), jax.ShapeDtypeStruct((d_ff, d_out), BF16)],
        scratch_shapes=[pltpu.VMEM((k, tn), BF16), pltpu.VMEM((k, tn), BF16)],
        compiler_params=_params(2),
        name="ffn_in",
    )(h, w_ffn_in, w_ffn_in, w_ffn_out)


def _t5_causal_bucket(dist):
    n = jnp.maximum(dist, 0)
    nf = jnp.maximum(n, 1).astype(jnp.float32)
    large = MAX_EXACT + (jnp.log(nf / MAX_EXACT) / math.log(MAX_DISTANCE / MAX_EXACT)
                         * (NUM_BUCKETS - MAX_EXACT)).astype(jnp.int32)
    large = jnp.minimum(large, NUM_BUCKETS - 1)
    return jnp.where(n < MAX_EXACT, n, large)


def _band_buckets():
    q_off = jnp.arange(BLOCK)
    k_off = jnp.arange(2 * BLOCK)
    dist = q_off[:, None] + BLOCK - k_off[None, :]
    allowed = (dist >= 0) & (dist < WINDOW)
    return jnp.where(allowed, _t5_causal_bucket(dist), -1).astype(jnp.int32)


def _attn_build_bias(first_step, rb_ref, sink_ref, bucket_ref, bias_ref, *, n_q, n_kv):
    q_per_kv = n_q // n_kv

    @pl.when(first_step)
    def _():
        bucket = bucket_ref[...]
        col = lax.broadcasted_iota(jnp.int32, bucket.shape, 1)
        for slot in range(n_q):
            head = _orig_head(slot, n_kv, q_per_kv)
            tile = jnp.zeros(bucket.shape, F32)
            for b in range(NUM_BUCKETS):
                tile = jnp.where(bucket == b, rb_ref[b * n_q + head] * LOG2E, tile)
            tile = jnp.where(bucket < 0, -jnp.inf, tile)
            sink = sink_ref[head] * LOG2E
            bias_ref[0, slot] = jnp.where(col == 0, sink, tile)
            bias_ref[1, slot] = jnp.where(col == 0, sink, jnp.where(col < BLOCK, -jnp.inf, tile))


def _attn_block(seq_start, qmask_ref, krow_ref, q_ref, kc_ref, kp_ref, vc_ref, vp_ref, o_ref,
                bias_ref, lg_ref, p_ref, pv_ref, rs_ref, *, n_q, n_kv):
    q_per_kv = n_q // n_kv
    kv_w = n_kv * HEAD_DIM
    lane_slot = lax.broadcasted_iota(jnp.int32, (BLOCK, BLOCK), 1) // HEAD_DIM
    ones = jnp.ones((2 * BLOCK, BLOCK), BF16)
    bias_idx = jnp.where(seq_start, 1, 0)

    keys = jnp.concatenate([kp_ref[...], kc_ref[...]], axis=0) * krow_ref[...]
    vals = jnp.concatenate([vp_ref[...], vc_ref[...]], axis=0) * krow_ref[...]
    lhs = jnp.concatenate(
        [q_ref[:, j * kv_w:(j + 1) * kv_w] * qmask_ref[g]
         for j in range(q_per_kv) for g in range(n_kv)], axis=0)
    lg_ref[...] = lax.dot_general(lhs, keys, (((1,), (1,)), ((), ())),
                                  preferred_element_type=F32).reshape(n_q, BLOCK, 2 * BLOCK)
    for slot in range(n_q):
        logit = lg_ref[slot] + bias_ref[bias_idx, slot]
        m = jnp.max(logit, axis=-1, keepdims=True)
        p_ref[slot * BLOCK:(slot + 1) * BLOCK, :] = jnp.exp2(logit - m).astype(BF16)
    p = p_ref[...]
    pv_ref[...] = jnp.dot(p, vals, preferred_element_type=F32).reshape(n_q, BLOCK, kv_w)
    rs_ref[...] = jnp.dot(p, ones, preferred_element_type=F32).reshape(n_q, BLOCK, BLOCK)
    per_half = BLOCK // HEAD_DIM
    for j in range(q_per_kv):
        for half in range(kv_w // BLOCK):
            lanes = slice(half * BLOCK, (half + 1) * BLOCK)
            slots = [j * n_kv + half * per_half + i for i in range(per_half)]
            num = pv_ref[slots[-1], :, lanes]
            den = rs_ref[slots[-1]]
            for i in range(per_half - 2, -1, -1):
                num = jnp.where(lane_slot == i, pv_ref[slots[i], :, lanes], num)
                den = jnp.where(lane_slot == i, rs_ref[slots[i]], den)
            o_ref[:, j * kv_w + half * BLOCK:j * kv_w + (half + 1) * BLOCK] = (
                num * (1.0 / den)).astype(o_ref.dtype)


_CONV_HALO = 32
_CONV_LANES = 256
_SUBLANES = 8


def _conv_zero_history(seq_start, u_ref):
    @pl.when(seq_start)
    def _():
        u_ref[:_CONV_HALO, :] = jnp.zeros((_CONV_HALO, u_ref.shape[1]), u_ref.dtype)


def _conv_shift_matrix(rows):
    i = jnp.arange(_SUBLANES * rows) % rows
    r = jnp.arange(_SUBLANES * rows) // rows
    return (jnp.arange(rows)[None, :] == (i + r)[:, None]).astype(BF16)


def _conv_block(uin_ref, shift_ref, w_ref, cb_ref, lg_ref, lb_ref, o_ref, u_ref, sh_ref, y_ref):
    width = w_ref.shape[0]
    ts, channels = o_ref.shape
    u_ref[_CONV_HALO:, :] = uin_ref[...]
    sh_ref[...] = jnp.dot(shift_ref[...], u_ref[...], preferred_element_type=F32).reshape(sh_ref.shape)
    first = _CONV_HALO - (width - 1)
    unknown_zero = jnp.minimum(pl.program_id(0), 0)
    for c0 in range(0, channels, _CONV_LANES):
        cs = slice(c0, c0 + _CONV_LANES)
        acc = jnp.broadcast_to(cb_ref[:, cs], (ts, _CONV_LANES))
        for j in range(width):
            tile, r = divmod(first + j, _SUBLANES)
            rows = pl.ds(pl.multiple_of(tile * _SUBLANES + unknown_zero, _SUBLANES), ts)
            acc = acc + sh_ref[r, rows, cs] * w_ref[j:j + 1, cs]
        y_ref[:, cs] = acc
    y = y_ref[...]
    mu = jnp.mean(y, axis=-1, keepdims=True)
    yc = y - mu
    var = jnp.mean(yc * yc, axis=-1, keepdims=True)
    z = yc * lax.rsqrt(var + EPS) * lg_ref[...] + lb_ref[...]
    o_ref[...] = (z * _sigmoid(z)).astype(o_ref.dtype)
    u_ref[:_CONV_HALO, :] = u_ref[ts:, :]


def _core_body(h_ref, w_ref,
               uin_ref, shift_ref, cw_ref, cb_ref, lg_ref, lb_ref,
               rb_ref, sink_ref, bucket_ref, qmask_ref, krow_ref, q_ref, kc_ref, kp_ref, vc_ref, vp_ref,
               gates_ref, yconv_ref, yattn_ref,
               wbf_ref, u_ref, sh_ref, y_ref, bias_ref, lgt_ref, p_ref, pv_ref, rs_ref,
               *, n_q, n_kv, blocks_per_seq):
    t = pl.program_id(0) * pl.num_programs(1) + pl.program_id(1)
    seq_start = t % blocks_per_seq == 0
    _cast_weights_once([w_ref], [wbf_ref])
    _conv_zero_history(seq_start, u_ref)
    _attn_build_bias(t == 0, rb_ref, sink_ref, bucket_ref, bias_ref, n_q=n_q, n_kv=n_kv)

    _conv_block(uin_ref, shift_ref, cw_ref, cb_ref, lg_ref, lb_ref, yconv_ref, u_ref, sh_ref, y_ref)
    acc = jnp.dot(h_ref[...], wbf_ref[...], preferred_element_type=F32)
    gates_ref[...] = acc.astype(gates_ref.dtype)
    _attn_block(seq_start, qmask_ref, krow_ref, q_ref, kc_ref, kp_ref, vc_ref, vp_ref, yattn_ref,
                bias_ref, lgt_ref, p_ref, pv_ref, rs_ref, n_q=n_q, n_kv=n_kv)


def _mixer_core(h, w_in, layer, gate_col, conv_in, qkv, conv_w, conv_b, ln_g, ln_b, rel_bias, sinks,
                s_len, n_q, n_kv, tn=512):
    m, k = h.shape
    n = w_in.shape[2] - gate_col
    width, channels = conv_w.shape
    attn_w, kv_w = n_q * HEAD_DIM, n_kv * HEAD_DIM
    n_blocks = m // BLOCK
    n_tiles = n // tn
    assert gate_col % tn == 0 and n % tn == 0 and n_blocks % n_tiles == 0
    assert s_len % BLOCK == 0 and width - 1 <= _CONV_HALO and _CONV_HALO % _SUBLANES == 0
    assert attn_w % kv_w == 0 and kv_w == 2 * BLOCK and BLOCK % HEAD_DIM == 0
    m_tiles = n_blocks // n_tiles
    tm = m // m_tiles
    col_blk0 = gate_col // tn
    k_blk = attn_w // kv_w
    v_blk = k_blk + 1

    step = lambda j, i: j * m_tiles + i
    prev_step = lambda j, i: jnp.maximum(step(j, i) - 1, 0)
    const2 = lambda j, i: (0, 0)
    row = lambda v: v.reshape(1, channels)
    lane_group = jnp.arange(kv_w) // HEAD_DIM
    qmask = jnp.broadcast_to((lane_group[None, :] == jnp.arange(n_kv)[:, None])[:, None, :],
                             (n_kv, BLOCK, kv_w)).astype(BF16)
    krow = jnp.broadcast_to((jnp.arange(2 * BLOCK) > 0)[:, None], (2 * BLOCK, kv_w)).astype(BF16)
    body = functools.partial(_core_body, n_q=n_q, n_kv=n_kv, blocks_per_seq=s_len // BLOCK)
    return pl.pallas_call(
        body,
        grid=(n_tiles, m_tiles),
        in_specs=[
            pl.BlockSpec((tm, k), lambda j, i: (i, 0)),
            pl.BlockSpec((None, k, tn), lambda j, i: (layer, 0, col_blk0 + j)),
            pl.BlockSpec((BLOCK, channels), lambda j, i: (step(j, i), 0)),
            pl.BlockSpec((_SUBLANES * (_CONV_HALO + BLOCK), _CONV_HALO + BLOCK), const2),
            pl.BlockSpec((width, channels), const2),
            pl.BlockSpec((1, channels), const2),
            pl.BlockSpec((1, channels), const2),
            pl.BlockSpec((1, channels), const2),
            pl.BlockSpec(memory_space=pltpu.SMEM),
            pl.BlockSpec(memory_space=pltpu.SMEM),
            pl.BlockSpec((BLOCK, 2 * BLOCK), const2),
            pl.BlockSpec((n_kv, BLOCK, kv_w), lambda j, i: (0, 0, 0)),
            pl.BlockSpec((2 * BLOCK, kv_w), const2),
            pl.BlockSpec((BLOCK, attn_w), lambda j, i: (step(j, i), 0)),
            pl.BlockSpec((BLOCK, kv_w), lambda j, i: (step(j, i), k_blk)),
            pl.BlockSpec((BLOCK, kv_w), lambda j, i: (prev_step(j, i), k_blk)),
            pl.BlockSpec((BLOCK, kv_w), lambda j, i: (step(j, i), v_blk)),
            pl.BlockSpec((BLOCK, kv_w), lambda j, i: (prev_step(j, i), v_blk)),
        ],
        out_specs=[
            pl.BlockSpec((tm, tn), lambda j, i: (i, j)),
            pl.BlockSpec((BLOCK, channels), lambda j, i: (step(j, i), 0)),
            pl.BlockSpec((BLOCK, attn_w), lambda j, i: (step(j, i), 0)),
        ],
        out_shape=[
            jax.ShapeDtypeStruct((m, n), BF16),
            jax.ShapeDtypeStruct((m, channels), BF16),
            jax.ShapeDtypeStruct((m, attn_w), BF16),
        ],
        scratch_shapes=[
            pltpu.VMEM((k, tn), BF16),
            pltpu.VMEM((_CONV_HALO + BLOCK, channels), BF16),
            pltpu.VMEM((_SUBLANES, _CONV_HALO + BLOCK, channels), F32),
            pltpu.VMEM((BLOCK, channels), F32),
            pltpu.VMEM((2, n_q, BLOCK, 2 * BLOCK), F32),
            pltpu.VMEM((n_q, BLOCK, 2 * BLOCK), F32),
            pltpu.VMEM((n_q * BLOCK, 2 * BLOCK), BF16),
            pltpu.VMEM((n_q, BLOCK, kv_w), F32),
            pltpu.VMEM((n_q, BLOCK, BLOCK), F32),
        ],
        compiler_params=_params(2),
        name="mixer_core",
    )(h, w_in,
      conv_in, _conv_shift_matrix(_CONV_HALO + BLOCK), conv_w, row(conv_b), row(ln_g), row(ln_b),
      rel_bias.astype(F32).reshape(-1), sinks.astype(F32), _band_buckets(), qmask, krow,
      qkv, qkv, qkv, qkv, qkv)


def kernel(x, c, w_ada, b_ada, norm_mix_g, w_in, q_norm_g, k_norm_g, attn_sinks, rel_bias,
           w_attn_out, conv_w, conv_b, conv_ln_g, conv_ln_b, w_conv_out, w_mix_out,
           norm_ffn_g, w_ffn_in, w_ffn_out):
    b_sz, s_len, d = x.shape
    depth = w_ada.shape[0]
    n_q = attn_sinks.shape[1]
    attn_w = w_attn_out.shape[1]
    channels = conv_w.shape[2]
    kv_w = (w_in.shape[2] - attn_w - 2 * channels - 2 * d) // 2
    n_kv = kv_w // HEAD_DIM
    conv_col = attn_w + 2 * kv_w
    gate_col = conv_col + 2 * channels
    m = b_sz * s_len

    for l in range(depth):
        c_pad = jnp.pad(c, ((0, 8 - b_sz), (0, 0)))
        b_row = b_ada[l].reshape(1, N_MOD * d)
        mod_early = _ada(c_pad, w_ada, b_row, l, N_MOD_EARLY * d)[:b_sz].reshape(b_sz * N_MOD_EARLY, 1, d)

        qkv, h = _qkv_proj(x, norm_mix_g[l], mod_early, 1, 0, w_in, l, q_norm_g[l], k_norm_g[l], n_q, n_kv)
        h = h.reshape(m, d)
        proj_tm, proj_tn = 1024, 512
        assert (channels // proj_tn) * (m // proj_tm) == n_q
        slot_head = lambda slot: _orig_head(slot, n_kv, n_q // n_kv)
        same = lambda step: step
        conv_in, mod_late, wa_bf, wc_bf, wmix_bf = _glu_proj(
            h, w_in, l, conv_col, conv_col + channels, channels, proj_tm, proj_tn,
            c_pad, w_ada, b_row, N_MOD_EARLY * d,
            cast_jobs=((w_attn_out, slot_head), (w_conv_out, same), (w_mix_out, same)))
        mod_late = mod_late[:b_sz].reshape(b_sz * (N_MOD - N_MOD_EARLY), 1, d)
        gates, y_conv, y_attn = _mixer_core(h, w_in, l, gate_col, conv_in, qkv, conv_w[l], conv_b[l],
                                            conv_ln_g[l], conv_ln_b[l], rel_bias, attn_sinks[l],
                                            s_len, n_q, n_kv)
        x1, h = _merge_mix(y_attn, y_conv, wa_bf, wc_bf, gates, 0, d, wmix_bf, x, mod_late, 0,
                           norm_ffn_g[l], 2, 1)

        act, wo_bf = _ffn_in(h.reshape(m, d), w_ffn_in, w_ffn_out, l)
        x = _residual_proj(act, wo_bf, x1.reshape(m, d), mod_late, 3, s_len,
                           tm=512, tn=1024).reshape(b_sz, s_len, d)
    return x
```

```python
import functools
import math

import jax
import jax.numpy as jnp
from jax import lax
from jax.experimental import pallas as pl
from jax.experimental.pallas import tpu as pltpu

F32 = jnp.float32
BF16 = jnp.bfloat16

HEAD_DIM = 64
WINDOW = 128
BLOCK = 128
NUM_BUCKETS = 32
MAX_EXACT = NUM_BUCKETS // 2
MAX_DISTANCE = 128
N_MOD = 6
N_MOD_EARLY = 2
EPS = 1e-6
LOG2E = math.log2(math.e)

V7X_VMEM_LIMIT_BYTES = 56 * 1024 * 1024


def _params(n_axes):
    return pltpu.CompilerParams(
        dimension_semantics=("arbitrary",) * n_axes,
        vmem_limit_bytes=V7X_VMEM_LIMIT_BYTES,
    )


def _sigmoid(x):
    return jax.nn.sigmoid(x)


def _orig_head(slot, n_kv, q_per_kv):
    return (slot % n_kv) * q_per_kv + slot // n_kv


def _ada_body(c_ref, w_ref, b_ref, o_ref):
    c = c_ref[...]
    act = (c * _sigmoid(c)).astype(BF16)
    o_ref[...] = jnp.dot(act, w_ref[...].astype(BF16), preferred_element_type=F32) + b_ref[...]


def _ada(c_pad, w_ada, b_row, layer, n, tn=2048):
    rows, d = c_pad.shape
    assert n % tn == 0
    return pl.pallas_call(
        _ada_body,
        grid=(n // tn,),
        in_specs=[
            pl.BlockSpec((rows, d), lambda j: (0, 0)),
            pl.BlockSpec((None, d, tn), lambda j: (layer, 0, j)),
            pl.BlockSpec((1, tn), lambda j: (0, j)),
        ],
        out_specs=pl.BlockSpec((rows, tn), lambda j: (0, j)),
        out_shape=jax.ShapeDtypeStruct((rows, n), F32),
        compiler_params=_params(1),
        name="ada_mod",
    )(c_pad, w_ada, b_row)


def _norm_mod_body(x_ref, g_ref, sc_ref, sh_ref, o_ref):
    x = x_ref[0]
    ms = jnp.mean(x * x, axis=-1, keepdims=True)
    gain = g_ref[...] * (1.0 + sc_ref[0])
    o_ref[0] = (x * lax.rsqrt(ms + EPS) * gain + sh_ref[0]).astype(o_ref.dtype)


def _qkv_body(x_ref, g_ref, sc_ref, sh_ref, w_ref, gq_ref, gk_ref, o_ref, h_ref, wbf_ref, seg_ref,
              *, n_q, n_kv, half):
    attn_w = n_q * HEAD_DIM
    kv_w = n_kv * HEAD_DIM
    _norm_mod_body(x_ref, g_ref, sc_ref, sh_ref, h_ref)

    @pl.when(jnp.logical_and(pl.program_id(0) == 0, pl.program_id(1) == 0))
    def _():
        for slot in range(n_q):
            src = _orig_head(slot, n_kv, n_q // n_kv) * HEAD_DIM
            wbf_ref[:, slot * HEAD_DIM:(slot + 1) * HEAD_DIM] = w_ref[:, src:src + HEAD_DIM].astype(BF16)
        wbf_ref[:, attn_w:] = w_ref[:, attn_w:].astype(BF16)
        r = lax.broadcasted_iota(jnp.int32, (half, half), 0) // HEAD_DIM
        c = lax.broadcasted_iota(jnp.int32, (half, half), 1) // HEAD_DIM
        seg_ref[...] = (r == c).astype(BF16)

    a = h_ref[0]

    def head_norm(acc, seg, gain):
        ss = jnp.dot((acc * acc).astype(BF16), seg, preferred_element_type=F32)
        return acc * lax.rsqrt(ss / HEAD_DIM + EPS) * gain

    for c0 in range(0, attn_w, half):
        acc = jnp.dot(a, wbf_ref[:, c0:c0 + half], preferred_element_type=F32)
        o_ref[:, c0:c0 + half] = head_norm(acc, seg_ref[...], gq_ref[...]).astype(o_ref.dtype)
    acc = jnp.dot(a, wbf_ref[:, attn_w:], preferred_element_type=F32)
    o_ref[:, attn_w:attn_w + kv_w] = head_norm(acc[:, :kv_w], seg_ref[:kv_w, :kv_w],
                                               gk_ref[...]).astype(o_ref.dtype)
    o_ref[:, attn_w + kv_w:] = acc[:, kv_w:].astype(o_ref.dtype)


def _qkv_proj(x, norm_g, mod3, scale_idx, shift_idx, w_in, layer, q_g, k_g, n_q, n_kv, ts=512, half=512):
    b_sz, s_len, d = x.shape
    attn_w, kv_w = n_q * HEAD_DIM, n_kv * HEAD_DIM
    width = attn_w + 2 * kv_w
    assert attn_w % half == 0 and kv_w <= half and half % HEAD_DIM == 0
    gq = jnp.tile(q_g * (HEAD_DIM ** -0.5 * LOG2E), half // HEAD_DIM).reshape(1, half)
    gk = jnp.tile(k_g, n_kv).reshape(1, kv_w)
    tiles = s_len // ts
    mod_rows = mod3.shape[0] // b_sz
    const2 = lambda b, s: (0, 0)
    body = functools.partial(_qkv_body, n_q=n_q, n_kv=n_kv, half=half)
    return pl.pallas_call(
        body,
        grid=(b_sz, tiles),
        in_specs=[
            pl.BlockSpec((1, ts, d), lambda b, s: (b, s, 0)),
            pl.BlockSpec((1, d), const2),
            pl.BlockSpec((1, 1, d), lambda b, s: (b * mod_rows + scale_idx, 0, 0)),
            pl.BlockSpec((1, 1, d), lambda b, s: (b * mod_rows + shift_idx, 0, 0)),
            pl.BlockSpec((None, d, width), lambda b, s: (layer, 0, 0), pipeline_mode=pl.Buffered(1)),
            pl.BlockSpec((1, half), const2),
            pl.BlockSpec((1, kv_w), const2),
        ],
        out_specs=[
            pl.BlockSpec((ts, width), lambda b, s: (b * tiles + s, 0)),
            pl.BlockSpec((1, ts, d), lambda b, s: (b, s, 0)),
        ],
        out_shape=[
            jax.ShapeDtypeStruct((b_sz * s_len, width), BF16),
            jax.ShapeDtypeStruct((b_sz, s_len, d), BF16),
        ],
        scratch_shapes=[pltpu.VMEM((d, width), BF16), pltpu.VMEM((half, half), BF16)],
        compiler_params=_params(2),
        name="qkv_proj",
    )(x, norm_g.reshape(1, d), mod3, mod3, w_in, gq, gk)


def _cast_weights_once(w_refs, wbf_refs):
    @pl.when(pl.program_id(1) == 0)
    def _():
        for w_ref, wbf_ref in zip(w_refs, wbf_refs):
            wbf_ref[...] = w_ref[...].astype(BF16)


def _glu_proj_body(h_ref, wa_ref, wg_ref, c_ref, wada_ref, bada_ref, *refs):
    n_jobs = (len(refs) - 4) // 2
    job_in, o_ref, mod_ref, job_out = refs[:n_jobs], refs[n_jobs], refs[n_jobs + 1], refs[n_jobs + 2:-2]
    wabf_ref, wgbf_ref = refs[-2:]
    _cast_weights_once([wa_ref, wg_ref], [wabf_ref, wgbf_ref])
    _ada_body(c_ref, wada_ref, bada_ref, mod_ref)
    for src_ref, dst_ref in zip(job_in, job_out):
        dst_ref[...] = src_ref[...].astype(BF16)
    for r0 in range(0, h_ref.shape[0], _MATMUL_ROWS):
        rs = slice(r0, r0 + _MATMUL_ROWS)
        h = h_ref[rs, :]
        a = jnp.dot(h, wabf_ref[...], preferred_element_type=F32)
        g = jnp.dot(h, wgbf_ref[...], preferred_element_type=F32)
        o_ref[rs, :] = (a * _sigmoid(g)).astype(o_ref.dtype)


def _glu_proj(h, w, layer, a_col, g_col, n, tm, tn, c_pad, w_ada, b_row, mod_col, cast_jobs=()):
    m, k = h.shape
    assert a_col % tn == 0 and g_col % tn == 0 and n % tn == 0
    a_blk0, g_blk0 = a_col // tn, g_col // tn
    m_tiles = m // tm
    steps = (n // tn) * m_tiles
    step = lambda j, i: j * m_tiles + i
    mod_n = w_ada.shape[2] - mod_col
    assert mod_n % steps == 0 and mod_col % (mod_n // steps) == 0
    mod_slab = mod_n // steps
    mod_blk0 = mod_col // mod_slab
    c_rows, d_c = c_pad.shape
    job_in, job_out, job_shapes = [], [], []
    for wj, row_block_of_step in cast_jobs:
        rows, cols = wj.shape[1:]
        assert rows % steps == 0
        slab = rows // steps
        job_in.append(pl.BlockSpec((None, slab, cols),
                                   lambda j, i, f=row_block_of_step: (layer, f(step(j, i)), 0)))
        job_out.append(pl.BlockSpec((slab, cols), lambda j, i: (step(j, i), 0)))
        job_shapes.append(jax.ShapeDtypeStruct((rows, cols), BF16))
    return pl.pallas_call(
        _glu_proj_body,
        grid=(n // tn, m_tiles),
        in_specs=[
            pl.BlockSpec((tm, k), lambda j, i: (i, 0)),
            pl.BlockSpec((None, k, tn), lambda j, i: (layer, 0, a_blk0 + j)),
            pl.BlockSpec((None, k, tn), lambda j, i: (layer, 0, g_blk0 + j)),
            pl.BlockSpec((c_rows, d_c), lambda j, i: (0, 0)),
            pl.BlockSpec((None, d_c, mod_slab), lambda j, i: (layer, 0, mod_blk0 + step(j, i))),
            pl.BlockSpec((1, mod_slab), lambda j, i: (0, mod_blk0 + step(j, i))),
        ] + job_in,
        out_specs=[pl.BlockSpec((tm, tn), lambda j, i: (i, j)),
                   pl.BlockSpec((c_rows, mod_slab), lambda j, i: (0, step(j, i)))] + job_out,
        out_shape=[jax.ShapeDtypeStruct((m, n), BF16),
                   jax.ShapeDtypeStruct((c_rows, mod_n), F32)] + job_shapes,
        scratch_shapes=[pltpu.VMEM((k, tn), BF16), pltpu.VMEM((k, tn), BF16)],
        compiler_params=_params(2),
        name="glu_proj",
    )(h, w, w, c_pad, w_ada, b_row, *[wj for wj, _ in cast_jobs])


_EPILOGUE_LANES = 512
_MATMUL_ROWS = 1024


def _merge_mix_body(ya_ref, yc_ref, wa_ref, wc_ref, ga_ref, gc_ref, w_ref, x_ref, gt_ref, g_ref, sc_ref, sh_ref,
                    x1_ref, h_ref, merged_ref):
    d = w_ref.shape[1]
    ya = ya_ref[...]
    yc = yc_ref[...]
    for n0 in range(0, d, _EPILOGUE_LANES):
        ns = slice(n0, n0 + _EPILOGUE_LANES)
        acc_a = jnp.dot(ya, wa_ref[:, ns], preferred_element_type=F32)
        acc_c = jnp.dot(yc, wc_ref[:, ns], preferred_element_type=F32)
        ga = _sigmoid(ga_ref[:, ns].astype(F32))
        gc = _sigmoid(gc_ref[:, ns].astype(F32))
        merged_ref[:, ns] = (ga * acc_a + gc * acc_c).astype(merged_ref.dtype)
    a = merged_ref[...]
    sumsq = jnp.zeros((a.shape[0], 1), F32)
    for n0 in range(0, d, _EPILOGUE_LANES):
        ns = slice(n0, n0 + _EPILOGUE_LANES)
        acc = jnp.dot(a, w_ref[:, ns], preferred_element_type=F32)
        x1 = x_ref[0, :, ns] + gt_ref[0, :, ns] * acc
        x1_ref[0, :, ns] = x1
        sumsq = sumsq + jnp.sum(x1 * x1, axis=-1, keepdims=True)
    inv = lax.rsqrt(sumsq / d + EPS)
    gain = g_ref[...] * (1.0 + sc_ref[0])
    h_ref[0] = (x1_ref[0] * inv * gain + sh_ref[0]).astype(h_ref.dtype)


def _merge_mix(attn, conv, wa_bf, wc_bf, gates, ga_col, gc_col, wmix_bf, x, mod3, gate_idx, norm_g,
               scale_idx, shift_idx, ts=256):
    b_sz, s_len, d = x.shape
    ka, kc = attn.shape[1], conv.shape[1]
    assert d % _EPILOGUE_LANES == 0 and ga_col % d == 0 and gc_col % d == 0
    tiles = s_len // ts
    const2 = lambda b, s: (0, 0)
    rows = lambda width, col_blk: pl.BlockSpec((ts, width), lambda b, s: (b * tiles + s, col_blk))
    mod_rows = mod3.shape[0] // b_sz
    mod_row = lambda idx: pl.BlockSpec((1, 1, d), lambda b, s: (b * mod_rows + idx, 0, 0))
    act = pl.BlockSpec((1, ts, d), lambda b, s: (b, s, 0))
    resident = pl.Buffered(1)
    return pl.pallas_call(
        _merge_mix_body,
        grid=(b_sz, tiles),
        in_specs=[
            rows(ka, 0),
            rows(kc, 0),
            pl.BlockSpec((ka, d), const2, pipeline_mode=resident),
            pl.BlockSpec((kc, d), const2, pipeline_mode=resident),
            rows(d, ga_col // d),
            rows(d, gc_col // d),
            pl.BlockSpec((d, d), const2, pipeline_mode=resident),
            act,
            mod_row(gate_idx),
            pl.BlockSpec((1, d), const2),
            mod_row(scale_idx),
            mod_row(shift_idx),
        ],
        out_specs=[act, act],
        out_shape=[jax.ShapeDtypeStruct((b_sz, s_len, d), F32), jax.ShapeDtypeStruct((b_sz, s_len, d), BF16)],
        scratch_shapes=[pltpu.VMEM((ts, d), BF16)],
        compiler_params=_params(2),
        name="merge_mix",
    )(attn, conv, wa_bf, wc_bf, gates, gates, wmix_bf, x, mod3, norm_g.reshape(1, d), mod3, mod3)


def _residual_proj_body(a_ref, w_ref, x_ref, gt_ref, o_ref):
    a = a_ref[...]
    for n0 in range(0, o_ref.shape[1], _EPILOGUE_LANES):
        ns = slice(n0, n0 + _EPILOGUE_LANES)
        acc = jnp.dot(a, w_ref[:, ns], preferred_element_type=F32)
        o_ref[:, ns] = x_ref[:, ns] + gt_ref[0, :, ns] * acc


def _residual_proj(a, w_bf, x2d, mod3, gate_idx, rows_per_batch, tm, tn):
    m, k = a.shape
    n = w_bf.shape[1]
    assert n % tn == 0 and tn % _EPILOGUE_LANES == 0 and rows_per_batch % tm == 0
    tiles_per_batch = rows_per_batch // tm
    mod_rows = mod3.shape[0] // (m // rows_per_batch)
    return pl.pallas_call(
        _residual_proj_body,
        grid=(n // tn, m // tm),
        in_specs=[
            pl.BlockSpec((tm, k), lambda j, i: (i, 0)),
            pl.BlockSpec((k, tn), lambda j, i: (0, j)),
            pl.BlockSpec((tm, tn), lambda j, i: (i, j)),
            pl.BlockSpec((1, 1, tn), lambda j, i: ((i // tiles_per_batch) * mod_rows + gate_idx, 0, j)),
        ],
        out_specs=pl.BlockSpec((tm, tn), lambda j, i: (i, j)),
        out_shape=jax.ShapeDtypeStruct((m, n), F32),
        compiler_params=_params(2),
        name="residual_proj",
    )(a, w_bf, x2d, mod3)


def _ffn_in_body(a_ref, wg_ref, wu_ref, wo_ref, o_ref, wobf_ref, wgbf_ref, wubf_ref):
    _cast_weights_once([wg_ref, wu_ref], [wgbf_ref, wubf_ref])
    wobf_ref[...] = wo_ref[...].astype(BF16)
    for r0 in range(0, a_ref.shape[0], _MATMUL_ROWS):
        rs = slice(r0, r0 + _MATMUL_ROWS)
        a = a_ref[rs, :]
        gate = jnp.dot(a, wgbf_ref[...], preferred_element_type=F32)
        up = jnp.dot(a, wubf_ref[...], preferred_element_type=F32)
        o_ref[rs, :] = (gate * _sigmoid(gate) * up).astype(o_ref.dtype)


def _ffn_in(h, w_ffn_in, w_ffn_out, layer, tm=2048, tn=512):
    m, k = h.shape
    d_ff = w_ffn_in.shape[2] // 2
    d_out = w_ffn_out.shape[2]
    up_blk = d_ff // tn
    m_tiles = m // tm
    steps = (d_ff // tn) * m_tiles
    assert d_ff % steps == 0
    slab = d_ff // steps
    step = lambda j, i: j * m_tiles + i
    return pl.pallas_call(
        _ffn_in_body,
        grid=(d_ff // tn, m_tiles),
        in_specs=[
            pl.BlockSpec((tm, k), lambda j, i: (i, 0)),
            pl.BlockSpec((None, k, tn), lambda j, i: (layer, 0, j)),
            pl.BlockSpec((None, k, tn), lambda j, i: (layer, 0, up_blk + j)),
            pl.BlockSpec((None, slab, d_out), lambda j, i: (layer, step(j, i), 0)),
        ],
        out_specs=[
            pl.BlockSpec((tm, tn), lambda j, i: (i, j)),
            pl.BlockSpec((slab, d_out), lambda j, i: (step(j, i), 0)),
        ],
        out_shape=[jax.ShapeDtypeStruct((m, d_ff), BF16), jax.ShapeDtypeStruct((d_ff, d_out), BF16)],
        scratch_shapes=[pltpu.VMEM((k, tn), BF16), pltpu.VMEM((k, tn), BF16)],
        compiler_params=_params(2),
        name="ffn_in",
    )(h, w_ffn_in, w_ffn_in, w_ffn_out)


def _t5_causal_bucket(dist):
    n = jnp.maximum(dist, 0)
    nf = jnp.maximum(n, 1).astype(jnp.float32)
    large = MAX_EXACT + (jnp.log(nf / MAX_EXACT) / math.log(MAX_DISTANCE / MAX_EXACT)
                         * (NUM_BUCKETS - MAX_EXACT)).astype(jnp.int32)
    large = jnp.minimum(large, NUM_BUCKETS - 1)
    return jnp.where(n < MAX_EXACT, n, large)


def _band_buckets():
    q_off = jnp.arange(BLOCK)
    k_off = jnp.arange(2 * BLOCK)
    dist = q_off[:, None] + BLOCK - k_off[None, :]
    allowed = (dist >= 0) & (dist < WINDOW)
    return jnp.where(allowed, _t5_causal_bucket(dist), -1).astype(jnp.int32)


def _attn_build_bias(first_step, rb_ref, sink_ref, bucket_ref, bias_ref, *, n_q, n_kv):
    q_per_kv = n_q // n_kv

    @pl.when(first_step)
    def _():
        bucket = bucket_ref[...]
        col = lax.broadcasted_iota(jnp.int32, bucket.shape, 1)
        for slot in range(n_q):
            head = _orig_head(slot, n_kv, q_per_kv)
            tile = jnp.zeros(bucket.shape, F32)
            for b in range(NUM_BUCKETS):
                tile = jnp.where(bucket == b, rb_ref[b * n_q + head] * LOG2E, tile)
            tile = jnp.where(bucket < 0, -jnp.inf, tile)
            sink = sink_ref[head] * LOG2E
            bias_ref[0, slot] = jnp.where(col == 0, sink, tile)
            bias_ref[1, slot] = jnp.where(col == 0, sink, jnp.where(col < BLOCK, -jnp.inf, tile))


def _attn_block(seq_start, qmask_ref, krow_ref, q_ref, kc_ref, kp_ref, vc_ref, vp_ref, o_ref,
                bias_ref, lg_ref, p_ref, pv_ref, rs_ref, *, n_q, n_kv):
    q_per_kv = n_q // n_kv
    kv_w = n_kv * HEAD_DIM
    lane_slot = lax.broadcasted_iota(jnp.int32, (BLOCK, BLOCK), 1) // HEAD_DIM
    ones = jnp.ones((2 * BLOCK, BLOCK), BF16)
    bias_idx = jnp.where(seq_start, 1, 0)

    keys = jnp.concatenate([kp_ref[...], kc_ref[...]], axis=0) * krow_ref[...]
    vals = jnp.concatenate([vp_ref[...], vc_ref[...]], axis=0) * krow_ref[...]
    lhs = jnp.concatenate(
        [q_ref[:, j * kv_w:(j + 1) * kv_w] * qmask_ref[g]
         for j in range(q_per_kv) for g in range(n_kv)], axis=0)
    lg_ref[...] = lax.dot_general(lhs, keys, (((1,), (1,)), ((), ())),
                                  preferred_element_type=F32).reshape(n_q, BLOCK, 2 * BLOCK)
    for slot in range(n_q):
        logit = lg_ref[slot] + bias_ref[bias_idx, slot]
        m = jnp.max(logit, axis=-1, keepdims=True)
        p_ref[slot * BLOCK:(slot + 1) * BLOCK, :] = jnp.exp2(logit - m).astype(BF16)
    p = p_ref[...]
    pv_ref[...] = jnp.dot(p, vals, preferred_element_type=F32).reshape(n_q, BLOCK, kv_w)
    rs_ref[...] = jnp.dot(p, ones, preferred_element_type=F32).reshape(n_q, BLOCK, BLOCK)
    per_half = BLOCK // HEAD_DIM
    for j in range(q_per_kv):
        for half in range(kv_w // BLOCK):
            lanes = slice(half * BLOCK, (half + 1) * BLOCK)
            slots = [j * n_kv + half * per_half + i for i in range(per_half)]
            num = pv_ref[slots[-1], :, lanes]
            den = rs_ref[slots[-1]]
            for i in range(per_half - 2, -1, -1):
                num = jnp.where(lane_slot == i, pv_ref[slots[i], :, lanes], num)
                den = jnp.where(lane_slot == i, rs_ref[slots[i]], den)
            o_ref[:, j * kv_w + half * BLOCK:j * kv_w + (half + 1) * BLOCK] = (
                num * (1.0 / den)).astype(o_ref.dtype)


_CONV_HALO = 32
_CONV_LANES = 256
_SUBLANES = 8


def _conv_zero_history(seq_start, u_ref):
    @pl.when(seq_start)
    def _():
        u_ref[:_CONV_HALO, :] = jnp.zeros((_CONV_HALO, u_ref.shape[1]), u_ref.dtype)


def _conv_shift_matrix(rows):
    i = jnp.arange(_SUBLANES * rows) % rows
    r = jnp.arange(_SUBLANES * rows) // rows
    return (jnp.arange(rows)[None, :] == (i + r)[:, None]).astype(BF16)


def _conv_block(uin_ref, shift_ref, w_ref, cb_ref, lg_ref, lb_ref, o_ref, u_ref, sh_ref, y_ref):
    width = w_ref.shape[0]
    ts, channels = o_ref.shape
    u_ref[_CONV_HALO:, :] = uin_ref[...]
    sh_ref[...] = jnp.dot(shift_ref[...], u_ref[...], preferred_element_type=F32).reshape(sh_ref.shape)
    first = _CONV_HALO - (width - 1)
    unknown_zero = jnp.minimum(pl.program_id(0), 0)
    for c0 in range(0, channels, _CONV_LANES):
        cs = slice(c0, c0 + _CONV_LANES)
        acc = jnp.broadcast_to(cb_ref[:, cs], (ts, _CONV_LANES))
        for j in range(width):
            tile, r = divmod(first + j, _SUBLANES)
            rows = pl.ds(pl.multiple_of(tile * _SUBLANES + unknown_zero, _SUBLANES), ts)
            acc = acc + sh_ref[r, rows, cs] * w_ref[j:j + 1, cs]
        y_ref[:, cs] = acc
    y = y_ref[...]
    mu = jnp.mean(y, axis=-1, keepdims=True)
    yc = y - mu
    var = jnp.mean(yc * yc, axis=-1, keepdims=True)
    z = yc * lax.rsqrt(var + EPS) * lg_ref[...] + lb_ref[...]
    o_ref[...] = (z * _sigmoid(z)).astype(o_ref.dtype)
    u_ref[:_CONV_HALO, :] = u_ref[ts:, :]


def _core_body(h_ref, w_ref,
               uin_ref, shift_ref, cw_ref, cb_ref, lg_ref, lb_ref,
               rb_ref, sink_ref, bucket_ref, qmask_ref, krow_ref, q_ref, kc_ref, kp_ref, vc_ref, vp_ref,
               gates_ref, yconv_ref, yattn_ref,
               wbf_ref, u_ref, sh_ref, y_ref, bias_ref, lgt_ref, p_ref, pv_ref, rs_ref,
               *, n_q, n_kv, blocks_per_seq):
    t = pl.program_id(0) * pl.num_programs(1) + pl.program_id(1)
    seq_start = t % blocks_per_seq == 0
    _cast_weights_once([w_ref], [wbf_ref])
    _conv_zero_history(seq_start, u_ref)
    _attn_build_bias(t == 0, rb_ref, sink_ref, bucket_ref, bias_ref, n_q=n_q, n_kv=n_kv)

    _conv_block(uin_ref, shift_ref, cw_ref, cb_ref, lg_ref, lb_ref, yconv_ref, u_ref, sh_ref, y_ref)
    acc = jnp.dot(h_ref[...], wbf_ref[...], preferred_element_type=F32)
    gates_ref[...] = acc.astype(gates_ref.dtype)
    _attn_block(seq_start, qmask_ref, krow_ref, q_ref, kc_ref, kp_ref, vc_ref, vp_ref, yattn_ref,
                bias_ref, lgt_ref, p_ref, pv_ref, rs_ref, n_q=n_q, n_kv=n_kv)


def _mixer_core(h, w_in, layer, gate_col, conv_in, qkv, conv_w, conv_b, ln_g, ln_b, rel_bias, sinks,
                s_len, n_q, n_kv, tn=512):
    m, k = h.shape
    n = w_in.shape[2] - gate_col
    width, channels = conv_w.shape
    attn_w, kv_w = n_q * HEAD_DIM, n_kv * HEAD_DIM
    n_blocks = m // BLOCK
    n_tiles = n // tn
    assert gate_col % tn == 0 and n % tn == 0 and n_blocks % n_tiles == 0
    assert s_len % BLOCK == 0 and width - 1 <= _CONV_HALO and _CONV_HALO % _SUBLANES == 0
    assert attn_w % kv_w == 0 and kv_w == 2 * BLOCK and BLOCK % HEAD_DIM == 0
    m_tiles = n_blocks // n_tiles
    tm = m // m_tiles
    col_blk0 = gate_col // tn
    k_blk = attn_w // kv_w
    v_blk = k_blk + 1

    step = lambda j, i: j * m_tiles + i
    prev_step = lambda j, i: jnp.maximum(step(j, i) - 1, 0)
    const2 = lambda j, i: (0, 0)
    row = lambda v: v.reshape(1, channels)
    lane_group = jnp.arange(kv_w) // HEAD_DIM
    qmask = jnp.broadcast_to((lane_group[None, :] == jnp.arange(n_kv)[:, None])[:, None, :],
                             (n_kv, BLOCK, kv_w)).astype(BF16)
    krow = jnp.broadcast_to((jnp.arange(2 * BLOCK) > 0)[:, None], (2 * BLOCK, kv_w)).astype(BF16)
    body = functools.partial(_core_body, n_q=n_q, n_kv=n_kv, blocks_per_seq=s_len // BLOCK)
    return pl.pallas_call(
        body,
        grid=(n_tiles, m_tiles),
        in_specs=[
            pl.BlockSpec((tm, k), lambda j, i: (i, 0)),
            pl.BlockSpec((None, k, tn), lambda j, i: (layer, 0, col_blk0 + j)),
            pl.BlockSpec((BLOCK, channels), lambda j, i: (step(j, i), 0)),
            pl.BlockSpec((_SUBLANES * (_CONV_HALO + BLOCK), _CONV_HALO + BLOCK), const2),
            pl.BlockSpec((width, channels), const2),
            pl.BlockSpec((1, channels), const2),
            pl.BlockSpec((1, channels), const2),
            pl.BlockSpec((1, channels), const2),
            pl.BlockSpec(memory_space=pltpu.SMEM),
            pl.BlockSpec(memory_space=pltpu.SMEM),
            pl.BlockSpec((BLOCK, 2 * BLOCK), const2),
            pl.BlockSpec((n_kv, BLOCK, kv_w), lambda j, i: (0, 0, 0)),
            pl.BlockSpec((2 * BLOCK, kv_w), const2),
            pl.BlockSpec((BLOCK, attn_w), lambda j, i: (step(j, i), 0)),
            pl.BlockSpec((BLOCK, kv_w), lambda j, i: (step(j, i), k_blk)),
            pl.BlockSpec((BLOCK, kv_w), lambda j, i: (prev_step(j, i), k_blk)),
            pl.BlockSpec((BLOCK, kv_w), lambda j, i: (step(j, i), v_blk)),
            pl.BlockSpec((BLOCK, kv_w), lambda j, i: (prev_step(j, i), v_blk)),
        ],
        out_specs=[
            pl.BlockSpec((tm, tn), lambda j, i: (i, j)),
            pl.BlockSpec((BLOCK, channels), lambda j, i: (step(j, i), 0)),
            pl.BlockSpec((BLOCK, attn_w), lambda j, i: (step(j, i), 0)),
        ],
        out_shape=[
            jax.ShapeDtypeStruct((m, n), BF16),
            jax.ShapeDtypeStruct((m, channels), BF16),
            jax.ShapeDtypeStruct((m, attn_w), BF16),
        ],
        scratch_shapes=[
            pltpu.VMEM((k, tn), BF16),
            pltpu.VMEM((_CONV_HALO + BLOCK, channels), BF16),
            pltpu.VMEM((_SUBLANES, _CONV_HALO + BLOCK, channels), F32),
            pltpu.VMEM((BLOCK, channels), F32),
            pltpu.VMEM((2, n_q, BLOCK, 2 * BLOCK), F32),
            pltpu.VMEM((n_q, BLOCK, 2 * BLOCK), F32),
            pltpu.VMEM((n_q * BLOCK, 2 * BLOCK), BF16),
            pltpu.VMEM((n_q, BLOCK, kv_w), F32),
            pltpu.VMEM((n_q, BLOCK, BLOCK), F32),
        ],
        compiler_params=_params(2),
        name="mixer_core",
    )(h, w_in,
      conv_in, _conv_shift_matrix(_CONV_HALO + BLOCK), conv_w, row(conv_b), row(ln_g), row(ln_b),
      rel_bias.astype(F32).reshape(-1), sinks.astype(F32), _band_buckets(), qmask, krow,
      qkv, qkv, qkv, qkv, qkv)


def kernel(x, c, w_ada, b_ada, norm_mix_g, w_in, q_norm_g, k_norm_g, attn_sinks, rel_bias,
           w_attn_out, conv_w, conv_b, conv_ln_g, conv_ln_b, w_conv_out, w_mix_out,
           norm_ffn_g, w_ffn_in, w_ffn_out):
    b_sz, s_len, d = x.shape
    depth = w_ada.shape[0]
    n_q = attn_sinks.shape[1]
    attn_w = w_attn_out.shape[1]
    channels = conv_w.shape[2]
    kv_w = (w_in.shape[2] - attn_w - 2 * channels - 2 * d) // 2
    n_kv = kv_w // HEAD_DIM
    conv_col = attn_w + 2 * kv_w
    gate_col = conv_col + 2 * channels
    m = b_sz * s_len

    for l in range(depth):
        c_pad = jnp.pad(c, ((0, 8 - b_sz), (0, 0)))
        b_row = b_ada[l].reshape(1, N_MOD * d)
        mod_early = _ada(c_pad, w_ada, b_row, l, N_MOD_EARLY * d)[:b_sz].reshape(b_sz * N_MOD_EARLY, 1, d)

        qkv, h = _qkv_proj(x, norm_mix_g[l], mod_early, 1, 0, w_in, l, q_norm_g[l], k_norm_g[l], n_q, n_kv)
        h = h.reshape(m, d)
        proj_tm, proj_tn = 1024, 512
        assert (channels // proj_tn) * (m // proj_tm) == n_q
        slot_head = lambda slot: _orig_head(slot, n_kv, n_q // n_kv)
        same = lambda step: step
        conv_in, mod_late, wa_bf, wc_bf, wmix_bf = _glu_proj(
            h, w_in, l, conv_col, conv_col + channels, channels, proj_tm, proj_tn,
            c_pad, w_ada, b_row, N_MOD_EARLY * d,
            cast_jobs=((w_attn_out, slot_head), (w_conv_out, same), (w_mix_out, same)))
        mod_late = mod_late[:b_sz].reshape(b_sz * (N_MOD - N_MOD_EARLY), 1, d)
        gates, y_conv, y_attn = _mixer_core(h, w_in, l, gate_col, conv_in, qkv, conv_w[l], conv_b[l],
                                            conv_ln_g[l], conv_ln_b[l], rel_bias, attn_sinks[l],
                                            s_len, n_q, n_kv)
        x1, h = _merge_mix(y_attn, y_conv, wa_bf, wc_bf, gates, 0, d, wmix_bf, x, mod_late, 0,
                           norm_ffn_g[l], 2, 1)

        act, wo_bf = _ffn_in(h.reshape(m, d), w_ffn_in, w_ffn_out, l)
        x = _residual_proj(act, wo_bf, x1.reshape(m, d), mod_late, 3, s_len,
                           tm=512, tn=1024).reshape(b_sz, s_len, d)
    return x
```

```python
import functools
import math

import jax
import jax.numpy as jnp
from jax import lax
from jax.experimental import pallas as pl
from jax.experimental.pallas import tpu as pltpu

F32 = jnp.float32
BF16 = jnp.bfloat16

HEAD_DIM = 64
WINDOW = 128
BLOCK = 128
NUM_BUCKETS = 32
MAX_EXACT = NUM_BUCKETS // 2
MAX_DISTANCE = 128
N_MOD = 6
N_MOD_EARLY = 2
EPS = 1e-6
LOG2E = math.log2(math.e)

V7X_VMEM_LIMIT_BYTES = 56 * 1024 * 1024


def _params(n_axes):
    return pltpu.CompilerParams(
        dimension_semantics=("arbitrary",) * n_axes,
        vmem_limit_bytes=V7X_VMEM_LIMIT_BYTES,
    )


def _sigmoid(x):
    return jax.nn.sigmoid(x)


def _orig_head(slot, n_kv, q_per_kv):
    return (slot % n_kv) * q_per_kv + slot // n_kv


def _ada_body(c_ref, w_ref, b_ref, o_ref):
    c = c_ref[...]
    act = (c * _sigmoid(c)).astype(BF16)
    o_ref[...] = jnp.dot(act, w_ref[...].astype(BF16), preferred_element_type=F32) + b_ref[...]


def _ada(c_pad, w_ada, b_row, layer, n, tn=2048):
    rows, d = c_pad.shape
    assert n % tn == 0
    return pl.pallas_call(
        _ada_body,
        grid=(n // tn,),
        in_specs=[
            pl.BlockSpec((rows, d), lambda j: (0, 0)),
            pl.BlockSpec((None, d, tn), lambda j: (layer, 0, j)),
            pl.BlockSpec((1, tn), lambda j: (0, j)),
        ],
        out_specs=pl.BlockSpec((rows, tn), lambda j: (0, j)),
        out_shape=jax.ShapeDtypeStruct((rows, n), F32),
        compiler_params=_params(1),
        name="ada_mod",
    )(c_pad, w_ada, b_row)


def _norm_mod_body(x_ref, g_ref, sc_ref, sh_ref, o_ref):
    x = x_ref[0]
    ms = jnp.mean(x * x, axis=-1, keepdims=True)
    gain = g_ref[...] * (1.0 + sc_ref[0])
    o_ref[0] = (x * lax.rsqrt(ms + EPS) * gain + sh_ref[0]).astype(o_ref.dtype)


def _qkv_body(x_ref, g_ref, sc_ref, sh_ref, w_ref, gq_ref, gk_ref, o_ref, h_ref, wbf_ref, seg_ref,
              *, n_q, n_kv, half):
    attn_w = n_q * HEAD_DIM
    kv_w = n_kv * HEAD_DIM
    _norm_mod_body(x_ref, g_ref, sc_ref, sh_ref, h_ref)

    @pl.when(jnp.logical_and(pl.program_id(0) == 0, pl.program_id(1) == 0))
    def _():
        for slot in range(n_q):
            src = _orig_head(slot, n_kv, n_q // n_kv) * HEAD_DIM
            wbf_ref[:, slot * HEAD_DIM:(slot + 1) * HEAD_DIM] = w_ref[:, src:src + HEAD_DIM].astype(BF16)
        wbf_ref[:, attn_w:] = w_ref[:, attn_w:].astype(BF16)
        r = lax.broadcasted_iota(jnp.int32, (half, half), 0) // HEAD_DIM
        c = lax.broadcasted_iota(jnp.int32, (half, half), 1) // HEAD_DIM
        seg_ref[...] = (r == c).astype(BF16)

    a = h_ref[0]

    def head_norm(acc, seg, gain):
        ss = jnp.dot((acc * acc).astype(BF16), seg, preferred_element_type=F32)
        return acc * lax.rsqrt(ss / HEAD_DIM + EPS) * gain

    for c0 in range(0, attn_w, half):
        acc = jnp.dot(a, wbf_ref[:, c0:c0 + half], preferred_element_type=F32)
        o_ref[:, c0:c0 + half] = head_norm(acc, seg_ref[...], gq_ref[...]).astype(o_ref.dtype)
    acc = jnp.dot(a, wbf_ref[:, attn_w:], preferred_element_type=F32)
    o_ref[:, attn_w:attn_w + kv_w] = head_norm(acc[:, :kv_w], seg_ref[:kv_w, :kv_w],
                                               gk_ref[...]).astype(o_ref.dtype)
    o_ref[:, attn_w + kv_w:] = acc[:, kv_w:].astype(o_ref.dtype)


def _qkv_proj(x, norm_g, mod3, scale_idx, shift_idx, w_in, layer, q_g, k_g, n_q, n_kv, ts=512, half=512):
    b_sz, s_len, d = x.shape
    attn_w, kv_w = n_q * HEAD_DIM, n_kv * HEAD_DIM
    width = attn_w + 2 * kv_w
    assert attn_w % half == 0 and kv_w <= half and half % HEAD_DIM == 0
    gq = jnp.tile(q_g * (HEAD_DIM ** -0.5 * LOG2E), half // HEAD_DIM).reshape(1, half)
    gk = jnp.tile(k_g, n_kv).reshape(1, kv_w)
    tiles = s_len // ts
    mod_rows = mod3.shape[0] // b_sz
    const2 = lambda b, s: (0, 0)
    body = functools.partial(_qkv_body, n_q=n_q, n_kv=n_kv, half=half)
    return pl.pallas_call(
        body,
        grid=(b_sz, tiles),
        in_specs=[
            pl.BlockSpec((1, ts, d), lambda b, s: (b, s, 0)),
            pl.BlockSpec((1, d), const2),
            pl.BlockSpec((1, 1, d), lambda b, s: (b * mod_rows + scale_idx, 0, 0)),
            pl.BlockSpec((1, 1, d), lambda b, s: (b * mod_rows + shift_idx, 0, 0)),
            pl.BlockSpec((None, d, width), lambda b, s: (layer, 0, 0), pipeline_mode=pl.Buffered(1)),
            pl.BlockSpec((1, half), const2),
            pl.BlockSpec((1, kv_w), const2),
        ],
        out_specs=[
            pl.BlockSpec((ts, width), lambda b, s: (b * tiles + s, 0)),
            pl.BlockSpec((1, ts, d), lambda b, s: (b, s, 0)),
        ],
        out_shape=[
            jax.ShapeDtypeStruct((b_sz * s_len, width), BF16),
            jax.ShapeDtypeStruct((b_sz, s_len, d), BF16),
        ],
        scratch_shapes=[pltpu.VMEM((d, width), BF16), pltpu.VMEM((half, half), BF16)],
        compiler_params=_params(2),
        name="qkv_proj",
    )(x, norm_g.reshape(1, d), mod3, mod3, w_in, gq, gk)


def _cast_weights_once(w_refs, wbf_refs):
    @pl.when(pl.program_id(1) == 0)
    def _():
        for w_ref, wbf_ref in zip(w_refs, wbf_refs):
            wbf_ref[...] = w_ref[...].astype(BF16)


def _glu_proj_body(h_ref, c_ref, wada_ref, bada_ref, *refs, n_w):
    w_refs, refs = refs[:2 * n_w], refs[2 * n_w:]
    n_jobs = (len(refs) - 3) // 2
    job_in, o_ref, mod_ref, job_out = refs[:n_jobs], refs[n_jobs], refs[n_jobs + 1], refs[n_jobs + 2:-1]
    wbf_ref = refs[-1]
    _cast_weights_once(w_refs, [wbf_ref.at[c] for c in range(2 * n_w)])
    _ada_body(c_ref, wada_ref, bada_ref, mod_ref)
    for src_ref, dst_ref in zip(job_in, job_out):
        dst_ref[...] = src_ref[...].astype(BF16)
    h = h_ref[...]
    lanes = wbf_ref.shape[2]
    for c in range(n_w):
        a = jnp.dot(h, wbf_ref[c], preferred_element_type=F32)
        g = jnp.dot(h, wbf_ref[n_w + c], preferred_element_type=F32)
        o_ref[:, c * lanes:(c + 1) * lanes] = (a * _sigmoid(g)).astype(o_ref.dtype)


def _glu_proj(h, w, layer, a_col, g_col, n, tm, c_pad, w_ada, b_row, mod_col, cast_jobs=()):
    m, k = h.shape
    lanes = _EPILOGUE_LANES
    assert a_col % lanes == 0 and g_col % lanes == 0 and n % lanes == 0
    n_w = n // lanes
    steps = m // tm
    mod_n = w_ada.shape[2] - mod_col
    assert mod_n % steps == 0 and mod_col % (mod_n // steps) == 0
    mod_slab = mod_n // steps
    mod_blk0 = mod_col // mod_slab
    c_rows, d_c = c_pad.shape
    weight_tile = lambda col: pl.BlockSpec((None, k, lanes), lambda j, i: (layer, 0, col // lanes),
                                           pipeline_mode=pl.Buffered(1))
    weight_cols = [a_col + c * lanes for c in range(n_w)] + [g_col + c * lanes for c in range(n_w)]
    job_in, job_out, job_shapes = [], [], []
    for wj, row_block_of_step in cast_jobs:
        rows, cols = wj.shape[1:]
        assert rows % steps == 0
        slab = rows // steps
        job_in.append(pl.BlockSpec((None, slab, cols), lambda j, i, f=row_block_of_step: (layer, f(i), 0)))
        job_out.append(pl.BlockSpec((slab, cols), lambda j, i: (i, 0)))
        job_shapes.append(jax.ShapeDtypeStruct((rows, cols), BF16))
    return pl.pallas_call(
        functools.partial(_glu_proj_body, n_w=n_w),
        grid=(1, steps),
        in_specs=[
            pl.BlockSpec((tm, k), lambda j, i: (i, 0)),
            pl.BlockSpec((c_rows, d_c), lambda j, i: (0, 0)),
            pl.BlockSpec((None, d_c, mod_slab), lambda j, i: (layer, 0, mod_blk0 + i)),
            pl.BlockSpec((1, mod_slab), lambda j, i: (0, mod_blk0 + i)),
        ] + [weight_tile(col) for col in weight_cols] + job_in,
        out_specs=[pl.BlockSpec((tm, n), lambda j, i: (i, 0)),
                   pl.BlockSpec((c_rows, mod_slab), lambda j, i: (0, i))] + job_out,
        out_shape=[jax.ShapeDtypeStruct((m, n), BF16),
                   jax.ShapeDtypeStruct((c_rows, mod_n), F32)] + job_shapes,
        scratch_shapes=[pltpu.VMEM((2 * n_w, k, lanes), BF16)],
        compiler_params=_params(2),
        name="glu_proj",
    )(h, c_pad, w_ada, b_row, *([w] * (2 * n_w)), *[wj for wj, _ in cast_jobs])


_EPILOGUE_LANES = 512
_MATMUL_ROWS = 1024


def _merge_mix_body(ya_ref, yc_ref, wa_ref, wc_ref, ga_ref, gc_ref, w_ref, x_ref, gt_ref, g_ref, sc_ref, sh_ref,
                    x1_ref, h_ref, merged_ref):
    d = w_ref.shape[1]
    ya = ya_ref[...]
    yc = yc_ref[...]
    for n0 in range(0, d, _EPILOGUE_LANES):
        ns = slice(n0, n0 + _EPILOGUE_LANES)
        acc_a = jnp.dot(ya, wa_ref[:, ns], preferred_element_type=F32)
        acc_c = jnp.dot(yc, wc_ref[:, ns], preferred_element_type=F32)
        ga = _sigmoid(ga_ref[:, ns].astype(F32))
        gc = _sigmoid(gc_ref[:, ns].astype(F32))
        merged_ref[:, ns] = (ga * acc_a + gc * acc_c).astype(merged_ref.dtype)
    a = merged_ref[...]
    sumsq = jnp.zeros((a.shape[0], 1), F32)
    for n0 in range(0, d, _EPILOGUE_LANES):
        ns = slice(n0, n0 + _EPILOGUE_LANES)
        acc = jnp.dot(a, w_ref[:, ns], preferred_element_type=F32)
        x1 = x_ref[0, :, ns] + gt_ref[0, :, ns] * acc
        x1_ref[0, :, ns] = x1
        sumsq = sumsq + jnp.sum(x1 * x1, axis=-1, keepdims=True)
    inv = lax.rsqrt(sumsq / d + EPS)
    gain = g_ref[...] * (1.0 + sc_ref[0])
    h_ref[0] = (x1_ref[0] * inv * gain + sh_ref[0]).astype(h_ref.dtype)


def _merge_mix(attn, conv, wa_bf, wc_bf, gates, ga_col, gc_col, wmix_bf, x, mod3, gate_idx, norm_g,
               scale_idx, shift_idx, ts=256):
    b_sz, s_len, d = x.shape
    ka, kc = attn.shape[1], conv.shape[1]
    assert d % _EPILOGUE_LANES == 0 and ga_col % d == 0 and gc_col % d == 0
    tiles = s_len // ts
    const2 = lambda b, s: (0, 0)
    rows = lambda width, col_blk: pl.BlockSpec((ts, width), lambda b, s: (b * tiles + s, col_blk))
    mod_rows = mod3.shape[0] // b_sz
    mod_row = lambda idx: pl.BlockSpec((1, 1, d), lambda b, s: (b * mod_rows + idx, 0, 0))
    act = pl.BlockSpec((1, ts, d), lambda b, s: (b, s, 0))
    resident = pl.Buffered(1)
    return pl.pallas_call(
        _merge_mix_body,
        grid=(b_sz, tiles),
        in_specs=[
            rows(ka, 0),
            rows(kc, 0),
            pl.BlockSpec((ka, d), const2, pipeline_mode=resident),
            pl.BlockSpec((kc, d), const2, pipeline_mode=resident),
            rows(d, ga_col // d),
            rows(d, gc_col // d),
            pl.BlockSpec((d, d), const2, pipeline_mode=resident),
            act,
            mod_row(gate_idx),
            pl.BlockSpec((1, d), const2),
            mod_row(scale_idx),
            mod_row(shift_idx),
        ],
        out_specs=[act, act],
        out_shape=[jax.ShapeDtypeStruct((b_sz, s_len, d), F32), jax.ShapeDtypeStruct((b_sz, s_len, d), BF16)],
        scratch_shapes=[pltpu.VMEM((ts, d), BF16)],
        compiler_params=_params(2),
        name="merge_mix",
    )(attn, conv, wa_bf, wc_bf, gates, gates, wmix_bf, x, mod3, norm_g.reshape(1, d), mod3, mod3)


def _residual_proj_body(a_ref, w_ref, x_ref, gt_ref, o_ref):
    a = a_ref[...]
    for n0 in range(0, o_ref.shape[1], _EPILOGUE_LANES):
        ns = slice(n0, n0 + _EPILOGUE_LANES)
        acc = jnp.dot(a, w_ref[:, ns], preferred_element_type=F32)
        o_ref[:, ns] = x_ref[:, ns] + gt_ref[0, :, ns] * acc


def _residual_proj(a, w_bf, x2d, mod3, gate_idx, rows_per_batch, tm, tn):
    m, k = a.shape
    n = w_bf.shape[1]
    assert n % tn == 0 and tn % _EPILOGUE_LANES == 0 and rows_per_batch % tm == 0
    tiles_per_batch = rows_per_batch // tm
    mod_rows = mod3.shape[0] // (m // rows_per_batch)
    return pl.pallas_call(
        _residual_proj_body,
        grid=(n // tn, m // tm),
        in_specs=[
            pl.BlockSpec((tm, k), lambda j, i: (i, 0)),
            pl.BlockSpec((k, tn), lambda j, i: (0, j)),
            pl.BlockSpec((tm, tn), lambda j, i: (i, j)),
            pl.BlockSpec((1, 1, tn), lambda j, i: ((i // tiles_per_batch) * mod_rows + gate_idx, 0, j)),
        ],
        out_specs=pl.BlockSpec((tm, tn), lambda j, i: (i, j)),
        out_shape=jax.ShapeDtypeStruct((m, n), F32),
        compiler_params=_params(2),
        name="residual_proj",
    )(a, w_bf, x2d, mod3)


def _ffn_in_body(a_ref, wg_ref, wu_ref, wo_ref, o_ref, wobf_ref, wgbf_ref, wubf_ref):
    _cast_weights_once([wg_ref, wu_ref], [wgbf_ref, wubf_ref])
    wobf_ref[...] = wo_ref[...].astype(BF16)
    for r0 in range(0, a_ref.shape[0], _MATMUL_ROWS):
        rs = slice(r0, r0 + _MATMUL_ROWS)
        a = a_ref[rs, :]
        gate = jnp.dot(a, wgbf_ref[...], preferred_element_type=F32)
        up = jnp.dot(a, wubf_ref[...], preferred_element_type=F32)
        o_ref[rs, :] = (gate * _sigmoid(gate) * up).astype(o_ref.dtype)


def _ffn_in(h, w_ffn_in, w_ffn_out, layer, tm=2048, tn=512):
    m, k = h.shape
    d_ff = w_ffn_in.shape[2] // 2
    d_out = w_ffn_out.shape[2]
    up_blk = d_ff // tn
    m_tiles = m // tm
    steps = (d_ff // tn) * m_tiles
    assert d_ff % steps == 0
    slab = d_ff // steps
    step = lambda j, i: j * m_tiles + i
    return pl.pallas_call(
        _ffn_in_body,
        grid=(d_ff // tn, m_tiles),
        in_specs=[
            pl.BlockSpec((tm, k), lambda j, i: (i, 0)),
            pl.BlockSpec((None, k, tn), lambda j, i: (layer, 0, j)),
            pl.BlockSpec((None, k, tn), lambda j, i: (layer, 0, up_blk + j)),
            pl.BlockSpec((None, slab, d_out), lambda j, i: (layer, step(j, i), 0)),
        ],
        out_specs=[
            pl.BlockSpec((tm, tn), lambda j, i: (i, j)),
            pl.BlockSpec((slab, d_out), lambda j, i: (step(j, i), 0)),
        ],
        out_shape=[jax.ShapeDtypeStruct((m, d_ff), BF16), jax.ShapeDtypeStruct((d_ff, d_out), BF16)],
        scratch_shapes=[pltpu.VMEM((k, tn), BF16), pltpu.VMEM((k, tn), BF16)],
        compiler_params=_params(2),
        name="ffn_in",
    )(h, w_ffn_in, w_ffn_in, w_ffn_out)


def _t5_causal_bucket(dist):
    n = jnp.maximum(dist, 0)
    nf = jnp.maximum(n, 1).astype(jnp.float32)
    large = MAX_EXACT + (jnp.log(nf / MAX_EXACT) / math.log(MAX_DISTANCE / MAX_EXACT)
                         * (NUM_BUCKETS - MAX_EXACT)).astype(jnp.int32)
    large = jnp.minimum(large, NUM_BUCKETS - 1)
    return jnp.where(n < MAX_EXACT, n, large)


def _band_buckets():
    q_off = jnp.arange(BLOCK)
    k_off = jnp.arange(2 * BLOCK)
    dist = q_off[:, None] + BLOCK - k_off[None, :]
    allowed = (dist >= 0) & (dist < WINDOW)
    return jnp.where(allowed, _t5_causal_bucket(dist), -1).astype(jnp.int32)


def _attn_build_bias(first_step, rb_ref, sink_ref, bucket_ref, bias_ref, *, n_q, n_kv):
    q_per_kv = n_q // n_kv

    @pl.when(first_step)
    def _():
        bucket = bucket_ref[...]
        col = lax.broadcasted_iota(jnp.int32, bucket.shape, 1)
        for slot in range(n_q):
            head = _orig_head(slot, n_kv, q_per_kv)
            tile = jnp.zeros(bucket.shape, F32)
            for b in range(NUM_BUCKETS):
                tile = jnp.where(bucket == b, rb_ref[b * n_q + head] * LOG2E, tile)
            tile = jnp.where(bucket < 0, -jnp.inf, tile)
            sink = sink_ref[head] * LOG2E
            bias_ref[0, slot] = jnp.where(col == 0, sink, tile)
            bias_ref[1, slot] = jnp.where(col == 0, sink, jnp.where(col < BLOCK, -jnp.inf, tile))


def _attn_block(seq_start, qmask_ref, krow_ref, q_ref, kc_ref, kp_ref, vc_ref, vp_ref, o_ref,
                bias_ref, lg_ref, p_ref, pv_ref, rs_ref, *, n_q, n_kv):
    q_per_kv = n_q // n_kv
    kv_w = n_kv * HEAD_DIM
    lane_slot = lax.broadcasted_iota(jnp.int32, (BLOCK, BLOCK), 1) // HEAD_DIM
    ones = jnp.ones((2 * BLOCK, BLOCK), BF16)
    bias_idx = jnp.where(seq_start, 1, 0)

    keys = jnp.concatenate([kp_ref[...], kc_ref[...]], axis=0) * krow_ref[...]
    vals = jnp.concatenate([vp_ref[...], vc_ref[...]], axis=0) * krow_ref[...]
    lhs = jnp.concatenate(
        [q_ref[:, j * kv_w:(j + 1) * kv_w] * qmask_ref[g]
         for j in range(q_per_kv) for g in range(n_kv)], axis=0)
    lg_ref[...] = lax.dot_general(lhs, keys, (((1,), (1,)), ((), ())),
                                  preferred_element_type=F32).reshape(n_q, BLOCK, 2 * BLOCK)
    for slot in range(n_q):
        logit = lg_ref[slot] + bias_ref[bias_idx, slot]
        m = jnp.max(logit, axis=-1, keepdims=True)
        p_ref[slot * BLOCK:(slot + 1) * BLOCK, :] = jnp.exp2(logit - m).astype(BF16)
    p = p_ref[...]
    pv_ref[...] = jnp.dot(p, vals, preferred_element_type=F32).reshape(n_q, BLOCK, kv_w)
    rs_ref[...] = jnp.dot(p, ones, preferred_element_type=F32).reshape(n_q, BLOCK, BLOCK)
    per_half = BLOCK // HEAD_DIM
    for j in range(q_per_kv):
        for half in range(kv_w // BLOCK):
            lanes = slice(half * BLOCK, (half + 1) * BLOCK)
            slots = [j * n_kv + half * per_half + i for i in range(per_half)]
            num = pv_ref[slots[-1], :, lanes]
            den = rs_ref[slots[-1]]
            for i in range(per_half - 2, -1, -1):
                num = jnp.where(lane_slot == i, pv_ref[slots[i], :, lanes], num)
                den = jnp.where(lane_slot == i, rs_ref[slots[i]], den)
            o_ref[:, j * kv_w + half * BLOCK:j * kv_w + (half + 1) * BLOCK] = (
                num * (1.0 / den)).astype(o_ref.dtype)


_CONV_HALO = 32
_CONV_LANES = 256
_SUBLANES = 8


def _conv_zero_history(seq_start, u_ref):
    @pl.when(seq_start)
    def _():
        u_ref[:_CONV_HALO, :] = jnp.zeros((_CONV_HALO, u_ref.shape[1]), u_ref.dtype)


def _conv_shift_matrix(rows):
    i = jnp.arange(_SUBLANES * rows) % rows
    r = jnp.arange(_SUBLANES * rows) // rows
    return (jnp.arange(rows)[None, :] == (i + r)[:, None]).astype(BF16)


def _conv_block(uin_ref, shift_ref, w_ref, cb_ref, lg_ref, lb_ref, o_ref, u_ref, sh_ref, y_ref):
    width = w_ref.shape[0]
    ts, channels = o_ref.shape
    u_ref[_CONV_HALO:, :] = uin_ref[...]
    sh_ref[...] = jnp.dot(shift_ref[...], u_ref[...], preferred_element_type=F32).reshape(sh_ref.shape)
    first = _CONV_HALO - (width - 1)
    unknown_zero = jnp.minimum(pl.program_id(0), 0)
    for c0 in range(0, channels, _CONV_LANES):
        cs = slice(c0, c0 + _CONV_LANES)
        acc = jnp.broadcast_to(cb_ref[:, cs], (ts, _CONV_LANES))
        for j in range(width):
            tile, r = divmod(first + j, _SUBLANES)
            rows = pl.ds(pl.multiple_of(tile * _SUBLANES + unknown_zero, _SUBLANES), ts)
            acc = acc + sh_ref[r, rows, cs] * w_ref[j:j + 1, cs]
        y_ref[:, cs] = acc
    y = y_ref[...]
    mu = jnp.mean(y, axis=-1, keepdims=True)
    yc = y - mu
    var = jnp.mean(yc * yc, axis=-1, keepdims=True)
    z = yc * lax.rsqrt(var + EPS) * lg_ref[...] + lb_ref[...]
    o_ref[...] = (z * _sigmoid(z)).astype(o_ref.dtype)
    u_ref[:_CONV_HALO, :] = u_ref[ts:, :]


def _core_body(h_ref, w_ref,
               uin_ref, shift_ref, cw_ref, cb_ref, lg_ref, lb_ref,
               rb_ref, sink_ref, bucket_ref, qmask_ref, krow_ref, qkv_ref, kvp_ref,
               gates_ref, yconv_ref, yattn_ref,
               wbf_ref, u_ref, sh_ref, y_ref, bias_ref, lgt_ref, p_ref, pv_ref, rs_ref,
               *, n_q, n_kv, blocks_per_seq):
    attn_w, kv_w = n_q * HEAD_DIM, n_kv * HEAD_DIM
    t = pl.program_id(0) * pl.num_programs(1) + pl.program_id(1)
    seq_start = t % blocks_per_seq == 0
    _cast_weights_once([w_ref], [wbf_ref])
    _conv_zero_history(seq_start, u_ref)
    _attn_build_bias(t == 0, rb_ref, sink_ref, bucket_ref, bias_ref, n_q=n_q, n_kv=n_kv)

    _conv_block(uin_ref, shift_ref, cw_ref, cb_ref, lg_ref, lb_ref, yconv_ref, u_ref, sh_ref, y_ref)
    acc = jnp.dot(h_ref[...], wbf_ref[...], preferred_element_type=F32)
    gates_ref[...] = acc.astype(gates_ref.dtype)
    k_cols, v_cols = pl.ds(attn_w, kv_w), pl.ds(attn_w + kv_w, kv_w)
    _attn_block(seq_start, qmask_ref, krow_ref, qkv_ref.at[:, pl.ds(0, attn_w)],
                qkv_ref.at[:, k_cols], kvp_ref.at[:, pl.ds(0, kv_w)],
                qkv_ref.at[:, v_cols], kvp_ref.at[:, pl.ds(kv_w, kv_w)],
                yattn_ref, bias_ref, lgt_ref, p_ref, pv_ref, rs_ref, n_q=n_q, n_kv=n_kv)


def _mixer_core(h, w_in, layer, gate_col, conv_in, qkv, conv_w, conv_b, ln_g, ln_b, rel_bias, sinks,
                s_len, n_q, n_kv, tn=512):
    m, k = h.shape
    n = w_in.shape[2] - gate_col
    width, channels = conv_w.shape
    attn_w, kv_w = n_q * HEAD_DIM, n_kv * HEAD_DIM
    n_blocks = m // BLOCK
    n_tiles = n // tn
    assert gate_col % tn == 0 and n % tn == 0 and n_blocks % n_tiles == 0
    assert s_len % BLOCK == 0 and width - 1 <= _CONV_HALO and _CONV_HALO % _SUBLANES == 0
    assert attn_w % (2 * kv_w) == 0 and kv_w == 2 * BLOCK and BLOCK % HEAD_DIM == 0
    m_tiles = n_blocks // n_tiles
    tm = m // m_tiles
    col_blk0 = gate_col // tn
    kv_blk = attn_w // (2 * kv_w)

    step = lambda j, i: j * m_tiles + i
    prev_step = lambda j, i: jnp.maximum(step(j, i) - 1, 0)
    const2 = lambda j, i: (0, 0)
    row = lambda v: v.reshape(1, channels)
    lane_group = jnp.arange(kv_w) // HEAD_DIM
    qmask = jnp.broadcast_to((lane_group[None, :] == jnp.arange(n_kv)[:, None])[:, None, :],
                             (n_kv, BLOCK, kv_w)).astype(BF16)
    krow = jnp.broadcast_to((jnp.arange(2 * BLOCK) > 0)[:, None], (2 * BLOCK, kv_w)).astype(BF16)
    body = functools.partial(_core_body, n_q=n_q, n_kv=n_kv, blocks_per_seq=s_len // BLOCK)
    return pl.pallas_call(
        body,
        grid=(n_tiles, m_tiles),
        in_specs=[
            pl.BlockSpec((tm, k), lambda j, i: (i, 0)),
            pl.BlockSpec((None, k, tn), lambda j, i: (layer, 0, col_blk0 + j)),
            pl.BlockSpec((BLOCK, channels), lambda j, i: (step(j, i), 0)),
            pl.BlockSpec((_SUBLANES * (_CONV_HALO + BLOCK), _CONV_HALO + BLOCK), const2),
            pl.BlockSpec((width, channels), const2),
            pl.BlockSpec((1, channels), const2),
            pl.BlockSpec((1, channels), const2),
            pl.BlockSpec((1, channels), const2),
            pl.BlockSpec(memory_space=pltpu.SMEM),
            pl.BlockSpec(memory_space=pltpu.SMEM),
            pl.BlockSpec((BLOCK, 2 * BLOCK), const2),
            pl.BlockSpec((n_kv, BLOCK, kv_w), lambda j, i: (0, 0, 0)),
            pl.BlockSpec((2 * BLOCK, kv_w), const2),
            pl.BlockSpec((BLOCK, attn_w + 2 * kv_w), lambda j, i: (step(j, i), 0)),
            pl.BlockSpec((BLOCK, 2 * kv_w), lambda j, i: (prev_step(j, i), kv_blk)),
        ],
        out_specs=[
            pl.BlockSpec((tm, tn), lambda j, i: (i, j)),
            pl.BlockSpec((BLOCK, channels), lambda j, i: (step(j, i), 0)),
            pl.BlockSpec((BLOCK, attn_w), lambda j, i: (step(j, i), 0)),
        ],
        out_shape=[
            jax.ShapeDtypeStruct((m, n), BF16),
            jax.ShapeDtypeStruct((m, channels), BF16),
            jax.ShapeDtypeStruct((m, attn_w), BF16),
        ],
        scratch_shapes=[
            pltpu.VMEM((k, tn), BF16),
            pltpu.VMEM((_CONV_HALO + BLOCK, channels), BF16),
            pltpu.VMEM((_SUBLANES, _CONV_HALO + BLOCK, channels), F32),
            pltpu.VMEM((BLOCK, channels), F32),
            pltpu.VMEM((2, n_q, BLOCK, 2 * BLOCK), F32),
            pltpu.VMEM((n_q, BLOCK, 2 * BLOCK), F32),
            pltpu.VMEM((n_q * BLOCK, 2 * BLOCK), BF16),
            pltpu.VMEM((n_q, BLOCK, kv_w), F32),
            pltpu.VMEM((n_q, BLOCK, BLOCK), F32),
        ],
        compiler_params=_params(2),
        name="mixer_core",
    )(h, w_in,
      conv_in, _conv_shift_matrix(_CONV_HALO + BLOCK), conv_w, row(conv_b), row(ln_g), row(ln_b),
      rel_bias.astype(F32).reshape(-1), sinks.astype(F32), _band_buckets(), qmask, krow,
      qkv, qkv)


def kernel(x, c, w_ada, b_ada, norm_mix_g, w_in, q_norm_g, k_norm_g, attn_sinks, rel_bias,
           w_attn_out, conv_w, conv_b, conv_ln_g, conv_ln_b, w_conv_out, w_mix_out,
           norm_ffn_g, w_ffn_in, w_ffn_out):
    b_sz, s_len, d = x.shape
    depth = w_ada.shape[0]
    n_q = attn_sinks.shape[1]
    attn_w = w_attn_out.shape[1]
    channels = conv_w.shape[2]
    kv_w = (w_in.shape[2] - attn_w - 2 * channels - 2 * d) // 2
    n_kv = kv_w // HEAD_DIM
    conv_col = attn_w + 2 * kv_w
    gate_col = conv_col + 2 * channels
    m = b_sz * s_len

    for l in range(depth):
        c_pad = jnp.pad(c, ((0, 8 - b_sz), (0, 0)))
        b_row = b_ada[l].reshape(1, N_MOD * d)
        mod_early = _ada(c_pad, w_ada, b_row, l, N_MOD_EARLY * d)[:b_sz].reshape(b_sz * N_MOD_EARLY, 1, d)

        qkv, h = _qkv_proj(x, norm_mix_g[l], mod_early, 1, 0, w_in, l, q_norm_g[l], k_norm_g[l], n_q, n_kv)
        h = h.reshape(m, d)
        assert m % n_q == 0
        slot_head = lambda slot: _orig_head(slot, n_kv, n_q // n_kv)
        same = lambda step: step
        conv_in, mod_late, wa_bf, wc_bf, wmix_bf = _glu_proj(
            h, w_in, l, conv_col, conv_col + channels, channels, m // n_q,
            c_pad, w_ada, b_row, N_MOD_EARLY * d,
            cast_jobs=((w_attn_out, slot_head), (w_conv_out, same), (w_mix_out, same)))
        mod_late = mod_late[:b_sz].reshape(b_sz * (N_MOD - N_MOD_EARLY), 1, d)
        gates, y_conv, y_attn = _mixer_core(h, w_in, l, gate_col, conv_in, qkv, conv_w[l], conv_b[l],
                                            conv_ln_g[l], conv_ln_b[l], rel_bias, attn_sinks[l],
                                            s_len, n_q, n_kv)
        x1, h = _merge_mix(y_attn, y_conv, wa_bf, wc_bf, gates, 0, d, wmix_bf, x, mod_late, 0,
                           norm_ffn_g[l], 2, 1)

        act, wo_bf = _ffn_in(h.reshape(m, d), w_ffn_in, w_ffn_out, l)
        x = _residual_proj(act, wo_bf, x1.reshape(m, d), mod_late, 3, s_len,
                           tm=512, tn=1024).reshape(b_sz, s_len, d)
    return x
```

```python
import functools
import math

import jax
import jax.numpy as jnp
from jax import lax
from jax.experimental import pallas as pl
from jax.experimental.pallas import tpu as pltpu

F32 = jnp.float32
BF16 = jnp.bfloat16

HEAD_DIM = 64
WINDOW = 128
BLOCK = 128
NUM_BUCKETS = 32
MAX_EXACT = NUM_BUCKETS // 2
MAX_DISTANCE = 128
N_MOD = 6
N_MOD_EARLY = 2
EPS = 1e-6
LOG2E = math.log2(math.e)

V7X_VMEM_LIMIT_BYTES = 56 * 1024 * 1024


def _params(n_axes):
    return pltpu.CompilerParams(
        dimension_semantics=("arbitrary",) * n_axes,
        vmem_limit_bytes=V7X_VMEM_LIMIT_BYTES,
    )


def _sigmoid(x):
    return jax.nn.sigmoid(x)


def _orig_head(slot, n_kv, q_per_kv):
    return (slot % n_kv) * q_per_kv + slot // n_kv


def _ada_body(c_ref, w_ref, b_ref, o_ref):
    c = c_ref[...]
    act = (c * _sigmoid(c)).astype(BF16)
    o_ref[...] = jnp.dot(act, w_ref[...].astype(BF16), preferred_element_type=F32) + b_ref[...]


def _ada(c_pad, w_ada, b_row, layer, n, tn=2048):
    rows, d = c_pad.shape
    assert n % tn == 0
    return pl.pallas_call(
        _ada_body,
        grid=(n // tn,),
        in_specs=[
            pl.BlockSpec((rows, d), lambda j: (0, 0)),
            pl.BlockSpec((None, d, tn), lambda j: (layer, 0, j)),
            pl.BlockSpec((1, tn), lambda j: (0, j)),
        ],
        out_specs=pl.BlockSpec((rows, tn), lambda j: (0, j)),
        out_shape=jax.ShapeDtypeStruct((rows, n), F32),
        compiler_params=_params(1),
        name="ada_mod",
    )(c_pad, w_ada, b_row)


def _norm_mod_body(x_ref, g_ref, sc_ref, sh_ref, o_ref):
    x = x_ref[0]
    ms = jnp.mean(x * x, axis=-1, keepdims=True)
    gain = g_ref[...] * (1.0 + sc_ref[0])
    o_ref[0] = (x * lax.rsqrt(ms + EPS) * gain + sh_ref[0]).astype(o_ref.dtype)


def _qkv_body(x_ref, g_ref, sc_ref, sh_ref, w_ref, gq_ref, gk_ref, o_ref, h_ref, wbf_ref, seg_ref,
              *, n_q, n_kv, half):
    attn_w = n_q * HEAD_DIM
    kv_w = n_kv * HEAD_DIM

    @pl.when(jnp.logical_and(pl.program_id(0) == 0, pl.program_id(1) == 0))
    def _():
        for slot in range(n_q):
            src = _orig_head(slot, n_kv, n_q // n_kv) * HEAD_DIM
            wbf_ref[:, slot * HEAD_DIM:(slot + 1) * HEAD_DIM] = w_ref[:, src:src + HEAD_DIM].astype(BF16)
        wbf_ref[:, attn_w:] = w_ref[:, attn_w:].astype(BF16)
        r = lax.broadcasted_iota(jnp.int32, (half, half), 0) // HEAD_DIM
        c = lax.broadcasted_iota(jnp.int32, (half, half), 1) // HEAD_DIM
        seg_ref[...] = (r == c).astype(BF16)

    def head_norm(acc, seg, gain):
        ss = jnp.dot((acc * acc).astype(BF16), seg, preferred_element_type=F32)
        return acc * lax.rsqrt(ss / HEAD_DIM + EPS) * gain

    for r0 in range(0, x_ref.shape[1], _QKV_CHAIN_ROWS):
        rs = slice(r0, r0 + _QKV_CHAIN_ROWS)
        x = x_ref[0, rs, :]
        ms = jnp.mean(x * x, axis=-1, keepdims=True)
        gain = g_ref[...] * (1.0 + sc_ref[0])
        a = (x * lax.rsqrt(ms + EPS) * gain + sh_ref[0]).astype(BF16)
        h_ref[0, rs, :] = a
        for c0 in range(0, attn_w, half):
            acc = jnp.dot(a, wbf_ref[:, c0:c0 + half], preferred_element_type=F32)
            o_ref[rs, c0:c0 + half] = head_norm(acc, seg_ref[...], gq_ref[...]).astype(o_ref.dtype)
        acc = jnp.dot(a, wbf_ref[:, attn_w:], preferred_element_type=F32)
        o_ref[rs, attn_w:attn_w + kv_w] = head_norm(acc[:, :kv_w], seg_ref[:kv_w, :kv_w],
                                                    gk_ref[...]).astype(o_ref.dtype)
        o_ref[rs, attn_w + kv_w:] = acc[:, kv_w:].astype(o_ref.dtype)


_QKV_CHAIN_ROWS = 512


def _qkv_proj(x, norm_g, mod3, scale_idx, shift_idx, w_in, layer, q_g, k_g, n_q, n_kv, ts=1024, half=512):
    b_sz, s_len, d = x.shape
    attn_w, kv_w = n_q * HEAD_DIM, n_kv * HEAD_DIM
    width = attn_w + 2 * kv_w
    assert attn_w % half == 0 and kv_w <= half and half % HEAD_DIM == 0
    gq = jnp.tile(q_g * (HEAD_DIM ** -0.5 * LOG2E), half // HEAD_DIM).reshape(1, half)
    gk = jnp.tile(k_g, n_kv).reshape(1, kv_w)
    tiles = s_len // ts
    mod_rows = mod3.shape[0] // b_sz
    const2 = lambda b, s: (0, 0)
    body = functools.partial(_qkv_body, n_q=n_q, n_kv=n_kv, half=half)
    return pl.pallas_call(
        body,
        grid=(b_sz, tiles),
        in_specs=[
            pl.BlockSpec((1, ts, d), lambda b, s: (b, s, 0)),
            pl.BlockSpec((1, d), const2),
            pl.BlockSpec((1, 1, d), lambda b, s: (b * mod_rows + scale_idx, 0, 0)),
            pl.BlockSpec((1, 1, d), lambda b, s: (b * mod_rows + shift_idx, 0, 0)),
            pl.BlockSpec((None, d, width), lambda b, s: (layer, 0, 0), pipeline_mode=pl.Buffered(1)),
            pl.BlockSpec((1, half), const2),
            pl.BlockSpec((1, kv_w), const2),
        ],
        out_specs=[
            pl.BlockSpec((ts, width), lambda b, s: (b * tiles + s, 0)),
            pl.BlockSpec((1, ts, d), lambda b, s: (b, s, 0)),
        ],
        out_shape=[
            jax.ShapeDtypeStruct((b_sz * s_len, width), BF16),
            jax.ShapeDtypeStruct((b_sz, s_len, d), BF16),
        ],
        scratch_shapes=[pltpu.VMEM((d, width), BF16), pltpu.VMEM((half, half), BF16)],
        compiler_params=_params(2),
        name="qkv_proj",
    )(x, norm_g.reshape(1, d), mod3, mod3, w_in, gq, gk)


def _cast_weights_once(w_refs, wbf_refs):
    @pl.when(pl.program_id(1) == 0)
    def _():
        for w_ref, wbf_ref in zip(w_refs, wbf_refs):
            wbf_ref[...] = w_ref[...].astype(BF16)


def _glu_proj_body(h_ref, c_ref, wada_ref, bada_ref, *refs, n_w):
    w_refs, refs = refs[:2 * n_w], refs[2 * n_w:]
    n_jobs = (len(refs) - 3) // 2
    job_in, o_ref, mod_ref, job_out = refs[:n_jobs], refs[n_jobs], refs[n_jobs + 1], refs[n_jobs + 2:-1]
    wbf_ref = refs[-1]
    _cast_weights_once(w_refs, [wbf_ref.at[c] for c in range(2 * n_w)])
    _ada_body(c_ref, wada_ref, bada_ref, mod_ref)
    for src_ref, dst_ref in zip(job_in, job_out):
        dst_ref[...] = src_ref[...].astype(BF16)
    h = h_ref[...]
    lanes = wbf_ref.shape[2]
    for c in range(n_w):
        a = jnp.dot(h, wbf_ref[c], preferred_element_type=F32)
        g = jnp.dot(h, wbf_ref[n_w + c], preferred_element_type=F32)
        o_ref[:, c * lanes:(c + 1) * lanes] = (a * _sigmoid(g)).astype(o_ref.dtype)


def _glu_proj(h, w, layer, a_col, g_col, n, tm, c_pad, w_ada, b_row, mod_col, cast_jobs=()):
    m, k = h.shape
    lanes = _EPILOGUE_LANES
    assert a_col % lanes == 0 and g_col % lanes == 0 and n % lanes == 0
    n_w = n // lanes
    steps = m // tm
    mod_n = w_ada.shape[2] - mod_col
    assert mod_n % steps == 0 and mod_col % (mod_n // steps) == 0
    mod_slab = mod_n // steps
    mod_blk0 = mod_col // mod_slab
    c_rows, d_c = c_pad.shape
    weight_tile = lambda col: pl.BlockSpec((None, k, lanes), lambda j, i: (layer, 0, col // lanes),
                                           pipeline_mode=pl.Buffered(1))
    weight_cols = [a_col + c * lanes for c in range(n_w)] + [g_col + c * lanes for c in range(n_w)]
    job_in, job_out, job_shapes = [], [], []
    for wj, row_block_of_step in cast_jobs:
        rows, cols = wj.shape[1:]
        assert rows % steps == 0
        slab = rows // steps
        job_in.append(pl.BlockSpec((None, slab, cols), lambda j, i, f=row_block_of_step: (layer, f(i), 0)))
        job_out.append(pl.BlockSpec((slab, cols), lambda j, i: (i, 0)))
        job_shapes.append(jax.ShapeDtypeStruct((rows, cols), BF16))
    return pl.pallas_call(
        functools.partial(_glu_proj_body, n_w=n_w),
        grid=(1, steps),
        in_specs=[
            pl.BlockSpec((tm, k), lambda j, i: (i, 0)),
            pl.BlockSpec((c_rows, d_c), lambda j, i: (0, 0)),
            pl.BlockSpec((None, d_c, mod_slab), lambda j, i: (layer, 0, mod_blk0 + i)),
            pl.BlockSpec((1, mod_slab), lambda j, i: (0, mod_blk0 + i)),
        ] + [weight_tile(col) for col in weight_cols] + job_in,
        out_specs=[pl.BlockSpec((tm, n), lambda j, i: (i, 0)),
                   pl.BlockSpec((c_rows, mod_slab), lambda j, i: (0, i))] + job_out,
        out_shape=[jax.ShapeDtypeStruct((m, n), BF16),
                   jax.ShapeDtypeStruct((c_rows, mod_n), F32)] + job_shapes,
        scratch_shapes=[pltpu.VMEM((2 * n_w, k, lanes), BF16)],
        compiler_params=_params(2),
        name="glu_proj",
    )(h, c_pad, w_ada, b_row, *([w] * (2 * n_w)), *[wj for wj, _ in cast_jobs])


_EPILOGUE_LANES = 512
_MATMUL_ROWS = 1024


def _merge_mix_body(ya_ref, yc_ref, wa_ref, wc_ref, ga_ref, gc_ref, w_ref, x_ref, gt_ref, g_ref, sc_ref, sh_ref,
                    x1_ref, h_ref, merged_ref):
    d = w_ref.shape[1]
    ya = ya_ref[...]
    yc = yc_ref[...]
    for n0 in range(0, d, _EPILOGUE_LANES):
        ns = slice(n0, n0 + _EPILOGUE_LANES)
        acc_a = jnp.dot(ya, wa_ref[:, ns], preferred_element_type=F32)
        acc_c = jnp.dot(yc, wc_ref[:, ns], preferred_element_type=F32)
        ga = _sigmoid(ga_ref[:, ns].astype(F32))
        gc = _sigmoid(gc_ref[:, ns].astype(F32))
        merged_ref[:, ns] = (ga * acc_a + gc * acc_c).astype(merged_ref.dtype)
    a = merged_ref[...]
    sumsq = jnp.zeros((a.shape[0], 1), F32)
    for n0 in range(0, d, _EPILOGUE_LANES):
        ns = slice(n0, n0 + _EPILOGUE_LANES)
        acc = jnp.dot(a, w_ref[:, ns], preferred_element_type=F32)
        x1 = x_ref[0, :, ns] + gt_ref[0, :, ns] * acc
        x1_ref[0, :, ns] = x1
        sumsq = sumsq + jnp.sum(x1 * x1, axis=-1, keepdims=True)
    inv = lax.rsqrt(sumsq / d + EPS)
    gain = g_ref[...] * (1.0 + sc_ref[0])
    h_ref[0] = (x1_ref[0] * inv * gain + sh_ref[0]).astype(h_ref.dtype)


def _merge_mix(attn, conv, wa_bf, wc_bf, gates, ga_col, gc_col, wmix_bf, x, mod3, gate_idx, norm_g,
               scale_idx, shift_idx, ts=256):
    b_sz, s_len, d = x.shape
    ka, kc = attn.shape[1], conv.shape[1]
    assert d % _EPILOGUE_LANES == 0 and ga_col % d == 0 and gc_col % d == 0
    tiles = s_len // ts
    const2 = lambda b, s: (0, 0)
    rows = lambda width, col_blk: pl.BlockSpec((ts, width), lambda b, s: (b * tiles + s, col_blk))
    mod_rows = mod3.shape[0] // b_sz
    mod_row = lambda idx: pl.BlockSpec((1, 1, d), lambda b, s: (b * mod_rows + idx, 0, 0))
    act = pl.BlockSpec((1, ts, d), lambda b, s: (b, s, 0))
    resident = pl.Buffered(1)
    return pl.pallas_call(
        _merge_mix_body,
        grid=(b_sz, tiles),
        in_specs=[
            rows(ka, 0),
            rows(kc, 0),
            pl.BlockSpec((ka, d), const2, pipeline_mode=resident),
            pl.BlockSpec((kc, d), const2, pipeline_mode=resident),
            rows(d, ga_col // d),
            rows(d, gc_col // d),
            pl.BlockSpec((d, d), const2, pipeline_mode=resident),
            act,
            mod_row(gate_idx),
            pl.BlockSpec((1, d), const2),
            mod_row(scale_idx),
            mod_row(shift_idx),
        ],
        out_specs=[act, act],
        out_shape=[jax.ShapeDtypeStruct((b_sz, s_len, d), F32), jax.ShapeDtypeStruct((b_sz, s_len, d), BF16)],
        scratch_shapes=[pltpu.VMEM((ts, d), BF16)],
        compiler_params=_params(2),
        name="merge_mix",
    )(attn, conv, wa_bf, wc_bf, gates, gates, wmix_bf, x, mod3, norm_g.reshape(1, d), mod3, mod3)


def _residual_proj_body(a_ref, w_ref, x_ref, gt_ref, o_ref):
    a = a_ref[...]
    for n0 in range(0, o_ref.shape[1], _EPILOGUE_LANES):
        ns = slice(n0, n0 + _EPILOGUE_LANES)
        acc = jnp.dot(a, w_ref[:, ns], preferred_element_type=F32)
        o_ref[:, ns] = x_ref[:, ns] + gt_ref[0, :, ns] * acc


def _residual_proj(a, w_bf, x2d, mod3, gate_idx, rows_per_batch, tm, tn):
    m, k = a.shape
    n = w_bf.shape[1]
    assert n % tn == 0 and tn % _EPILOGUE_LANES == 0 and rows_per_batch % tm == 0
    tiles_per_batch = rows_per_batch // tm
    mod_rows = mod3.shape[0] // (m // rows_per_batch)
    return pl.pallas_call(
        _residual_proj_body,
        grid=(n // tn, m // tm),
        in_specs=[
            pl.BlockSpec((tm, k), lambda j, i: (i, 0)),
            pl.BlockSpec((k, tn), lambda j, i: (0, j)),
            pl.BlockSpec((tm, tn), lambda j, i: (i, j)),
            pl.BlockSpec((1, 1, tn), lambda j, i: ((i // tiles_per_batch) * mod_rows + gate_idx, 0, j)),
        ],
        out_specs=pl.BlockSpec((tm, tn), lambda j, i: (i, j)),
        out_shape=jax.ShapeDtypeStruct((m, n), F32),
        compiler_params=_params(2),
        name="residual_proj",
    )(a, w_bf, x2d, mod3)


def _ffn_in_body(a_ref, wg_ref, wu_ref, wo_ref, o_ref, wobf_ref, wgbf_ref, wubf_ref):
    _cast_weights_once([wg_ref, wu_ref], [wgbf_ref, wubf_ref])
    wobf_ref[...] = wo_ref[...].astype(BF16)
    for r0 in range(0, a_ref.shape[0], _MATMUL_ROWS):
        rs = slice(r0, r0 + _MATMUL_ROWS)
        a = a_ref[rs, :]
        gate = jnp.dot(a, wgbf_ref[...], preferred_element_type=F32)
        up = jnp.dot(a, wubf_ref[...], preferred_element_type=F32)
        o_ref[rs, :] = (gate * _sigmoid(gate) * up).astype(o_ref.dtype)


def _ffn_in(h, w_ffn_in, w_ffn_out, layer, tm=2048, tn=512):
    m, k = h.shape
    d_ff = w_ffn_in.shape[2] // 2
    d_out = w_ffn_out.shape[2]
    up_blk = d_ff // tn
    m_tiles = m // tm
    steps = (d_ff // tn) * m_tiles
    assert d_ff % steps == 0
    slab = d_ff // steps
    step = lambda j, i: j * m_tiles + i
    return pl.pallas_call(
        _ffn_in_body,
        grid=(d_ff // tn, m_tiles),
        in_specs=[
            pl.BlockSpec((tm, k), lambda j, i: (i, 0)),
            pl.BlockSpec((None, k, tn), lambda j, i: (layer, 0, j)),
            pl.BlockSpec((None, k, tn), lambda j, i: (layer, 0, up_blk + j)),
            pl.BlockSpec((None, slab, d_out), lambda j, i: (layer, step(j, i), 0)),
        ],
        out_specs=[
            pl.BlockSpec((tm, tn), lambda j, i: (i, j)),
            pl.BlockSpec((slab, d_out), lambda j, i: (step(j, i), 0)),
        ],
        out_shape=[jax.ShapeDtypeStruct((m, d_ff), BF16), jax.ShapeDtypeStruct((d_ff, d_out), BF16)],
        scratch_shapes=[pltpu.VMEM((k, tn), BF16), pltpu.VMEM((k, tn), BF16)],
        compiler_params=_params(2),
        name="ffn_in",
    )(h, w_ffn_in, w_ffn_in, w_ffn_out)


def _t5_causal_bucket(dist):
    n = jnp.maximum(dist, 0)
    nf = jnp.maximum(n, 1).astype(jnp.float32)
    large = MAX_EXACT + (jnp.log(nf / MAX_EXACT) / math.log(MAX_DISTANCE / MAX_EXACT)
                         * (NUM_BUCKETS - MAX_EXACT)).astype(jnp.int32)
    large = jnp.minimum(large, NUM_BUCKETS - 1)
    return jnp.where(n < MAX_EXACT, n, large)


def _band_buckets():
    q_off = jnp.arange(BLOCK)
    k_off = jnp.arange(2 * BLOCK)
    dist = q_off[:, None] + BLOCK - k_off[None, :]
    allowed = (dist >= 0) & (dist < WINDOW)
    return jnp.where(allowed, _t5_causal_bucket(dist), -1).astype(jnp.int32)


def _attn_build_bias(first_step, rb_ref, sink_ref, bucket_ref, bias_ref, *, n_q, n_kv):
    q_per_kv = n_q // n_kv

    @pl.when(first_step)
    def _():
        bucket = bucket_ref[...]
        col = lax.broadcasted_iota(jnp.int32, bucket.shape, 1)
        for slot in range(n_q):
            head = _orig_head(slot, n_kv, q_per_kv)
            tile = jnp.zeros(bucket.shape, F32)
            for b in range(NUM_BUCKETS):
                tile = jnp.where(bucket == b, rb_ref[b * n_q + head] * LOG2E, tile)
            tile = jnp.where(bucket < 0, -jnp.inf, tile)
            sink = sink_ref[head] * LOG2E
            bias_ref[0, slot] = jnp.where(col == 0, sink, tile)
            bias_ref[1, slot] = jnp.where(col == 0, sink, jnp.where(col < BLOCK, -jnp.inf, tile))


def _attn_block(seq_start, qmask_ref, krow_ref, q_ref, kc_ref, kp_ref, vc_ref, vp_ref, o_ref,
                bias_ref, lg_ref, p_ref, pv_ref, rs_ref, *, n_q, n_kv):
    q_per_kv = n_q // n_kv
    kv_w = n_kv * HEAD_DIM
    lane_slot = lax.broadcasted_iota(jnp.int32, (BLOCK, BLOCK), 1) // HEAD_DIM
    ones = jnp.ones((2 * BLOCK, BLOCK), BF16)
    bias_idx = jnp.where(seq_start, 1, 0)

    keys = jnp.concatenate([kp_ref[...], kc_ref[...]], axis=0) * krow_ref[...]
    vals = jnp.concatenate([vp_ref[...], vc_ref[...]], axis=0) * krow_ref[...]
    lhs = jnp.concatenate(
        [q_ref[:, j * kv_w:(j + 1) * kv_w] * qmask_ref[g]
         for j in range(q_per_kv) for g in range(n_kv)], axis=0)
    lg_ref[...] = lax.dot_general(lhs, keys, (((1,), (1,)), ((), ())),
                                  preferred_element_type=F32).reshape(n_q, BLOCK, 2 * BLOCK)
    for slot in range(n_q):
        logit = lg_ref[slot] + bias_ref[bias_idx, slot]
        m = jnp.max(logit, axis=-1, keepdims=True)
        p_ref[slot * BLOCK:(slot + 1) * BLOCK, :] = jnp.exp2(logit - m).astype(BF16)
    p = p_ref[...]
    pv_ref[...] = jnp.dot(p, vals, preferred_element_type=F32).reshape(n_q, BLOCK, kv_w)
    rs_ref[...] = jnp.dot(p, ones, preferred_element_type=F32).reshape(n_q, BLOCK, BLOCK)
    per_half = BLOCK // HEAD_DIM
    for j in range(q_per_kv):
        for half in range(kv_w // BLOCK):
            lanes = slice(half * BLOCK, (half + 1) * BLOCK)
            slots = [j * n_kv + half * per_half + i for i in range(per_half)]
            num = pv_ref[slots[-1], :, lanes]
            den = rs_ref[slots[-1]]
            for i in range(per_half - 2, -1, -1):
                num = jnp.where(lane_slot == i, pv_ref[slots[i], :, lanes], num)
                den = jnp.where(lane_slot == i, rs_ref[slots[i]], den)
            o_ref[:, j * kv_w + half * BLOCK:j * kv_w + (half + 1) * BLOCK] = (
                num * (1.0 / den)).astype(o_ref.dtype)


_CONV_HALO = 32
_CONV_LANES = 256
_SUBLANES = 8


def _conv_zero_history(seq_start, u_ref):
    @pl.when(seq_start)
    def _():
        u_ref[:_CONV_HALO, :] = jnp.zeros((_CONV_HALO, u_ref.shape[1]), u_ref.dtype)


def _conv_shift_matrix(rows):
    i = jnp.arange(_SUBLANES * rows) % rows
    r = jnp.arange(_SUBLANES * rows) // rows
    return (jnp.arange(rows)[None, :] == (i + r)[:, None]).astype(BF16)


def _conv_block(uin_ref, shift_ref, w_ref, cb_ref, lg_ref, lb_ref, o_ref, u_ref, sh_ref, y_ref):
    width = w_ref.shape[0]
    ts, channels = o_ref.shape
    u_ref[_CONV_HALO:, :] = uin_ref[...]
    sh_ref[...] = jnp.dot(shift_ref[...], u_ref[...], preferred_element_type=F32).reshape(sh_ref.shape)
    first = _CONV_HALO - (width - 1)
    unknown_zero = jnp.minimum(pl.program_id(0), 0)
    for c0 in range(0, channels, _CONV_LANES):
        cs = slice(c0, c0 + _CONV_LANES)
        acc = jnp.broadcast_to(cb_ref[:, cs], (ts, _CONV_LANES))
        for j in range(width):
            tile, r = divmod(first + j, _SUBLANES)
            rows = pl.ds(pl.multiple_of(tile * _SUBLANES + unknown_zero, _SUBLANES), ts)
            acc = acc + sh_ref[r, rows, cs] * w_ref[j:j + 1, cs]
        y_ref[:, cs] = acc
    y = y_ref[...]
    mu = jnp.mean(y, axis=-1, keepdims=True)
    yc = y - mu
    var = jnp.mean(yc * yc, axis=-1, keepdims=True)
    z = yc * lax.rsqrt(var + EPS) * lg_ref[...] + lb_ref[...]
    o_ref[...] = (z * _sigmoid(z)).astype(o_ref.dtype)
    u_ref[:_CONV_HALO, :] = u_ref[ts:, :]


def _core_body(h_ref, w_ref,
               uin_ref, shift_ref, cw_ref, cb_ref, lg_ref, lb_ref,
               rb_ref, sink_ref, bucket_ref, qmask_ref, krow_ref, qkv_ref, kvp_ref,
               gates_ref, yconv_ref, yattn_ref,
               wbf_ref, u_ref, sh_ref, y_ref, bias_ref, lgt_ref, p_ref, pv_ref, rs_ref,
               *, n_q, n_kv, blocks_per_seq):
    attn_w, kv_w = n_q * HEAD_DIM, n_kv * HEAD_DIM
    t = pl.program_id(0) * pl.num_programs(1) + pl.program_id(1)
    seq_start = t % blocks_per_seq == 0
    _cast_weights_once([w_ref], [wbf_ref])
    _conv_zero_history(seq_start, u_ref)
    _attn_build_bias(t == 0, rb_ref, sink_ref, bucket_ref, bias_ref, n_q=n_q, n_kv=n_kv)

    _conv_block(uin_ref, shift_ref, cw_ref, cb_ref, lg_ref, lb_ref, yconv_ref, u_ref, sh_ref, y_ref)
    acc = jnp.dot(h_ref[...], wbf_ref[...], preferred_element_type=F32)
    gates_ref[...] = acc.astype(gates_ref.dtype)
    k_cols, v_cols = pl.ds(attn_w, kv_w), pl.ds(attn_w + kv_w, kv_w)
    _attn_block(seq_start, qmask_ref, krow_ref, qkv_ref.at[:, pl.ds(0, attn_w)],
                qkv_ref.at[:, k_cols], kvp_ref.at[:, pl.ds(0, kv_w)],
                qkv_ref.at[:, v_cols], kvp_ref.at[:, pl.ds(kv_w, kv_w)],
                yattn_ref, bias_ref, lgt_ref, p_ref, pv_ref, rs_ref, n_q=n_q, n_kv=n_kv)


def _mixer_core(h, w_in, layer, gate_col, conv_in, qkv, conv_w, conv_b, ln_g, ln_b, rel_bias, sinks,
                s_len, n_q, n_kv, tn=512):
    m, k = h.shape
    n = w_in.shape[2] - gate_col
    width, channels = conv_w.shape
    attn_w, kv_w = n_q * HEAD_DIM, n_kv * HEAD_DIM
    n_blocks = m // BLOCK
    n_tiles = n // tn
    assert gate_col % tn == 0 and n % tn == 0 and n_blocks % n_tiles == 0
    assert s_len % BLOCK == 0 and width - 1 <= _CONV_HALO and _CONV_HALO % _SUBLANES == 0
    assert attn_w % (2 * kv_w) == 0 and kv_w == 2 * BLOCK and BLOCK % HEAD_DIM == 0
    m_tiles = n_blocks // n_tiles
    tm = m // m_tiles
    col_blk0 = gate_col // tn
    kv_blk = attn_w // (2 * kv_w)

    step = lambda j, i: j * m_tiles + i
    prev_step = lambda j, i: jnp.maximum(step(j, i) - 1, 0)
    const2 = lambda j, i: (0, 0)
    row = lambda v: v.reshape(1, channels)
    lane_group = jnp.arange(kv_w) // HEAD_DIM
    qmask = jnp.broadcast_to((lane_group[None, :] == jnp.arange(n_kv)[:, None])[:, None, :],
                             (n_kv, BLOCK, kv_w)).astype(BF16)
    krow = jnp.broadcast_to((jnp.arange(2 * BLOCK) > 0)[:, None], (2 * BLOCK, kv_w)).astype(BF16)
    body = functools.partial(_core_body, n_q=n_q, n_kv=n_kv, blocks_per_seq=s_len // BLOCK)
    return pl.pallas_call(
        body,
        grid=(n_tiles, m_tiles),
        in_specs=[
            pl.BlockSpec((tm, k), lambda j, i: (i, 0)),
            pl.BlockSpec((None, k, tn), lambda j, i: (layer, 0, col_blk0 + j)),
            pl.BlockSpec((BLOCK, channels), lambda j, i: (step(j, i), 0)),
            pl.BlockSpec((_SUBLANES * (_CONV_HALO + BLOCK), _CONV_HALO + BLOCK), const2),
            pl.BlockSpec((width, channels), const2),
            pl.BlockSpec((1, channels), const2),
            pl.BlockSpec((1, channels), const2),
            pl.BlockSpec((1, channels), const2),
            pl.BlockSpec(memory_space=pltpu.SMEM),
            pl.BlockSpec(memory_space=pltpu.SMEM),
            pl.BlockSpec((BLOCK, 2 * BLOCK), const2),
            pl.BlockSpec((n_kv, BLOCK, kv_w), lambda j, i: (0, 0, 0)),
            pl.BlockSpec((2 * BLOCK, kv_w), const2),
            pl.BlockSpec((BLOCK, attn_w + 2 * kv_w), lambda j, i: (step(j, i), 0)),
            pl.BlockSpec((BLOCK, 2 * kv_w), lambda j, i: (prev_step(j, i), kv_blk)),
        ],
        out_specs=[
            pl.BlockSpec((tm, tn), lambda j, i: (i, j)),
            pl.BlockSpec((BLOCK, channels), lambda j, i: (step(j, i), 0)),
            pl.BlockSpec((BLOCK, attn_w), lambda j, i: (step(j, i), 0)),
        ],
        out_shape=[
            jax.ShapeDtypeStruct((m, n), BF16),
            jax.ShapeDtypeStruct((m, channels), BF16),
            jax.ShapeDtypeStruct((m, attn_w), BF16),
        ],
        scratch_shapes=[
            pltpu.VMEM((k, tn), BF16),
            pltpu.VMEM((_CONV_HALO + BLOCK, channels), BF16),
            pltpu.VMEM((_SUBLANES, _CONV_HALO + BLOCK, channels), F32),
            pltpu.VMEM((BLOCK, channels), F32),
            pltpu.VMEM((2, n_q, BLOCK, 2 * BLOCK), F32),
            pltpu.VMEM((n_q, BLOCK, 2 * BLOCK), F32),
            pltpu.VMEM((n_q * BLOCK, 2 * BLOCK), BF16),
            pltpu.VMEM((n_q, BLOCK, kv_w), F32),
            pltpu.VMEM((n_q, BLOCK, BLOCK), F32),
        ],
        compiler_params=_params(2),
        name="mixer_core",
    )(h, w_in,
      conv_in, _conv_shift_matrix(_CONV_HALO + BLOCK), conv_w, row(conv_b), row(ln_g), row(ln_b),
      rel_bias.astype(F32).reshape(-1), sinks.astype(F32), _band_buckets(), qmask, krow,
      qkv, qkv)


def kernel(x, c, w_ada, b_ada, norm_mix_g, w_in, q_norm_g, k_norm_g, attn_sinks, rel_bias,
           w_attn_out, conv_w, conv_b, conv_ln_g, conv_ln_b, w_conv_out, w_mix_out,
           norm_ffn_g, w_ffn_in, w_ffn_out):
    b_sz, s_len, d = x.shape
    depth = w_ada.shape[0]
    n_q = attn_sinks.shape[1]
    attn_w = w_attn_out.shape[1]
    channels = conv_w.shape[2]
    kv_w = (w_in.shape[2] - attn_w - 2 * channels - 2 * d) // 2
    n_kv = kv_w // HEAD_DIM
    conv_col = attn_w + 2 * kv_w
    gate_col = conv_col + 2 * channels
    m = b_sz * s_len

    for l in range(depth):
        c_pad = jnp.pad(c, ((0, 8 - b_sz), (0, 0)))
        b_row = b_ada[l].reshape(1, N_MOD * d)
        mod_early = _ada(c_pad, w_ada, b_row, l, N_MOD_EARLY * d)[:b_sz].reshape(b_sz * N_MOD_EARLY, 1, d)

        qkv, h = _qkv_proj(x, norm_mix_g[l], mod_early, 1, 0, w_in, l, q_norm_g[l], k_norm_g[l], n_q, n_kv)
        h = h.reshape(m, d)
        assert m % n_q == 0
        slot_head = lambda slot: _orig_head(slot, n_kv, n_q // n_kv)
        same = lambda step: step
        conv_in, mod_late, wa_bf, wc_bf, wmix_bf = _glu_proj(
            h, w_in, l, conv_col, conv_col + channels, channels, m // n_q,
            c_pad, w_ada, b_row, N_MOD_EARLY * d,
            cast_jobs=((w_attn_out, slot_head), (w_conv_out, same), (w_mix_out, same)))
        mod_late = mod_late[:b_sz].reshape(b_sz * (N_MOD - N_MOD_EARLY), 1, d)
        gates, y_conv, y_attn = _mixer_core(h, w_in, l, gate_col, conv_in, qkv, conv_w[l], conv_b[l],
                                            conv_ln_g[l], conv_ln_b[l], rel_bias, attn_sinks[l],
                                            s_len, n_q, n_kv)
        x1, h = _merge_mix(y_attn, y_conv, wa_bf, wc_bf, gates, 0, d, wmix_bf, x, mod_late, 0,
                           norm_ffn_g[l], 2, 1)

        act, wo_bf = _ffn_in(h.reshape(m, d), w_ffn_in, w_ffn_out, l)
        x = _residual_proj(act, wo_bf, x1.reshape(m, d), mod_late, 3, s_len,
                           tm=512, tn=1024).reshape(b_sz, s_len, d)
    return x
```

```python
import functools
import math

import jax
import jax.numpy as jnp
import numpy as np
from jax import lax
from jax.experimental import pallas as pl
from jax.experimental.pallas import tpu as pltpu

F32 = jnp.float32
BF16 = jnp.bfloat16

HEAD_DIM = 64
WINDOW = 128
BLOCK = 128
NUM_BUCKETS = 32
MAX_EXACT = NUM_BUCKETS // 2
MAX_DISTANCE = 128
N_MOD = 6
N_MOD_EARLY = 2
EPS = 1e-6
LOG2E = math.log2(math.e)

V7X_VMEM_LIMIT_BYTES = 56 * 1024 * 1024


def _params(n_axes):
    return pltpu.CompilerParams(
        dimension_semantics=("arbitrary",) * n_axes,
        vmem_limit_bytes=V7X_VMEM_LIMIT_BYTES,
    )


def _sigmoid(x):
    return jax.nn.sigmoid(x)


def _orig_head(slot, n_kv, q_per_kv):
    return (slot % n_kv) * q_per_kv + slot // n_kv


def _ada_body(c_ref, w_ref, b_ref, o_ref):
    c = c_ref[...]
    act = (c * _sigmoid(c)).astype(BF16)
    o_ref[...] = jnp.dot(act, w_ref[...].astype(BF16), preferred_element_type=F32) + b_ref[...]


def _ada(c_pad, w_ada, b_row, layer, n, tn=2048):
    rows, d = c_pad.shape
    assert n % tn == 0
    return pl.pallas_call(
        _ada_body,
        grid=(n // tn,),
        in_specs=[
            pl.BlockSpec((rows, d), lambda j: (0, 0)),
            pl.BlockSpec((None, d, tn), lambda j: (layer, 0, j)),
            pl.BlockSpec((1, tn), lambda j: (0, j)),
        ],
        out_specs=pl.BlockSpec((rows, tn), lambda j: (0, j)),
        out_shape=jax.ShapeDtypeStruct((rows, n), F32),
        compiler_params=_params(1),
        name="ada_mod",
    )(c_pad, w_ada, b_row)


def _norm_mod_body(x_ref, g_ref, sc_ref, sh_ref, o_ref):
    x = x_ref[0]
    ms = jnp.mean(x * x, axis=-1, keepdims=True)
    gain = g_ref[...] * (1.0 + sc_ref[0])
    o_ref[0] = (x * lax.rsqrt(ms + EPS) * gain + sh_ref[0]).astype(o_ref.dtype)


def _qkv_body(x_ref, g_ref, sc_ref, sh_ref, w_ref, gq_ref, gk_ref, o_ref, h_ref, wbf_ref, seg_ref,
              *, n_q, n_kv, half):
    attn_w = n_q * HEAD_DIM
    kv_w = n_kv * HEAD_DIM

    @pl.when(jnp.logical_and(pl.program_id(0) == 0, pl.program_id(1) == 0))
    def _():
        for slot in range(n_q):
            src = _orig_head(slot, n_kv, n_q // n_kv) * HEAD_DIM
            wbf_ref[:, slot * HEAD_DIM:(slot + 1) * HEAD_DIM] = w_ref[:, src:src + HEAD_DIM].astype(BF16)
        wbf_ref[:, attn_w:] = w_ref[:, attn_w:].astype(BF16)
        r = lax.broadcasted_iota(jnp.int32, (half, half), 0) // HEAD_DIM
        c = lax.broadcasted_iota(jnp.int32, (half, half), 1) // HEAD_DIM
        seg_ref[...] = (r == c).astype(BF16)

    def head_norm(acc, seg, gain):
        ss = jnp.dot((acc * acc).astype(BF16), seg, preferred_element_type=F32)
        return acc * lax.rsqrt(ss / HEAD_DIM + EPS) * gain

    for r0 in range(0, x_ref.shape[1], _QKV_CHAIN_ROWS):
        rs = slice(r0, r0 + _QKV_CHAIN_ROWS)
        x = x_ref[0, rs, :]
        ms = jnp.mean(x * x, axis=-1, keepdims=True)
        gain = g_ref[...] * (1.0 + sc_ref[0])
        a = (x * lax.rsqrt(ms + EPS) * gain + sh_ref[0]).astype(BF16)
        h_ref[0, rs, :] = a
        for c0 in range(0, attn_w, half):
            acc = jnp.dot(a, wbf_ref[:, c0:c0 + half], preferred_element_type=F32)
            o_ref[rs, c0:c0 + half] = head_norm(acc, seg_ref[...], gq_ref[...]).astype(o_ref.dtype)
        acc = jnp.dot(a, wbf_ref[:, attn_w:], preferred_element_type=F32)
        o_ref[rs, attn_w:attn_w + kv_w] = head_norm(acc[:, :kv_w], seg_ref[:kv_w, :kv_w],
                                                    gk_ref[...]).astype(o_ref.dtype)
        o_ref[rs, attn_w + kv_w:] = acc[:, kv_w:].astype(o_ref.dtype)


_QKV_CHAIN_ROWS = 512


def _qkv_proj(x, norm_g, mod3, scale_idx, shift_idx, w_in, layer, q_g, k_g, n_q, n_kv, ts=1024, half=512):
    b_sz, s_len, d = x.shape
    attn_w, kv_w = n_q * HEAD_DIM, n_kv * HEAD_DIM
    width = attn_w + 2 * kv_w
    assert attn_w % half == 0 and kv_w <= half and half % HEAD_DIM == 0
    gq = jnp.tile(q_g * (HEAD_DIM ** -0.5 * LOG2E), half // HEAD_DIM).reshape(1, half)
    gk = jnp.tile(k_g, n_kv).reshape(1, kv_w)
    tiles = s_len // ts
    mod_rows = mod3.shape[0] // b_sz
    const2 = lambda b, s: (0, 0)
    body = functools.partial(_qkv_body, n_q=n_q, n_kv=n_kv, half=half)
    return pl.pallas_call(
        body,
        grid=(b_sz, tiles),
        in_specs=[
            pl.BlockSpec((1, ts, d), lambda b, s: (b, s, 0)),
            pl.BlockSpec((1, d), const2),
            pl.BlockSpec((1, 1, d), lambda b, s: (b * mod_rows + scale_idx, 0, 0)),
            pl.BlockSpec((1, 1, d), lambda b, s: (b * mod_rows + shift_idx, 0, 0)),
            pl.BlockSpec((None, d, width), lambda b, s: (layer, 0, 0), pipeline_mode=pl.Buffered(1)),
            pl.BlockSpec((1, half), const2),
            pl.BlockSpec((1, kv_w), const2),
        ],
        out_specs=[
            pl.BlockSpec((ts, width), lambda b, s: (b * tiles + s, 0)),
            pl.BlockSpec((1, ts, d), lambda b, s: (b, s, 0)),
        ],
        out_shape=[
            jax.ShapeDtypeStruct((b_sz * s_len, width), BF16),
            jax.ShapeDtypeStruct((b_sz, s_len, d), BF16),
        ],
        scratch_shapes=[pltpu.VMEM((d, width), BF16), pltpu.VMEM((half, half), BF16)],
        compiler_params=_params(2),
        name="qkv_proj",
    )(x, norm_g.reshape(1, d), mod3, mod3, w_in, gq, gk)


def _cast_weights_once(w_refs, wbf_refs):
    @pl.when(pl.program_id(1) == 0)
    def _():
        for w_ref, wbf_ref in zip(w_refs, wbf_refs):
            wbf_ref[...] = w_ref[...].astype(BF16)


def _glu_proj_body(h_ref, c_ref, wada_ref, bada_ref, *refs, n_w):
    w_refs, refs = refs[:2 * n_w], refs[2 * n_w:]
    n_jobs = (len(refs) - 3) // 2
    job_in, o_ref, mod_ref, job_out = refs[:n_jobs], refs[n_jobs], refs[n_jobs + 1], refs[n_jobs + 2:-1]
    wbf_ref = refs[-1]
    _cast_weights_once(w_refs, [wbf_ref.at[c] for c in range(2 * n_w)])
    _ada_body(c_ref, wada_ref, bada_ref, mod_ref)
    for src_ref, dst_ref in zip(job_in, job_out):
        dst_ref[...] = src_ref[...].astype(BF16)
    h = h_ref[...]
    lanes = wbf_ref.shape[2]
    for c in range(n_w):
        a = jnp.dot(h, wbf_ref[c], preferred_element_type=F32)
        g = jnp.dot(h, wbf_ref[n_w + c], preferred_element_type=F32)
        o_ref[:, c * lanes:(c + 1) * lanes] = (a * _sigmoid(g)).astype(o_ref.dtype)


def _glu_proj(h, w, layer, a_col, g_col, n, tm, c_pad, w_ada, b_row, mod_col, cast_jobs=()):
    m, k = h.shape
    lanes = _EPILOGUE_LANES
    assert a_col % lanes == 0 and g_col % lanes == 0 and n % lanes == 0
    n_w = n // lanes
    steps = m // tm
    mod_n = w_ada.shape[2] - mod_col
    assert mod_n % steps == 0 and mod_col % (mod_n // steps) == 0
    mod_slab = mod_n // steps
    mod_blk0 = mod_col // mod_slab
    c_rows, d_c = c_pad.shape
    weight_tile = lambda col: pl.BlockSpec((None, k, lanes), lambda j, i: (layer, 0, col // lanes),
                                           pipeline_mode=pl.Buffered(1))
    weight_cols = [a_col + c * lanes for c in range(n_w)] + [g_col + c * lanes for c in range(n_w)]
    job_in, job_out, job_shapes = [], [], []
    for wj, row_block_of_step in cast_jobs:
        rows, cols = wj.shape[1:]
        assert rows % steps == 0
        slab = rows // steps
        job_in.append(pl.BlockSpec((None, slab, cols), lambda j, i, f=row_block_of_step: (layer, f(i), 0)))
        job_out.append(pl.BlockSpec((slab, cols), lambda j, i: (i, 0)))
        job_shapes.append(jax.ShapeDtypeStruct((rows, cols), BF16))
    return pl.pallas_call(
        functools.partial(_glu_proj_body, n_w=n_w),
        grid=(1, steps),
        in_specs=[
            pl.BlockSpec((tm, k), lambda j, i: (i, 0)),
            pl.BlockSpec((c_rows, d_c), lambda j, i: (0, 0)),
            pl.BlockSpec((None, d_c, mod_slab), lambda j, i: (layer, 0, mod_blk0 + i)),
            pl.BlockSpec((1, mod_slab), lambda j, i: (0, mod_blk0 + i)),
        ] + [weight_tile(col) for col in weight_cols] + job_in,
        out_specs=[pl.BlockSpec((tm, n), lambda j, i: (i, 0)),
                   pl.BlockSpec((c_rows, mod_slab), lambda j, i: (0, i))] + job_out,
        out_shape=[jax.ShapeDtypeStruct((m, n), BF16),
                   jax.ShapeDtypeStruct((c_rows, mod_n), F32)] + job_shapes,
        scratch_shapes=[pltpu.VMEM((2 * n_w, k, lanes), BF16)],
        compiler_params=_params(2),
        name="glu_proj",
    )(h, c_pad, w_ada, b_row, *([w] * (2 * n_w)), *[wj for wj, _ in cast_jobs])


_EPILOGUE_LANES = 512
_MATMUL_ROWS = 1024


def _merge_mix_body(ya_ref, yc_ref, wa_ref, wc_ref, ga_ref, gc_ref, w_ref, x_ref, gt_ref, g_ref, sc_ref, sh_ref,
                    x1_ref, h_ref, merged_ref):
    d = w_ref.shape[1]
    ya = ya_ref[...]
    yc = yc_ref[...]
    for n0 in range(0, d, _EPILOGUE_LANES):
        ns = slice(n0, n0 + _EPILOGUE_LANES)
        acc_a = jnp.dot(ya, wa_ref[:, ns], preferred_element_type=F32)
        acc_c = jnp.dot(yc, wc_ref[:, ns], preferred_element_type=F32)
        ga = _sigmoid(ga_ref[:, ns].astype(F32))
        gc = _sigmoid(gc_ref[:, ns].astype(F32))
        merged_ref[:, ns] = (ga * acc_a + gc * acc_c).astype(merged_ref.dtype)
    a = merged_ref[...]
    sumsq = jnp.zeros((a.shape[0], 1), F32)
    for n0 in range(0, d, _EPILOGUE_LANES):
        ns = slice(n0, n0 + _EPILOGUE_LANES)
        acc = jnp.dot(a, w_ref[:, ns], preferred_element_type=F32)
        x1 = x_ref[0, :, ns] + gt_ref[0, :, ns] * acc
        x1_ref[0, :, ns] = x1
        sumsq = sumsq + jnp.sum(x1 * x1, axis=-1, keepdims=True)
    inv = lax.rsqrt(sumsq / d + EPS)
    gain = g_ref[...] * (1.0 + sc_ref[0])
    h_ref[0] = (x1_ref[0] * inv * gain + sh_ref[0]).astype(h_ref.dtype)


def _merge_mix(attn, conv, wa_bf, wc_bf, gates, ga_col, gc_col, wmix_bf, x, mod3, gate_idx, norm_g,
               scale_idx, shift_idx, ts=256):
    b_sz, s_len, d = x.shape
    ka, kc = attn.shape[1], conv.shape[1]
    assert d % _EPILOGUE_LANES == 0 and ga_col % d == 0 and gc_col % d == 0
    tiles = s_len // ts
    const2 = lambda b, s: (0, 0)
    rows = lambda width, col_blk: pl.BlockSpec((ts, width), lambda b, s: (b * tiles + s, col_blk))
    mod_rows = mod3.shape[0] // b_sz
    mod_row = lambda idx: pl.BlockSpec((1, 1, d), lambda b, s: (b * mod_rows + idx, 0, 0))
    act = pl.BlockSpec((1, ts, d), lambda b, s: (b, s, 0))
    resident = pl.Buffered(1)
    return pl.pallas_call(
        _merge_mix_body,
        grid=(b_sz, tiles),
        in_specs=[
            rows(ka, 0),
            rows(kc, 0),
            pl.BlockSpec((ka, d), const2, pipeline_mode=resident),
            pl.BlockSpec((kc, d), const2, pipeline_mode=resident),
            rows(d, ga_col // d),
            rows(d, gc_col // d),
            pl.BlockSpec((d, d), const2, pipeline_mode=resident),
            act,
            mod_row(gate_idx),
            pl.BlockSpec((1, d), const2),
            mod_row(scale_idx),
            mod_row(shift_idx),
        ],
        out_specs=[act, act],
        out_shape=[jax.ShapeDtypeStruct((b_sz, s_len, d), F32), jax.ShapeDtypeStruct((b_sz, s_len, d), BF16)],
        scratch_shapes=[pltpu.VMEM((ts, d), BF16)],
        compiler_params=_params(2),
        name="merge_mix",
    )(attn, conv, wa_bf, wc_bf, gates, gates, wmix_bf, x, mod3, norm_g.reshape(1, d), mod3, mod3)


def _residual_proj_body(a_ref, w_ref, x_ref, gt_ref, o_ref):
    a = a_ref[...]
    for n0 in range(0, o_ref.shape[1], _EPILOGUE_LANES):
        ns = slice(n0, n0 + _EPILOGUE_LANES)
        acc = jnp.dot(a, w_ref[:, ns], preferred_element_type=F32)
        o_ref[:, ns] = x_ref[:, ns] + gt_ref[0, :, ns] * acc


def _residual_proj(a, w_bf, x2d, mod3, gate_idx, rows_per_batch, tm, tn):
    m, k = a.shape
    n = w_bf.shape[1]
    assert n % tn == 0 and tn % _EPILOGUE_LANES == 0 and rows_per_batch % tm == 0
    tiles_per_batch = rows_per_batch // tm
    mod_rows = mod3.shape[0] // (m // rows_per_batch)
    return pl.pallas_call(
        _residual_proj_body,
        grid=(n // tn, m // tm),
        in_specs=[
            pl.BlockSpec((tm, k), lambda j, i: (i, 0)),
            pl.BlockSpec((k, tn), lambda j, i: (0, j)),
            pl.BlockSpec((tm, tn), lambda j, i: (i, j)),
            pl.BlockSpec((1, 1, tn), lambda j, i: ((i // tiles_per_batch) * mod_rows + gate_idx, 0, j)),
        ],
        out_specs=pl.BlockSpec((tm, tn), lambda j, i: (i, j)),
        out_shape=jax.ShapeDtypeStruct((m, n), F32),
        compiler_params=_params(2),
        name="residual_proj",
    )(a, w_bf, x2d, mod3)


def _ffn_in_body(a_ref, wg_ref, wu_ref, wo_ref, o_ref, wobf_ref, wgbf_ref, wubf_ref):
    _cast_weights_once([wg_ref, wu_ref], [wgbf_ref, wubf_ref])
    wobf_ref[...] = wo_ref[...].astype(BF16)
    for r0 in range(0, a_ref.shape[0], _MATMUL_ROWS):
        rs = slice(r0, r0 + _MATMUL_ROWS)
        a = a_ref[rs, :]
        gate = jnp.dot(a, wgbf_ref[...], preferred_element_type=F32)
        up = jnp.dot(a, wubf_ref[...], preferred_element_type=F32)
        o_ref[rs, :] = (gate * _sigmoid(gate) * up).astype(o_ref.dtype)


def _ffn_in(h, w_ffn_in, w_ffn_out, layer, tm=2048, tn=512):
    m, k = h.shape
    d_ff = w_ffn_in.shape[2] // 2
    d_out = w_ffn_out.shape[2]
    up_blk = d_ff // tn
    m_tiles = m // tm
    steps = (d_ff // tn) * m_tiles
    assert d_ff % steps == 0
    slab = d_ff // steps
    step = lambda j, i: j * m_tiles + i
    return pl.pallas_call(
        _ffn_in_body,
        grid=(d_ff // tn, m_tiles),
        in_specs=[
            pl.BlockSpec((tm, k), lambda j, i: (i, 0)),
            pl.BlockSpec((None, k, tn), lambda j, i: (layer, 0, j)),
            pl.BlockSpec((None, k, tn), lambda j, i: (layer, 0, up_blk + j)),
            pl.BlockSpec((None, slab, d_out), lambda j, i: (layer, step(j, i), 0)),
        ],
        out_specs=[
            pl.BlockSpec((tm, tn), lambda j, i: (i, j)),
            pl.BlockSpec((slab, d_out), lambda j, i: (step(j, i), 0)),
        ],
        out_shape=[jax.ShapeDtypeStruct((m, d_ff), BF16), jax.ShapeDtypeStruct((d_ff, d_out), BF16)],
        scratch_shapes=[pltpu.VMEM((k, tn), BF16), pltpu.VMEM((k, tn), BF16)],
        compiler_params=_params(2),
        name="ffn_in",
    )(h, w_ffn_in, w_ffn_in, w_ffn_out)


def _t5_causal_bucket(dist):
    n = jnp.maximum(dist, 0)
    nf = jnp.maximum(n, 1).astype(jnp.float32)
    large = MAX_EXACT + (jnp.log(nf / MAX_EXACT) / math.log(MAX_DISTANCE / MAX_EXACT)
                         * (NUM_BUCKETS - MAX_EXACT)).astype(jnp.int32)
    large = jnp.minimum(large, NUM_BUCKETS - 1)
    return jnp.where(n < MAX_EXACT, n, large)


def _band_buckets():
    q_off = jnp.arange(BLOCK)
    k_off = jnp.arange(2 * BLOCK)
    dist = q_off[:, None] + BLOCK - k_off[None, :]
    allowed = (dist >= 0) & (dist < WINDOW)
    return jnp.where(allowed, _t5_causal_bucket(dist), -1).astype(jnp.int32)


def _attn_build_bias(first_step, rb_ref, sink_ref, bucket_ref, bias_ref, *, n_q, n_kv):
    q_per_kv = n_q // n_kv

    @pl.when(first_step)
    def _():
        bucket = bucket_ref[...]
        col = lax.broadcasted_iota(jnp.int32, bucket.shape, 1)
        for slot in range(n_q):
            head = _orig_head(slot, n_kv, q_per_kv)
            tile = jnp.zeros(bucket.shape, F32)
            for b in range(NUM_BUCKETS):
                tile = jnp.where(bucket == b, rb_ref[b * n_q + head] * LOG2E, tile)
            tile = jnp.where(bucket < 0, -jnp.inf, tile)
            sink = sink_ref[head] * LOG2E
            bias_ref[0, slot] = jnp.where(col == 0, sink, tile)
            bias_ref[1, slot] = jnp.where(col == 0, sink, jnp.where(col < BLOCK, -jnp.inf, tile))


def _attn_block(seq_start, qmask_ref, krow_ref, q_ref, kc_ref, kp_ref, vc_ref, vp_ref, o_ref,
                bias_ref, lg_ref, p_ref, pv_ref, rs_ref, *, n_q, n_kv):
    q_per_kv = n_q // n_kv
    kv_w = n_kv * HEAD_DIM
    lane_slot = lax.broadcasted_iota(jnp.int32, (BLOCK, BLOCK), 1) // HEAD_DIM
    ones = jnp.ones((2 * BLOCK, BLOCK), BF16)
    bias_idx = jnp.where(seq_start, 1, 0)

    keys = jnp.concatenate([kp_ref[...], kc_ref[...]], axis=0) * krow_ref[...]
    vals = jnp.concatenate([vp_ref[...], vc_ref[...]], axis=0) * krow_ref[...]
    lhs = jnp.concatenate(
        [q_ref[:, j * kv_w:(j + 1) * kv_w] * qmask_ref[g]
         for j in range(q_per_kv) for g in range(n_kv)], axis=0)
    lg_ref[...] = lax.dot_general(lhs, keys, (((1,), (1,)), ((), ())),
                                  preferred_element_type=F32).reshape(n_q, BLOCK, 2 * BLOCK)
    for slot in range(n_q):
        logit = lg_ref[slot] + bias_ref[bias_idx, slot]
        m = jnp.max(logit, axis=-1, keepdims=True)
        p_ref[slot * BLOCK:(slot + 1) * BLOCK, :] = jnp.exp2(logit - m).astype(BF16)
    p = p_ref[...]
    pv_ref[...] = jnp.dot(p, vals, preferred_element_type=F32).reshape(n_q, BLOCK, kv_w)
    rs_ref[...] = jnp.dot(p, ones, preferred_element_type=F32).reshape(n_q, BLOCK, BLOCK)
    per_half = BLOCK // HEAD_DIM
    for j in range(q_per_kv):
        for half in range(kv_w // BLOCK):
            lanes = slice(half * BLOCK, (half + 1) * BLOCK)
            slots = [j * n_kv + half * per_half + i for i in range(per_half)]
            num = pv_ref[slots[-1], :, lanes]
            den = rs_ref[slots[-1]]
            for i in range(per_half - 2, -1, -1):
                num = jnp.where(lane_slot == i, pv_ref[slots[i], :, lanes], num)
                den = jnp.where(lane_slot == i, rs_ref[slots[i]], den)
            o_ref[:, j * kv_w + half * BLOCK:j * kv_w + (half + 1) * BLOCK] = (
                num * (1.0 / den)).astype(o_ref.dtype)


_CONV_HALO = 32
_CONV_LANES = 256
_SUBLANES = 8


def _conv_zero_history(seq_start, u_ref):
    @pl.when(seq_start)
    def _():
        u_ref[:_CONV_HALO, :] = jnp.zeros((_CONV_HALO, u_ref.shape[1]), u_ref.dtype)


def _conv_shift_matrix(rows):
    i = np.arange(_SUBLANES * rows) % rows
    r = np.arange(_SUBLANES * rows) // rows
    return (np.arange(rows)[None, :] == (i + r)[:, None]).astype(BF16)


def _conv_block(uin_ref, shift_ref, w_ref, cb_ref, lg_ref, lb_ref, o_ref, u_ref, sh_ref, y_ref):
    width = w_ref.shape[0]
    ts, channels = o_ref.shape
    u_ref[_CONV_HALO:, :] = uin_ref[...]
    sh_ref[...] = jnp.dot(shift_ref[...], u_ref[...], preferred_element_type=F32).reshape(sh_ref.shape)
    first = _CONV_HALO - (width - 1)
    unknown_zero = jnp.minimum(pl.program_id(0), 0)
    for c0 in range(0, channels, _CONV_LANES):
        cs = slice(c0, c0 + _CONV_LANES)
        acc = jnp.broadcast_to(cb_ref[:, cs], (ts, _CONV_LANES))
        for j in range(width):
            tile, r = divmod(first + j, _SUBLANES)
            rows = pl.ds(pl.multiple_of(tile * _SUBLANES + unknown_zero, _SUBLANES), ts)
            acc = acc + sh_ref[r, rows, cs] * w_ref[j:j + 1, cs]
        y_ref[:, cs] = acc
    y = y_ref[...]
    mu = jnp.mean(y, axis=-1, keepdims=True)
    yc = y - mu
    var = jnp.mean(yc * yc, axis=-1, keepdims=True)
    z = yc * lax.rsqrt(var + EPS) * lg_ref[...] + lb_ref[...]
    o_ref[...] = (z * _sigmoid(z)).astype(o_ref.dtype)
    u_ref[:_CONV_HALO, :] = u_ref[ts:, :]


def _core_body(h_ref, w_ref,
               uin_ref, shift_ref, cw_ref, cb_ref, lg_ref, lb_ref,
               rb_ref, sink_ref, bucket_ref, qmask_ref, krow_ref, qkv_ref, kvp_ref,
               gates_ref, yconv_ref, yattn_ref,
               wbf_ref, u_ref, sh_ref, y_ref, bias_ref, lgt_ref, p_ref, pv_ref, rs_ref,
               *, n_q, n_kv, blocks_per_seq):
    attn_w, kv_w = n_q * HEAD_DIM, n_kv * HEAD_DIM
    t = pl.program_id(0) * pl.num_programs(1) + pl.program_id(1)
    seq_start = t % blocks_per_seq == 0
    _cast_weights_once([w_ref], [wbf_ref])
    _conv_zero_history(seq_start, u_ref)
    _attn_build_bias(t == 0, rb_ref, sink_ref, bucket_ref, bias_ref, n_q=n_q, n_kv=n_kv)

    _conv_block(uin_ref, shift_ref, cw_ref, cb_ref, lg_ref, lb_ref, yconv_ref, u_ref, sh_ref, y_ref)
    acc = jnp.dot(h_ref[...], wbf_ref[...], preferred_element_type=F32)
    gates_ref[...] = acc.astype(gates_ref.dtype)
    k_cols, v_cols = pl.ds(attn_w, kv_w), pl.ds(attn_w + kv_w, kv_w)
    _attn_block(seq_start, qmask_ref, krow_ref, qkv_ref.at[:, pl.ds(0, attn_w)],
                qkv_ref.at[:, k_cols], kvp_ref.at[:, pl.ds(0, kv_w)],
                qkv_ref.at[:, v_cols], kvp_ref.at[:, pl.ds(kv_w, kv_w)],
                yattn_ref, bias_ref, lgt_ref, p_ref, pv_ref, rs_ref, n_q=n_q, n_kv=n_kv)


def _mixer_core(h, w_in, layer, gate_col, conv_in, qkv, conv_w, conv_b, ln_g, ln_b, rel_bias, sinks,
                s_len, n_q, n_kv, tn=512):
    m, k = h.shape
    n = w_in.shape[2] - gate_col
    width, channels = conv_w.shape
    attn_w, kv_w = n_q * HEAD_DIM, n_kv * HEAD_DIM
    n_blocks = m // BLOCK
    n_tiles = n // tn
    assert gate_col % tn == 0 and n % tn == 0 and n_blocks % n_tiles == 0
    assert s_len % BLOCK == 0 and width - 1 <= _CONV_HALO and _CONV_HALO % _SUBLANES == 0
    assert attn_w % (2 * kv_w) == 0 and kv_w == 2 * BLOCK and BLOCK % HEAD_DIM == 0
    m_tiles = n_blocks // n_tiles
    tm = m // m_tiles
    col_blk0 = gate_col // tn
    kv_blk = attn_w // (2 * kv_w)

    step = lambda j, i: j * m_tiles + i
    prev_step = lambda j, i: jnp.maximum(step(j, i) - 1, 0)
    const2 = lambda j, i: (0, 0)
    row = lambda v: v.reshape(1, channels)
    lane_group = np.arange(kv_w) // HEAD_DIM
    qmask = np.broadcast_to((lane_group[None, :] == np.arange(n_kv)[:, None])[:, None, :],
                            (n_kv, BLOCK, kv_w)).astype(BF16)
    krow = np.broadcast_to((np.arange(2 * BLOCK) > 0)[:, None], (2 * BLOCK, kv_w)).astype(BF16)
    body = functools.partial(_core_body, n_q=n_q, n_kv=n_kv, blocks_per_seq=s_len // BLOCK)
    return pl.pallas_call(
        body,
        grid=(n_tiles, m_tiles),
        in_specs=[
            pl.BlockSpec((tm, k), lambda j, i: (i, 0)),
            pl.BlockSpec((None, k, tn), lambda j, i: (layer, 0, col_blk0 + j)),
            pl.BlockSpec((BLOCK, channels), lambda j, i: (step(j, i), 0)),
            pl.BlockSpec((_SUBLANES * (_CONV_HALO + BLOCK), _CONV_HALO + BLOCK), const2),
            pl.BlockSpec((width, channels), const2),
            pl.BlockSpec((1, channels), const2),
            pl.BlockSpec((1, channels), const2),
            pl.BlockSpec((1, channels), const2),
            pl.BlockSpec(memory_space=pltpu.SMEM),
            pl.BlockSpec(memory_space=pltpu.SMEM),
            pl.BlockSpec((BLOCK, 2 * BLOCK), const2),
            pl.BlockSpec((n_kv, BLOCK, kv_w), lambda j, i: (0, 0, 0)),
            pl.BlockSpec((2 * BLOCK, kv_w), const2),
            pl.BlockSpec((BLOCK, attn_w + 2 * kv_w), lambda j, i: (step(j, i), 0)),
            pl.BlockSpec((BLOCK, 2 * kv_w), lambda j, i: (prev_step(j, i), kv_blk)),
        ],
        out_specs=[
            pl.BlockSpec((tm, tn), lambda j, i: (i, j)),
            pl.BlockSpec((BLOCK, channels), lambda j, i: (step(j, i), 0)),
            pl.BlockSpec((BLOCK, attn_w), lambda j, i: (step(j, i), 0)),
        ],
        out_shape=[
            jax.ShapeDtypeStruct((m, n), BF16),
            jax.ShapeDtypeStruct((m, channels), BF16),
            jax.ShapeDtypeStruct((m, attn_w), BF16),
        ],
        scratch_shapes=[
            pltpu.VMEM((k, tn), BF16),
            pltpu.VMEM((_CONV_HALO + BLOCK, channels), BF16),
            pltpu.VMEM((_SUBLANES, _CONV_HALO + BLOCK, channels), F32),
            pltpu.VMEM((BLOCK, channels), F32),
            pltpu.VMEM((2, n_q, BLOCK, 2 * BLOCK), F32),
            pltpu.VMEM((n_q, BLOCK, 2 * BLOCK), F32),
            pltpu.VMEM((n_q * BLOCK, 2 * BLOCK), BF16),
            pltpu.VMEM((n_q, BLOCK, kv_w), F32),
            pltpu.VMEM((n_q, BLOCK, BLOCK), F32),
        ],
        compiler_params=_params(2),
        name="mixer_core",
    )(h, w_in,
      conv_in, _conv_shift_matrix(_CONV_HALO + BLOCK), conv_w, row(conv_b), row(ln_g), row(ln_b),
      rel_bias.astype(F32).reshape(-1), sinks.astype(F32), _band_buckets(), qmask, krow,
      qkv, qkv)


def kernel(x, c, w_ada, b_ada, norm_mix_g, w_in, q_norm_g, k_norm_g, attn_sinks, rel_bias,
           w_attn_out, conv_w, conv_b, conv_ln_g, conv_ln_b, w_conv_out, w_mix_out,
           norm_ffn_g, w_ffn_in, w_ffn_out):
    b_sz, s_len, d = x.shape
    depth = w_ada.shape[0]
    n_q = attn_sinks.shape[1]
    attn_w = w_attn_out.shape[1]
    channels = conv_w.shape[2]
    kv_w = (w_in.shape[2] - attn_w - 2 * channels - 2 * d) // 2
    n_kv = kv_w // HEAD_DIM
    conv_col = attn_w + 2 * kv_w
    gate_col = conv_col + 2 * channels
    m = b_sz * s_len

    for l in range(depth):
        c_pad = jnp.pad(c, ((0, 8 - b_sz), (0, 0)))
        b_row = b_ada[l].reshape(1, N_MOD * d)
        mod_early = _ada(c_pad, w_ada, b_row, l, N_MOD_EARLY * d)[:b_sz].reshape(b_sz * N_MOD_EARLY, 1, d)

        qkv, h = _qkv_proj(x, norm_mix_g[l], mod_early, 1, 0, w_in, l, q_norm_g[l], k_norm_g[l], n_q, n_kv)
        h = h.reshape(m, d)
        assert m % n_q == 0
        slot_head = lambda slot: _orig_head(slot, n_kv, n_q // n_kv)
        same = lambda step: step
        conv_in, mod_late, wa_bf, wc_bf, wmix_bf = _glu_proj(
            h, w_in, l, conv_col, conv_col + channels, channels, m // n_q,
            c_pad, w_ada, b_row, N_MOD_EARLY * d,
            cast_jobs=((w_attn_out, slot_head), (w_conv_out, same), (w_mix_out, same)))
        mod_late = mod_late[:b_sz].reshape(b_sz * (N_MOD - N_MOD_EARLY), 1, d)
        gates, y_conv, y_attn = _mixer_core(h, w_in, l, gate_col, conv_in, qkv, conv_w[l], conv_b[l],
                                            conv_ln_g[l], conv_ln_b[l], rel_bias, attn_sinks[l],
                                            s_len, n_q, n_kv)
        x1, h = _merge_mix(y_attn, y_conv, wa_bf, wc_bf, gates, 0, d, wmix_bf, x, mod_late, 0,
                           norm_ffn_g[l], 2, 1)

        act, wo_bf = _ffn_in(h.reshape(m, d), w_ffn_in, w_ffn_out, l)
        x = _residual_proj(act, wo_bf, x1.reshape(m, d), mod_late, 3, s_len,
                           tm=512, tn=1024).reshape(b_sz, s_len, d)
    return x
```

```python
import functools
import math

import jax
import jax.numpy as jnp
import numpy as np
from jax import lax
from jax.experimental import pallas as pl
from jax.experimental.pallas import tpu as pltpu

F32 = jnp.float32
BF16 = jnp.bfloat16

HEAD_DIM = 64
WINDOW = 128
BLOCK = 128
NUM_BUCKETS = 32
MAX_EXACT = NUM_BUCKETS // 2
MAX_DISTANCE = 128
N_MOD = 6
N_MOD_EARLY = 2
EPS = 1e-6
LOG2E = math.log2(math.e)

V7X_VMEM_LIMIT_BYTES = 56 * 1024 * 1024


def _params(n_axes):
    return pltpu.CompilerParams(
        dimension_semantics=("arbitrary",) * n_axes,
        vmem_limit_bytes=V7X_VMEM_LIMIT_BYTES,
    )


def _sigmoid(x):
    return jax.nn.sigmoid(x)


def _orig_head(slot, n_kv, q_per_kv):
    return (slot % n_kv) * q_per_kv + slot // n_kv


def _ada_body(c_ref, w_ref, b_ref, o_ref):
    c = c_ref[...]
    act = (c * _sigmoid(c)).astype(BF16)
    o_ref[...] = jnp.dot(act, w_ref[...].astype(BF16), preferred_element_type=F32) + b_ref[...]


def _ada(c, w_ada, b_row, layer, n, tn=2048):
    rows, d = c.shape
    assert n % tn == 0
    return pl.pallas_call(
        _ada_body,
        grid=(n // tn,),
        in_specs=[
            pl.BlockSpec((rows, d), lambda j: (0, 0)),
            pl.BlockSpec((None, d, tn), lambda j: (layer, 0, j)),
            pl.BlockSpec((1, tn), lambda j: (0, j)),
        ],
        out_specs=pl.BlockSpec((rows, tn), lambda j: (0, j)),
        out_shape=jax.ShapeDtypeStruct((rows, n), F32),
        compiler_params=_params(1),
        name="ada_mod",
    )(c, w_ada, b_row)


def _norm_mod_body(x_ref, g_ref, sc_ref, sh_ref, o_ref):
    x = x_ref[0]
    ms = jnp.mean(x * x, axis=-1, keepdims=True)
    gain = g_ref[...] * (1.0 + sc_ref[0])
    o_ref[0] = (x * lax.rsqrt(ms + EPS) * gain + sh_ref[0]).astype(o_ref.dtype)


def _qkv_body(x_ref, g_ref, sc_ref, sh_ref, w_ref, gq_ref, gk_ref, o_ref, h_ref, wbf_ref, seg_ref,
              *, n_q, n_kv, half):
    attn_w = n_q * HEAD_DIM
    kv_w = n_kv * HEAD_DIM

    @pl.when(jnp.logical_and(pl.program_id(0) == 0, pl.program_id(1) == 0))
    def _():
        for slot in range(n_q):
            src = _orig_head(slot, n_kv, n_q // n_kv) * HEAD_DIM
            wbf_ref[:, slot * HEAD_DIM:(slot + 1) * HEAD_DIM] = w_ref[:, src:src + HEAD_DIM].astype(BF16)
        wbf_ref[:, attn_w:] = w_ref[:, attn_w:].astype(BF16)
        r = lax.broadcasted_iota(jnp.int32, (half, half), 0) // HEAD_DIM
        c = lax.broadcasted_iota(jnp.int32, (half, half), 1) // HEAD_DIM
        seg_ref[...] = (r == c).astype(BF16)

    def head_norm(acc, seg, gain):
        ss = jnp.dot((acc * acc).astype(BF16), seg, preferred_element_type=F32)
        return acc * lax.rsqrt(ss / HEAD_DIM + EPS) * gain

    for r0 in range(0, x_ref.shape[1], _QKV_CHAIN_ROWS):
        rs = slice(r0, r0 + _QKV_CHAIN_ROWS)
        x = x_ref[0, rs, :]
        ms = jnp.mean(x * x, axis=-1, keepdims=True)
        gain = g_ref[...] * (1.0 + sc_ref[0])
        a = (x * lax.rsqrt(ms + EPS) * gain + sh_ref[0]).astype(BF16)
        h_ref[0, rs, :] = a
        for c0 in range(0, attn_w, half):
            acc = jnp.dot(a, wbf_ref[:, c0:c0 + half], preferred_element_type=F32)
            o_ref[rs, c0:c0 + half] = head_norm(acc, seg_ref[...], gq_ref[...]).astype(o_ref.dtype)
        acc = jnp.dot(a, wbf_ref[:, attn_w:], preferred_element_type=F32)
        o_ref[rs, attn_w:attn_w + kv_w] = head_norm(acc[:, :kv_w], seg_ref[:kv_w, :kv_w],
                                                    gk_ref[...]).astype(o_ref.dtype)
        o_ref[rs, attn_w + kv_w:] = acc[:, kv_w:].astype(o_ref.dtype)


_QKV_CHAIN_ROWS = 512


def _qkv_proj(x, norm_g, mod3, scale_idx, shift_idx, w_in, layer, q_g, k_g, n_q, n_kv, ts=1024, half=512):
    b_sz, s_len, d = x.shape
    attn_w, kv_w = n_q * HEAD_DIM, n_kv * HEAD_DIM
    width = attn_w + 2 * kv_w
    assert attn_w % half == 0 and kv_w <= half and half % HEAD_DIM == 0
    gq = jnp.tile(q_g * (HEAD_DIM ** -0.5 * LOG2E), half // HEAD_DIM).reshape(1, half)
    gk = jnp.tile(k_g, n_kv).reshape(1, kv_w)
    tiles = s_len // ts
    mod_rows = mod3.shape[0] // b_sz
    const2 = lambda b, s: (0, 0)
    body = functools.partial(_qkv_body, n_q=n_q, n_kv=n_kv, half=half)
    return pl.pallas_call(
        body,
        grid=(b_sz, tiles),
        in_specs=[
            pl.BlockSpec((1, ts, d), lambda b, s: (b, s, 0)),
            pl.BlockSpec((1, d), const2),
            pl.BlockSpec((1, 1, d), lambda b, s: (b * mod_rows + scale_idx, 0, 0)),
            pl.BlockSpec((1, 1, d), lambda b, s: (b * mod_rows + shift_idx, 0, 0)),
            pl.BlockSpec((None, d, width), lambda b, s: (layer, 0, 0), pipeline_mode=pl.Buffered(1)),
            pl.BlockSpec((1, half), const2),
            pl.BlockSpec((1, kv_w), const2),
        ],
        out_specs=[
            pl.BlockSpec((ts, width), lambda b, s: (b * tiles + s, 0)),
            pl.BlockSpec((1, ts, d), lambda b, s: (b, s, 0)),
        ],
        out_shape=[
            jax.ShapeDtypeStruct((b_sz * s_len, width), BF16),
            jax.ShapeDtypeStruct((b_sz, s_len, d), BF16),
        ],
        scratch_shapes=[pltpu.VMEM((d, width), BF16), pltpu.VMEM((half, half), BF16)],
        compiler_params=_params(2),
        name="qkv_proj",
    )(x, norm_g.reshape(1, d), mod3, mod3, w_in, gq, gk)


def _cast_weights_once(w_refs, wbf_refs):
    @pl.when(pl.program_id(1) == 0)
    def _():
        for w_ref, wbf_ref in zip(w_refs, wbf_refs):
            wbf_ref[...] = w_ref[...].astype(BF16)


def _glu_proj_body(h_ref, c_ref, wada_ref, bada_ref, *refs, n_w):
    w_refs, refs = refs[:2 * n_w], refs[2 * n_w:]
    n_jobs = (len(refs) - 3) // 2
    job_in, o_ref, mod_ref, job_out = refs[:n_jobs], refs[n_jobs], refs[n_jobs + 1], refs[n_jobs + 2:-1]
    wbf_ref = refs[-1]
    _cast_weights_once(w_refs, [wbf_ref.at[c] for c in range(2 * n_w)])
    _ada_body(c_ref, wada_ref, bada_ref, mod_ref)
    for src_ref, dst_ref in zip(job_in, job_out):
        dst_ref[...] = src_ref[...].astype(BF16)
    h = h_ref[...]
    lanes = wbf_ref.shape[2]
    for c in range(n_w):
        a = jnp.dot(h, wbf_ref[c], preferred_element_type=F32)
        g = jnp.dot(h, wbf_ref[n_w + c], preferred_element_type=F32)
        o_ref[:, c * lanes:(c + 1) * lanes] = (a * _sigmoid(g)).astype(o_ref.dtype)


def _glu_proj(h, w, layer, a_col, g_col, n, tm, c, w_ada, b_row, mod_col, cast_jobs=()):
    m, k = h.shape
    lanes = _EPILOGUE_LANES
    assert a_col % lanes == 0 and g_col % lanes == 0 and n % lanes == 0
    n_w = n // lanes
    steps = m // tm
    mod_n = w_ada.shape[2] - mod_col
    assert mod_n % steps == 0 and mod_col % (mod_n // steps) == 0
    mod_slab = mod_n // steps
    mod_blk0 = mod_col // mod_slab
    c_rows, d_c = c.shape
    weight_tile = lambda col: pl.BlockSpec((None, k, lanes), lambda j, i: (layer, 0, col // lanes),
                                           pipeline_mode=pl.Buffered(1))
    weight_cols = [a_col + c * lanes for c in range(n_w)] + [g_col + c * lanes for c in range(n_w)]
    job_in, job_out, job_shapes = [], [], []
    for wj, row_block_of_step in cast_jobs:
        rows, cols = wj.shape[1:]
        assert rows % steps == 0
        slab = rows // steps
        job_in.append(pl.BlockSpec((None, slab, cols), lambda j, i, f=row_block_of_step: (layer, f(i), 0)))
        job_out.append(pl.BlockSpec((slab, cols), lambda j, i: (i, 0)))
        job_shapes.append(jax.ShapeDtypeStruct((rows, cols), BF16))
    return pl.pallas_call(
        functools.partial(_glu_proj_body, n_w=n_w),
        grid=(1, steps),
        in_specs=[
            pl.BlockSpec((tm, k), lambda j, i: (i, 0)),
            pl.BlockSpec((c_rows, d_c), lambda j, i: (0, 0)),
            pl.BlockSpec((None, d_c, mod_slab), lambda j, i: (layer, 0, mod_blk0 + i)),
            pl.BlockSpec((1, mod_slab), lambda j, i: (0, mod_blk0 + i)),
        ] + [weight_tile(col) for col in weight_cols] + job_in,
        out_specs=[pl.BlockSpec((tm, n), lambda j, i: (i, 0)),
                   pl.BlockSpec((c_rows, mod_slab), lambda j, i: (0, i))] + job_out,
        out_shape=[jax.ShapeDtypeStruct((m, n), BF16),
                   jax.ShapeDtypeStruct((c_rows, mod_n), F32)] + job_shapes,
        scratch_shapes=[pltpu.VMEM((2 * n_w, k, lanes), BF16)],
        compiler_params=_params(2),
        name="glu_proj",
    )(h, c, w_ada, b_row, *([w] * (2 * n_w)), *[wj for wj, _ in cast_jobs])


_EPILOGUE_LANES = 512
_MATMUL_ROWS = 1024


def _merge_mix_body(ya_ref, yc_ref, wa_ref, wc_ref, ga_ref, gc_ref, w_ref, x_ref, gt_ref, g_ref, sc_ref, sh_ref,
                    x1_ref, h_ref, merged_ref):
    d = w_ref.shape[1]
    ya = ya_ref[...]
    yc = yc_ref[...]
    for n0 in range(0, d, _EPILOGUE_LANES):
        ns = slice(n0, n0 + _EPILOGUE_LANES)
        acc_a = jnp.dot(ya, wa_ref[:, ns], preferred_element_type=F32)
        acc_c = jnp.dot(yc, wc_ref[:, ns], preferred_element_type=F32)
        ga = _sigmoid(ga_ref[:, ns].astype(F32))
        gc = _sigmoid(gc_ref[:, ns].astype(F32))
        merged_ref[:, ns] = (ga * acc_a + gc * acc_c).astype(merged_ref.dtype)
    a = merged_ref[...]
    sumsq = jnp.zeros((a.shape[0], 1), F32)
    for n0 in range(0, d, _EPILOGUE_LANES):
        ns = slice(n0, n0 + _EPILOGUE_LANES)
        acc = jnp.dot(a, w_ref[:, ns], preferred_element_type=F32)
        x1 = x_ref[0, :, ns] + gt_ref[0, :, ns] * acc
        x1_ref[0, :, ns] = x1
        sumsq = sumsq + jnp.sum(x1 * x1, axis=-1, keepdims=True)
    inv = lax.rsqrt(sumsq / d + EPS)
    gain = g_ref[...] * (1.0 + sc_ref[0])
    h_ref[0] = (x1_ref[0] * inv * gain + sh_ref[0]).astype(h_ref.dtype)


def _merge_mix(attn, conv, wa_bf, wc_bf, gates, ga_col, gc_col, wmix_bf, x, mod3, gate_idx, norm_g,
               scale_idx, shift_idx, ts=256):
    b_sz, s_len, d = x.shape
    ka, kc = attn.shape[1], conv.shape[1]
    assert d % _EPILOGUE_LANES == 0 and ga_col % d == 0 and gc_col % d == 0
    tiles = s_len // ts
    const2 = lambda b, s: (0, 0)
    rows = lambda width, col_blk: pl.BlockSpec((ts, width), lambda b, s: (b * tiles + s, col_blk))
    mod_rows = mod3.shape[0] // b_sz
    mod_row = lambda idx: pl.BlockSpec((1, 1, d), lambda b, s: (b * mod_rows + idx, 0, 0))
    act = pl.BlockSpec((1, ts, d), lambda b, s: (b, s, 0))
    resident = pl.Buffered(1)
    return pl.pallas_call(
        _merge_mix_body,
        grid=(b_sz, tiles),
        in_specs=[
            rows(ka, 0),
            rows(kc, 0),
            pl.BlockSpec((ka, d), const2, pipeline_mode=resident),
            pl.BlockSpec((kc, d), const2, pipeline_mode=resident),
            rows(d, ga_col // d),
            rows(d, gc_col // d),
            pl.BlockSpec((d, d), const2, pipeline_mode=resident),
            act,
            mod_row(gate_idx),
            pl.BlockSpec((1, d), const2),
            mod_row(scale_idx),
            mod_row(shift_idx),
        ],
        out_specs=[act, act],
        out_shape=[jax.ShapeDtypeStruct((b_sz, s_len, d), F32), jax.ShapeDtypeStruct((b_sz, s_len, d), BF16)],
        scratch_shapes=[pltpu.VMEM((ts, d), BF16)],
        compiler_params=_params(2),
        name="merge_mix",
    )(attn, conv, wa_bf, wc_bf, gates, gates, wmix_bf, x, mod3, norm_g.reshape(1, d), mod3, mod3)


def _residual_proj_body(a_ref, w_ref, x_ref, gt_ref, o_ref):
    a = a_ref[...]
    for n0 in range(0, o_ref.shape[1], _EPILOGUE_LANES):
        ns = slice(n0, n0 + _EPILOGUE_LANES)
        acc = jnp.dot(a, w_ref[:, ns], preferred_element_type=F32)
        o_ref[:, ns] = x_ref[:, ns] + gt_ref[0, :, ns] * acc


def _residual_proj(a, w_bf, x2d, mod3, gate_idx, rows_per_batch, tm, tn):
    m, k = a.shape
    n = w_bf.shape[1]
    assert n % tn == 0 and tn % _EPILOGUE_LANES == 0 and rows_per_batch % tm == 0
    tiles_per_batch = rows_per_batch // tm
    mod_rows = mod3.shape[0] // (m // rows_per_batch)
    return pl.pallas_call(
        _residual_proj_body,
        grid=(n // tn, m // tm),
        in_specs=[
            pl.BlockSpec((tm, k), lambda j, i: (i, 0)),
            pl.BlockSpec((k, tn), lambda j, i: (0, j)),
            pl.BlockSpec((tm, tn), lambda j, i: (i, j)),
            pl.BlockSpec((1, 1, tn), lambda j, i: ((i // tiles_per_batch) * mod_rows + gate_idx, 0, j)),
        ],
        out_specs=pl.BlockSpec((tm, tn), lambda j, i: (i, j)),
        out_shape=jax.ShapeDtypeStruct((m, n), F32),
        compiler_params=_params(2),
        name="residual_proj",
    )(a, w_bf, x2d, mod3)


def _ffn_in_body(a_ref, wg_ref, wu_ref, wo_ref, o_ref, wobf_ref, wgbf_ref, wubf_ref):
    _cast_weights_once([wg_ref, wu_ref], [wgbf_ref, wubf_ref])
    wobf_ref[...] = wo_ref[...].astype(BF16)
    for r0 in range(0, a_ref.shape[0], _MATMUL_ROWS):
        rs = slice(r0, r0 + _MATMUL_ROWS)
        a = a_ref[rs, :]
        gate = jnp.dot(a, wgbf_ref[...], preferred_element_type=F32)
        up = jnp.dot(a, wubf_ref[...], preferred_element_type=F32)
        o_ref[rs, :] = (gate * _sigmoid(gate) * up).astype(o_ref.dtype)


def _ffn_in(h, w_ffn_in, w_ffn_out, layer, tm=2048, tn=512):
    m, k = h.shape
    d_ff = w_ffn_in.shape[2] // 2
    d_out = w_ffn_out.shape[2]
    up_blk = d_ff // tn
    m_tiles = m // tm
    steps = (d_ff // tn) * m_tiles
    assert d_ff % steps == 0
    slab = d_ff // steps
    step = lambda j, i: j * m_tiles + i
    return pl.pallas_call(
        _ffn_in_body,
        grid=(d_ff // tn, m_tiles),
        in_specs=[
            pl.BlockSpec((tm, k), lambda j, i: (i, 0)),
            pl.BlockSpec((None, k, tn), lambda j, i: (layer, 0, j)),
            pl.BlockSpec((None, k, tn), lambda j, i: (layer, 0, up_blk + j)),
            pl.BlockSpec((None, slab, d_out), lambda j, i: (layer, step(j, i), 0)),
        ],
        out_specs=[
            pl.BlockSpec((tm, tn), lambda j, i: (i, j)),
            pl.BlockSpec((slab, d_out), lambda j, i: (step(j, i), 0)),
        ],
        out_shape=[jax.ShapeDtypeStruct((m, d_ff), BF16), jax.ShapeDtypeStruct((d_ff, d_out), BF16)],
        scratch_shapes=[pltpu.VMEM((k, tn), BF16), pltpu.VMEM((k, tn), BF16)],
        compiler_params=_params(2),
        name="ffn_in",
    )(h, w_ffn_in, w_ffn_in, w_ffn_out)


def _t5_causal_bucket(dist):
    n = jnp.maximum(dist, 0)
    nf = jnp.maximum(n, 1).astype(jnp.float32)
    large = MAX_EXACT + (jnp.log(nf / MAX_EXACT) / math.log(MAX_DISTANCE / MAX_EXACT)
                         * (NUM_BUCKETS - MAX_EXACT)).astype(jnp.int32)
    large = jnp.minimum(large, NUM_BUCKETS - 1)
    return jnp.where(n < MAX_EXACT, n, large)


def _band_buckets():
    q_off = jnp.arange(BLOCK)
    k_off = jnp.arange(2 * BLOCK)
    dist = q_off[:, None] + BLOCK - k_off[None, :]
    allowed = (dist >= 0) & (dist < WINDOW)
    return jnp.where(allowed, _t5_causal_bucket(dist), -1).astype(jnp.int32)


def _attn_build_bias(first_step, rb_ref, sink_ref, bucket_ref, bias_ref, *, n_q, n_kv):
    q_per_kv = n_q // n_kv

    @pl.when(first_step)
    def _():
        bucket = bucket_ref[...]
        col = lax.broadcasted_iota(jnp.int32, bucket.shape, 1)
        for slot in range(n_q):
            head = _orig_head(slot, n_kv, q_per_kv)
            tile = jnp.zeros(bucket.shape, F32)
            for b in range(NUM_BUCKETS):
                tile = jnp.where(bucket == b, rb_ref[b * n_q + head] * LOG2E, tile)
            tile = jnp.where(bucket < 0, -jnp.inf, tile)
            sink = sink_ref[head] * LOG2E
            bias_ref[0, slot] = jnp.where(col == 0, sink, tile)
            bias_ref[1, slot] = jnp.where(col == 0, sink, jnp.where(col < BLOCK, -jnp.inf, tile))


def _attn_block(seq_start, qmask_ref, krow_ref, q_ref, kc_ref, kp_ref, vc_ref, vp_ref, o_ref,
                bias_ref, lg_ref, p_ref, pv_ref, rs_ref, *, n_q, n_kv):
    q_per_kv = n_q // n_kv
    kv_w = n_kv * HEAD_DIM
    lane_slot = lax.broadcasted_iota(jnp.int32, (BLOCK, BLOCK), 1) // HEAD_DIM
    ones = jnp.ones((2 * BLOCK, BLOCK), BF16)
    bias_idx = jnp.where(seq_start, 1, 0)

    keys = jnp.concatenate([kp_ref[...], kc_ref[...]], axis=0) * krow_ref[...]
    vals = jnp.concatenate([vp_ref[...], vc_ref[...]], axis=0) * krow_ref[...]
    lhs = jnp.concatenate(
        [q_ref[:, j * kv_w:(j + 1) * kv_w] * qmask_ref[g]
         for j in range(q_per_kv) for g in range(n_kv)], axis=0)
    lg_ref[...] = lax.dot_general(lhs, keys, (((1,), (1,)), ((), ())),
                                  preferred_element_type=F32).reshape(n_q, BLOCK, 2 * BLOCK)
    for slot in range(n_q):
        logit = lg_ref[slot] + bias_ref[bias_idx, slot]
        m = jnp.max(logit, axis=-1, keepdims=True)
        p_ref[slot * BLOCK:(slot + 1) * BLOCK, :] = jnp.exp2(logit - m).astype(BF16)
    p = p_ref[...]
    pv_ref[...] = jnp.dot(p, vals, preferred_element_type=F32).reshape(n_q, BLOCK, kv_w)
    rs_ref[...] = jnp.dot(p, ones, preferred_element_type=F32).reshape(n_q, BLOCK, BLOCK)
    per_half = BLOCK // HEAD_DIM
    for j in range(q_per_kv):
        for half in range(kv_w // BLOCK):
            lanes = slice(half * BLOCK, (half + 1) * BLOCK)
            slots = [j * n_kv + half * per_half + i for i in range(per_half)]
            num = pv_ref[slots[-1], :, lanes]
            den = rs_ref[slots[-1]]
            for i in range(per_half - 2, -1, -1):
                num = jnp.where(lane_slot == i, pv_ref[slots[i], :, lanes], num)
                den = jnp.where(lane_slot == i, rs_ref[slots[i]], den)
            o_ref[:, j * kv_w + half * BLOCK:j * kv_w + (half + 1) * BLOCK] = (
                num * (1.0 / den)).astype(o_ref.dtype)


_CONV_HALO = 32
_CONV_LANES = 256
_SUBLANES = 8


def _conv_zero_history(seq_start, u_ref):
    @pl.when(seq_start)
    def _():
        u_ref[:_CONV_HALO, :] = jnp.zeros((_CONV_HALO, u_ref.shape[1]), u_ref.dtype)


def _conv_shift_matrix(rows):
    i = np.arange(_SUBLANES * rows) % rows
    r = np.arange(_SUBLANES * rows) // rows
    return (np.arange(rows)[None, :] == (i + r)[:, None]).astype(BF16)


def _conv_block(uin_ref, shift_ref, w_ref, cb_ref, lg_ref, lb_ref, o_ref, u_ref, sh_ref, y_ref):
    width = w_ref.shape[0]
    ts, channels = o_ref.shape
    u_ref[_CONV_HALO:, :] = uin_ref[...]
    sh_ref[...] = jnp.dot(shift_ref[...], u_ref[...], preferred_element_type=F32).reshape(sh_ref.shape)
    first = _CONV_HALO - (width - 1)
    unknown_zero = jnp.minimum(pl.program_id(0), 0)
    for c0 in range(0, channels, _CONV_LANES):
        cs = slice(c0, c0 + _CONV_LANES)
        acc = jnp.broadcast_to(cb_ref[:, cs], (ts, _CONV_LANES))
        for j in range(width):
            tile, r = divmod(first + j, _SUBLANES)
            rows = pl.ds(pl.multiple_of(tile * _SUBLANES + unknown_zero, _SUBLANES), ts)
            acc = acc + sh_ref[r, rows, cs] * w_ref[j:j + 1, cs]
        y_ref[:, cs] = acc
    y = y_ref[...]
    mu = jnp.mean(y, axis=-1, keepdims=True)
    yc = y - mu
    var = jnp.mean(yc * yc, axis=-1, keepdims=True)
    z = yc * lax.rsqrt(var + EPS) * lg_ref[...] + lb_ref[...]
    o_ref[...] = (z * _sigmoid(z)).astype(o_ref.dtype)
    u_ref[:_CONV_HALO, :] = u_ref[ts:, :]


def _core_body(h_ref, w_ref,
               uin_ref, shift_ref, cw_ref, cb_ref, lg_ref, lb_ref,
               rb_ref, sink_ref, bucket_ref, qmask_ref, krow_ref, qkv_ref, kvp_ref,
               gates_ref, yconv_ref, yattn_ref,
               wbf_ref, u_ref, sh_ref, y_ref, bias_ref, lgt_ref, p_ref, pv_ref, rs_ref,
               *, n_q, n_kv, blocks_per_seq):
    attn_w, kv_w = n_q * HEAD_DIM, n_kv * HEAD_DIM
    t = pl.program_id(0) * pl.num_programs(1) + pl.program_id(1)
    seq_start = t % blocks_per_seq == 0
    _cast_weights_once([w_ref], [wbf_ref])
    _conv_zero_history(seq_start, u_ref)
    _attn_build_bias(t == 0, rb_ref, sink_ref, bucket_ref, bias_ref, n_q=n_q, n_kv=n_kv)

    _conv_block(uin_ref, shift_ref, cw_ref, cb_ref, lg_ref, lb_ref, yconv_ref, u_ref, sh_ref, y_ref)
    acc = jnp.dot(h_ref[...], wbf_ref[...], preferred_element_type=F32)
    gates_ref[...] = acc.astype(gates_ref.dtype)
    k_cols, v_cols = pl.ds(attn_w, kv_w), pl.ds(attn_w + kv_w, kv_w)
    _attn_block(seq_start, qmask_ref, krow_ref, qkv_ref.at[:, pl.ds(0, attn_w)],
                qkv_ref.at[:, k_cols], kvp_ref.at[:, pl.ds(0, kv_w)],
                qkv_ref.at[:, v_cols], kvp_ref.at[:, pl.ds(kv_w, kv_w)],
                yattn_ref, bias_ref, lgt_ref, p_ref, pv_ref, rs_ref, n_q=n_q, n_kv=n_kv)


def _mixer_core(h, w_in, layer, gate_col, conv_in, qkv, conv_w, conv_b, ln_g, ln_b, rel_bias, sinks,
                s_len, n_q, n_kv, tn=512):
    m, k = h.shape
    n = w_in.shape[2] - gate_col
    width, channels = conv_w.shape
    attn_w, kv_w = n_q * HEAD_DIM, n_kv * HEAD_DIM
    n_blocks = m // BLOCK
    n_tiles = n // tn
    assert gate_col % tn == 0 and n % tn == 0 and n_blocks % n_tiles == 0
    assert s_len % BLOCK == 0 and width - 1 <= _CONV_HALO and _CONV_HALO % _SUBLANES == 0
    assert attn_w % (2 * kv_w) == 0 and kv_w == 2 * BLOCK and BLOCK % HEAD_DIM == 0
    m_tiles = n_blocks // n_tiles
    tm = m // m_tiles
    col_blk0 = gate_col // tn
    kv_blk = attn_w // (2 * kv_w)

    step = lambda j, i: j * m_tiles + i
    prev_step = lambda j, i: jnp.maximum(step(j, i) - 1, 0)
    const2 = lambda j, i: (0, 0)
    row = lambda v: v.reshape(1, channels)
    lane_group = np.arange(kv_w) // HEAD_DIM
    qmask = np.broadcast_to((lane_group[None, :] == np.arange(n_kv)[:, None])[:, None, :],
                            (n_kv, BLOCK, kv_w)).astype(BF16)
    krow = np.broadcast_to((np.arange(2 * BLOCK) > 0)[:, None], (2 * BLOCK, kv_w)).astype(BF16)
    body = functools.partial(_core_body, n_q=n_q, n_kv=n_kv, blocks_per_seq=s_len // BLOCK)
    return pl.pallas_call(
        body,
        grid=(n_tiles, m_tiles),
        in_specs=[
            pl.BlockSpec((tm, k), lambda j, i: (i, 0)),
            pl.BlockSpec((None, k, tn), lambda j, i: (layer, 0, col_blk0 + j)),
            pl.BlockSpec((BLOCK, channels), lambda j, i: (step(j, i), 0)),
            pl.BlockSpec((_SUBLANES * (_CONV_HALO + BLOCK), _CONV_HALO + BLOCK), const2),
            pl.BlockSpec((width, channels), const2),
            pl.BlockSpec((1, channels), const2),
            pl.BlockSpec((1, channels), const2),
            pl.BlockSpec((1, channels), const2),
            pl.BlockSpec(memory_space=pltpu.SMEM),
            pl.BlockSpec(memory_space=pltpu.SMEM),
            pl.BlockSpec((BLOCK, 2 * BLOCK), const2),
            pl.BlockSpec((n_kv, BLOCK, kv_w), lambda j, i: (0, 0, 0)),
            pl.BlockSpec((2 * BLOCK, kv_w), const2),
            pl.BlockSpec((BLOCK, attn_w + 2 * kv_w), lambda j, i: (step(j, i), 0)),
            pl.BlockSpec((BLOCK, 2 * kv_w), lambda j, i: (prev_step(j, i), kv_blk)),
        ],
        out_specs=[
            pl.BlockSpec((tm, tn), lambda j, i: (i, j)),
            pl.BlockSpec((BLOCK, channels), lambda j, i: (step(j, i), 0)),
            pl.BlockSpec((BLOCK, attn_w), lambda j, i: (step(j, i), 0)),
        ],
        out_shape=[
            jax.ShapeDtypeStruct((m, n), BF16),
            jax.ShapeDtypeStruct((m, channels), BF16),
            jax.ShapeDtypeStruct((m, attn_w), BF16),
        ],
        scratch_shapes=[
            pltpu.VMEM((k, tn), BF16),
            pltpu.VMEM((_CONV_HALO + BLOCK, channels), BF16),
            pltpu.VMEM((_SUBLANES, _CONV_HALO + BLOCK, channels), F32),
            pltpu.VMEM((BLOCK, channels), F32),
            pltpu.VMEM((2, n_q, BLOCK, 2 * BLOCK), F32),
            pltpu.VMEM((n_q, BLOCK, 2 * BLOCK), F32),
            pltpu.VMEM((n_q * BLOCK, 2 * BLOCK), BF16),
            pltpu.VMEM((n_q, BLOCK, kv_w), F32),
            pltpu.VMEM((n_q, BLOCK, BLOCK), F32),
        ],
        compiler_params=_params(2),
        name="mixer_core",
    )(h, w_in,
      conv_in, _conv_shift_matrix(_CONV_HALO + BLOCK), conv_w, row(conv_b), row(ln_g), row(ln_b),
      rel_bias.astype(F32).reshape(-1), sinks.astype(F32), _band_buckets(), qmask, krow,
      qkv, qkv)


def kernel(x, c, w_ada, b_ada, norm_mix_g, w_in, q_norm_g, k_norm_g, attn_sinks, rel_bias,
           w_attn_out, conv_w, conv_b, conv_ln_g, conv_ln_b, w_conv_out, w_mix_out,
           norm_ffn_g, w_ffn_in, w_ffn_out):
    b_sz, s_len, d = x.shape
    depth = w_ada.shape[0]
    n_q = attn_sinks.shape[1]
    attn_w = w_attn_out.shape[1]
    channels = conv_w.shape[2]
    kv_w = (w_in.shape[2] - attn_w - 2 * channels - 2 * d) // 2
    n_kv = kv_w // HEAD_DIM
    conv_col = attn_w + 2 * kv_w
    gate_col = conv_col + 2 * channels
    m = b_sz * s_len

    for l in range(depth):
        b_row = b_ada[l].reshape(1, N_MOD * d)
        mod_early = _ada(c, w_ada, b_row, l, N_MOD_EARLY * d).reshape(b_sz * N_MOD_EARLY, 1, d)

        qkv, h = _qkv_proj(x, norm_mix_g[l], mod_early, 1, 0, w_in, l, q_norm_g[l], k_norm_g[l], n_q, n_kv)
        h = h.reshape(m, d)
        assert m % n_q == 0
        slot_head = lambda slot: _orig_head(slot, n_kv, n_q // n_kv)
        same = lambda step: step
        conv_in, mod_late, wa_bf, wc_bf, wmix_bf = _glu_proj(
            h, w_in, l, conv_col, conv_col + channels, channels, m // n_q,
            c, w_ada, b_row, N_MOD_EARLY * d,
            cast_jobs=((w_attn_out, slot_head), (w_conv_out, same), (w_mix_out, same)))
        mod_late = mod_late.reshape(b_sz * (N_MOD - N_MOD_EARLY), 1, d)
        gates, y_conv, y_attn = _mixer_core(h, w_in, l, gate_col, conv_in, qkv, conv_w[l], conv_b[l],
                                            conv_ln_g[l], conv_ln_b[l], rel_bias, attn_sinks[l],
                                            s_len, n_q, n_kv)
        x1, h = _merge_mix(y_attn, y_conv, wa_bf, wc_bf, gates, 0, d, wmix_bf, x, mod_late, 0,
                           norm_ffn_g[l], 2, 1)

        act, wo_bf = _ffn_in(h.reshape(m, d), w_ffn_in, w_ffn_out, l)
        x = _residual_proj(act, wo_bf, x1.reshape(m, d), mod_late, 3, s_len,
                           tm=512, tn=1024).reshape(b_sz, s_len, d)
    return x
```

```python
import functools
import math

import jax
import jax.numpy as jnp
import numpy as np
from jax import lax
from jax.experimental import pallas as pl
from jax.experimental.pallas import tpu as pltpu

F32 = jnp.float32
BF16 = jnp.bfloat16

HEAD_DIM = 64
WINDOW = 128
BLOCK = 128
NUM_BUCKETS = 32
MAX_EXACT = NUM_BUCKETS // 2
MAX_DISTANCE = 128
N_MOD = 6
N_MOD_EARLY = 2
EPS = 1e-6
LOG2E = math.log2(math.e)

V7X_VMEM_LIMIT_BYTES = 56 * 1024 * 1024


def _params(n_axes):
    return pltpu.CompilerParams(
        dimension_semantics=("arbitrary",) * n_axes,
        vmem_limit_bytes=V7X_VMEM_LIMIT_BYTES,
    )


def _sigmoid(x):
    return jax.nn.sigmoid(x)


def _orig_head(slot, n_kv, q_per_kv):
    return (slot % n_kv) * q_per_kv + slot // n_kv


def _ada_body(c_ref, w_ref, b_ref, o_ref):
    c = c_ref[...]
    act = (c * _sigmoid(c)).astype(BF16)
    o_ref[...] = jnp.dot(act, w_ref[...].astype(BF16), preferred_element_type=F32) + b_ref[...]


def _ada(c, w_ada, b_row, layer, n, tn=2048):
    rows, d = c.shape
    assert n % tn == 0
    return pl.pallas_call(
        _ada_body,
        grid=(n // tn,),
        in_specs=[
            pl.BlockSpec((rows, d), lambda j: (0, 0)),
            pl.BlockSpec((None, d, tn), lambda j: (layer, 0, j)),
            pl.BlockSpec((1, tn), lambda j: (0, j)),
        ],
        out_specs=pl.BlockSpec((rows, tn), lambda j: (0, j)),
        out_shape=jax.ShapeDtypeStruct((rows, n), F32),
        compiler_params=_params(1),
        name="ada_mod",
    )(c, w_ada, b_row)


def _qkv_body(x_ref, g_ref, sc_ref, sh_ref, w_ref, gq_ref, gk_ref, o_ref, h_ref, wbf_ref, seg_ref,
              *, n_q, n_kv, half):
    attn_w = n_q * HEAD_DIM
    kv_w = n_kv * HEAD_DIM

    @pl.when(jnp.logical_and(pl.program_id(0) == 0, pl.program_id(1) == 0))
    def _():
        for slot in range(n_q):
            src = _orig_head(slot, n_kv, n_q // n_kv) * HEAD_DIM
            wbf_ref[:, slot * HEAD_DIM:(slot + 1) * HEAD_DIM] = w_ref[:, src:src + HEAD_DIM].astype(BF16)
        wbf_ref[:, attn_w:] = w_ref[:, attn_w:].astype(BF16)
        r = lax.broadcasted_iota(jnp.int32, (half, half), 0) // HEAD_DIM
        c = lax.broadcasted_iota(jnp.int32, (half, half), 1) // HEAD_DIM
        seg_ref[...] = (r == c).astype(BF16)

    def head_norm(acc, seg, gain):
        ss = jnp.dot((acc * acc).astype(BF16), seg, preferred_element_type=F32)
        return acc * lax.rsqrt(ss / HEAD_DIM + EPS) * gain

    for r0 in range(0, x_ref.shape[1], _QKV_CHAIN_ROWS):
        rs = slice(r0, r0 + _QKV_CHAIN_ROWS)
        x = x_ref[0, rs, :]
        ms = jnp.mean(x * x, axis=-1, keepdims=True)
        gain = g_ref[...] * (1.0 + sc_ref[0])
        a = (x * lax.rsqrt(ms + EPS) * gain + sh_ref[0]).astype(BF16)
        h_ref[0, rs, :] = a
        for c0 in range(0, attn_w, half):
            acc = jnp.dot(a, wbf_ref[:, c0:c0 + half], preferred_element_type=F32)
            o_ref[rs, c0:c0 + half] = head_norm(acc, seg_ref[...], gq_ref[...]).astype(o_ref.dtype)
        acc = jnp.dot(a, wbf_ref[:, attn_w:], preferred_element_type=F32)
        o_ref[rs, attn_w:attn_w + kv_w] = head_norm(acc[:, :kv_w], seg_ref[:kv_w, :kv_w],
                                                    gk_ref[...]).astype(o_ref.dtype)
        o_ref[rs, attn_w + kv_w:] = acc[:, kv_w:].astype(o_ref.dtype)


_QKV_CHAIN_ROWS = 512


def _qkv_proj(x, norm_g, mod3, scale_idx, shift_idx, w_in, layer, q_g, k_g, n_q, n_kv, ts=1024, half=512):
    b_sz, s_len, d = x.shape
    attn_w, kv_w = n_q * HEAD_DIM, n_kv * HEAD_DIM
    width = attn_w + 2 * kv_w
    assert attn_w % half == 0 and kv_w <= half and half % HEAD_DIM == 0
    gq = jnp.tile(q_g * (HEAD_DIM ** -0.5 * LOG2E), half // HEAD_DIM).reshape(1, half)
    gk = jnp.tile(k_g, n_kv).reshape(1, kv_w)
    tiles = s_len // ts
    mod_rows = mod3.shape[0] // b_sz
    const2 = lambda b, s: (0, 0)
    body = functools.partial(_qkv_body, n_q=n_q, n_kv=n_kv, half=half)
    return pl.pallas_call(
        body,
        grid=(b_sz, tiles),
        in_specs=[
            pl.BlockSpec((1, ts, d), lambda b, s: (b, s, 0)),
            pl.BlockSpec((1, d), const2),
            pl.BlockSpec((1, 1, d), lambda b, s: (b * mod_rows + scale_idx, 0, 0)),
            pl.BlockSpec((1, 1, d), lambda b, s: (b * mod_rows + shift_idx, 0, 0)),
            pl.BlockSpec((None, d, width), lambda b, s: (layer, 0, 0), pipeline_mode=pl.Buffered(1)),
            pl.BlockSpec((1, half), const2),
            pl.BlockSpec((1, kv_w), const2),
        ],
        out_specs=[
            pl.BlockSpec((ts, width), lambda b, s: (b * tiles + s, 0)),
            pl.BlockSpec((1, ts, d), lambda b, s: (b, s, 0)),
        ],
        out_shape=[
            jax.ShapeDtypeStruct((b_sz * s_len, width), BF16),
            jax.ShapeDtypeStruct((b_sz, s_len, d), BF16),
        ],
        scratch_shapes=[pltpu.VMEM((d, width), BF16), pltpu.VMEM((half, half), BF16)],
        compiler_params=_params(2),
        name="qkv_proj",
    )(x, norm_g.reshape(1, d), mod3, mod3, w_in, gq, gk)


def _cast_weights_once(w_refs, wbf_refs):
    @pl.when(pl.program_id(1) == 0)
    def _():
        for w_ref, wbf_ref in zip(w_refs, wbf_refs):
            wbf_ref[...] = w_ref[...].astype(BF16)


def _glu_proj_body(h_ref, c_ref, wada_ref, bada_ref, *refs, n_w):
    w_refs, refs = refs[:2 * n_w], refs[2 * n_w:]
    n_jobs = (len(refs) - 3) // 2
    job_in, o_ref, mod_ref, job_out = refs[:n_jobs], refs[n_jobs], refs[n_jobs + 1], refs[n_jobs + 2:-1]
    wbf_ref = refs[-1]
    _cast_weights_once(w_refs, [wbf_ref.at[c] for c in range(2 * n_w)])
    _ada_body(c_ref, wada_ref, bada_ref, mod_ref)
    for src_ref, dst_ref in zip(job_in, job_out):
        dst_ref[...] = src_ref[...].astype(BF16)
    h = h_ref[...]
    lanes = wbf_ref.shape[2]
    for c in range(n_w):
        a = jnp.dot(h, wbf_ref[c], preferred_element_type=F32)
        g = jnp.dot(h, wbf_ref[n_w + c], preferred_element_type=F32)
        o_ref[:, c * lanes:(c + 1) * lanes] = (a * _sigmoid(g)).astype(o_ref.dtype)


def _glu_proj(h, w, layer, a_col, g_col, n, tm, c, w_ada, b_row, mod_col, cast_jobs=()):
    m, k = h.shape
    lanes = _EPILOGUE_LANES
    assert a_col % lanes == 0 and g_col % lanes == 0 and n % lanes == 0
    n_w = n // lanes
    steps = m // tm
    mod_n = w_ada.shape[2] - mod_col
    assert mod_n % steps == 0 and mod_col % (mod_n // steps) == 0
    mod_slab = mod_n // steps
    mod_blk0 = mod_col // mod_slab
    c_rows, d_c = c.shape
    weight_tile = lambda col: pl.BlockSpec((None, k, lanes), lambda j, i: (layer, 0, col // lanes),
                                           pipeline_mode=pl.Buffered(1))
    weight_cols = [a_col + c * lanes for c in range(n_w)] + [g_col + c * lanes for c in range(n_w)]
    job_in, job_out, job_shapes = [], [], []
    for wj, row_block_of_step in cast_jobs:
        rows, cols = wj.shape[1:]
        assert rows % steps == 0
        slab = rows // steps
        job_in.append(pl.BlockSpec((None, slab, cols), lambda j, i, f=row_block_of_step: (layer, f(i), 0)))
        job_out.append(pl.BlockSpec((slab, cols), lambda j, i: (i, 0)))
        job_shapes.append(jax.ShapeDtypeStruct((rows, cols), BF16))
    return pl.pallas_call(
        functools.partial(_glu_proj_body, n_w=n_w),
        grid=(1, steps),
        in_specs=[
            pl.BlockSpec((tm, k), lambda j, i: (i, 0)),
            pl.BlockSpec((c_rows, d_c), lambda j, i: (0, 0)),
            pl.BlockSpec((None, d_c, mod_slab), lambda j, i: (layer, 0, mod_blk0 + i)),
            pl.BlockSpec((1, mod_slab), lambda j, i: (0, mod_blk0 + i)),
        ] + [weight_tile(col) for col in weight_cols] + job_in,
        out_specs=[pl.BlockSpec((tm, n), lambda j, i: (i, 0)),
                   pl.BlockSpec((c_rows, mod_slab), lambda j, i: (0, i))] + job_out,
        out_shape=[jax.ShapeDtypeStruct((m, n), BF16),
                   jax.ShapeDtypeStruct((c_rows, mod_n), F32)] + job_shapes,
        scratch_shapes=[pltpu.VMEM((2 * n_w, k, lanes), BF16)],
        compiler_params=_params(2),
        name="glu_proj",
    )(h, c, w_ada, b_row, *([w] * (2 * n_w)), *[wj for wj, _ in cast_jobs])


_EPILOGUE_LANES = 512
_MATMUL_ROWS = 1024


def _merge_mix_body(ya_ref, yc_ref, wa_ref, wc_ref, ga_ref, gc_ref, w_ref, x_ref, gt_ref, g_ref, sc_ref, sh_ref,
                    x1_ref, h_ref, merged_ref):
    d = w_ref.shape[1]
    ya = ya_ref[...]
    yc = yc_ref[...]
    for n0 in range(0, d, _EPILOGUE_LANES):
        ns = slice(n0, n0 + _EPILOGUE_LANES)
        acc_a = jnp.dot(ya, wa_ref[:, ns], preferred_element_type=F32)
        acc_c = jnp.dot(yc, wc_ref[:, ns], preferred_element_type=F32)
        ga = _sigmoid(ga_ref[:, ns].astype(F32))
        gc = _sigmoid(gc_ref[:, ns].astype(F32))
        merged_ref[:, ns] = (ga * acc_a + gc * acc_c).astype(merged_ref.dtype)
    a = merged_ref[...]
    sumsq = jnp.zeros((a.shape[0], 1), F32)
    for n0 in range(0, d, _EPILOGUE_LANES):
        ns = slice(n0, n0 + _EPILOGUE_LANES)
        acc = jnp.dot(a, w_ref[:, ns], preferred_element_type=F32)
        x1 = x_ref[0, :, ns] + gt_ref[0, :, ns] * acc
        x1_ref[0, :, ns] = x1
        sumsq = sumsq + jnp.sum(x1 * x1, axis=-1, keepdims=True)
    inv = lax.rsqrt(sumsq / d + EPS)
    gain = g_ref[...] * (1.0 + sc_ref[0])
    h_ref[0] = (x1_ref[0] * inv * gain + sh_ref[0]).astype(h_ref.dtype)


def _merge_mix(attn, conv, wa_bf, wc_bf, gates, ga_col, gc_col, wmix_bf, x, mod3, gate_idx, norm_g,
               scale_idx, shift_idx, ts=256):
    b_sz, s_len, d = x.shape
    ka, kc = attn.shape[1], conv.shape[1]
    assert d % _EPILOGUE_LANES == 0 and ga_col % d == 0 and gc_col % d == 0
    tiles = s_len // ts
    const2 = lambda b, s: (0, 0)
    rows = lambda width, col_blk: pl.BlockSpec((ts, width), lambda b, s: (b * tiles + s, col_blk))
    mod_rows = mod3.shape[0] // b_sz
    mod_row = lambda idx: pl.BlockSpec((1, 1, d), lambda b, s: (b * mod_rows + idx, 0, 0))
    act = pl.BlockSpec((1, ts, d), lambda b, s: (b, s, 0))
    resident = pl.Buffered(1)
    return pl.pallas_call(
        _merge_mix_body,
        grid=(b_sz, tiles),
        in_specs=[
            rows(ka, 0),
            rows(kc, 0),
            pl.BlockSpec((ka, d), const2, pipeline_mode=resident),
            pl.BlockSpec((kc, d), const2, pipeline_mode=resident),
            rows(d, ga_col // d),
            rows(d, gc_col // d),
            pl.BlockSpec((d, d), const2, pipeline_mode=resident),
            act,
            mod_row(gate_idx),
            pl.BlockSpec((1, d), const2),
            mod_row(scale_idx),
            mod_row(shift_idx),
        ],
        out_specs=[act, act],
        out_shape=[jax.ShapeDtypeStruct((b_sz, s_len, d), F32), jax.ShapeDtypeStruct((b_sz, s_len, d), BF16)],
        scratch_shapes=[pltpu.VMEM((ts, d), BF16)],
        compiler_params=_params(2),
        name="merge_mix",
    )(attn, conv, wa_bf, wc_bf, gates, gates, wmix_bf, x, mod3, norm_g.reshape(1, d), mod3, mod3)


def _residual_proj_body(a_ref, w_ref, x_ref, gt_ref, o_ref):
    a = a_ref[...]
    for n0 in range(0, o_ref.shape[1], _EPILOGUE_LANES):
        ns = slice(n0, n0 + _EPILOGUE_LANES)
        acc = jnp.dot(a, w_ref[:, ns], preferred_element_type=F32)
        o_ref[:, ns] = x_ref[:, ns] + gt_ref[0, :, ns] * acc


def _residual_proj(a, w_bf, x2d, mod3, gate_idx, rows_per_batch, tm, tn):
    m, k = a.shape
    n = w_bf.shape[1]
    assert n % tn == 0 and tn % _EPILOGUE_LANES == 0 and rows_per_batch % tm == 0
    tiles_per_batch = rows_per_batch // tm
    mod_rows = mod3.shape[0] // (m // rows_per_batch)
    return pl.pallas_call(
        _residual_proj_body,
        grid=(n // tn, m // tm),
        in_specs=[
            pl.BlockSpec((tm, k), lambda j, i: (i, 0)),
            pl.BlockSpec((k, tn), lambda j, i: (0, j)),
            pl.BlockSpec((tm, tn), lambda j, i: (i, j)),
            pl.BlockSpec((1, 1, tn), lambda j, i: ((i // tiles_per_batch) * mod_rows + gate_idx, 0, j)),
        ],
        out_specs=pl.BlockSpec((tm, tn), lambda j, i: (i, j)),
        out_shape=jax.ShapeDtypeStruct((m, n), F32),
        compiler_params=_params(2),
        name="residual_proj",
    )(a, w_bf, x2d, mod3)


def _ffn_in_body(a_ref, wg_ref, wu_ref, wo_ref, o_ref, wobf_ref, wgbf_ref, wubf_ref):
    _cast_weights_once([wg_ref, wu_ref], [wgbf_ref, wubf_ref])
    wobf_ref[...] = wo_ref[...].astype(BF16)
    for r0 in range(0, a_ref.shape[0], _MATMUL_ROWS):
        rs = slice(r0, r0 + _MATMUL_ROWS)
        a = a_ref[rs, :]
        gate = jnp.dot(a, wgbf_ref[...], preferred_element_type=F32)
        up = jnp.dot(a, wubf_ref[...], preferred_element_type=F32)
        o_ref[rs, :] = (gate * _sigmoid(gate) * up).astype(o_ref.dtype)


def _ffn_in(h, w_ffn_in, w_ffn_out, layer, tm=2048, tn=512):
    m, k = h.shape
    d_ff = w_ffn_in.shape[2] // 2
    d_out = w_ffn_out.shape[2]
    up_blk = d_ff // tn
    m_tiles = m // tm
    steps = (d_ff // tn) * m_tiles
    assert d_ff % steps == 0
    slab = d_ff // steps
    step = lambda j, i: j * m_tiles + i
    return pl.pallas_call(
        _ffn_in_body,
        grid=(d_ff // tn, m_tiles),
        in_specs=[
            pl.BlockSpec((tm, k), lambda j, i: (i, 0)),
            pl.BlockSpec((None, k, tn), lambda j, i: (layer, 0, j)),
            pl.BlockSpec((None, k, tn), lambda j, i: (layer, 0, up_blk + j)),
            pl.BlockSpec((None, slab, d_out), lambda j, i: (layer, step(j, i), 0)),
        ],
        out_specs=[
            pl.BlockSpec((tm, tn), lambda j, i: (i, j)),
            pl.BlockSpec((slab, d_out), lambda j, i: (step(j, i), 0)),
        ],
        out_shape=[jax.ShapeDtypeStruct((m, d_ff), BF16), jax.ShapeDtypeStruct((d_ff, d_out), BF16)],
        scratch_shapes=[pltpu.VMEM((k, tn), BF16), pltpu.VMEM((k, tn), BF16)],
        compiler_params=_params(2),
        name="ffn_in",
    )(h, w_ffn_in, w_ffn_in, w_ffn_out)


def _t5_causal_bucket(dist):
    n = jnp.maximum(dist, 0)
    nf = jnp.maximum(n, 1).astype(jnp.float32)
    large = MAX_EXACT + (jnp.log(nf / MAX_EXACT) / math.log(MAX_DISTANCE / MAX_EXACT)
                         * (NUM_BUCKETS - MAX_EXACT)).astype(jnp.int32)
    large = jnp.minimum(large, NUM_BUCKETS - 1)
    return jnp.where(n < MAX_EXACT, n, large)


def _band_buckets():
    q_off = jnp.arange(BLOCK)
    k_off = jnp.arange(2 * BLOCK)
    dist = q_off[:, None] + BLOCK - k_off[None, :]
    allowed = (dist >= 0) & (dist < WINDOW)
    return jnp.where(allowed, _t5_causal_bucket(dist), -1).astype(jnp.int32)


def _attn_build_bias(first_step, rb_ref, sink_ref, bucket_ref, bias_ref, *, n_q, n_kv):
    q_per_kv = n_q // n_kv

    @pl.when(first_step)
    def _():
        bucket = bucket_ref[...]
        col = lax.broadcasted_iota(jnp.int32, bucket.shape, 1)
        for slot in range(n_q):
            head = _orig_head(slot, n_kv, q_per_kv)
            tile = jnp.zeros(bucket.shape, F32)
            for b in range(NUM_BUCKETS):
                tile = jnp.where(bucket == b, rb_ref[b * n_q + head] * LOG2E, tile)
            tile = jnp.where(bucket < 0, -jnp.inf, tile)
            sink = sink_ref[head] * LOG2E
            bias_ref[0, slot] = jnp.where(col == 0, sink, tile)
            bias_ref[1, slot] = jnp.where(col == 0, sink, jnp.where(col < BLOCK, -jnp.inf, tile))


def _attn_block(seq_start, qmask_ref, krow_ref, q_ref, kc_ref, kp_ref, vc_ref, vp_ref, o_ref,
                bias_ref, lg_ref, p_ref, pv_ref, rs_ref, *, n_q, n_kv):
    q_per_kv = n_q // n_kv
    kv_w = n_kv * HEAD_DIM
    lane_slot = lax.broadcasted_iota(jnp.int32, (BLOCK, BLOCK), 1) // HEAD_DIM
    ones = jnp.ones((2 * BLOCK, BLOCK), BF16)
    bias_idx = jnp.where(seq_start, 1, 0)

    keys = jnp.concatenate([kp_ref[...], kc_ref[...]], axis=0) * krow_ref[...]
    vals = jnp.concatenate([vp_ref[...], vc_ref[...]], axis=0) * krow_ref[...]
    lhs = jnp.concatenate(
        [q_ref[:, j * kv_w:(j + 1) * kv_w] * qmask_ref[g]
         for j in range(q_per_kv) for g in range(n_kv)], axis=0)
    lg_ref[...] = lax.dot_general(lhs, keys, (((1,), (1,)), ((), ())),
                                  preferred_element_type=F32).reshape(n_q, BLOCK, 2 * BLOCK)
    for slot in range(n_q):
        logit = lg_ref[slot] + bias_ref[bias_idx, slot]
        m = jnp.max(logit, axis=-1, keepdims=True)
        p_ref[slot * BLOCK:(slot + 1) * BLOCK, :] = jnp.exp2(logit - m).astype(BF16)
    p = p_ref[...]
    pv_ref[...] = jnp.dot(p, vals, preferred_element_type=F32).reshape(n_q, BLOCK, kv_w)
    rs_ref[...] = jnp.dot(p, ones, preferred_element_type=F32).reshape(n_q, BLOCK, BLOCK)
    per_half = BLOCK // HEAD_DIM
    for j in range(q_per_kv):
        for half in range(kv_w // BLOCK):
            lanes = slice(half * BLOCK, (half + 1) * BLOCK)
            slots = [j * n_kv + half * per_half + i for i in range(per_half)]
            num = pv_ref[slots[-1], :, lanes]
            den = rs_ref[slots[-1]]
            for i in range(per_half - 2, -1, -1):
                num = jnp.where(lane_slot == i, pv_ref[slots[i], :, lanes], num)
                den = jnp.where(lane_slot == i, rs_ref[slots[i]], den)
            o_ref[:, j * kv_w + half * BLOCK:j * kv_w + (half + 1) * BLOCK] = (
                num * (1.0 / den)).astype(o_ref.dtype)


_CONV_HALO = 32
_CONV_LANES = 256
_SUBLANES = 8


def _conv_zero_history(seq_start, u_ref):
    @pl.when(seq_start)
    def _():
        u_ref[:_CONV_HALO, :] = jnp.zeros((_CONV_HALO, u_ref.shape[1]), u_ref.dtype)


def _conv_shift_matrix(rows):
    i = np.arange(_SUBLANES * rows) % rows
    r = np.arange(_SUBLANES * rows) // rows
    return (np.arange(rows)[None, :] == (i + r)[:, None]).astype(BF16)


def _conv_block(uin_ref, shift_ref, w_ref, cb_ref, lg_ref, lb_ref, o_ref, u_ref, sh_ref, y_ref):
    width = w_ref.shape[0]
    ts, channels = o_ref.shape
    u_ref[_CONV_HALO:, :] = uin_ref[...]
    sh_ref[...] = jnp.dot(shift_ref[...], u_ref[...], preferred_element_type=F32).reshape(sh_ref.shape)
    first = _CONV_HALO - (width - 1)
    unknown_zero = jnp.minimum(pl.program_id(0), 0)
    for c0 in range(0, channels, _CONV_LANES):
        cs = slice(c0, c0 + _CONV_LANES)
        acc = jnp.broadcast_to(cb_ref[:, cs], (ts, _CONV_LANES))
        for j in range(width):
            tile, r = divmod(first + j, _SUBLANES)
            rows = pl.ds(pl.multiple_of(tile * _SUBLANES + unknown_zero, _SUBLANES), ts)
            acc = acc + sh_ref[r, rows, cs] * w_ref[j:j + 1, cs]
        y_ref[:, cs] = acc
    y = y_ref[...]
    mu = jnp.mean(y, axis=-1, keepdims=True)
    yc = y - mu
    var = jnp.mean(yc * yc, axis=-1, keepdims=True)
    z = yc * lax.rsqrt(var + EPS) * lg_ref[...] + lb_ref[...]
    o_ref[...] = (z * _sigmoid(z)).astype(o_ref.dtype)
    u_ref[:_CONV_HALO, :] = u_ref[ts:, :]


def _core_body(h_ref, w_ref,
               uin_ref, shift_ref, cw_ref, cb_ref, lg_ref, lb_ref,
               rb_ref, sink_ref, bucket_ref, qmask_ref, krow_ref, qkv_ref, kvp_ref,
               gates_ref, yconv_ref, yattn_ref,
               wbf_ref, u_ref, sh_ref, y_ref, bias_ref, lgt_ref, p_ref, pv_ref, rs_ref,
               *, n_q, n_kv, blocks_per_seq):
    attn_w, kv_w = n_q * HEAD_DIM, n_kv * HEAD_DIM
    t = pl.program_id(0) * pl.num_programs(1) + pl.program_id(1)
    seq_start = t % blocks_per_seq == 0
    _cast_weights_once([w_ref], [wbf_ref])
    _conv_zero_history(seq_start, u_ref)
    _attn_build_bias(t == 0, rb_ref, sink_ref, bucket_ref, bias_ref, n_q=n_q, n_kv=n_kv)

    _conv_block(uin_ref, shift_ref, cw_ref, cb_ref, lg_ref, lb_ref, yconv_ref, u_ref, sh_ref, y_ref)
    acc = jnp.dot(h_ref[...], wbf_ref[...], preferred_element_type=F32)
    gates_ref[...] = acc.astype(gates_ref.dtype)
    k_cols, v_cols = pl.ds(attn_w, kv_w), pl.ds(attn_w + kv_w, kv_w)
    _attn_block(seq_start, qmask_ref, krow_ref, qkv_ref.at[:, pl.ds(0, attn_w)],
                qkv_ref.at[:, k_cols], kvp_ref.at[:, pl.ds(0, kv_w)],
                qkv_ref.at[:, v_cols], kvp_ref.at[:, pl.ds(kv_w, kv_w)],
                yattn_ref, bias_ref, lgt_ref, p_ref, pv_ref, rs_ref, n_q=n_q, n_kv=n_kv)


def _mixer_core(h, w_in, layer, gate_col, conv_in, qkv, conv_w, conv_b, ln_g, ln_b, rel_bias, sinks,
                s_len, n_q, n_kv, tn=512):
    m, k = h.shape
    n = w_in.shape[2] - gate_col
    width, channels = conv_w.shape
    attn_w, kv_w = n_q * HEAD_DIM, n_kv * HEAD_DIM
    n_blocks = m // BLOCK
    n_tiles = n // tn
    assert gate_col % tn == 0 and n % tn == 0 and n_blocks % n_tiles == 0
    assert s_len % BLOCK == 0 and width - 1 <= _CONV_HALO and _CONV_HALO % _SUBLANES == 0
    assert attn_w % (2 * kv_w) == 0 and kv_w == 2 * BLOCK and BLOCK % HEAD_DIM == 0
    m_tiles = n_blocks // n_tiles
    tm = m // m_tiles
    col_blk0 = gate_col // tn
    kv_blk = attn_w // (2 * kv_w)

    step = lambda j, i: j * m_tiles + i
    prev_step = lambda j, i: jnp.maximum(step(j, i) - 1, 0)
    const2 = lambda j, i: (0, 0)
    row = lambda v: v.reshape(1, channels)
    lane_group = np.arange(kv_w) // HEAD_DIM
    qmask = np.broadcast_to((lane_group[None, :] == np.arange(n_kv)[:, None])[:, None, :],
                            (n_kv, BLOCK, kv_w)).astype(BF16)
    krow = np.broadcast_to((np.arange(2 * BLOCK) > 0)[:, None], (2 * BLOCK, kv_w)).astype(BF16)
    body = functools.partial(_core_body, n_q=n_q, n_kv=n_kv, blocks_per_seq=s_len // BLOCK)
    return pl.pallas_call(
        body,
        grid=(n_tiles, m_tiles),
        in_specs=[
            pl.BlockSpec((tm, k), lambda j, i: (i, 0)),
            pl.BlockSpec((None, k, tn), lambda j, i: (layer, 0, col_blk0 + j)),
            pl.BlockSpec((BLOCK, channels), lambda j, i: (step(j, i), 0)),
            pl.BlockSpec((_SUBLANES * (_CONV_HALO + BLOCK), _CONV_HALO + BLOCK), const2),
            pl.BlockSpec((width, channels), const2),
            pl.BlockSpec((1, channels), const2),
            pl.BlockSpec((1, channels), const2),
            pl.BlockSpec((1, channels), const2),
            pl.BlockSpec(memory_space=pltpu.SMEM),
            pl.BlockSpec(memory_space=pltpu.SMEM),
            pl.BlockSpec((BLOCK, 2 * BLOCK), const2),
            pl.BlockSpec((n_kv, BLOCK, kv_w), lambda j, i: (0, 0, 0)),
            pl.BlockSpec((2 * BLOCK, kv_w), const2),
            pl.BlockSpec((BLOCK, attn_w + 2 * kv_w), lambda j, i: (step(j, i), 0)),
            pl.BlockSpec((BLOCK, 2 * kv_w), lambda j, i: (prev_step(j, i), kv_blk)),
        ],
        out_specs=[
            pl.BlockSpec((tm, tn), lambda j, i: (i, j)),
            pl.BlockSpec((BLOCK, channels), lambda j, i: (step(j, i), 0)),
            pl.BlockSpec((BLOCK, attn_w), lambda j, i: (step(j, i), 0)),
        ],
        out_shape=[
            jax.ShapeDtypeStruct((m, n), BF16),
            jax.ShapeDtypeStruct((m, channels), BF16),
            jax.ShapeDtypeStruct((m, attn_w), BF16),
        ],
        scratch_shapes=[
            pltpu.VMEM((k, tn), BF16),
            pltpu.VMEM((_CONV_HALO + BLOCK, channels), BF16),
            pltpu.VMEM((_SUBLANES, _CONV_HALO + BLOCK, channels), F32),
            pltpu.VMEM((BLOCK, channels), F32),
            pltpu.VMEM((2, n_q, BLOCK, 2 * BLOCK), F32),
            pltpu.VMEM((n_q, BLOCK, 2 * BLOCK), F32),
            pltpu.VMEM((n_q * BLOCK, 2 * BLOCK), BF16),
            pltpu.VMEM((n_q, BLOCK, kv_w), F32),
            pltpu.VMEM((n_q, BLOCK, BLOCK), F32),
        ],
        compiler_params=_params(2),
        name="mixer_core",
    )(h, w_in,
      conv_in, _conv_shift_matrix(_CONV_HALO + BLOCK), conv_w, row(conv_b), row(ln_g), row(ln_b),
      rel_bias.astype(F32).reshape(-1), sinks.astype(F32), _band_buckets(), qmask, krow,
      qkv, qkv)


def kernel(x, c, w_ada, b_ada, norm_mix_g, w_in, q_norm_g, k_norm_g, attn_sinks, rel_bias,
           w_attn_out, conv_w, conv_b, conv_ln_g, conv_ln_b, w_conv_out, w_mix_out,
           norm_ffn_g, w_ffn_in, w_ffn_out):
    b_sz, s_len, d = x.shape
    depth = w_ada.shape[0]
    n_q = attn_sinks.shape[1]
    attn_w = w_attn_out.shape[1]
    channels = conv_w.shape[2]
    kv_w = (w_in.shape[2] - attn_w - 2 * channels - 2 * d) // 2
    n_kv = kv_w // HEAD_DIM
    conv_col = attn_w + 2 * kv_w
    gate_col = conv_col + 2 * channels
    m = b_sz * s_len

    for l in range(depth):
        b_row = b_ada[l].reshape(1, N_MOD * d)
        mod_early = _ada(c, w_ada, b_row, l, N_MOD_EARLY * d).reshape(b_sz * N_MOD_EARLY, 1, d)

        qkv, h = _qkv_proj(x, norm_mix_g[l], mod_early, 1, 0, w_in, l, q_norm_g[l], k_norm_g[l], n_q, n_kv)
        h = h.reshape(m, d)
        assert m % n_q == 0
        slot_head = lambda slot: _orig_head(slot, n_kv, n_q // n_kv)
        same = lambda step: step
        conv_in, mod_late, wa_bf, wc_bf, wmix_bf = _glu_proj(
            h, w_in, l, conv_col, conv_col + channels, channels, m // n_q,
            c, w_ada, b_row, N_MOD_EARLY * d,
            cast_jobs=((w_attn_out, slot_head), (w_conv_out, same), (w_mix_out, same)))
        mod_late = mod_late.reshape(b_sz * (N_MOD - N_MOD_EARLY), 1, d)
        gates, y_conv, y_attn = _mixer_core(h, w_in, l, gate_col, conv_in, qkv, conv_w[l], conv_b[l],
                                            conv_ln_g[l], conv_ln_b[l], rel_bias, attn_sinks[l],
                                            s_len, n_q, n_kv)
        x1, h = _merge_mix(y_attn, y_conv, wa_bf, wc_bf, gates, 0, d, wmix_bf, x, mod_late, 0,
                           norm_ffn_g[l], 2, 1)

        act, wo_bf = _ffn_in(h.reshape(m, d), w_ffn_in, w_ffn_out, l)
        x = _residual_proj(act, wo_bf, x1.reshape(m, d), mod_late, 3, s_len,
                           tm=512, tn=1024).reshape(b_sz, s_len, d)
    return x
```

```python
import functools
import math

import jax
import jax.numpy as jnp
import numpy as np
from jax import lax
from jax.experimental import pallas as pl
from jax.experimental.pallas import tpu as pltpu

F32 = jnp.float32
BF16 = jnp.bfloat16

HEAD_DIM = 64
WINDOW = 128
BLOCK = 128
NUM_BUCKETS = 32
MAX_EXACT = NUM_BUCKETS // 2
MAX_DISTANCE = 128
N_MOD = 6
N_MOD_EARLY = 2
EPS = 1e-6
LOG2E = math.log2(math.e)

V7X_VMEM_LIMIT_BYTES = 56 * 1024 * 1024


def _params(n_axes):
    return pltpu.CompilerParams(
        dimension_semantics=("arbitrary",) * n_axes,
        vmem_limit_bytes=V7X_VMEM_LIMIT_BYTES,
    )


def _sigmoid(x):
    return jax.nn.sigmoid(x)


def _orig_head(slot, n_kv, q_per_kv):
    return (slot % n_kv) * q_per_kv + slot // n_kv


def _ada_body(c_ref, w_ref, b_ref, o_ref):
    c = c_ref[...]
    act = (c * _sigmoid(c)).astype(BF16)
    mod = jnp.dot(act, w_ref[...].astype(BF16), preferred_element_type=F32) + b_ref[...]
    for b in range(mod.shape[0]):
        o_ref[b] = mod[b:b + 1, :]


def _ada(c, w_ada, b_row, layer, n):
    rows, d = c.shape
    assert n % d == 0
    return pl.pallas_call(
        _ada_body,
        grid=(n // d,),
        in_specs=[
            pl.BlockSpec((rows, d), lambda j: (0, 0)),
            pl.BlockSpec((None, d, d), lambda j: (layer, 0, j)),
            pl.BlockSpec((1, d), lambda j: (0, j)),
        ],
        out_specs=pl.BlockSpec((rows, None, 1, d), lambda j: (0, j, 0, 0)),
        out_shape=jax.ShapeDtypeStruct((rows, n // d, 1, d), F32),
        compiler_params=_params(1),
        name="ada_mod",
    )(c, w_ada, b_row)


def _qkv_body(x_ref, g_ref, sc_ref, sh_ref, w_ref, gq_ref, gk_ref, o_ref, h_ref, wbf_ref, seg_ref,
              *, n_q, n_kv, half):
    attn_w = n_q * HEAD_DIM
    kv_w = n_kv * HEAD_DIM

    @pl.when(jnp.logical_and(pl.program_id(0) == 0, pl.program_id(1) == 0))
    def _():
        for slot in range(n_q):
            src = _orig_head(slot, n_kv, n_q // n_kv) * HEAD_DIM
            wbf_ref[:, slot * HEAD_DIM:(slot + 1) * HEAD_DIM] = w_ref[:, src:src + HEAD_DIM].astype(BF16)
        wbf_ref[:, attn_w:] = w_ref[:, attn_w:].astype(BF16)
        r = lax.broadcasted_iota(jnp.int32, (half, half), 0) // HEAD_DIM
        c = lax.broadcasted_iota(jnp.int32, (half, half), 1) // HEAD_DIM
        seg_ref[...] = (r == c).astype(BF16)

    def head_norm(acc, seg, gain):
        ss = jnp.dot((acc * acc).astype(BF16), seg, preferred_element_type=F32)
        return acc * lax.rsqrt(ss / HEAD_DIM + EPS) * gain

    for r0 in range(0, x_ref.shape[1], _QKV_CHAIN_ROWS):
        rs = slice(r0, r0 + _QKV_CHAIN_ROWS)
        x = x_ref[0, rs, :]
        ms = jnp.mean(x * x, axis=-1, keepdims=True)
        gain = g_ref[...] * (1.0 + sc_ref[0])
        a = (x * lax.rsqrt(ms + EPS) * gain + sh_ref[0]).astype(BF16)
        h_ref[0, rs, :] = a
        for c0 in range(0, attn_w, half):
            acc = jnp.dot(a, wbf_ref[:, c0:c0 + half], preferred_element_type=F32)
            o_ref[rs, c0:c0 + half] = head_norm(acc, seg_ref[...], gq_ref[...]).astype(o_ref.dtype)
        acc = jnp.dot(a, wbf_ref[:, attn_w:], preferred_element_type=F32)
        o_ref[rs, attn_w:attn_w + kv_w] = head_norm(acc[:, :kv_w], seg_ref[:kv_w, :kv_w],
                                                    gk_ref[...]).astype(o_ref.dtype)
        o_ref[rs, attn_w + kv_w:] = acc[:, kv_w:].astype(o_ref.dtype)


_QKV_CHAIN_ROWS = 512


def _qkv_proj(x, norm_g, mod3, scale_idx, shift_idx, w_in, layer, q_g, k_g, n_q, n_kv, ts=1024, half=512):
    b_sz, s_len, d = x.shape
    attn_w, kv_w = n_q * HEAD_DIM, n_kv * HEAD_DIM
    width = attn_w + 2 * kv_w
    assert attn_w % half == 0 and kv_w <= half and half % HEAD_DIM == 0
    gq = jnp.tile(q_g * (HEAD_DIM ** -0.5 * LOG2E), half // HEAD_DIM).reshape(1, half)
    gk = jnp.tile(k_g, n_kv).reshape(1, kv_w)
    tiles = s_len // ts
    mod_rows = mod3.shape[0] // b_sz
    const2 = lambda b, s: (0, 0)
    body = functools.partial(_qkv_body, n_q=n_q, n_kv=n_kv, half=half)
    return pl.pallas_call(
        body,
        grid=(b_sz, tiles),
        in_specs=[
            pl.BlockSpec((1, ts, d), lambda b, s: (b, s, 0)),
            pl.BlockSpec((1, d), const2),
            pl.BlockSpec((1, 1, d), lambda b, s: (b * mod_rows + scale_idx, 0, 0)),
            pl.BlockSpec((1, 1, d), lambda b, s: (b * mod_rows + shift_idx, 0, 0)),
            pl.BlockSpec((None, d, width), lambda b, s: (layer, 0, 0), pipeline_mode=pl.Buffered(1)),
            pl.BlockSpec((1, half), const2),
            pl.BlockSpec((1, kv_w), const2),
        ],
        out_specs=[
            pl.BlockSpec((ts, width), lambda b, s: (b * tiles + s, 0)),
            pl.BlockSpec((1, ts, d), lambda b, s: (b, s, 0)),
        ],
        out_shape=[
            jax.ShapeDtypeStruct((b_sz * s_len, width), BF16),
            jax.ShapeDtypeStruct((b_sz, s_len, d), BF16),
        ],
        scratch_shapes=[pltpu.VMEM((d, width), BF16), pltpu.VMEM((half, half), BF16)],
        compiler_params=_params(2),
        name="qkv_proj",
    )(x, norm_g.reshape(1, d), mod3, mod3, w_in, gq, gk)


def _cast_weights_once(w_refs, wbf_refs):
    @pl.when(pl.program_id(1) == 0)
    def _():
        for w_ref, wbf_ref in zip(w_refs, wbf_refs):
            wbf_ref[...] = w_ref[...].astype(BF16)


def _glu_proj_body(h_ref, c_ref, wada_ref, bada_ref, *refs, n_w):
    w_refs, refs = refs[:2 * n_w], refs[2 * n_w:]
    n_jobs = (len(refs) - 3) // 2
    job_in, o_ref, mod_ref, job_out = refs[:n_jobs], refs[n_jobs], refs[n_jobs + 1], refs[n_jobs + 2:-1]
    wbf_ref = refs[-1]
    _cast_weights_once(w_refs, [wbf_ref.at[c] for c in range(2 * n_w)])
    _ada_body(c_ref, wada_ref, bada_ref, mod_ref)
    for src_ref, dst_ref in zip(job_in, job_out):
        dst_ref[...] = src_ref[...].astype(BF16)
    h = h_ref[...]
    lanes = wbf_ref.shape[2]
    for c in range(n_w):
        a = jnp.dot(h, wbf_ref[c], preferred_element_type=F32)
        g = jnp.dot(h, wbf_ref[n_w + c], preferred_element_type=F32)
        o_ref[:, c * lanes:(c + 1) * lanes] = (a * _sigmoid(g)).astype(o_ref.dtype)


def _glu_proj(h, w, layer, a_col, g_col, n, tm, c, w_ada, b_row, mod_col, cast_jobs=()):
    m, k = h.shape
    lanes = _EPILOGUE_LANES
    assert a_col % lanes == 0 and g_col % lanes == 0 and n % lanes == 0
    n_w = n // lanes
    steps = m // tm
    mod_n = w_ada.shape[2] - mod_col
    assert mod_n % steps == 0 and mod_col % (mod_n // steps) == 0
    mod_slab = mod_n // steps
    mod_blk0 = mod_col // mod_slab
    c_rows, d_c = c.shape
    assert d_c % mod_slab == 0 and mod_n % d_c == 0
    slabs_per_chunk = d_c // mod_slab
    weight_tile = lambda col: pl.BlockSpec((None, k, lanes), lambda j, i: (layer, 0, col // lanes),
                                           pipeline_mode=pl.Buffered(1))
    weight_cols = [a_col + c * lanes for c in range(n_w)] + [g_col + c * lanes for c in range(n_w)]
    job_in, job_out, job_shapes = [], [], []
    for wj, row_block_of_step in cast_jobs:
        rows, cols = wj.shape[1:]
        assert rows % steps == 0
        slab = rows // steps
        job_in.append(pl.BlockSpec((None, slab, cols), lambda j, i, f=row_block_of_step: (layer, f(i), 0)))
        job_out.append(pl.BlockSpec((slab, cols), lambda j, i: (i, 0)))
        job_shapes.append(jax.ShapeDtypeStruct((rows, cols), BF16))
    return pl.pallas_call(
        functools.partial(_glu_proj_body, n_w=n_w),
        grid=(1, steps),
        in_specs=[
            pl.BlockSpec((tm, k), lambda j, i: (i, 0)),
            pl.BlockSpec((c_rows, d_c), lambda j, i: (0, 0)),
            pl.BlockSpec((None, d_c, mod_slab), lambda j, i: (layer, 0, mod_blk0 + i)),
            pl.BlockSpec((1, mod_slab), lambda j, i: (0, mod_blk0 + i)),
        ] + [weight_tile(col) for col in weight_cols] + job_in,
        out_specs=[pl.BlockSpec((tm, n), lambda j, i: (i, 0)),
                   pl.BlockSpec((c_rows, None, 1, mod_slab),
                                lambda j, i: (0, i // slabs_per_chunk, 0, i % slabs_per_chunk))] + job_out,
        out_shape=[jax.ShapeDtypeStruct((m, n), BF16),
                   jax.ShapeDtypeStruct((c_rows, mod_n // d_c, 1, d_c), F32)] + job_shapes,
        scratch_shapes=[pltpu.VMEM((2 * n_w, k, lanes), BF16)],
        compiler_params=_params(2),
        name="glu_proj",
    )(h, c, w_ada, b_row, *([w] * (2 * n_w)), *[wj for wj, _ in cast_jobs])


_EPILOGUE_LANES = 512
_MATMUL_ROWS = 1024


def _merge_mix_body(ya_ref, yc_ref, wa_ref, wc_ref, ga_ref, gc_ref, w_ref, x_ref, gt_ref, g_ref, sc_ref, sh_ref,
                    x1_ref, h_ref, merged_ref):
    d = w_ref.shape[1]
    ya = ya_ref[...]
    yc = yc_ref[...]
    for n0 in range(0, d, _EPILOGUE_LANES):
        ns = slice(n0, n0 + _EPILOGUE_LANES)
        acc_a = jnp.dot(ya, wa_ref[:, ns], preferred_element_type=F32)
        acc_c = jnp.dot(yc, wc_ref[:, ns], preferred_element_type=F32)
        ga = _sigmoid(ga_ref[:, ns].astype(F32))
        gc = _sigmoid(gc_ref[:, ns].astype(F32))
        merged_ref[:, ns] = (ga * acc_a + gc * acc_c).astype(merged_ref.dtype)
    a = merged_ref[...]
    sumsq = jnp.zeros((a.shape[0], 1), F32)
    for n0 in range(0, d, _EPILOGUE_LANES):
        ns = slice(n0, n0 + _EPILOGUE_LANES)
        acc = jnp.dot(a, w_ref[:, ns], preferred_element_type=F32)
        x1 = x_ref[0, :, ns] + gt_ref[0, :, ns] * acc
        x1_ref[0, :, ns] = x1
        sumsq = sumsq + jnp.sum(x1 * x1, axis=-1, keepdims=True)
    inv = lax.rsqrt(sumsq / d + EPS)
    gain = g_ref[...] * (1.0 + sc_ref[0])
    h_ref[0] = (x1_ref[0] * inv * gain + sh_ref[0]).astype(h_ref.dtype)


def _merge_mix(attn, conv, wa_bf, wc_bf, gates, ga_col, gc_col, wmix_bf, x, mod3, gate_idx, norm_g,
               scale_idx, shift_idx, ts=256):
    b_sz, s_len, d = x.shape
    ka, kc = attn.shape[1], conv.shape[1]
    assert d % _EPILOGUE_LANES == 0 and ga_col % d == 0 and gc_col % d == 0
    tiles = s_len // ts
    const2 = lambda b, s: (0, 0)
    rows = lambda width, col_blk: pl.BlockSpec((ts, width), lambda b, s: (b * tiles + s, col_blk))
    mod_rows = mod3.shape[0] // b_sz
    mod_row = lambda idx: pl.BlockSpec((1, 1, d), lambda b, s: (b * mod_rows + idx, 0, 0))
    act = pl.BlockSpec((1, ts, d), lambda b, s: (b, s, 0))
    resident = pl.Buffered(1)
    return pl.pallas_call(
        _merge_mix_body,
        grid=(b_sz, tiles),
        in_specs=[
            rows(ka, 0),
            rows(kc, 0),
            pl.BlockSpec((ka, d), const2, pipeline_mode=resident),
            pl.BlockSpec((kc, d), const2, pipeline_mode=resident),
            rows(d, ga_col // d),
            rows(d, gc_col // d),
            pl.BlockSpec((d, d), const2, pipeline_mode=resident),
            act,
            mod_row(gate_idx),
            pl.BlockSpec((1, d), const2),
            mod_row(scale_idx),
            mod_row(shift_idx),
        ],
        out_specs=[act, act],
        out_shape=[jax.ShapeDtypeStruct((b_sz, s_len, d), F32), jax.ShapeDtypeStruct((b_sz, s_len, d), BF16)],
        scratch_shapes=[pltpu.VMEM((ts, d), BF16)],
        compiler_params=_params(2),
        name="merge_mix",
    )(attn, conv, wa_bf, wc_bf, gates, gates, wmix_bf, x, mod3, norm_g.reshape(1, d), mod3, mod3)


def _residual_proj_body(a_ref, w_ref, x_ref, gt_ref, o_ref):
    a = a_ref[...]
    for n0 in range(0, o_ref.shape[1], _EPILOGUE_LANES):
        ns = slice(n0, n0 + _EPILOGUE_LANES)
        acc = jnp.dot(a, w_ref[:, ns], preferred_element_type=F32)
        o_ref[:, ns] = x_ref[:, ns] + gt_ref[0, :, ns] * acc


def _residual_proj(a, w_bf, x2d, mod3, gate_idx, rows_per_batch, tm, tn):
    m, k = a.shape
    n = w_bf.shape[1]
    assert n % tn == 0 and tn % _EPILOGUE_LANES == 0 and rows_per_batch % tm == 0
    tiles_per_batch = rows_per_batch // tm
    mod_rows = mod3.shape[0] // (m // rows_per_batch)
    return pl.pallas_call(
        _residual_proj_body,
        grid=(n // tn, m // tm),
        in_specs=[
            pl.BlockSpec((tm, k), lambda j, i: (i, 0)),
            pl.BlockSpec((k, tn), lambda j, i: (0, j)),
            pl.BlockSpec((tm, tn), lambda j, i: (i, j)),
            pl.BlockSpec((1, 1, tn), lambda j, i: ((i // tiles_per_batch) * mod_rows + gate_idx, 0, j)),
        ],
        out_specs=pl.BlockSpec((tm, tn), lambda j, i: (i, j)),
        out_shape=jax.ShapeDtypeStruct((m, n), F32),
        compiler_params=_params(2),
        name="residual_proj",
    )(a, w_bf, x2d, mod3)


def _ffn_in_body(a_ref, wg_ref, wu_ref, wo_ref, o_ref, wobf_ref, wgbf_ref, wubf_ref):
    _cast_weights_once([wg_ref, wu_ref], [wgbf_ref, wubf_ref])
    wobf_ref[...] = wo_ref[...].astype(BF16)
    for r0 in range(0, a_ref.shape[0], _MATMUL_ROWS):
        rs = slice(r0, r0 + _MATMUL_ROWS)
        a = a_ref[rs, :]
        gate = jnp.dot(a, wgbf_ref[...], preferred_element_type=F32)
        up = jnp.dot(a, wubf_ref[...], preferred_element_type=F32)
        o_ref[rs, :] = (gate * _sigmoid(gate) * up).astype(o_ref.dtype)


def _ffn_in(h, w_ffn_in, w_ffn_out, layer, tm=2048, tn=512):
    m, k = h.shape
    d_ff = w_ffn_in.shape[2] // 2
    d_out = w_ffn_out.shape[2]
    up_blk = d_ff // tn
    m_tiles = m // tm
    steps = (d_ff // tn) * m_tiles
    assert d_ff % steps == 0
    slab = d_ff // steps
    step = lambda j, i: j * m_tiles + i
    return pl.pallas_call(
        _ffn_in_body,
        grid=(d_ff // tn, m_tiles),
        in_specs=[
            pl.BlockSpec((tm, k), lambda j, i: (i, 0)),
            pl.BlockSpec((None, k, tn), lambda j, i: (layer, 0, j)),
            pl.BlockSpec((None, k, tn), lambda j, i: (layer, 0, up_blk + j)),
            pl.BlockSpec((None, slab, d_out), lambda j, i: (layer, step(j, i), 0)),
        ],
        out_specs=[
            pl.BlockSpec((tm, tn), lambda j, i: (i, j)),
            pl.BlockSpec((slab, d_out), lambda j, i: (step(j, i), 0)),
        ],
        out_shape=[jax.ShapeDtypeStruct((m, d_ff), BF16), jax.ShapeDtypeStruct((d_ff, d_out), BF16)],
        scratch_shapes=[pltpu.VMEM((k, tn), BF16), pltpu.VMEM((k, tn), BF16)],
        compiler_params=_params(2),
        name="ffn_in",
    )(h, w_ffn_in, w_ffn_in, w_ffn_out)


def _t5_causal_bucket(dist):
    n = jnp.maximum(dist, 0)
    nf = jnp.maximum(n, 1).astype(jnp.float32)
    large = MAX_EXACT + (jnp.log(nf / MAX_EXACT) / math.log(MAX_DISTANCE / MAX_EXACT)
                         * (NUM_BUCKETS - MAX_EXACT)).astype(jnp.int32)
    large = jnp.minimum(large, NUM_BUCKETS - 1)
    return jnp.where(n < MAX_EXACT, n, large)


def _band_buckets():
    q_off = jnp.arange(BLOCK)
    k_off = jnp.arange(2 * BLOCK)
    dist = q_off[:, None] + BLOCK - k_off[None, :]
    allowed = (dist >= 0) & (dist < WINDOW)
    return jnp.where(allowed, _t5_causal_bucket(dist), -1).astype(jnp.int32)


def _attn_build_bias(first_step, rb_ref, sink_ref, bucket_ref, bias_ref, *, n_q, n_kv):
    q_per_kv = n_q // n_kv

    @pl.when(first_step)
    def _():
        bucket = bucket_ref[...]
        col = lax.broadcasted_iota(jnp.int32, bucket.shape, 1)
        for slot in range(n_q):
            head = _orig_head(slot, n_kv, q_per_kv)
            tile = jnp.zeros(bucket.shape, F32)
            for b in range(NUM_BUCKETS):
                tile = jnp.where(bucket == b, rb_ref[b * n_q + head] * LOG2E, tile)
            tile = jnp.where(bucket < 0, -jnp.inf, tile)
            sink = sink_ref[head] * LOG2E
            bias_ref[0, slot] = jnp.where(col == 0, sink, tile)
            bias_ref[1, slot] = jnp.where(col == 0, sink, jnp.where(col < BLOCK, -jnp.inf, tile))


def _attn_block(seq_start, qmask_ref, krow_ref, q_ref, kc_ref, kp_ref, vc_ref, vp_ref, o_ref,
                bias_ref, lg_ref, p_ref, pv_ref, rs_ref, *, n_q, n_kv):
    q_per_kv = n_q // n_kv
    kv_w = n_kv * HEAD_DIM
    lane_slot = lax.broadcasted_iota(jnp.int32, (BLOCK, BLOCK), 1) // HEAD_DIM
    ones = jnp.ones((2 * BLOCK, BLOCK), BF16)
    bias_idx = jnp.where(seq_start, 1, 0)

    keys = jnp.concatenate([kp_ref[...], kc_ref[...]], axis=0) * krow_ref[...]
    vals = jnp.concatenate([vp_ref[...], vc_ref[...]], axis=0) * krow_ref[...]
    lhs = jnp.concatenate(
        [q_ref[:, j * kv_w:(j + 1) * kv_w] * qmask_ref[g]
         for j in range(q_per_kv) for g in range(n_kv)], axis=0)
    lg_ref[...] = lax.dot_general(lhs, keys, (((1,), (1,)), ((), ())),
                                  preferred_element_type=F32).reshape(n_q, BLOCK, 2 * BLOCK)
    for slot in range(n_q):
        logit = lg_ref[slot] + bias_ref[bias_idx, slot]
        m = jnp.max(logit, axis=-1, keepdims=True)
        p_ref[slot * BLOCK:(slot + 1) * BLOCK, :] = jnp.exp2(logit - m).astype(BF16)
    p = p_ref[...]
    pv_ref[...] = jnp.dot(p, vals, preferred_element_type=F32).reshape(n_q, BLOCK, kv_w)
    rs_ref[...] = jnp.dot(p, ones, preferred_element_type=F32).reshape(n_q, BLOCK, BLOCK)
    per_half = BLOCK // HEAD_DIM
    for j in range(q_per_kv):
        for half in range(kv_w // BLOCK):
            lanes = slice(half * BLOCK, (half + 1) * BLOCK)
            slots = [j * n_kv + half * per_half + i for i in range(per_half)]
            num = pv_ref[slots[-1], :, lanes]
            den = rs_ref[slots[-1]]
            for i in range(per_half - 2, -1, -1):
                num = jnp.where(lane_slot == i, pv_ref[slots[i], :, lanes], num)
                den = jnp.where(lane_slot == i, rs_ref[slots[i]], den)
            o_ref[:, j * kv_w + half * BLOCK:j * kv_w + (half + 1) * BLOCK] = (
                num * (1.0 / den)).astype(o_ref.dtype)


_CONV_HALO = 32
_CONV_LANES = 256
_SUBLANES = 8


def _conv_zero_history(seq_start, u_ref):
    @pl.when(seq_start)
    def _():
        u_ref[:_CONV_HALO, :] = jnp.zeros((_CONV_HALO, u_ref.shape[1]), u_ref.dtype)


def _conv_shift_matrix(rows):
    i = np.arange(_SUBLANES * rows) % rows
    r = np.arange(_SUBLANES * rows) // rows
    return (np.arange(rows)[None, :] == (i + r)[:, None]).astype(BF16)


def _conv_block(uin_ref, shift_ref, w_ref, cb_ref, lg_ref, lb_ref, o_ref, u_ref, sh_ref, y_ref):
    width = w_ref.shape[0]
    ts, channels = o_ref.shape
    u_ref[_CONV_HALO:, :] = uin_ref[...]
    sh_ref[...] = jnp.dot(shift_ref[...], u_ref[...], preferred_element_type=F32).reshape(sh_ref.shape)
    first = _CONV_HALO - (width - 1)
    unknown_zero = jnp.minimum(pl.program_id(0), 0)
    for c0 in range(0, channels, _CONV_LANES):
        cs = slice(c0, c0 + _CONV_LANES)
        acc = jnp.broadcast_to(cb_ref[:, cs], (ts, _CONV_LANES))
        for j in range(width):
            tile, r = divmod(first + j, _SUBLANES)
            rows = pl.ds(pl.multiple_of(tile * _SUBLANES + unknown_zero, _SUBLANES), ts)
            acc = acc + sh_ref[r, rows, cs] * w_ref[j:j + 1, cs]
        y_ref[:, cs] = acc
    y = y_ref[...]
    mu = jnp.mean(y, axis=-1, keepdims=True)
    yc = y - mu
    var = jnp.mean(yc * yc, axis=-1, keepdims=True)
    z = yc * lax.rsqrt(var + EPS) * lg_ref[...] + lb_ref[...]
    o_ref[...] = (z * _sigmoid(z)).astype(o_ref.dtype)
    u_ref[:_CONV_HALO, :] = u_ref[ts:, :]


def _core_body(h_ref, w_ref,
               uin_ref, shift_ref, cw_ref, cb_ref, lg_ref, lb_ref,
               rb_ref, sink_ref, bucket_ref, qmask_ref, krow_ref, qkv_ref, kvp_ref,
               gates_ref, yconv_ref, yattn_ref,
               wbf_ref, u_ref, sh_ref, y_ref, bias_ref, lgt_ref, p_ref, pv_ref, rs_ref,
               *, n_q, n_kv, blocks_per_seq):
    attn_w, kv_w = n_q * HEAD_DIM, n_kv * HEAD_DIM
    t = pl.program_id(0) * pl.num_programs(1) + pl.program_id(1)
    seq_start = t % blocks_per_seq == 0
    _cast_weights_once([w_ref], [wbf_ref])
    _conv_zero_history(seq_start, u_ref)
    _attn_build_bias(t == 0, rb_ref, sink_ref, bucket_ref, bias_ref, n_q=n_q, n_kv=n_kv)

    _conv_block(uin_ref, shift_ref, cw_ref, cb_ref, lg_ref, lb_ref, yconv_ref, u_ref, sh_ref, y_ref)
    acc = jnp.dot(h_ref[...], wbf_ref[...], preferred_element_type=F32)
    gates_ref[...] = acc.astype(gates_ref.dtype)
    k_cols, v_cols = pl.ds(attn_w, kv_w), pl.ds(attn_w + kv_w, kv_w)
    _attn_block(seq_start, qmask_ref, krow_ref, qkv_ref.at[:, pl.ds(0, attn_w)],
                qkv_ref.at[:, k_cols], kvp_ref.at[:, pl.ds(0, kv_w)],
                qkv_ref.at[:, v_cols], kvp_ref.at[:, pl.ds(kv_w, kv_w)],
                yattn_ref, bias_ref, lgt_ref, p_ref, pv_ref, rs_ref, n_q=n_q, n_kv=n_kv)


def _mixer_core(h, w_in, layer, gate_col, conv_in, qkv, conv_w, conv_b, ln_g, ln_b, rel_bias, sinks,
                s_len, n_q, n_kv, tn=512):
    m, k = h.shape
    n = w_in.shape[2] - gate_col
    width, channels = conv_w.shape
    attn_w, kv_w = n_q * HEAD_DIM, n_kv * HEAD_DIM
    n_blocks = m // BLOCK
    n_tiles = n // tn
    assert gate_col % tn == 0 and n % tn == 0 and n_blocks % n_tiles == 0
    assert s_len % BLOCK == 0 and width - 1 <= _CONV_HALO and _CONV_HALO % _SUBLANES == 0
    assert attn_w % (2 * kv_w) == 0 and kv_w == 2 * BLOCK and BLOCK % HEAD_DIM == 0
    m_tiles = n_blocks // n_tiles
    tm = m // m_tiles
    col_blk0 = gate_col // tn
    kv_blk = attn_w // (2 * kv_w)

    step = lambda j, i: j * m_tiles + i
    prev_step = lambda j, i: jnp.maximum(step(j, i) - 1, 0)
    const2 = lambda j, i: (0, 0)
    row = lambda v: v.reshape(1, channels)
    lane_group = np.arange(kv_w) // HEAD_DIM
    qmask = np.broadcast_to((lane_group[None, :] == np.arange(n_kv)[:, None])[:, None, :],
                            (n_kv, BLOCK, kv_w)).astype(BF16)
    krow = np.broadcast_to((np.arange(2 * BLOCK) > 0)[:, None], (2 * BLOCK, kv_w)).astype(BF16)
    body = functools.partial(_core_body, n_q=n_q, n_kv=n_kv, blocks_per_seq=s_len // BLOCK)
    return pl.pallas_call(
        body,
        grid=(n_tiles, m_tiles),
        in_specs=[
            pl.BlockSpec((tm, k), lambda j, i: (i, 0)),
            pl.BlockSpec((None, k, tn), lambda j, i: (layer, 0, col_blk0 + j)),
            pl.BlockSpec((BLOCK, channels), lambda j, i: (step(j, i), 0)),
            pl.BlockSpec((_SUBLANES * (_CONV_HALO + BLOCK), _CONV_HALO + BLOCK), const2),
            pl.BlockSpec((width, channels), const2),
            pl.BlockSpec((1, channels), const2),
            pl.BlockSpec((1, channels), const2),
            pl.BlockSpec((1, channels), const2),
            pl.BlockSpec(memory_space=pltpu.SMEM),
            pl.BlockSpec(memory_space=pltpu.SMEM),
            pl.BlockSpec((BLOCK, 2 * BLOCK), const2),
            pl.BlockSpec((n_kv, BLOCK, kv_w), lambda j, i: (0, 0, 0)),
            pl.BlockSpec((2 * BLOCK, kv_w), const2),
            pl.BlockSpec((BLOCK, attn_w + 2 * kv_w), lambda j, i: (step(j, i), 0)),
            pl.BlockSpec((BLOCK, 2 * kv_w), lambda j, i: (prev_step(j, i), kv_blk)),
        ],
        out_specs=[
            pl.BlockSpec((tm, tn), lambda j, i: (i, j)),
            pl.BlockSpec((BLOCK, channels), lambda j, i: (step(j, i), 0)),
            pl.BlockSpec((BLOCK, attn_w), lambda j, i: (step(j, i), 0)),
        ],
        out_shape=[
            jax.ShapeDtypeStruct((m, n), BF16),
            jax.ShapeDtypeStruct((m, channels), BF16),
            jax.ShapeDtypeStruct((m, attn_w), BF16),
        ],
        scratch_shapes=[
            pltpu.VMEM((k, tn), BF16),
            pltpu.VMEM((_CONV_HALO + BLOCK, channels), BF16),
            pltpu.VMEM((_SUBLANES, _CONV_HALO + BLOCK, channels), F32),
            pltpu.VMEM((BLOCK, channels), F32),
            pltpu.VMEM((2, n_q, BLOCK, 2 * BLOCK), F32),
            pltpu.VMEM((n_q, BLOCK, 2 * BLOCK), F32),
            pltpu.VMEM((n_q * BLOCK, 2 * BLOCK), BF16),
            pltpu.VMEM((n_q, BLOCK, kv_w), F32),
            pltpu.VMEM((n_q, BLOCK, BLOCK), F32),
        ],
        compiler_params=_params(2),
        name="mixer_core",
    )(h, w_in,
      conv_in, _conv_shift_matrix(_CONV_HALO + BLOCK), conv_w, row(conv_b), row(ln_g), row(ln_b),
      rel_bias.astype(F32).reshape(-1), sinks.astype(F32), _band_buckets(), qmask, krow,
      qkv, qkv)


def kernel(x, c, w_ada, b_ada, norm_mix_g, w_in, q_norm_g, k_norm_g, attn_sinks, rel_bias,
           w_attn_out, conv_w, conv_b, conv_ln_g, conv_ln_b, w_conv_out, w_mix_out,
           norm_ffn_g, w_ffn_in, w_ffn_out):
    b_sz, s_len, d = x.shape
    depth = w_ada.shape[0]
    n_q = attn_sinks.shape[1]
    attn_w = w_attn_out.shape[1]
    channels = conv_w.shape[2]
    kv_w = (w_in.shape[2] - attn_w - 2 * channels - 2 * d) // 2
    n_kv = kv_w // HEAD_DIM
    conv_col = attn_w + 2 * kv_w
    gate_col = conv_col + 2 * channels
    m = b_sz * s_len

    for l in range(depth):
        b_row = b_ada[l].reshape(1, N_MOD * d)
        mod_early = _ada(c, w_ada, b_row, l, N_MOD_EARLY * d).reshape(b_sz * N_MOD_EARLY, 1, d)

        qkv, h = _qkv_proj(x, norm_mix_g[l], mod_early, 1, 0, w_in, l, q_norm_g[l], k_norm_g[l], n_q, n_kv)
        h = h.reshape(m, d)
        assert m % n_q == 0
        slot_head = lambda slot: _orig_head(slot, n_kv, n_q // n_kv)
        same = lambda step: step
        conv_in, mod_late, wa_bf, wc_bf, wmix_bf = _glu_proj(
            h, w_in, l, conv_col, conv_col + channels, channels, m // n_q,
            c, w_ada, b_row, N_MOD_EARLY * d,
            cast_jobs=((w_attn_out, slot_head), (w_conv_out, same), (w_mix_out, same)))
        mod_late = mod_late.reshape(b_sz * (N_MOD - N_MOD_EARLY), 1, d)
        gates, y_conv, y_attn = _mixer_core(h, w_in, l, gate_col, conv_in, qkv, conv_w[l], conv_b[l],
                                            conv_ln_g[l], conv_ln_b[l], rel_bias, attn_sinks[l],
                                            s_len, n_q, n_kv)
        x1, h = _merge_mix(y_attn, y_conv, wa_bf, wc_bf, gates, 0, d, wmix_bf, x, mod_late, 0,
                           norm_ffn_g[l], 2, 1)

        act, wo_bf = _ffn_in(h.reshape(m, d), w_ffn_in, w_ffn_out, l)
        x = _residual_proj(act, wo_bf, x1.reshape(m, d), mod_late, 3, s_len,
                           tm=512, tn=1024).reshape(b_sz, s_len, d)
    return x
```

```python
import functools
import math

import jax
import jax.numpy as jnp
import numpy as np
from jax import lax
from jax.experimental import pallas as pl
from jax.experimental.pallas import tpu as pltpu

F32 = jnp.float32
BF16 = jnp.bfloat16

HEAD_DIM = 64
WINDOW = 128
BLOCK = 128
NUM_BUCKETS = 32
MAX_EXACT = NUM_BUCKETS // 2
MAX_DISTANCE = 128
N_MOD = 6
N_MOD_EARLY = 2
EPS = 1e-6
LOG2E = math.log2(math.e)

V7X_VMEM_LIMIT_BYTES = 56 * 1024 * 1024


def _params(n_axes):
    return pltpu.CompilerParams(
        dimension_semantics=("arbitrary",) * n_axes,
        vmem_limit_bytes=V7X_VMEM_LIMIT_BYTES,
    )


def _sigmoid(x):
    return jax.nn.sigmoid(x)


def _orig_head(slot, n_kv, q_per_kv):
    return (slot % n_kv) * q_per_kv + slot // n_kv


def _ada_body(c_ref, w_ref, b_ref, o_ref):
    c = c_ref[...]
    act = (c * _sigmoid(c)).astype(BF16)
    mod = jnp.dot(act, w_ref[...].astype(BF16), preferred_element_type=F32) + b_ref[...]
    for b in range(mod.shape[0]):
        o_ref[b] = mod[b:b + 1, :]


def _ada(c, w_ada, b_row, layer, n):
    rows, d = c.shape
    assert n % d == 0
    return pl.pallas_call(
        _ada_body,
        grid=(n // d,),
        in_specs=[
            pl.BlockSpec((rows, d), lambda j: (0, 0)),
            pl.BlockSpec((None, d, d), lambda j: (layer, 0, j)),
            pl.BlockSpec((1, d), lambda j: (0, j)),
        ],
        out_specs=pl.BlockSpec((rows, None, 1, d), lambda j: (0, j, 0, 0)),
        out_shape=jax.ShapeDtypeStruct((rows, n // d, 1, d), F32),
        compiler_params=_params(1),
        name="ada_mod",
    )(c, w_ada, b_row)


def _qkv_body(x_ref, g_ref, sc_ref, sh_ref, w_ref, gq_ref, gk_ref, o_ref, h_ref, wbf_ref, seg_ref,
              *, n_q, n_kv, half):
    attn_w = n_q * HEAD_DIM
    kv_w = n_kv * HEAD_DIM

    @pl.when(jnp.logical_and(pl.program_id(0) == 0, pl.program_id(1) == 0))
    def _():
        for slot in range(n_q):
            src = _orig_head(slot, n_kv, n_q // n_kv) * HEAD_DIM
            wbf_ref[:, slot * HEAD_DIM:(slot + 1) * HEAD_DIM] = w_ref[:, src:src + HEAD_DIM].astype(BF16)
        wbf_ref[:, attn_w:] = w_ref[:, attn_w:].astype(BF16)
        r = lax.broadcasted_iota(jnp.int32, (half, half), 0) // HEAD_DIM
        c = lax.broadcasted_iota(jnp.int32, (half, half), 1) // HEAD_DIM
        seg_ref[...] = (r == c).astype(BF16)

    def head_norm(acc, seg, gain):
        ss = jnp.dot((acc * acc).astype(BF16), seg, preferred_element_type=F32)
        return acc * lax.rsqrt(ss / HEAD_DIM + EPS) * gain

    for r0 in range(0, x_ref.shape[1], _QKV_CHAIN_ROWS):
        rs = slice(r0, r0 + _QKV_CHAIN_ROWS)
        x = x_ref[0, rs, :]
        ms = jnp.mean(x * x, axis=-1, keepdims=True)
        gain = g_ref[...] * (1.0 + sc_ref[0])
        a = (x * lax.rsqrt(ms + EPS) * gain + sh_ref[0]).astype(BF16)
        h_ref[0, rs, :] = a
        for c0 in range(0, attn_w, half):
            acc = jnp.dot(a, wbf_ref[:, c0:c0 + half], preferred_element_type=F32)
            o_ref[rs, c0:c0 + half] = head_norm(acc, seg_ref[...], gq_ref[...]).astype(o_ref.dtype)
        acc = jnp.dot(a, wbf_ref[:, attn_w:], preferred_element_type=F32)
        o_ref[rs, attn_w:attn_w + kv_w] = head_norm(acc[:, :kv_w], seg_ref[:kv_w, :kv_w],
                                                    gk_ref[...]).astype(o_ref.dtype)
        o_ref[rs, attn_w + kv_w:] = acc[:, kv_w:].astype(o_ref.dtype)


_QKV_CHAIN_ROWS = 512


def _qkv_proj(x, norm_g, mod3, scale_idx, shift_idx, w_in, layer, q_g, k_g, n_q, n_kv, ts=1024, half=512):
    b_sz, s_len, d = x.shape
    attn_w, kv_w = n_q * HEAD_DIM, n_kv * HEAD_DIM
    width = attn_w + 2 * kv_w
    assert attn_w % half == 0 and kv_w <= half and half % HEAD_DIM == 0
    gq = jnp.tile(q_g * (HEAD_DIM ** -0.5 * LOG2E), half // HEAD_DIM).reshape(1, half)
    gk = jnp.tile(k_g, n_kv).reshape(1, kv_w)
    tiles = s_len // ts
    mod_rows = mod3.shape[0] // b_sz
    const2 = lambda b, s: (0, 0)
    body = functools.partial(_qkv_body, n_q=n_q, n_kv=n_kv, half=half)
    return pl.pallas_call(
        body,
        grid=(b_sz, tiles),
        in_specs=[
            pl.BlockSpec((1, ts, d), lambda b, s: (b, s, 0)),
            pl.BlockSpec((1, d), const2),
            pl.BlockSpec((1, 1, d), lambda b, s: (b * mod_rows + scale_idx, 0, 0)),
            pl.BlockSpec((1, 1, d), lambda b, s: (b * mod_rows + shift_idx, 0, 0)),
            pl.BlockSpec((None, d, width), lambda b, s: (layer, 0, 0), pipeline_mode=pl.Buffered(1)),
            pl.BlockSpec((1, half), const2),
            pl.BlockSpec((1, kv_w), const2),
        ],
        out_specs=[
            pl.BlockSpec((ts, width), lambda b, s: (b * tiles + s, 0)),
            pl.BlockSpec((1, ts, d), lambda b, s: (b, s, 0)),
        ],
        out_shape=[
            jax.ShapeDtypeStruct((b_sz * s_len, width), BF16),
            jax.ShapeDtypeStruct((b_sz, s_len, d), BF16),
        ],
        scratch_shapes=[pltpu.VMEM((d, width), BF16), pltpu.VMEM((half, half), BF16)],
        compiler_params=_params(2),
        name="qkv_proj",
    )(x, norm_g.reshape(1, d), mod3, mod3, w_in, gq, gk)


def _cast_weights_once(w_refs, wbf_refs):
    @pl.when(pl.program_id(1) == 0)
    def _():
        for w_ref, wbf_ref in zip(w_refs, wbf_refs):
            wbf_ref[...] = w_ref[...].astype(BF16)


def _glu_proj_body(h_ref, c_ref, wada_ref, bada_ref, *refs, n_w):
    w_refs, refs = refs[:2 * n_w], refs[2 * n_w:]
    n_jobs = (len(refs) - 3) // 2
    job_in, o_ref, mod_ref, job_out = refs[:n_jobs], refs[n_jobs], refs[n_jobs + 1], refs[n_jobs + 2:-1]
    wbf_ref = refs[-1]
    _cast_weights_once(w_refs, [wbf_ref.at[c] for c in range(2 * n_w)])
    _ada_body(c_ref, wada_ref, bada_ref, mod_ref)
    for src_ref, dst_ref in zip(job_in, job_out):
        dst_ref[...] = src_ref[...].astype(BF16)
    h = h_ref[...]
    lanes = wbf_ref.shape[2]
    for c in range(n_w):
        a = jnp.dot(h, wbf_ref[c], preferred_element_type=F32)
        g = jnp.dot(h, wbf_ref[n_w + c], preferred_element_type=F32)
        o_ref[:, c * lanes:(c + 1) * lanes] = (a * _sigmoid(g)).astype(o_ref.dtype)


def _glu_proj(h, w, layer, a_col, g_col, n, tm, c, w_ada, b_row, mod_col, cast_jobs=()):
    m, k = h.shape
    lanes = _EPILOGUE_LANES
    assert a_col % lanes == 0 and g_col % lanes == 0 and n % lanes == 0
    n_w = n // lanes
    steps = m // tm
    mod_n = w_ada.shape[2] - mod_col
    assert mod_n % steps == 0 and mod_col % (mod_n // steps) == 0
    mod_slab = mod_n // steps
    mod_blk0 = mod_col // mod_slab
    c_rows, d_c = c.shape
    assert d_c % mod_slab == 0 and mod_n % d_c == 0
    slabs_per_chunk = d_c // mod_slab
    weight_tile = lambda col: pl.BlockSpec((None, k, lanes), lambda j, i: (layer, 0, col // lanes),
                                           pipeline_mode=pl.Buffered(1))
    weight_cols = [a_col + c * lanes for c in range(n_w)] + [g_col + c * lanes for c in range(n_w)]
    job_in, job_out, job_shapes = [], [], []
    for wj, row_block_of_step in cast_jobs:
        rows, cols = wj.shape[1:]
        assert rows % steps == 0
        slab = rows // steps
        job_in.append(pl.BlockSpec((None, slab, cols), lambda j, i, f=row_block_of_step: (layer, f(i), 0)))
        job_out.append(pl.BlockSpec((slab, cols), lambda j, i: (i, 0)))
        job_shapes.append(jax.ShapeDtypeStruct((rows, cols), BF16))
    return pl.pallas_call(
        functools.partial(_glu_proj_body, n_w=n_w),
        grid=(1, steps),
        in_specs=[
            pl.BlockSpec((tm, k), lambda j, i: (i, 0)),
            pl.BlockSpec((c_rows, d_c), lambda j, i: (0, 0)),
            pl.BlockSpec((None, d_c, mod_slab), lambda j, i: (layer, 0, mod_blk0 + i)),
            pl.BlockSpec((1, mod_slab), lambda j, i: (0, mod_blk0 + i)),
        ] + [weight_tile(col) for col in weight_cols] + job_in,
        out_specs=[pl.BlockSpec((tm, n), lambda j, i: (i, 0)),
                   pl.BlockSpec((c_rows, None, 1, mod_slab),
                                lambda j, i: (0, i // slabs_per_chunk, 0, i % slabs_per_chunk))] + job_out,
        out_shape=[jax.ShapeDtypeStruct((m, n), BF16),
                   jax.ShapeDtypeStruct((c_rows, mod_n // d_c, 1, d_c), F32)] + job_shapes,
        scratch_shapes=[pltpu.VMEM((2 * n_w, k, lanes), BF16)],
        compiler_params=_params(2),
        name="glu_proj",
    )(h, c, w_ada, b_row, *([w] * (2 * n_w)), *[wj for wj, _ in cast_jobs])


_EPILOGUE_LANES = 512
_MATMUL_ROWS = 1024


def _merge_mix_body(ya_ref, yc_ref, wa_ref, wc_ref, ga_ref, gc_ref, w_ref, x_ref, gt_ref, g_ref, sc_ref, sh_ref,
                    x1_ref, h_ref, merged_ref):
    d = w_ref.shape[1]
    ya = ya_ref[...]
    yc = yc_ref[...]
    for n0 in range(0, d, _EPILOGUE_LANES):
        ns = slice(n0, n0 + _EPILOGUE_LANES)
        acc_a = jnp.dot(ya, wa_ref[:, ns], preferred_element_type=F32)
        acc_c = jnp.dot(yc, wc_ref[:, ns], preferred_element_type=F32)
        ga = _sigmoid(ga_ref[:, ns].astype(F32))
        gc = _sigmoid(gc_ref[:, ns].astype(F32))
        merged_ref[:, ns] = (ga * acc_a + gc * acc_c).astype(merged_ref.dtype)
    a = merged_ref[...]
    sumsq = jnp.zeros((a.shape[0], 1), F32)
    for n0 in range(0, d, _EPILOGUE_LANES):
        ns = slice(n0, n0 + _EPILOGUE_LANES)
        acc = jnp.dot(a, w_ref[:, ns], preferred_element_type=F32)
        x1 = x_ref[0, :, ns] + gt_ref[0, :, ns] * acc
        x1_ref[0, :, ns] = x1
        sumsq = sumsq + jnp.sum(x1 * x1, axis=-1, keepdims=True)
    inv = lax.rsqrt(sumsq / d + EPS)
    gain = g_ref[...] * (1.0 + sc_ref[0])
    h_ref[0] = (x1_ref[0] * inv * gain + sh_ref[0]).astype(h_ref.dtype)


def _merge_mix(attn, conv, wa_bf, wc_bf, gates, ga_col, gc_col, wmix_bf, x, mod3, gate_idx, norm_g,
               scale_idx, shift_idx, ts=256):
    b_sz, s_len, d = x.shape
    ka, kc = attn.shape[1], conv.shape[1]
    assert d % _EPILOGUE_LANES == 0 and ga_col % d == 0 and gc_col % d == 0
    tiles = s_len // ts
    const2 = lambda b, s: (0, 0)
    rows = lambda width, col_blk: pl.BlockSpec((ts, width), lambda b, s: (b * tiles + s, col_blk))
    mod_rows = mod3.shape[0] // b_sz
    mod_row = lambda idx: pl.BlockSpec((1, 1, d), lambda b, s: (b * mod_rows + idx, 0, 0))
    act = pl.BlockSpec((1, ts, d), lambda b, s: (b, s, 0))
    resident = pl.Buffered(1)
    return pl.pallas_call(
        _merge_mix_body,
        grid=(b_sz, tiles),
        in_specs=[
            rows(ka, 0),
            rows(kc, 0),
            pl.BlockSpec((ka, d), const2, pipeline_mode=resident),
            pl.BlockSpec((kc, d), const2, pipeline_mode=resident),
            rows(d, ga_col // d),
            rows(d, gc_col // d),
            pl.BlockSpec((d, d), const2, pipeline_mode=resident),
            act,
            mod_row(gate_idx),
            pl.BlockSpec((1, d), const2),
            mod_row(scale_idx),
            mod_row(shift_idx),
        ],
        out_specs=[act, act],
        out_shape=[jax.ShapeDtypeStruct((b_sz, s_len, d), F32), jax.ShapeDtypeStruct((b_sz, s_len, d), BF16)],
        scratch_shapes=[pltpu.VMEM((ts, d), BF16)],
        compiler_params=_params(2),
        name="merge_mix",
    )(attn, conv, wa_bf, wc_bf, gates, gates, wmix_bf, x, mod3, norm_g.reshape(1, d), mod3, mod3)


def _residual_proj_body(a_ref, w_ref, x_ref, gt_ref, o_ref):
    a = a_ref[...]
    for n0 in range(0, o_ref.shape[1], _EPILOGUE_LANES):
        ns = slice(n0, n0 + _EPILOGUE_LANES)
        acc = jnp.dot(a, w_ref[:, ns], preferred_element_type=F32)
        o_ref[:, ns] = x_ref[:, ns] + gt_ref[0, :, ns] * acc


def _residual_proj(a, w_bf, x2d, mod3, gate_idx, rows_per_batch, tm, tn):
    m, k = a.shape
    n = w_bf.shape[1]
    assert n % tn == 0 and tn % _EPILOGUE_LANES == 0 and rows_per_batch % tm == 0
    tiles_per_batch = rows_per_batch // tm
    mod_rows = mod3.shape[0] // (m // rows_per_batch)
    return pl.pallas_call(
        _residual_proj_body,
        grid=(n // tn, m // tm),
        in_specs=[
            pl.BlockSpec((tm, k), lambda j, i: (i, 0)),
            pl.BlockSpec((k, tn), lambda j, i: (0, j)),
            pl.BlockSpec((tm, tn), lambda j, i: (i, j)),
            pl.BlockSpec((1, 1, tn), lambda j, i: ((i // tiles_per_batch) * mod_rows + gate_idx, 0, j)),
        ],
        out_specs=pl.BlockSpec((tm, tn), lambda j, i: (i, j)),
        out_shape=jax.ShapeDtypeStruct((m, n), F32),
        compiler_params=_params(2),
        name="residual_proj",
    )(a, w_bf, x2d, mod3)


def _ffn_in_body(a_ref, wg_ref, wu_ref, wo_ref, o_ref, wobf_ref, wgbf_ref, wubf_ref):
    _cast_weights_once([wg_ref, wu_ref], [wgbf_ref, wubf_ref])
    wobf_ref[...] = wo_ref[...].astype(BF16)
    for r0 in range(0, a_ref.shape[0], _MATMUL_ROWS):
        rs = slice(r0, r0 + _MATMUL_ROWS)
        a = a_ref[rs, :]
        gate = jnp.dot(a, wgbf_ref[...], preferred_element_type=F32)
        up = jnp.dot(a, wubf_ref[...], preferred_element_type=F32)
        o_ref[rs, :] = (gate * _sigmoid(gate) * up).astype(o_ref.dtype)


def _ffn_in(h, w_ffn_in, w_ffn_out, layer, tm=2048, tn=512):
    m, k = h.shape
    d_ff = w_ffn_in.shape[2] // 2
    d_out = w_ffn_out.shape[2]
    up_blk = d_ff // tn
    m_tiles = m // tm
    steps = (d_ff // tn) * m_tiles
    assert d_ff % steps == 0
    slab = d_ff // steps
    step = lambda j, i: j * m_tiles + i
    return pl.pallas_call(
        _ffn_in_body,
        grid=(d_ff // tn, m_tiles),
        in_specs=[
            pl.BlockSpec((tm, k), lambda j, i: (i, 0)),
            pl.BlockSpec((None, k, tn), lambda j, i: (layer, 0, j)),
            pl.BlockSpec((None, k, tn), lambda j, i: (layer, 0, up_blk + j)),
            pl.BlockSpec((None, slab, d_out), lambda j, i: (layer, step(j, i), 0)),
        ],
        out_specs=[
            pl.BlockSpec((tm, tn), lambda j, i: (i, j)),
            pl.BlockSpec((slab, d_out), lambda j, i: (step(j, i), 0)),
        ],
        out_shape=[jax.ShapeDtypeStruct((m, d_ff), BF16), jax.ShapeDtypeStruct((d_ff, d_out), BF16)],
        scratch_shapes=[pltpu.VMEM((k, tn), BF16), pltpu.VMEM((k, tn), BF16)],
        compiler_params=_params(2),
        name="ffn_in",
    )(h, w_ffn_in, w_ffn_in, w_ffn_out)


def _t5_causal_bucket(dist):
    n = np.maximum(dist, 0)
    nf = np.maximum(n, 1).astype(np.float32)
    large = MAX_EXACT + (np.log(nf / MAX_EXACT) / math.log(MAX_DISTANCE / MAX_EXACT)
                         * (NUM_BUCKETS - MAX_EXACT)).astype(np.int32)
    large = np.minimum(large, NUM_BUCKETS - 1)
    return np.where(n < MAX_EXACT, n, large)


def _band_buckets():
    q_off = np.arange(BLOCK)
    k_off = np.arange(2 * BLOCK)
    dist = q_off[:, None] + BLOCK - k_off[None, :]
    allowed = (dist >= 0) & (dist < WINDOW)
    return np.where(allowed, _t5_causal_bucket(dist), -1).astype(np.int32)


def _attn_build_bias(first_step, rb_ref, sink_ref, bucket_ref, bias_ref, *, n_q, n_kv):
    q_per_kv = n_q // n_kv

    @pl.when(first_step)
    def _():
        bucket = bucket_ref[...]
        col = lax.broadcasted_iota(jnp.int32, bucket.shape, 1)
        for slot in range(n_q):
            head = _orig_head(slot, n_kv, q_per_kv)
            tile = jnp.zeros(bucket.shape, F32)
            for b in range(NUM_BUCKETS):
                tile = jnp.where(bucket == b, rb_ref[b * n_q + head] * LOG2E, tile)
            tile = jnp.where(bucket < 0, -jnp.inf, tile)
            sink = sink_ref[head] * LOG2E
            bias_ref[0, slot] = jnp.where(col == 0, sink, tile)
            bias_ref[1, slot] = jnp.where(col == 0, sink, jnp.where(col < BLOCK, -jnp.inf, tile))


def _attn_block(seq_start, qmask_ref, krow_ref, q_ref, kc_ref, kp_ref, vc_ref, vp_ref, o_ref,
                bias_ref, lg_ref, p_ref, pv_ref, rs_ref, *, n_q, n_kv):
    q_per_kv = n_q // n_kv
    kv_w = n_kv * HEAD_DIM
    lane_slot = lax.broadcasted_iota(jnp.int32, (BLOCK, BLOCK), 1) // HEAD_DIM
    ones = jnp.ones((2 * BLOCK, BLOCK), BF16)
    bias_idx = jnp.where(seq_start, 1, 0)

    keys = jnp.concatenate([kp_ref[...], kc_ref[...]], axis=0) * krow_ref[...]
    vals = jnp.concatenate([vp_ref[...], vc_ref[...]], axis=0) * krow_ref[...]
    lhs = jnp.concatenate(
        [q_ref[:, j * kv_w:(j + 1) * kv_w] * qmask_ref[g]
         for j in range(q_per_kv) for g in range(n_kv)], axis=0)
    lg_ref[...] = lax.dot_general(lhs, keys, (((1,), (1,)), ((), ())),
                                  preferred_element_type=F32).reshape(n_q, BLOCK, 2 * BLOCK)
    for slot in range(n_q):
        logit = lg_ref[slot] + bias_ref[bias_idx, slot]
        m = jnp.max(logit, axis=-1, keepdims=True)
        p_ref[slot * BLOCK:(slot + 1) * BLOCK, :] = jnp.exp2(logit - m).astype(BF16)
    p = p_ref[...]
    pv_ref[...] = jnp.dot(p, vals, preferred_element_type=F32).reshape(n_q, BLOCK, kv_w)
    rs_ref[...] = jnp.dot(p, ones, preferred_element_type=F32).reshape(n_q, BLOCK, BLOCK)
    per_half = BLOCK // HEAD_DIM
    for j in range(q_per_kv):
        for half in range(kv_w // BLOCK):
            lanes = slice(half * BLOCK, (half + 1) * BLOCK)
            slots = [j * n_kv + half * per_half + i for i in range(per_half)]
            num = pv_ref[slots[-1], :, lanes]
            den = rs_ref[slots[-1]]
            for i in range(per_half - 2, -1, -1):
                num = jnp.where(lane_slot == i, pv_ref[slots[i], :, lanes], num)
                den = jnp.where(lane_slot == i, rs_ref[slots[i]], den)
            o_ref[:, j * kv_w + half * BLOCK:j * kv_w + (half + 1) * BLOCK] = (
                num * (1.0 / den)).astype(o_ref.dtype)


_CONV_HALO = 32
_CONV_LANES = 256
_SUBLANES = 8


def _conv_zero_history(seq_start, u_ref):
    @pl.when(seq_start)
    def _():
        u_ref[:_CONV_HALO, :] = jnp.zeros((_CONV_HALO, u_ref.shape[1]), u_ref.dtype)


def _conv_shift_matrix(rows):
    i = np.arange(_SUBLANES * rows) % rows
    r = np.arange(_SUBLANES * rows) // rows
    return (np.arange(rows)[None, :] == (i + r)[:, None]).astype(BF16)


def _conv_block(uin_ref, shift_ref, w_ref, cb_ref, lg_ref, lb_ref, o_ref, u_ref, sh_ref, y_ref):
    width = w_ref.shape[0]
    ts, channels = o_ref.shape
    u_ref[_CONV_HALO:, :] = uin_ref[...]
    sh_ref[...] = jnp.dot(shift_ref[...], u_ref[...], preferred_element_type=F32).reshape(sh_ref.shape)
    first = _CONV_HALO - (width - 1)
    unknown_zero = jnp.minimum(pl.program_id(0), 0)
    for c0 in range(0, channels, _CONV_LANES):
        cs = slice(c0, c0 + _CONV_LANES)
        acc = jnp.broadcast_to(cb_ref[:, cs], (ts, _CONV_LANES))
        for j in range(width):
            tile, r = divmod(first + j, _SUBLANES)
            rows = pl.ds(pl.multiple_of(tile * _SUBLANES + unknown_zero, _SUBLANES), ts)
            acc = acc + sh_ref[r, rows, cs] * w_ref[j:j + 1, cs]
        y_ref[:, cs] = acc
    y = y_ref[...]
    mu = jnp.mean(y, axis=-1, keepdims=True)
    yc = y - mu
    var = jnp.mean(yc * yc, axis=-1, keepdims=True)
    z = yc * lax.rsqrt(var + EPS) * lg_ref[...] + lb_ref[...]
    o_ref[...] = (z * _sigmoid(z)).astype(o_ref.dtype)
    u_ref[:_CONV_HALO, :] = u_ref[ts:, :]


def _core_body(h_ref, w_ref,
               uin_ref, shift_ref, cw_ref, cb_ref, lg_ref, lb_ref,
               rb_ref, sink_ref, bucket_ref, qmask_ref, krow_ref, qkv_ref, kvp_ref,
               gates_ref, yconv_ref, yattn_ref,
               wbf_ref, u_ref, sh_ref, y_ref, bias_ref, lgt_ref, p_ref, pv_ref, rs_ref,
               *, n_q, n_kv, blocks_per_seq):
    attn_w, kv_w = n_q * HEAD_DIM, n_kv * HEAD_DIM
    t = pl.program_id(0) * pl.num_programs(1) + pl.program_id(1)
    seq_start = t % blocks_per_seq == 0
    _cast_weights_once([w_ref], [wbf_ref])
    _conv_zero_history(seq_start, u_ref)
    _attn_build_bias(t == 0, rb_ref, sink_ref, bucket_ref, bias_ref, n_q=n_q, n_kv=n_kv)

    _conv_block(uin_ref, shift_ref, cw_ref, cb_ref, lg_ref, lb_ref, yconv_ref, u_ref, sh_ref, y_ref)
    acc = jnp.dot(h_ref[...], wbf_ref[...], preferred_element_type=F32)
    gates_ref[...] = acc.astype(gates_ref.dtype)
    k_cols, v_cols = pl.ds(attn_w, kv_w), pl.ds(attn_w + kv_w, kv_w)
    _attn_block(seq_start, qmask_ref, krow_ref, qkv_ref.at[:, pl.ds(0, attn_w)],
                qkv_ref.at[:, k_cols], kvp_ref.at[:, pl.ds(0, kv_w)],
                qkv_ref.at[:, v_cols], kvp_ref.at[:, pl.ds(kv_w, kv_w)],
                yattn_ref, bias_ref, lgt_ref, p_ref, pv_ref, rs_ref, n_q=n_q, n_kv=n_kv)


def _mixer_core(h, w_in, layer, gate_col, conv_in, qkv, conv_w, conv_b, ln_g, ln_b, rel_bias, sinks,
                s_len, n_q, n_kv, tn=512):
    m, k = h.shape
    n = w_in.shape[2] - gate_col
    width, channels = conv_w.shape
    attn_w, kv_w = n_q * HEAD_DIM, n_kv * HEAD_DIM
    n_blocks = m // BLOCK
    n_tiles = n // tn
    assert gate_col % tn == 0 and n % tn == 0 and n_blocks % n_tiles == 0
    assert s_len % BLOCK == 0 and width - 1 <= _CONV_HALO and _CONV_HALO % _SUBLANES == 0
    assert attn_w % (2 * kv_w) == 0 and kv_w == 2 * BLOCK and BLOCK % HEAD_DIM == 0
    m_tiles = n_blocks // n_tiles
    tm = m // m_tiles
    col_blk0 = gate_col // tn
    kv_blk = attn_w // (2 * kv_w)

    step = lambda j, i: j * m_tiles + i
    prev_step = lambda j, i: jnp.maximum(step(j, i) - 1, 0)
    const2 = lambda j, i: (0, 0)
    row = lambda v: v.reshape(1, channels)
    lane_group = np.arange(kv_w) // HEAD_DIM
    qmask = np.broadcast_to((lane_group[None, :] == np.arange(n_kv)[:, None])[:, None, :],
                            (n_kv, BLOCK, kv_w)).astype(BF16)
    krow = np.broadcast_to((np.arange(2 * BLOCK) > 0)[:, None], (2 * BLOCK, kv_w)).astype(BF16)
    body = functools.partial(_core_body, n_q=n_q, n_kv=n_kv, blocks_per_seq=s_len // BLOCK)
    return pl.pallas_call(
        body,
        grid=(n_tiles, m_tiles),
        in_specs=[
            pl.BlockSpec((tm, k), lambda j, i: (i, 0)),
            pl.BlockSpec((None, k, tn), lambda j, i: (layer, 0, col_blk0 + j)),
            pl.BlockSpec((BLOCK, channels), lambda j, i: (step(j, i), 0)),
            pl.BlockSpec((_SUBLANES * (_CONV_HALO + BLOCK), _CONV_HALO + BLOCK), const2),
            pl.BlockSpec((width, channels), const2),
            pl.BlockSpec((1, channels), const2),
            pl.BlockSpec((1, channels), const2),
            pl.BlockSpec((1, channels), const2),
            pl.BlockSpec(memory_space=pltpu.SMEM),
            pl.BlockSpec(memory_space=pltpu.SMEM),
            pl.BlockSpec((BLOCK, 2 * BLOCK), const2),
            pl.BlockSpec((n_kv, BLOCK, kv_w), lambda j, i: (0, 0, 0)),
            pl.BlockSpec((2 * BLOCK, kv_w), const2),
            pl.BlockSpec((BLOCK, attn_w + 2 * kv_w), lambda j, i: (step(j, i), 0)),
            pl.BlockSpec((BLOCK, 2 * kv_w), lambda j, i: (prev_step(j, i), kv_blk)),
        ],
        out_specs=[
            pl.BlockSpec((tm, tn), lambda j, i: (i, j)),
            pl.BlockSpec((BLOCK, channels), lambda j, i: (step(j, i), 0)),
            pl.BlockSpec((BLOCK, attn_w), lambda j, i: (step(j, i), 0)),
        ],
        out_shape=[
            jax.ShapeDtypeStruct((m, n), BF16),
            jax.ShapeDtypeStruct((m, channels), BF16),
            jax.ShapeDtypeStruct((m, attn_w), BF16),
        ],
        scratch_shapes=[
            pltpu.VMEM((k, tn), BF16),
            pltpu.VMEM((_CONV_HALO + BLOCK, channels), BF16),
            pltpu.VMEM((_SUBLANES, _CONV_HALO + BLOCK, channels), F32),
            pltpu.VMEM((BLOCK, channels), F32),
            pltpu.VMEM((2, n_q, BLOCK, 2 * BLOCK), F32),
            pltpu.VMEM((n_q, BLOCK, 2 * BLOCK), F32),
            pltpu.VMEM((n_q * BLOCK, 2 * BLOCK), BF16),
            pltpu.VMEM((n_q, BLOCK, kv_w), F32),
            pltpu.VMEM((n_q, BLOCK, BLOCK), F32),
        ],
        compiler_params=_params(2),
        name="mixer_core",
    )(h, w_in,
      conv_in, _conv_shift_matrix(_CONV_HALO + BLOCK), conv_w, row(conv_b), row(ln_g), row(ln_b),
      rel_bias.astype(F32).reshape(-1), sinks.astype(F32), _band_buckets(), qmask, krow,
      qkv, qkv)


def kernel(x, c, w_ada, b_ada, norm_mix_g, w_in, q_norm_g, k_norm_g, attn_sinks, rel_bias,
           w_attn_out, conv_w, conv_b, conv_ln_g, conv_ln_b, w_conv_out, w_mix_out,
           norm_ffn_g, w_ffn_in, w_ffn_out):
    b_sz, s_len, d = x.shape
    depth = w_ada.shape[0]
    n_q = attn_sinks.shape[1]
    attn_w = w_attn_out.shape[1]
    channels = conv_w.shape[2]
    kv_w = (w_in.shape[2] - attn_w - 2 * channels - 2 * d) // 2
    n_kv = kv_w // HEAD_DIM
    conv_col = attn_w + 2 * kv_w
    gate_col = conv_col + 2 * channels
    m = b_sz * s_len

    for l in range(depth):
        b_row = b_ada[l].reshape(1, N_MOD * d)
        mod_early = _ada(c, w_ada, b_row, l, N_MOD_EARLY * d).reshape(b_sz * N_MOD_EARLY, 1, d)

        qkv, h = _qkv_proj(x, norm_mix_g[l], mod_early, 1, 0, w_in, l, q_norm_g[l], k_norm_g[l], n_q, n_kv)
        h = h.reshape(m, d)
        assert m % n_q == 0
        slot_head = lambda slot: _orig_head(slot, n_kv, n_q // n_kv)
        same = lambda step: step
        conv_in, mod_late, wa_bf, wc_bf, wmix_bf = _glu_proj(
            h, w_in, l, conv_col, conv_col + channels, channels, m // n_q,
            c, w_ada, b_row, N_MOD_EARLY * d,
            cast_jobs=((w_attn_out, slot_head), (w_conv_out, same), (w_mix_out, same)))
        mod_late = mod_late.reshape(b_sz * (N_MOD - N_MOD_EARLY), 1, d)
        gates, y_conv, y_attn = _mixer_core(h, w_in, l, gate_col, conv_in, qkv, conv_w[l], conv_b[l],
                                            conv_ln_g[l], conv_ln_b[l], rel_bias, attn_sinks[l],
                                            s_len, n_q, n_kv)
        x1, h = _merge_mix(y_attn, y_conv, wa_bf, wc_bf, gates, 0, d, wmix_bf, x, mod_late, 0,
                           norm_ffn_g[l], 2, 1)

        act, wo_bf = _ffn_in(h.reshape(m, d), w_ffn_in, w_ffn_out, l)
        x = _residual_proj(act, wo_bf, x1.reshape(m, d), mod_late, 3, s_len,
                           tm=512, tn=1024).reshape(b_sz, s_len, d)
    return x
```
